```python
import jax, jax.numpy as jnp
from jax import lax
import numpy as np

D_MODEL = 1024
BATCH = 2
SEQ = 8192
DEPTH = 1

GM_WIDTH = D_MODEL
GM_GROUPS = 8
GM_GROUP_DIM = GM_WIDTH // GM_GROUPS
GM_CHUNK = 128
N_HEADS = 8
N_KV_HEADS = 2
Q_PER_KV = N_HEADS // N_KV_HEADS
HEAD_DIM = 128
ATTN_WIDTH = N_HEADS * HEAD_DIM
KV_WIDTH = N_KV_HEADS * HEAD_DIM
CMP_BLOCK = 32
CMP_STRIDE = 16
SEL_BLOCK = 64
N_SELECT = 16
N_LOCAL_FORCED = 2
WINDOW = 512
Q_BLOCK = 128
N_NSA_BRANCH = 3
N_MERGE_BRANCH = 2
D_FF = 4 * D_MODEL
SPLIT_SIZES = (2 * GM_WIDTH, ATTN_WIDTH, KV_WIDTH, KV_WIDTH, KV_WIDTH, KV_WIDTH, KV_WIDTH, KV_WIDTH,
               N_NSA_BRANCH * N_HEADS, N_MERGE_BRANCH * D_MODEL)
D_IN = sum(SPLIT_SIZES)
DEEPNORM_ALPHA = (2.0 * DEPTH) ** 0.25
DEEPNORM_BETA = (8.0 * DEPTH) ** -0.25
LN_EPS = 1e-5
NEG_INF = -1e30
FORCED_SCORE = 1e9

kernel_name = "gmlp_nsa_griffin_merge_deepnorm"


def layer_norm(x, g, b):
    xf = x.astype(jnp.float32)
    mu = jnp.mean(xf, axis=-1, keepdims=True)
    var = jnp.mean(jnp.square(xf - mu), axis=-1, keepdims=True)
    y = (xf - mu) * lax.rsqrt(var + LN_EPS) * g.astype(jnp.float32) + b.astype(jnp.float32)
    return y.astype(x.dtype)


def alibi_slopes():
    h = jnp.arange(1, N_HEADS + 1, dtype=jnp.float32)
    return (2.0 ** (-8.0 * h / N_HEADS)).reshape(N_KV_HEADS, Q_PER_KV)


def spatial_gating(z, ln_g, ln_b, w_s, b_s):
    bsz, seq, _ = z.shape
    u, v = jnp.split(jax.nn.gelu(z), 2, axis=-1)
    v = layer_norm(v, ln_g, ln_b).reshape(bsz, seq // GM_CHUNK, GM_CHUNK, GM_GROUPS, GM_GROUP_DIM)
    causal = jnp.tril(jnp.ones((GM_CHUNK, GM_CHUNK), dtype=w_s.dtype))
    v = jnp.einsum("gij,bnjgd->bnigd", w_s * causal, v) + b_s.T[None, None, :, :, None]
    return u * v.reshape(bsz, seq, GM_WIDTH)


def compress(x, pe, w1, w2):
    bsz, seq = x.shape[:2]
    ch = x.reshape(bsz, seq // CMP_STRIDE, CMP_STRIDE, N_KV_HEADS, HEAD_DIM)
    blk = jnp.concatenate([ch[:, :-1], ch[:, 1:]], axis=2) + pe[None, None, :, None, :]
    flat = blk.transpose(0, 1, 3, 2, 4).reshape(bsz, seq // CMP_STRIDE - 1, N_KV_HEADS, CMP_BLOCK * HEAD_DIM)
    return jax.nn.gelu(flat @ w1) @ w2


def nsa_attention(q, k_cmp, v_cmp, k_slc, v_slc, k_win, v_win, gates):
    bsz, seq = q.shape[:2]
    n_cmp = seq // CMP_STRIDE - 1
    n_sel = seq // SEL_BLOCK
    k_top = min(N_SELECT, n_sel)
    slopes = alibi_slopes()
    q = (q * HEAD_DIM ** -0.5).reshape(bsz, seq, N_KV_HEADS, Q_PER_KV, HEAD_DIM)
    gates = gates.reshape(bsz, seq, N_KV_HEADS, Q_PER_KV, N_NSA_BRANCH)
    cmp_idx = jnp.arange(n_cmp)
    cmp_end = cmp_idx * CMP_STRIDE + CMP_BLOCK - 1
    cmp_ctr = cmp_idx.astype(jnp.float32) * CMP_STRIDE + 0.5 * (CMP_BLOCK - 1)
    sel_idx = jnp.arange(n_sel)
    overlap = ((cmp_end[:, None] >= sel_idx[None, :] * SEL_BLOCK)
               & (cmp_idx[:, None] * CMP_STRIDE <= sel_idx[None, :] * SEL_BLOCK + SEL_BLOCK - 1)).astype(jnp.float32)
    ks_blocks = k_slc.reshape(bsz, n_sel, SEL_BLOCK, N_KV_HEADS, HEAD_DIM).transpose(0, 3, 1, 2, 4)
    vs_blocks = v_slc.reshape(bsz, n_sel, SEL_BLOCK, N_KV_HEADS, HEAD_DIM).transpose(0, 3, 1, 2, 4)
    pad = ((0, 0), (WINDOW, 0), (0, 0), (0, 0))
    kw_pad = jnp.pad(k_win, pad)
    vw_pad = jnp.pad(v_win, pad)
    gather = jax.vmap(jax.vmap(lambda blocks, ix: blocks[ix]))
    offs_sel = jnp.arange(SEL_BLOCK)
    offs_win = jnp.arange(WINDOW + Q_BLOCK)

    def query_block(qb):
        q0 = qb * Q_BLOCK
        qc = lax.dynamic_slice_in_dim(q, q0, Q_BLOCK, axis=1)
        gc = lax.dynamic_slice_in_dim(gates, q0, Q_BLOCK, axis=1)
        t = q0 + jnp.arange(Q_BLOCK)
        valid_c = cmp_end[None, :] <= t[:, None]
        dist_c = t[:, None].astype(jnp.float32) - cmp_ctr[None, :]
        s = jnp.einsum("bqhgd,bnhd->bhgqn", qc, k_cmp).astype(jnp.float32) \
            - slopes[None, :, :, None, None] * dist_c
        p_c = jnp.where(valid_c, jax.nn.softmax(jnp.where(valid_c, s, NEG_INF), axis=-1), 0.0)
        o_c = jnp.einsum("bhgqn,bnhd->bqhgd", p_c.astype(v_cmp.dtype), v_cmp)
        imp = jnp.einsum("bhgqn,nm->bhqm", p_c, overlap)
        lag = (t // SEL_BLOCK)[:, None] - sel_idx[None, :]
        forced = (sel_idx[None, :] == 0) | ((lag >= 0) & (lag < N_LOCAL_FORCED))
        score = jnp.where(forced, FORCED_SCORE, jnp.where(lag >= 0, imp, -1.0))
        _, idx = lax.top_k(score, k_top)
        ks = gather(ks_blocks, idx)
        vs = gather(vs_blocks, idx)
        dist_s = (t[None, None, :, None, None] - (idx[..., None] * SEL_BLOCK + offs_sel))[:, :, None]
        s = jnp.einsum("bqhgd,bhqnkd->bhgqnk", qc, ks).astype(jnp.float32) \
            - slopes[None, :, :, None, None, None] * dist_s.astype(jnp.float32)
        s = jnp.where(dist_s >= 0, s, NEG_INF)
        shp = s.shape
        p_s = jax.nn.softmax(s.reshape(shp[0], shp[1], shp[2], shp[3], -1), axis=-1).reshape(shp)
        o_s = jnp.einsum("bhgqnk,bhqnkd->bqhgd", p_s.astype(vs.dtype), vs)
        kw = lax.dynamic_slice_in_dim(kw_pad, q0, WINDOW + Q_BLOCK, axis=1)
        vw = lax.dynamic_slice_in_dim(vw_pad, q0, WINDOW + Q_BLOCK, axis=1)
        dist_w = t[:, None] - (q0 - WINDOW + offs_win)[None, :]
        valid_w = (dist_w >= 0) & (dist_w < WINDOW) & (dist_w <= t[:, None])
        s = jnp.einsum("bqhgd,bkhd->bhgqk", qc, kw).astype(jnp.float32) \
            - slopes[None, :, :, None, None] * dist_w.astype(jnp.float32)
        p_w = jax.nn.softmax(jnp.where(valid_w, s, NEG_INF), axis=-1)
        o_w = jnp.einsum("bhgqk,bkhd->bqhgd", p_w.astype(vw.dtype), vw)
        return gc[..., 0:1] * o_c + gc[..., 1:2] * o_s + gc[..., 2:3] * o_w

    out = lax.map(query_block, jnp.arange(seq // Q_BLOCK))
    return jnp.moveaxis(out, 0, 1).reshape(bsz, seq, ATTN_WIDTH)


def setup_inputs(seed: int = 0) -> dict:
    key = jax.random.key(seed)
    ks = jax.random.split(key, 24)

    def nrm(k, shape, scale):
        return jax.random.normal(k, shape, jnp.float32) * scale

    L = DEPTH
    fan_cmp = CMP_BLOCK * HEAD_DIM
    return {
        "x": nrm(ks[0], (BATCH, SEQ, D_MODEL), 1.0),
        "w_in": nrm(ks[1], (L, D_MODEL, D_IN), D_MODEL ** -0.5),
        "gm_ln_g": 1.0 + nrm(ks[2], (L, GM_WIDTH), 0.02),
        "gm_ln_b": nrm(ks[3], (L, GM_WIDTH), 0.02),
        "gm_w_s": nrm(ks[4], (L, GM_GROUPS, GM_CHUNK, GM_CHUNK), GM_CHUNK ** -0.5),
        "gm_b_s": 1.0 + nrm(ks[5], (L, GM_GROUPS, GM_CHUNK), 0.02),
        "cmp_pe_k": nrm(ks[6], (L, CMP_BLOCK, HEAD_DIM), 0.02),
        "cmp_w1_k": nrm(ks[7], (L, fan_cmp, HEAD_DIM), fan_cmp ** -0.5),
        "cmp_w2_k": nrm(ks[8], (L, HEAD_DIM, HEAD_DIM), HEAD_DIM ** -0.5),
        "cmp_pe_v": nrm(ks[9], (L, CMP_BLOCK, HEAD_DIM), 0.02),
        "cmp_w1_v": nrm(ks[10], (L, fan_cmp, HEAD_DIM), fan_cmp ** -0.5),
        "cmp_w2_v": nrm(ks[11], (L, HEAD_DIM, HEAD_DIM), HEAD_DIM ** -0.5),
        "w_proj_gm": nrm(ks[12], (L, GM_WIDTH, D_MODEL), GM_WIDTH ** -0.5 * DEEPNORM_BETA),
        "w_proj_nsa": nrm(ks[13], (L, ATTN_WIDTH, D_MODEL), ATTN_WIDTH ** -0.5 * DEEPNORM_BETA),
        "w_out": nrm(ks[14], (L, D_MODEL, D_MODEL), D_MODEL ** -0.5 * DEEPNORM_BETA),
        "ln1_g": 1.0 + nrm(ks[15], (L, D_MODEL), 0.02),
        "ln1_b": nrm(ks[16], (L, D_MODEL), 0.02),
        "w_ff1": nrm(ks[17], (L, D_MODEL, D_FF), D_MODEL ** -0.5),
        "w_ff2": nrm(ks[18], (L, D_FF, D_MODEL), D_FF ** -0.5 * DEEPNORM_BETA),
        "ln2_g": 1.0 + nrm(ks[19], (L, D_MODEL), 0.02),
        "ln2_b": nrm(ks[20], (L, D_MODEL), 0.02),
    }


def reference(x, w_in, gm_ln_g, gm_ln_b, gm_w_s, gm_b_s, cmp_pe_k, cmp_w1_k, cmp_w2_k,
              cmp_pe_v, cmp_w1_v, cmp_w2_v, w_proj_gm, w_proj_nsa, w_out,
              ln1_g, ln1_b, w_ff1, w_ff2, ln2_g, ln2_b):
    bsz, seq, _ = x.shape
    split_points = np.cumsum(SPLIT_SIZES)[:-1].tolist()
    kv_shape = (bsz, seq, N_KV_HEADS, HEAD_DIM)
    h = x
    for l in range(DEPTH):
        z = h @ w_in[l]
        (z_gm, z_q, z_kc, z_vc, z_ks, z_vs, z_kw, z_vw, z_g, z_m) = jnp.split(z, split_points, axis=-1)
        y_gm = spatial_gating(z_gm, gm_ln_g[l], gm_ln_b[l], gm_w_s[l], gm_b_s[l])
        k_cmp = compress(z_kc.reshape(kv_shape), cmp_pe_k[l], cmp_w1_k[l], cmp_w2_k[l])
        v_cmp = compress(z_vc.reshape(kv_shape), cmp_pe_v[l], cmp_w1_v[l], cmp_w2_v[l])
        y_nsa = nsa_attention(z_q, k_cmp, v_cmp, z_ks.reshape(kv_shape), z_vs.reshape(kv_shape),
                              z_kw.reshape(kv_shape), z_vw.reshape(kv_shape), jax.nn.sigmoid(z_g))
        mg = jax.nn.sigmoid(z_m).reshape(bsz, seq, N_MERGE_BRANCH, D_MODEL)
        mix = (mg[:, :, 0] * (y_gm @ w_proj_gm[l]) + mg[:, :, 1] * (y_nsa @ w_proj_nsa[l])) @ w_out[l]
        h = layer_norm(DEEPNORM_ALPHA * h + mix, ln1_g[l], ln1_b[l])
        f = jnp.square(jax.nn.relu(h @ w_ff1[l])) @ w_ff2[l]
        h = layer_norm(DEEPNORM_ALPHA * h + f, ln2_g[l], ln2_b[l])
    return h
```

```python
import functools
import math

import jax
import jax.numpy as jnp
from jax import lax
from jax.experimental import pallas as pl
from jax.experimental.pallas import tpu as pltpu

D_MODEL = 1024
GM_GROUPS = 8
GM_CHUNK = 128
N_KV_HEADS = 2
Q_PER_KV = 4
HEAD_DIM = 128
CMP_BLOCK = 32
CMP_STRIDE = 16
SEL_BLOCK = 64
N_SELECT = 16
N_LOCAL_FORCED = 2
WINDOW = 512
Q_BLOCK = 128
D_FF = 4 * D_MODEL
DEEPNORM_ALPHA = 2.0 ** 0.25
LN_EPS = 1e-5
NEG_INF = -1e30
FORCED_SCORE = 1e9

LANES = 128
SEL_TILE = 512
ROW_TILE = 512
VMEM_LIMIT = 56 * 1024 * 1024

_BF = jnp.bfloat16
_F32 = jnp.float32
_NT = (((1,), (1,)), ((), ()))


def _dot(a, b):
    return jnp.dot(a, b, preferred_element_type=_F32)


def _dot_nt(a, b):
    return lax.dot_general(a, b, _NT, preferred_element_type=_F32)


def _gelu(x):
    c = math.sqrt(2.0 / math.pi)
    return 0.5 * x * (1.0 + jnp.tanh(c * (x + 0.044715 * (x * x * x))))


def _sigmoid(x):
    return 1.0 / (1.0 + jnp.exp(-x))


def _layer_norm(x, g, b):
    mu = jnp.mean(x, axis=-1, keepdims=True)
    xc = x - mu
    var = jnp.mean(xc * xc, axis=-1, keepdims=True)
    return xc * lax.rsqrt(var + LN_EPS) * g + b


def _gm_kernel(x_ref, wgm_ref, wm0_ref, lng_ref, lnb_ref, ws_ref, bs_ref, wpg_ref, o_ref, vg_ref):
    tm = x_ref.shape[0]
    xb = x_ref[...]
    z = _gelu(_dot(xb, wgm_ref[...]))
    u = z[:, :D_MODEL]
    v = _layer_norm(z[:, D_MODEL:], lng_ref[...], lnb_ref[...]).astype(_BF)
    row = lax.broadcasted_iota(jnp.int32, (GM_CHUNK, GM_CHUNK), 0)
    col = lax.broadcasted_iota(jnp.int32, (GM_CHUNK, GM_CHUNK), 1)
    for gi in range(GM_GROUPS):
        w = jnp.where(row >= col, ws_ref[gi], 0.0).astype(_BF)
        for c in range(tm // GM_CHUNK):
            blk = v[c * GM_CHUNK:(c + 1) * GM_CHUNK, gi * LANES:(gi + 1) * LANES]
            vg_ref[c * GM_CHUNK:(c + 1) * GM_CHUNK, gi * LANES:(gi + 1) * LANES] = _dot(w, blk) + bs_ref[gi]
    y = (u * vg_ref[...]).astype(_BF)
    gate = _sigmoid(_dot(xb, wm0_ref[...]))
    o_ref[...] = gate * _dot(y, wpg_ref[...])


def _gm_mixer(xb, wgm, wm0, lng, lnb, ws, bs, wpg):
    T = xb.shape[0]
    tm = ROW_TILE
    const2 = lambda i: (0, 0)
    const3 = lambda i: (0, 0, 0)
    return pl.pallas_call(
        _gm_kernel,
        out_shape=jax.ShapeDtypeStruct((T, D_MODEL), _F32),
        grid=(T // tm,),
        in_specs=[
            pl.BlockSpec((tm, D_MODEL), lambda i: (i, 0)),
            pl.BlockSpec((D_MODEL, 2 * D_MODEL), const2),
            pl.BlockSpec((D_MODEL, D_MODEL), const2),
            pl.BlockSpec((1, D_MODEL), const2),
            pl.BlockSpec((1, D_MODEL), const2),
            pl.BlockSpec((GM_GROUPS, GM_CHUNK, GM_CHUNK), const3),
            pl.BlockSpec((GM_GROUPS, GM_CHUNK, LANES), const3),
            pl.BlockSpec((D_MODEL, D_MODEL), const2),
        ],
        out_specs=pl.BlockSpec((tm, D_MODEL), lambda i: (i, 0)),
        scratch_shapes=[pltpu.VMEM((tm, D_MODEL), _F32)],
        compiler_params=pltpu.CompilerParams(
            dimension_semantics=("arbitrary",), vmem_limit_bytes=VMEM_LIMIT),
        name="gm_mixer",
    )(xb, wgm, wm0, lng, lnb, ws, bs, wpg)


def _qkv_kernel(x_ref, wq_ref, wkv_ref, wg_ref, q_ref, kc_ref, vc_ref, ksa_ref, vs_ref, kw_ref, vw_ref, g_ref):
    tm = x_ref.shape[0]
    xb = x_ref[...]
    q_ref[...] = (_dot(xb, wq_ref[...]) * (HEAD_DIM ** -0.5)).astype(_BF)
    z = _dot(xb, wkv_ref[...])
    kpos = pl.program_id(1) * tm + lax.broadcasted_iota(jnp.int32, (tm, LANES), 0)
    blk = lax.broadcasted_iota(jnp.int32, (tm, LANES), 1)
    ind = jnp.where((kpos >> 6) == blk, NEG_INF, 0.0).astype(_BF)
    for h in range(N_KV_HEADS):
        def col(j):
            return z[:, j * 2 * HEAD_DIM + h * HEAD_DIM: j * 2 * HEAD_DIM + (h + 1) * HEAD_DIM]
        kc_ref[0, h] = col(0)
        vc_ref[0, h] = col(1)
        ksa_ref[0, h, :, :HEAD_DIM] = col(2).astype(_BF)
        ksa_ref[0, h, :, HEAD_DIM:] = ind
        vs_ref[0, h] = col(3).astype(_BF)
        kw_ref[0, h] = col(4).astype(_BF)
        vw_ref[0, h] = col(5).astype(_BF)
    zg = _sigmoid(_dot(xb, wg_ref[...]))
    for h in range(N_KV_HEADS):
        g_ref[h] = zg[:, h * LANES:(h + 1) * LANES]


def _qkv_proj(xb, wq, wkv, wg, B, S):
    T = xb.shape[0]
    tm = ROW_TILE
    nsb = S // tm
    const2 = lambda b, s: (0, 0)
    kv_spec = lambda w: pl.BlockSpec((1, N_KV_HEADS, tm, w), lambda b, s: (b, 0, s, 0))
    kv_shape = lambda w, dt: jax.ShapeDtypeStruct((B, N_KV_HEADS, S, w), dt)
    return pl.pallas_call(
        _qkv_kernel,
        out_shape=(
            jax.ShapeDtypeStruct((T, D_MODEL), _BF),
            kv_shape(HEAD_DIM, _F32), kv_shape(HEAD_DIM, _F32),
            kv_shape(2 * HEAD_DIM, _BF), kv_shape(HEAD_DIM, _BF),
            kv_shape(HEAD_DIM, _BF), kv_shape(HEAD_DIM, _BF),
            jax.ShapeDtypeStruct((N_KV_HEADS, T, LANES), _F32),
        ),
        grid=(B, nsb),
        in_specs=[
            pl.BlockSpec((tm, D_MODEL), lambda b, s: (b * nsb + s, 0)),
            pl.BlockSpec((D_MODEL, D_MODEL), const2),
            pl.BlockSpec((D_MODEL, 6 * N_KV_HEADS * HEAD_DIM), const2),
            pl.BlockSpec((D_MODEL, N_KV_HEADS * LANES), const2),
        ],
        out_specs=(
            pl.BlockSpec((tm, D_MODEL), lambda b, s: (b * nsb + s, 0)),
            kv_spec(HEAD_DIM), kv_spec(HEAD_DIM), kv_spec(2 * HEAD_DIM), kv_spec(HEAD_DIM),
            kv_spec(HEAD_DIM), kv_spec(HEAD_DIM),
            pl.BlockSpec((N_KV_HEADS, tm, LANES), lambda b, s: (0, b * nsb + s, 0)),
        ),
        compiler_params=pltpu.CompilerParams(
            dimension_semantics=("arbitrary", "arbitrary"), vmem_limit_bytes=VMEM_LIMIT),
        name="qkv_proj",
    )(xb, wq, wkv, wg)


def _compress_kernel(kc_ref, vc_ref, pek_ref, pev_ref, w1k_ref, w1v_ref, w2k_ref, w2v_ref, ko_ref, vo_ref):
    nc = kc_ref.shape[1]
    for src, pe, w1, w2, dst in ((kc_ref, pek_ref, w1k_ref, w2k_ref, ko_ref),
                                 (vc_ref, pev_ref, w1v_ref, w2v_ref, vo_ref)):
        ch = src[0]
        first = _dot((ch + pe[0:1, :]).astype(_BF), w1[0])
        second = _dot((ch + pe[1:2, :]).astype(_BF), w1[1])
        pre = first + pltpu.roll(second, nc - 1, 0)
        dst[0] = _dot(_gelu(pre).astype(_BF), w2[...]).astype(_BF)


def _nsa_compress(kc, vc, pek, pev, w1k, w1v, w2k, w2v):
    BH, nc, width = kc.shape
    half = CMP_STRIDE * HEAD_DIM
    const2 = lambda i: (0, 0)
    const3 = lambda i: (0, 0, 0)
    row = lambda w: pl.BlockSpec((1, nc, w), lambda i: (i, 0, 0))
    return pl.pallas_call(
        _compress_kernel,
        out_shape=(jax.ShapeDtypeStruct((BH, nc, HEAD_DIM), _BF),) * 2,
        grid=(BH,),
        in_specs=[row(width), row(width),
                  pl.BlockSpec((2, half), const2), pl.BlockSpec((2, half), const2),
                  pl.BlockSpec((2, half, HEAD_DIM), const3), pl.BlockSpec((2, half, HEAD_DIM), const3),
                  pl.BlockSpec((HEAD_DIM, HEAD_DIM), const2), pl.BlockSpec((HEAD_DIM, HEAD_DIM), const2)],
        out_specs=(row(HEAD_DIM), row(HEAD_DIM)),
        compiler_params=pltpu.CompilerParams(
            dimension_semantics=("arbitrary",), vmem_limit_bytes=VMEM_LIMIT),
        name="nsa_compress",
    )(kc, vc, pek, pev, w1k, w1v, w2k, w2v)


def _top_blocks_not_selected(score, k_top):
    rows = lax.broadcasted_iota(jnp.int32, score.shape, 0).astype(_F32)
    notsel = jnp.ones(score.shape, _F32)
    for _ in range(k_top):
        mx = jnp.max(score, axis=0, keepdims=True)
        idx = jnp.min(jnp.where(score == mx, rows, float(LANES)), axis=0, keepdims=True)
        hit = rows == idx
        notsel = jnp.where(hit, 0.0, notsel)
        score = jnp.where(hit, -jnp.inf, score)
    return notsel


def _attn_kernel(q_ref, g_ref, kc_ref, vc_ref, ksa_ref, vs_ref, kw_ref, vw_ref, o_ref,
                 m_ref, l_ref, acc_ref, *, seq):
    h = pl.program_id(1)
    qb = pl.program_id(2)
    q0 = qb * Q_BLOCK
    nc = kc_ref.shape[1]
    n_sel = seq // SEL_BLOCK
    k_top = min(N_SELECT, n_sel)
    G = Q_PER_KV
    slopes = [jnp.where(h == 0, _F32(2.0 ** -(g + 1)), _F32(2.0 ** -(G + g + 1))) for g in range(G)]

    qblk = q_ref[...]
    qs = jnp.concatenate([qblk[:, g * HEAD_DIM:(g + 1) * HEAD_DIM] for g in range(G)], axis=0)

    tq = lax.broadcasted_iota(jnp.int32, (Q_BLOCK, nc), 0) + q0
    ci = lax.broadcasted_iota(jnp.int32, (Q_BLOCK, nc), 1)
    valid_c = ci * CMP_STRIDE + (CMP_BLOCK - 1) <= tq
    dist_c = tq.astype(_F32) - (ci.astype(_F32) * CMP_STRIDE + 0.5 * (CMP_BLOCK - 1))
    s_all = _dot_nt(qs, kc_ref[0])
    vcmp = vc_ref[0]
    psum = jnp.zeros((Q_BLOCK, nc), _F32)
    o_cmp = []
    for g in range(G):
        s = s_all[g * Q_BLOCK:(g + 1) * Q_BLOCK] - slopes[g] * dist_c
        s = jnp.where(valid_c, s, NEG_INF)
        e = jnp.exp(s - jnp.max(s, axis=-1, keepdims=True))
        p = jnp.where(valid_c, e / jnp.sum(e, axis=-1, keepdims=True), 0.0)
        psum = psum + p
        o_cmp.append(_dot(p.astype(_BF), vcmp))

    mi = lax.broadcasted_iota(jnp.int32, (LANES, nc), 0)
    ni = lax.broadcasted_iota(jnp.int32, (LANES, nc), 1)
    overlap_t = jnp.where((ni * CMP_STRIDE + (CMP_BLOCK - 1) >= mi * SEL_BLOCK)
                          & (ni * CMP_STRIDE <= mi * SEL_BLOCK + (SEL_BLOCK - 1)), 1.0, 0.0).astype(_BF)
    p_hi = psum.astype(_BF)
    p_lo = (psum - p_hi.astype(_F32)).astype(_BF)
    imp_t = _dot_nt(overlap_t, p_hi) + _dot_nt(overlap_t, p_lo)
    mrow = lax.broadcasted_iota(jnp.int32, (LANES, Q_BLOCK), 0)
    tcol = lax.broadcasted_iota(jnp.int32, (LANES, Q_BLOCK), 1) + q0
    lag = (tcol >> 6) - mrow
    forced = (mrow == 0) | ((lag >= 0) & (lag < N_LOCAL_FORCED))
    score = jnp.where(forced, FORCED_SCORE, jnp.where(lag >= 0, imp_t, -1.0))
    score = jnp.where(mrow < n_sel, score, -jnp.inf)
    notsel = _top_blocks_not_selected(score, k_top).T.astype(_BF)
    lhs = jnp.concatenate([qs, jnp.concatenate([notsel] * G, axis=0)], axis=1)

    m_ref[...] = jnp.full(m_ref.shape, NEG_INF, _F32)
    l_ref[...] = jnp.zeros(l_ref.shape, _F32)
    acc_ref[...] = jnp.zeros(acc_ref.shape, _F32)
    rel = (lax.broadcasted_iota(jnp.int32, (Q_BLOCK, SEL_TILE), 0)
           - lax.broadcasted_iota(jnp.int32, (Q_BLOCK, SEL_TILE), 1))

    def sel_tile(j, causal):
        k0 = pl.multiple_of(j * SEL_TILE, SEL_TILE)
        s_aug = _dot_nt(lhs, ksa_ref[0, 0, pl.ds(k0, SEL_TILE), :])
        vt = vs_ref[0, 0, pl.ds(k0, SEL_TILE), :]
        dist_i = rel + (q0 - k0)
        dist = dist_i.astype(_F32)
        for g in range(G):
            rs = slice(g * Q_BLOCK, (g + 1) * Q_BLOCK)
            s = s_aug[rs] - slopes[g] * dist
            if causal:
                s = jnp.where(dist_i >= 0, s, NEG_INF)
            m_old = m_ref[rs]
            m_new = jnp.maximum(m_old, jnp.max(s, axis=-1, keepdims=True))
            alpha = jnp.exp(m_old - m_new)
            p = jnp.exp(s - m_new[:, :1])
            l_ref[rs] = alpha * l_ref[rs] + jnp.sum(p, axis=-1, keepdims=True)
            m_ref[rs] = m_new
            acc_ref[rs] = alpha * acc_ref[rs] + _dot(p.astype(_BF), vt)

    n_full = q0 // SEL_TILE

    def loop_body(j, carry):
        sel_tile(j, False)
        return carry

    lax.fori_loop(0, n_full, loop_body, 0)
    sel_tile(n_full, True)

    wlen = WINDOW + Q_BLOCK
    w0 = pl.multiple_of(jnp.maximum(q0 - WINDOW, 0), Q_BLOCK)
    kwt = kw_ref[0, 0, pl.ds(w0, wlen), :]
    vwt = vw_ref[0, 0, pl.ds(w0, wlen), :]
    s_win = _dot_nt(qs, kwt)
    dw_i = (lax.broadcasted_iota(jnp.int32, (Q_BLOCK, wlen), 0)
            - lax.broadcasted_iota(jnp.int32, (Q_BLOCK, wlen), 1)) + (q0 - w0)
    valid_w = (dw_i >= 0) & (dw_i < WINDOW)
    dw = dw_i.astype(_F32)

    gates = g_ref[0]
    for g in range(G):
        rs = slice(g * Q_BLOCK, (g + 1) * Q_BLOCK)
        s = jnp.where(valid_w, s_win[rs] - slopes[g] * dw, NEG_INF)
        e = jnp.exp(s - jnp.max(s, axis=-1, keepdims=True))
        o_win = _dot(e.astype(_BF), vwt) / jnp.sum(e, axis=-1, keepdims=True)
        o_sel = acc_ref[rs] / l_ref[rs]
        out = (gates[:, 3 * g:3 * g + 1] * o_cmp[g] + gates[:, 3 * g + 1:3 * g + 2] * o_sel
               + gates[:, 3 * g + 2:3 * g + 3] * o_win)
        o_ref[:, g * HEAD_DIM:(g + 1) * HEAD_DIM] = out.astype(_BF)


def _nsa_attention(q, gates, kcmp, vcmp, ksa, vs, kw, vw, B, S):
    T = q.shape[0]
    nqb = S // Q_BLOCK
    nc = kcmp.shape[1]
    gd = Q_PER_KV * HEAD_DIM
    full = lambda w: pl.BlockSpec((1, 1, S, w), lambda b, h, i: (b, h, 0, 0))
    cmp_spec = pl.BlockSpec((1, nc, HEAD_DIM), lambda b, h, i: (b * N_KV_HEADS + h, 0, 0))
    return pl.pallas_call(
        functools.partial(_attn_kernel, seq=S),
        out_shape=jax.ShapeDtypeStruct((T, N_KV_HEADS * gd), _BF),
        grid=(B, N_KV_HEADS, nqb),
        in_specs=[
            pl.BlockSpec((Q_BLOCK, gd), lambda b, h, i: (b * nqb + i, h)),
            pl.BlockSpec((1, Q_BLOCK, LANES), lambda b, h, i: (h, b * nqb + i, 0)),
            cmp_spec, cmp_spec,
            full(2 * HEAD_DIM), full(HEAD_DIM), full(HEAD_DIM), full(HEAD_DIM),
        ],
        out_specs=pl.BlockSpec((Q_BLOCK, gd), lambda b, h, i: (b * nqb + i, h)),
        scratch_shapes=[pltpu.VMEM((Q_PER_KV * Q_BLOCK, LANES), _F32),
                        pltpu.VMEM((Q_PER_KV * Q_BLOCK, LANES), _F32),
                        pltpu.VMEM((Q_PER_KV * Q_BLOCK, HEAD_DIM), _F32)],
        compiler_params=pltpu.CompilerParams(
            dimension_semantics=("arbitrary", "arbitrary", "arbitrary"), vmem_limit_bytes=VMEM_LIMIT),
        name="nsa_attention",
    )(q, gates, kcmp, vcmp, ksa, vs, kw, vw)


def _merge_ffn_kernel(x_ref, gm_ref, yn_ref, wm1_ref, wpn_ref, wo_ref, g1_ref, b1_ref,
                      w1_ref, w2_ref, g2_ref, b2_ref, o_ref):
    x = x_ref[...]
    gate = _sigmoid(_dot(x.astype(_BF), wm1_ref[...]))
    merged = gm_ref[...] + gate * _dot(yn_ref[...], wpn_ref[...])
    mix = _dot(merged.astype(_BF), wo_ref[...])
    hid = _layer_norm(DEEPNORM_ALPHA * x + mix, g1_ref[...], b1_ref[...])
    hb = hid.astype(_BF)
    f = jnp.zeros(hid.shape, _F32)
    for c in range(D_FF // D_MODEL):
        a = jnp.maximum(_dot(hb, w1_ref[:, c * D_MODEL:(c + 1) * D_MODEL]), 0.0)
        f = f + _dot((a * a).astype(_BF), w2_ref[c * D_MODEL:(c + 1) * D_MODEL, :])
    o_ref[...] = _layer_norm(DEEPNORM_ALPHA * hid + f, g2_ref[...], b2_ref[...])


def _merge_ffn(x2, gm, yn, wm1, wpn, wo, g1, b1, w1, w2, g2, b2):
    T = x2.shape[0]
    tm = ROW_TILE
    rows = lambda w: pl.BlockSpec((tm, w), lambda i: (i, 0))
    const = lambda r, c: pl.BlockSpec((r, c), lambda i: (0, 0), pipeline_mode=pl.Buffered(1))
    return pl.pallas_call(
        _merge_ffn_kernel,
        out_shape=jax.ShapeDtypeStruct((T, D_MODEL), _F32),
        grid=(T // tm,),
        in_specs=[rows(D_MODEL), rows(D_MODEL), rows(D_MODEL),
                  const(D_MODEL, D_MODEL), const(D_MODEL, D_MODEL), const(D_MODEL, D_MODEL),
                  const(1, D_MODEL), const(1, D_MODEL),
                  const(D_MODEL, D_FF), const(D_FF, D_MODEL),
                  const(1, D_MODEL), const(1, D_MODEL)],
        out_specs=rows(D_MODEL),
        compiler_params=pltpu.CompilerParams(
            dimension_semantics=("arbitrary",), vmem_limit_bytes=VMEM_LIMIT),
        name="merge_ffn",
    )(x2, gm, yn, wm1, wpn, wo, g1, b1, w1, w2, g2, b2)


def _layer(x2, B, S, w_in, gm_ln_g, gm_ln_b, gm_w_s, gm_b_s, cmp_pe_k, cmp_w1_k, cmp_w2_k,
           cmp_pe_v, cmp_w1_v, cmp_w2_v, w_proj_gm, w_proj_nsa, w_out,
           ln1_g, ln1_b, w_ff1, w_ff2, ln2_g, ln2_b):
    gmw = 2 * D_MODEL
    aw = Q_PER_KV * N_KV_HEADS * HEAD_DIM
    kvw = N_KV_HEADS * HEAD_DIM
    o_q = gmw
    o_kv = o_q + aw
    o_g = o_kv + 6 * kvw
    n_gate = 3 * Q_PER_KV * N_KV_HEADS
    o_m = o_g + n_gate
    wb = w_in.astype(_BF)
    xb = x2.astype(_BF)
    row = lambda v: v.reshape(1, -1)

    wg = jnp.zeros((D_MODEL, N_KV_HEADS, LANES), _BF)
    wg = wg.at[:, :, :n_gate // N_KV_HEADS].set(wb[:, o_g:o_m].reshape(D_MODEL, N_KV_HEADS, -1))
    wg = wg.reshape(D_MODEL, N_KV_HEADS * LANES)

    gm = _gm_mixer(xb, wb[:, :gmw], wb[:, o_m:o_m + D_MODEL], row(gm_ln_g), row(gm_ln_b), gm_w_s,
                   jnp.broadcast_to(gm_b_s[:, :, None], (GM_GROUPS, GM_CHUNK, LANES)),
                   w_proj_gm.astype(_BF))

    q, kc, vc, ksa, vs, kw, vw, gates = _qkv_proj(xb, wb[:, o_q:o_kv], wb[:, o_kv:o_g], wg, B, S)

    nc = S // CMP_STRIDE
    half = CMP_STRIDE * HEAD_DIM
    flat = lambda a: a.reshape(B * N_KV_HEADS, nc, half)
    kcmp, vcmp = _nsa_compress(
        flat(kc), flat(vc), cmp_pe_k.reshape(2, half), cmp_pe_v.reshape(2, half),
        cmp_w1_k.astype(_BF).reshape(2, half, HEAD_DIM), cmp_w1_v.astype(_BF).reshape(2, half, HEAD_DIM),
        cmp_w2_k.astype(_BF), cmp_w2_v.astype(_BF))

    yn = _nsa_attention(q, gates, kcmp, vcmp, ksa, vs, kw, vw, B, S)

    return _merge_ffn(x2, gm, yn, wb[:, o_m + D_MODEL:o_m + 2 * D_MODEL], w_proj_nsa.astype(_BF),
                      w_out.astype(_BF), row(ln1_g), row(ln1_b), w_ff1.astype(_BF), w_ff2.astype(_BF),
                      row(ln2_g), row(ln2_b))


def kernel(x, w_in, gm_ln_g, gm_ln_b, gm_w_s, gm_b_s, cmp_pe_k, cmp_w1_k, cmp_w2_k, cmp_pe_v, cmp_w1_v, cmp_w2_v, w_proj_gm, w_proj_nsa, w_out, ln1_g, ln1_b, w_ff1, w_ff2, ln2_g, ln2_b):
    B, S, D = x.shape
    assert D == D_MODEL and S % SEL_TILE == 0 and SEL_TILE <= S <= SEL_BLOCK * LANES and S >= WINDOW + Q_BLOCK
    h = x.reshape(B * S, D)
    for l in range(w_in.shape[0]):
        h = _layer(h, B, S, w_in[l], gm_ln_g[l], gm_ln_b[l], gm_w_s[l], gm_b_s[l],
                   cmp_pe_k[l], cmp_w1_k[l], cmp_w2_k[l], cmp_pe_v[l], cmp_w1_v[l], cmp_w2_v[l],
                   w_proj_gm[l], w_proj_nsa[l], w_out[l], ln1_g[l], ln1_b[l],
                   w_ff1[l], w_ff2[l], ln2_g[l], ln2_b[l])
    return h.reshape(B, S, D)
```

```python
import functools
import math

import jax
import jax.numpy as jnp
from jax import lax
from jax.experimental import pallas as pl
from jax.experimental.pallas import tpu as pltpu

D_MODEL = 1024
GM_GROUPS = 8
GM_CHUNK = 128
N_KV_HEADS = 2
Q_PER_KV = 4
HEAD_DIM = 128
CMP_BLOCK = 32
CMP_STRIDE = 16
SEL_BLOCK = 64
N_SELECT = 16
N_LOCAL_FORCED = 2
WINDOW = 512
Q_BLOCK = 128
D_FF = 4 * D_MODEL
DEEPNORM_ALPHA = 2.0 ** 0.25
LN_EPS = 1e-5
NEG_INF = -1e30
FORCED_SCORE = 1e9

LANES = 128
SEL_TILE = 256
ROW_TILE = 512
VMEM_LIMIT = 56 * 1024 * 1024

_BF = jnp.bfloat16
_F32 = jnp.float32
_NT = (((1,), (1,)), ((), ()))


def _dot(a, b):
    return jnp.dot(a, b, preferred_element_type=_F32)


def _dot_nt(a, b):
    return lax.dot_general(a, b, _NT, preferred_element_type=_F32)


def _gelu(x):
    c = math.sqrt(2.0 / math.pi)
    return 0.5 * x * (1.0 + jnp.tanh(c * (x + 0.044715 * (x * x * x))))


def _sigmoid(x):
    return 1.0 / (1.0 + jnp.exp(-x))


def _layer_norm(x, g, b):
    mu = jnp.mean(x, axis=-1, keepdims=True)
    xc = x - mu
    var = jnp.mean(xc * xc, axis=-1, keepdims=True)
    return xc * lax.rsqrt(var + LN_EPS) * g + b


def _gm_kernel(x_ref, wgm_ref, wm0_ref, lng_ref, lnb_ref, ws_ref, bs_ref, wpg_ref, o_ref, vg_ref):
    tm = x_ref.shape[0]
    xb = x_ref[...]
    z = _gelu(_dot(xb, wgm_ref[...]))
    u = z[:, :D_MODEL]
    v = _layer_norm(z[:, D_MODEL:], lng_ref[...], lnb_ref[...]).astype(_BF)
    row = lax.broadcasted_iota(jnp.int32, (GM_CHUNK, GM_CHUNK), 0)
    col = lax.broadcasted_iota(jnp.int32, (GM_CHUNK, GM_CHUNK), 1)
    for gi in range(GM_GROUPS):
        w = jnp.where(row >= col, ws_ref[gi], 0.0).astype(_BF)
        for c in range(tm // GM_CHUNK):
            blk = v[c * GM_CHUNK:(c + 1) * GM_CHUNK, gi * LANES:(gi + 1) * LANES]
            vg_ref[c * GM_CHUNK:(c + 1) * GM_CHUNK, gi * LANES:(gi + 1) * LANES] = _dot(w, blk) + bs_ref[gi]
    y = (u * vg_ref[...]).astype(_BF)
    gate = _sigmoid(_dot(xb, wm0_ref[...]))
    o_ref[...] = gate * _dot(y, wpg_ref[...])


def _gm_mixer(xb, wgm, wm0, lng, lnb, ws, bs, wpg):
    T = xb.shape[0]
    tm = ROW_TILE
    const2 = lambda i: (0, 0)
    const3 = lambda i: (0, 0, 0)
    return pl.pallas_call(
        _gm_kernel,
        out_shape=jax.ShapeDtypeStruct((T, D_MODEL), _F32),
        grid=(T // tm,),
        in_specs=[
            pl.BlockSpec((tm, D_MODEL), lambda i: (i, 0)),
            pl.BlockSpec((D_MODEL, 2 * D_MODEL), const2),
            pl.BlockSpec((D_MODEL, D_MODEL), const2),
            pl.BlockSpec((1, D_MODEL), const2),
            pl.BlockSpec((1, D_MODEL), const2),
            pl.BlockSpec((GM_GROUPS, GM_CHUNK, GM_CHUNK), const3),
            pl.BlockSpec((GM_GROUPS, GM_CHUNK, LANES), const3),
            pl.BlockSpec((D_MODEL, D_MODEL), const2),
        ],
        out_specs=pl.BlockSpec((tm, D_MODEL), lambda i: (i, 0)),
        scratch_shapes=[pltpu.VMEM((tm, D_MODEL), _F32)],
        compiler_params=pltpu.CompilerParams(
            dimension_semantics=("arbitrary",), vmem_limit_bytes=VMEM_LIMIT),
        name="gm_mixer",
    )(xb, wgm, wm0, lng, lnb, ws, bs, wpg)


def _qkv_kernel(x_ref, wq_ref, wkv_ref, wg_ref, q_ref, kc_ref, vc_ref, ksa_ref, vs_ref, kw_ref, vw_ref, g_ref):
    tm = x_ref.shape[0]
    xb = x_ref[...]
    q_ref[...] = (_dot(xb, wq_ref[...]) * (HEAD_DIM ** -0.5)).astype(_BF)
    z = _dot(xb, wkv_ref[...])
    kpos = pl.program_id(1) * tm + lax.broadcasted_iota(jnp.int32, (tm, LANES), 0)
    blk = lax.broadcasted_iota(jnp.int32, (tm, LANES), 1)
    ind = jnp.where(blk == 0, (kpos & (SEL_TILE - 1)).astype(_F32),
                    jnp.where((kpos >> 6) == blk, NEG_INF, 0.0)).astype(_BF)
    for h in range(N_KV_HEADS):
        def col(j):
            return z[:, j * 2 * HEAD_DIM + h * HEAD_DIM: j * 2 * HEAD_DIM + (h + 1) * HEAD_DIM]
        kc_ref[0, h] = col(0)
        vc_ref[0, h] = col(1)
        ksa_ref[0, h, :, :HEAD_DIM] = col(2).astype(_BF)
        ksa_ref[0, h, :, HEAD_DIM:] = ind
        vs_ref[0, h] = col(3).astype(_BF)
        kw_ref[0, h] = col(4).astype(_BF)
        vw_ref[0, h] = col(5).astype(_BF)
    zg = _sigmoid(_dot(xb, wg_ref[...]))
    for h in range(N_KV_HEADS):
        g_ref[h] = zg[:, h * LANES:(h + 1) * LANES]


def _qkv_proj(xb, wq, wkv, wg, B, S):
    T = xb.shape[0]
    tm = ROW_TILE
    nsb = S // tm
    const2 = lambda b, s: (0, 0)
    kv_spec = lambda w: pl.BlockSpec((1, N_KV_HEADS, tm, w), lambda b, s: (b, 0, s, 0))
    kv_shape = lambda w, dt: jax.ShapeDtypeStruct((B, N_KV_HEADS, S, w), dt)
    return pl.pallas_call(
        _qkv_kernel,
        out_shape=(
            jax.ShapeDtypeStruct((T, D_MODEL), _BF),
            kv_shape(HEAD_DIM, _F32), kv_shape(HEAD_DIM, _F32),
            kv_shape(2 * HEAD_DIM, _BF), kv_shape(HEAD_DIM, _BF),
            kv_shape(HEAD_DIM, _BF), kv_shape(HEAD_DIM, _BF),
            jax.ShapeDtypeStruct((N_KV_HEADS, T, LANES), _F32),
        ),
        grid=(B, nsb),
        in_specs=[
            pl.BlockSpec((tm, D_MODEL), lambda b, s: (b * nsb + s, 0)),
            pl.BlockSpec((D_MODEL, D_MODEL), const2),
            pl.BlockSpec((D_MODEL, 6 * N_KV_HEADS * HEAD_DIM), const2),
            pl.BlockSpec((D_MODEL, N_KV_HEADS * LANES), const2),
        ],
        out_specs=(
            pl.BlockSpec((tm, D_MODEL), lambda b, s: (b * nsb + s, 0)),
            kv_spec(HEAD_DIM), kv_spec(HEAD_DIM), kv_spec(2 * HEAD_DIM), kv_spec(HEAD_DIM),
            kv_spec(HEAD_DIM), kv_spec(HEAD_DIM),
            pl.BlockSpec((N_KV_HEADS, tm, LANES), lambda b, s: (0, b * nsb + s, 0)),
        ),
        compiler_params=pltpu.CompilerParams(
            dimension_semantics=("arbitrary", "arbitrary"), vmem_limit_bytes=VMEM_LIMIT),
        name="qkv_proj",
    )(xb, wq, wkv, wg)


def _compress_kernel(kc_ref, vc_ref, pek_ref, pev_ref, w1k_ref, w1v_ref, w2k_ref, w2v_ref, ko_ref, vo_ref):
    nc = kc_ref.shape[1]
    for src, pe, w1, w2, dst in ((kc_ref, pek_ref, w1k_ref, w2k_ref, ko_ref),
                                 (vc_ref, pev_ref, w1v_ref, w2v_ref, vo_ref)):
        ch = src[0]
        first = _dot((ch + pe[0:1, :]).astype(_BF), w1[0])
        second = _dot((ch + pe[1:2, :]).astype(_BF), w1[1])
        pre = first + pltpu.roll(second, nc - 1, 0)
        dst[0] = _dot(_gelu(pre).astype(_BF), w2[...]).astype(_BF)


def _nsa_compress(kc, vc, pek, pev, w1k, w1v, w2k, w2v):
    BH, nc, width = kc.shape
    half = CMP_STRIDE * HEAD_DIM
    const2 = lambda i: (0, 0)
    const3 = lambda i: (0, 0, 0)
    row = lambda w: pl.BlockSpec((1, nc, w), lambda i: (i, 0, 0))
    return pl.pallas_call(
        _compress_kernel,
        out_shape=(jax.ShapeDtypeStruct((BH, nc, HEAD_DIM), _BF),) * 2,
        grid=(BH,),
        in_specs=[row(width), row(width),
                  pl.BlockSpec((2, half), const2), pl.BlockSpec((2, half), const2),
                  pl.BlockSpec((2, half, HEAD_DIM), const3), pl.BlockSpec((2, half, HEAD_DIM), const3),
                  pl.BlockSpec((HEAD_DIM, HEAD_DIM), const2), pl.BlockSpec((HEAD_DIM, HEAD_DIM), const2)],
        out_specs=(row(HEAD_DIM), row(HEAD_DIM)),
        compiler_params=pltpu.CompilerParams(
            dimension_semantics=("arbitrary",), vmem_limit_bytes=VMEM_LIMIT),
        name="nsa_compress",
    )(kc, vc, pek, pev, w1k, w1v, w2k, w2v)


def _mark_top_blocks(score, notsel, rounds):
    rows = lax.broadcasted_iota(jnp.int32, score.shape, 0).astype(_F32)
    for _ in range(rounds):
        mx = jnp.max(score, axis=0, keepdims=True)
        idx = jnp.min(jnp.where(score == mx, rows, float(LANES)), axis=0, keepdims=True)
        hit = rows == idx
        notsel = jnp.where(hit, 0.0, notsel)
        score = jnp.where(hit, -jnp.inf, score)
    return notsel


def _attn_kernel(q_ref, g_ref, kc_ref, vc_ref, ksa_ref, vs_ref, kw_ref, vw_ref, o_ref,
                 m_ref, l_ref, acc_ref, *, seq):
    h = pl.program_id(1)
    qb = pl.program_id(2)
    q0 = qb * Q_BLOCK
    nc = kc_ref.shape[1]
    n_sel = seq // SEL_BLOCK
    n_forced = 1 + N_LOCAL_FORCED
    k_top = min(N_SELECT, n_sel)
    G = Q_PER_KV
    slopes = [jnp.where(h == 0, _F32(2.0 ** -(g + 1)), _F32(2.0 ** -(G + g + 1))) for g in range(G)]

    qblk = q_ref[...]
    qs = jnp.concatenate([qblk[:, g * HEAD_DIM:(g + 1) * HEAD_DIM] for g in range(G)], axis=0)

    tq = lax.broadcasted_iota(jnp.int32, (Q_BLOCK, nc), 0) + q0
    ci = lax.broadcasted_iota(jnp.int32, (Q_BLOCK, nc), 1)
    valid_c = ci * CMP_STRIDE + (CMP_BLOCK - 1) <= tq
    dist_c = tq.astype(_F32) - (ci.astype(_F32) * CMP_STRIDE + 0.5 * (CMP_BLOCK - 1))
    s_all = _dot_nt(qs, kc_ref[0])
    vcmp = vc_ref[0]
    psum = jnp.zeros((Q_BLOCK, nc), _F32)
    o_cmp = []
    for g in range(G):
        s = s_all[g * Q_BLOCK:(g + 1) * Q_BLOCK] - slopes[g] * dist_c
        s = jnp.where(valid_c, s, NEG_INF)
        e = jnp.exp(s - jnp.max(s, axis=-1, keepdims=True))
        p = jnp.where(valid_c, e / jnp.sum(e, axis=-1, keepdims=True), 0.0)
        psum = psum + p
        o_cmp.append(_dot(p.astype(_BF), vcmp))

    mi = lax.broadcasted_iota(jnp.int32, (LANES, nc), 0)
    ni = lax.broadcasted_iota(jnp.int32, (LANES, nc), 1)
    overlap_t = jnp.where((ni * CMP_STRIDE + (CMP_BLOCK - 1) >= mi * SEL_BLOCK)
                          & (ni * CMP_STRIDE <= mi * SEL_BLOCK + (SEL_BLOCK - 1)), 1.0, 0.0).astype(_BF)
    p_hi = psum.astype(_BF)
    p_lo = (psum - p_hi.astype(_F32)).astype(_BF)
    imp_t = _dot_nt(overlap_t, p_hi) + _dot_nt(overlap_t, p_lo)
    mrow = lax.broadcasted_iota(jnp.int32, (LANES, Q_BLOCK), 0)
    tcol = lax.broadcasted_iota(jnp.int32, (LANES, Q_BLOCK), 1) + q0
    lag = (tcol >> 6) - mrow
    forced = (mrow == 0) | ((lag >= 0) & (lag < N_LOCAL_FORCED))
    score = jnp.where(forced | (lag < 0) | (mrow >= n_sel), -jnp.inf, imp_t)
    notsel_t = _mark_top_blocks(score, jnp.where(forced, 0.0, 1.0), k_top - n_forced)
    notsel = notsel_t.T

    blocks_per_tile = SEL_TILE // SEL_BLOCK
    active = jnp.where(jnp.min(notsel, axis=0, keepdims=True) < 0.5, 1.0, 0.0)
    bi = lax.broadcasted_iota(jnp.int32, (LANES, LANES), 0)
    ti = lax.broadcasted_iota(jnp.int32, (LANES, LANES), 1)
    group = jnp.where(bi // blocks_per_tile == ti, 1.0, 0.0).astype(_BF)
    cnt = _dot(jnp.broadcast_to(active, (8, LANES)).astype(_BF), group)[0:1]
    lane = lax.broadcasted_iota(jnp.int32, (1, LANES), 1)
    weight = jnp.left_shift(1, lane & 15).astype(_F32)
    on = cnt > 0.5
    bits_lo = jnp.sum(jnp.where(on & (lane < 16), weight, 0.0)).astype(jnp.int32)
    bits_hi = jnp.sum(jnp.where(on & (lane >= 16) & (lane < 32), weight, 0.0)).astype(jnp.int32)
    tile_bits = bits_lo | (bits_hi << 16)

    lane_q = lax.broadcasted_iota(jnp.int32, (Q_BLOCK, LANES), 1)
    aug = jnp.concatenate([jnp.where(lane_q == 0, slopes[g], notsel) for g in range(G)], axis=0).astype(_BF)
    lhs = jnp.concatenate([qs, aug], axis=1)

    m_ref[...] = jnp.full(m_ref.shape, NEG_INF, _F32)
    l_ref[...] = jnp.zeros(l_ref.shape, _F32)
    acc_ref[...] = jnp.zeros(acc_ref.shape, _F32)
    half = SEL_TILE // 2

    def sel_tile(j, causal):
        k0 = pl.multiple_of(j * SEL_TILE, SEL_TILE)
        s_aug = _dot_nt(lhs, ksa_ref[0, 0, pl.ds(k0, SEL_TILE), :])
        vt = vs_ref[0, 0, pl.ds(k0, SEL_TILE), :]
        if causal:
            ahead = (lax.broadcasted_iota(jnp.int32, (Q_BLOCK, SEL_TILE), 1)
                     - lax.broadcasted_iota(jnp.int32, (Q_BLOCK, SEL_TILE), 0)) > (q0 - k0)
        for g in range(G):
            rs = slice(g * Q_BLOCK, (g + 1) * Q_BLOCK)
            s = s_aug[rs]
            if causal:
                s = jnp.where(ahead, NEG_INF, s)
            shift = slopes[g] * (q0 - k0).astype(_F32)
            m_old = m_ref[rs]
            m_new = jnp.maximum(m_old, jnp.max(s, axis=-1, keepdims=True) - shift)
            alpha = jnp.exp(m_old - m_new)
            sub = m_new + shift
            p0 = jnp.exp(s[:, :half] - sub)
            p1 = jnp.exp(s[:, half:] - sub)
            l_ref[rs] = alpha * l_ref[rs] + (p0 + p1)
            m_ref[rs] = m_new
            pb = jnp.concatenate([p0, p1], axis=1).astype(_BF)
            acc_ref[rs] = alpha * acc_ref[rs] + _dot(pb, vt)

    n_below = q0 // SEL_TILE

    def loop_body(j, carry):
        @pl.when(((tile_bits >> j) & 1) == 1)
        def _():
            sel_tile(j, False)
        return carry

    lax.fori_loop(0, n_below, loop_body, 0)
    sel_tile(n_below, True)

    wlen = WINDOW + Q_BLOCK
    w0 = pl.multiple_of(jnp.maximum(q0 - WINDOW, 0), Q_BLOCK)
    kwt = kw_ref[0, 0, pl.ds(w0, wlen), :]
    vwt = vw_ref[0, 0, pl.ds(w0, wlen), :]
    s_win = _dot_nt(qs, kwt)
    dw_i = (lax.broadcasted_iota(jnp.int32, (Q_BLOCK, wlen), 0)
            - lax.broadcasted_iota(jnp.int32, (Q_BLOCK, wlen), 1)) + (q0 - w0)
    valid_w = (dw_i >= 0) & (dw_i < WINDOW)
    dw = dw_i.astype(_F32)

    gates = g_ref[0]
    for g in range(G):
        rs = slice(g * Q_BLOCK, (g + 1) * Q_BLOCK)
        s = jnp.where(valid_w, s_win[rs] - slopes[g] * dw, NEG_INF)
        e = jnp.exp(s - jnp.max(s, axis=-1, keepdims=True))
        o_win = _dot(e.astype(_BF), vwt) / jnp.sum(e, axis=-1, keepdims=True)
        o_sel = acc_ref[rs] / jnp.sum(l_ref[rs], axis=-1, keepdims=True)
        out = (gates[:, 3 * g:3 * g + 1] * o_cmp[g] + gates[:, 3 * g + 1:3 * g + 2] * o_sel
               + gates[:, 3 * g + 2:3 * g + 3] * o_win)
        o_ref[:, g * HEAD_DIM:(g + 1) * HEAD_DIM] = out.astype(_BF)


def _nsa_attention(q, gates, kcmp, vcmp, ksa, vs, kw, vw, B, S):
    T = q.shape[0]
    nqb = S // Q_BLOCK
    nc = kcmp.shape[1]
    gd = Q_PER_KV * HEAD_DIM
    full = lambda w: pl.BlockSpec((1, 1, S, w), lambda b, h, i: (b, h, 0, 0))
    cmp_spec = pl.BlockSpec((1, nc, HEAD_DIM), lambda b, h, i: (b * N_KV_HEADS + h, 0, 0))
    return pl.pallas_call(
        functools.partial(_attn_kernel, seq=S),
        out_shape=jax.ShapeDtypeStruct((T, N_KV_HEADS * gd), _BF),
        grid=(B, N_KV_HEADS, nqb),
        in_specs=[
            pl.BlockSpec((Q_BLOCK, gd), lambda b, h, i: (b * nqb + i, h)),
            pl.BlockSpec((1, Q_BLOCK, LANES), lambda b, h, i: (h, b * nqb + i, 0)),
            cmp_spec, cmp_spec,
            full(2 * HEAD_DIM), full(HEAD_DIM), full(HEAD_DIM), full(HEAD_DIM),
        ],
        out_specs=pl.BlockSpec((Q_BLOCK, gd), lambda b, h, i: (b * nqb + i, h)),
        scratch_shapes=[pltpu.VMEM((Q_PER_KV * Q_BLOCK, LANES), _F32),
                        pltpu.VMEM((Q_PER_KV * Q_BLOCK, LANES), _F32),
                        pltpu.VMEM((Q_PER_KV * Q_BLOCK, HEAD_DIM), _F32)],
        compiler_params=pltpu.CompilerParams(
            dimension_semantics=("arbitrary", "arbitrary", "arbitrary"), vmem_limit_bytes=VMEM_LIMIT),
        name="nsa_attention",
    )(q, gates, kcmp, vcmp, ksa, vs, kw, vw)


def _merge_ffn_kernel(x_ref, gm_ref, yn_ref, wm1_ref, wpn_ref, wo_ref, g1_ref, b1_ref,
                      w1_ref, w2_ref, g2_ref, b2_ref, o_ref):
    x = x_ref[...]
    gate = _sigmoid(_dot(x.astype(_BF), wm1_ref[...]))
    merged = gm_ref[...] + gate * _dot(yn_ref[...], wpn_ref[...])
    mix = _dot(merged.astype(_BF), wo_ref[...])
    hid = _layer_norm(DEEPNORM_ALPHA * x + mix, g1_ref[...], b1_ref[...])
    hb = hid.astype(_BF)
    f = jnp.zeros(hid.shape, _F32)
    for c in range(D_FF // D_MODEL):
        a = jnp.maximum(_dot(hb, w1_ref[:, c * D_MODEL:(c + 1) * D_MODEL]), 0.0)
        f = f + _dot((a * a).astype(_BF), w2_ref[c * D_MODEL:(c + 1) * D_MODEL, :])
    o_ref[...] = _layer_norm(DEEPNORM_ALPHA * hid + f, g2_ref[...], b2_ref[...])


def _merge_ffn(x2, gm, yn, wm1, wpn, wo, g1, b1, w1, w2, g2, b2):
    T = x2.shape[0]
    tm = ROW_TILE
    rows = lambda w: pl.BlockSpec((tm, w), lambda i: (i, 0))
    const = lambda r, c: pl.BlockSpec((r, c), lambda i: (0, 0), pipeline_mode=pl.Buffered(1))
    return pl.pallas_call(
        _merge_ffn_kernel,
        out_shape=jax.ShapeDtypeStruct((T, D_MODEL), _F32),
        grid=(T // tm,),
        in_specs=[rows(D_MODEL), rows(D_MODEL), rows(D_MODEL),
                  const(D_MODEL, D_MODEL), const(D_MODEL, D_MODEL), const(D_MODEL, D_MODEL),
                  const(1, D_MODEL), const(1, D_MODEL),
                  const(D_MODEL, D_FF), const(D_FF, D_MODEL),
                  const(1, D_MODEL), const(1, D_MODEL)],
        out_specs=rows(D_MODEL),
        compiler_params=pltpu.CompilerParams(
            dimension_semantics=("arbitrary",), vmem_limit_bytes=VMEM_LIMIT),
        name="merge_ffn",
    )(x2, gm, yn, wm1, wpn, wo, g1, b1, w1, w2, g2, b2)


def _layer(x2, B, S, w_in, gm_ln_g, gm_ln_b, gm_w_s, gm_b_s, cmp_pe_k, cmp_w1_k, cmp_w2_k,
           cmp_pe_v, cmp_w1_v, cmp_w2_v, w_proj_gm, w_proj_nsa, w_out,
           ln1_g, ln1_b, w_ff1, w_ff2, ln2_g, ln2_b):
    gmw = 2 * D_MODEL
    aw = Q_PER_KV * N_KV_HEADS * HEAD_DIM
    kvw = N_KV_HEADS * HEAD_DIM
    o_q = gmw
    o_kv = o_q + aw
    o_g = o_kv + 6 * kvw
    n_gate = 3 * Q_PER_KV * N_KV_HEADS
    o_m = o_g + n_gate
    wb = w_in.astype(_BF)
    xb = x2.astype(_BF)
    row = lambda v: v.reshape(1, -1)

    wg = jnp.zeros((D_MODEL, N_KV_HEADS, LANES), _BF)
    wg = wg.at[:, :, :n_gate // N_KV_HEADS].set(wb[:, o_g:o_m].reshape(D_MODEL, N_KV_HEADS, -1))
    wg = wg.reshape(D_MODEL, N_KV_HEADS * LANES)

    gm = _gm_mixer(xb, wb[:, :gmw], wb[:, o_m:o_m + D_MODEL], row(gm_ln_g), row(gm_ln_b), gm_w_s,
                   jnp.broadcast_to(gm_b_s[:, :, None], (GM_GROUPS, GM_CHUNK, LANES)),
                   w_proj_gm.astype(_BF))

    q, kc, vc, ksa, vs, kw, vw, gates = _qkv_proj(xb, wb[:, o_q:o_kv], wb[:, o_kv:o_g], wg, B, S)

    nc = S // CMP_STRIDE
    half = CMP_STRIDE * HEAD_DIM
    flat = lambda a: a.reshape(B * N_KV_HEADS, nc, half)
    kcmp, vcmp = _nsa_compress(
        flat(kc), flat(vc), cmp_pe_k.reshape(2, half), cmp_pe_v.reshape(2, half),
        cmp_w1_k.astype(_BF).reshape(2, half, HEAD_DIM), cmp_w1_v.astype(_BF).reshape(2, half, HEAD_DIM),
        cmp_w2_k.astype(_BF), cmp_w2_v.astype(_BF))

    yn = _nsa_attention(q, gates, kcmp, vcmp, ksa, vs, kw, vw, B, S)

    return _merge_ffn(x2, gm, yn, wb[:, o_m + D_MODEL:o_m + 2 * D_MODEL], w_proj_nsa.astype(_BF),
                      w_out.astype(_BF), row(ln1_g), row(ln1_b), w_ff1.astype(_BF), w_ff2.astype(_BF),
                      row(ln2_g), row(ln2_b))


def kernel(x, w_in, gm_ln_g, gm_ln_b, gm_w_s, gm_b_s, cmp_pe_k, cmp_w1_k, cmp_w2_k, cmp_pe_v, cmp_w1_v, cmp_w2_v, w_proj_gm, w_proj_nsa, w_out, ln1_g, ln1_b, w_ff1, w_ff2, ln2_g, ln2_b):
    B, S, D = x.shape
    assert D == D_MODEL and S % SEL_TILE == 0 and SEL_TILE <= S <= SEL_BLOCK * LANES and S >= WINDOW + Q_BLOCK
    h = x.reshape(B * S, D)
    for l in range(w_in.shape[0]):
        h = _layer(h, B, S, w_in[l], gm_ln_g[l], gm_ln_b[l], gm_w_s[l], gm_b_s[l],
                   cmp_pe_k[l], cmp_w1_k[l], cmp_w2_k[l], cmp_pe_v[l], cmp_w1_v[l], cmp_w2_v[l],
                   w_proj_gm[l], w_proj_nsa[l], w_out[l], ln1_g[l], ln1_b[l],
                   w_ff1[l], w_ff2[l], ln2_g[l], ln2_b[l])
    return h.reshape(B, S, D)
```

```python
import functools
import math

import jax
import jax.numpy as jnp
from jax import lax
from jax.experimental import pallas as pl
from jax.experimental.pallas import tpu as pltpu

D_MODEL = 1024
GM_GROUPS = 8
GM_CHUNK = 128
N_KV_HEADS = 2
Q_PER_KV = 4
HEAD_DIM = 128
CMP_BLOCK = 32
CMP_STRIDE = 16
SEL_BLOCK = 64
N_SELECT = 16
N_LOCAL_FORCED = 2
WINDOW = 512
Q_BLOCK = 128
D_FF = 4 * D_MODEL
DEEPNORM_ALPHA = 2.0 ** 0.25
LN_EPS = 1e-5
NEG_INF = -1e30

LANES = 128
SEL_TILE = 256
LOCAL_TILES = 6
POS_RADIX = 256
ROW_TILE = 512
GATE_ROWS = 16
VMEM_LIMIT = 56 * 1024 * 1024

_BF = jnp.bfloat16
_F32 = jnp.float32
_NT = (((1,), (1,)), ((), ()))


def _dot(a, b):
    return jnp.dot(a, b, preferred_element_type=_F32)


def _dot_nt(a, b):
    return lax.dot_general(a, b, _NT, preferred_element_type=_F32)


def _gelu(x):
    c = math.sqrt(2.0 / math.pi)
    return 0.5 * x * (1.0 + jnp.tanh(c * (x + 0.044715 * (x * x * x))))


def _sigmoid(x):
    return 1.0 / (1.0 + jnp.exp(-x))


def _layer_norm(x, g, b):
    mu = jnp.mean(x, axis=-1, keepdims=True)
    xc = x - mu
    var = jnp.mean(xc * xc, axis=-1, keepdims=True)
    return xc * lax.rsqrt(var + LN_EPS) * g + b


def _position_lanes(pos, shape):
    lane = lax.broadcasted_iota(jnp.int32, shape, 1)
    return jnp.where(lane == 0, (pos % POS_RADIX).astype(_F32),
                     jnp.where(lane == 1, (pos // POS_RADIX).astype(_F32), 0.0)).astype(_BF)


def _gm_kernel(x_ref, wgm_ref, wm0_ref, lng_ref, lnb_ref, ws_ref, bs_ref, wpg_ref, o_ref, vg_ref):
    tm = x_ref.shape[0]
    xb = x_ref[...]
    z = _gelu(_dot(xb, wgm_ref[...]))
    u = z[:, :D_MODEL]
    v = _layer_norm(z[:, D_MODEL:], lng_ref[...], lnb_ref[...]).astype(_BF)
    row = lax.broadcasted_iota(jnp.int32, (GM_CHUNK, GM_CHUNK), 0)
    col = lax.broadcasted_iota(jnp.int32, (GM_CHUNK, GM_CHUNK), 1)
    for gi in range(GM_GROUPS):
        w = jnp.where(row >= col, ws_ref[gi], 0.0).astype(_BF)
        for c in range(tm // GM_CHUNK):
            blk = v[c * GM_CHUNK:(c + 1) * GM_CHUNK, gi * LANES:(gi + 1) * LANES]
            vg_ref[c * GM_CHUNK:(c + 1) * GM_CHUNK, gi * LANES:(gi + 1) * LANES] = _dot(w, blk) + bs_ref[gi]
    y = (u * vg_ref[...]).astype(_BF)
    gate = _sigmoid(_dot(xb, wm0_ref[...]))
    o_ref[...] = gate * _dot(y, wpg_ref[...])


def _gm_mixer(xb, wgm, wm0, lng, lnb, ws, bs, wpg):
    T = xb.shape[0]
    tm = ROW_TILE
    const2 = lambda i: (0, 0)
    const3 = lambda i: (0, 0, 0)
    return pl.pallas_call(
        _gm_kernel,
        out_shape=jax.ShapeDtypeStruct((T, D_MODEL), _F32),
        grid=(T // tm,),
        in_specs=[
            pl.BlockSpec((tm, D_MODEL), lambda i: (i, 0)),
            pl.BlockSpec((D_MODEL, 2 * D_MODEL), const2),
            pl.BlockSpec((D_MODEL, D_MODEL), const2),
            pl.BlockSpec((1, D_MODEL), const2),
            pl.BlockSpec((1, D_MODEL), const2),
            pl.BlockSpec((GM_GROUPS, GM_CHUNK, GM_CHUNK), const3),
            pl.BlockSpec((GM_GROUPS, GM_CHUNK, LANES), const3),
            pl.BlockSpec((D_MODEL, D_MODEL), const2),
        ],
        out_specs=pl.BlockSpec((tm, D_MODEL), lambda i: (i, 0)),
        scratch_shapes=[pltpu.VMEM((tm, D_MODEL), _F32)],
        compiler_params=pltpu.CompilerParams(
            dimension_semantics=("arbitrary",), vmem_limit_bytes=VMEM_LIMIT),
        name="gm_mixer",
    )(xb, wgm, wm0, lng, lnb, ws, bs, wpg)


def _qkv_kernel(x_ref, wq_ref, wkv_ref, wg_ref, qt_ref, kc_ref, vc_ref, ksa_ref, vst_ref, kwa_ref, vwt_ref, gt_ref):
    tm = x_ref.shape[0]
    xb = x_ref[...]
    zq = _dot(xb, wq_ref[...]) * (HEAD_DIM ** -0.5)
    for tb in range(tm // Q_BLOCK):
        for h in range(N_KV_HEADS):
            for g in range(Q_PER_KV):
                c0 = (h * Q_PER_KV + g) * HEAD_DIM
                blk = zq[tb * Q_BLOCK:(tb + 1) * Q_BLOCK, c0:c0 + HEAD_DIM]
                qt_ref[tb, h, :, g * Q_BLOCK:(g + 1) * Q_BLOCK] = blk.T.astype(_BF)
    z = _dot(xb, wkv_ref[...])
    kpos = pl.program_id(1) * tm + lax.broadcasted_iota(jnp.int32, (tm, LANES), 0)
    blk_lane = lax.broadcasted_iota(jnp.int32, (tm, LANES), 1)
    sel_lanes = jnp.where(blk_lane == 0, (kpos % SEL_TILE).astype(_F32),
                          jnp.where(kpos // SEL_BLOCK == blk_lane, NEG_INF, 0.0)).astype(_BF)
    win_lanes = _position_lanes(kpos, (tm, LANES))
    for h in range(N_KV_HEADS):
        def col(j):
            return z[:, j * 2 * HEAD_DIM + h * HEAD_DIM: j * 2 * HEAD_DIM + (h + 1) * HEAD_DIM]
        kc_ref[0, h] = col(0)
        vc_ref[0, h] = col(1)
        ksa_ref[0, h, :, :HEAD_DIM] = col(2).astype(_BF)
        ksa_ref[0, h, :, HEAD_DIM:] = sel_lanes
        vst_ref[0, h] = col(3).T.astype(_BF)
        kwa_ref[0, h, :, :HEAD_DIM] = col(4).astype(_BF)
        kwa_ref[0, h, :, HEAD_DIM:] = win_lanes
        vwt_ref[0, h] = col(5).T.astype(_BF)
    zg = _sigmoid(_dot(xb, wg_ref[...]))
    for h in range(N_KV_HEADS):
        gt_ref[h] = zg[:, h * LANES:(h + 1) * LANES].T[:GATE_ROWS]


def _qkv_proj(xb, wq, wkv, wg, B, S):
    T = xb.shape[0]
    tm = ROW_TILE
    nsb = S // tm
    gd = Q_PER_KV * Q_BLOCK
    const2 = lambda b, s: (0, 0)
    rows_spec = lambda w: pl.BlockSpec((1, N_KV_HEADS, tm, w), lambda b, s: (b, 0, s, 0))
    rows_shape = lambda w, dt: jax.ShapeDtypeStruct((B, N_KV_HEADS, S, w), dt)
    cols_spec = pl.BlockSpec((1, N_KV_HEADS, HEAD_DIM, tm), lambda b, s: (b, 0, 0, s))
    cols_shape = jax.ShapeDtypeStruct((B, N_KV_HEADS, HEAD_DIM, S), _BF)
    return pl.pallas_call(
        _qkv_kernel,
        out_shape=(
            jax.ShapeDtypeStruct((T // Q_BLOCK, N_KV_HEADS, HEAD_DIM, gd), _BF),
            rows_shape(HEAD_DIM, _F32), rows_shape(HEAD_DIM, _F32),
            rows_shape(2 * HEAD_DIM, _BF), cols_shape,
            rows_shape(2 * HEAD_DIM, _BF), cols_shape,
            jax.ShapeDtypeStruct((N_KV_HEADS, GATE_ROWS, T), _F32),
        ),
        grid=(B, nsb),
        in_specs=[
            pl.BlockSpec((tm, D_MODEL), lambda b, s: (b * nsb + s, 0)),
            pl.BlockSpec((D_MODEL, D_MODEL), const2),
            pl.BlockSpec((D_MODEL, 6 * N_KV_HEADS * HEAD_DIM), const2),
            pl.BlockSpec((D_MODEL, N_KV_HEADS * LANES), const2),
        ],
        out_specs=(
            pl.BlockSpec((tm // Q_BLOCK, N_KV_HEADS, HEAD_DIM, gd), lambda b, s: (b * nsb + s, 0, 0, 0)),
            rows_spec(HEAD_DIM), rows_spec(HEAD_DIM), rows_spec(2 * HEAD_DIM), cols_spec,
            rows_spec(2 * HEAD_DIM), cols_spec,
            pl.BlockSpec((N_KV_HEADS, GATE_ROWS, tm), lambda b, s: (0, 0, b * nsb + s)),
        ),
        compiler_params=pltpu.CompilerParams(
            dimension_semantics=("arbitrary", "arbitrary"), vmem_limit_bytes=VMEM_LIMIT),
        name="qkv_proj",
    )(xb, wq, wkv, wg)


def _compress_kernel(kc_ref, vc_ref, pek_ref, pev_ref, w1k_ref, w1v_ref, w2k_ref, w2v_ref, ko_ref, vo_ref):
    nc = kc_ref.shape[1]

    def tokens(src, pe, w1, w2):
        ch = src[0]
        first = _dot((ch + pe[0:1, :]).astype(_BF), w1[0])
        second = _dot((ch + pe[1:2, :]).astype(_BF), w1[1])
        pre = first + pltpu.roll(second, nc - 1, 0)
        return _dot(_gelu(pre).astype(_BF), w2[...])

    ko_ref[0, :, :HEAD_DIM] = tokens(kc_ref, pek_ref, w1k_ref, w2k_ref).astype(_BF)
    start = lax.broadcasted_iota(jnp.int32, (nc, LANES), 0) * CMP_STRIDE
    ko_ref[0, :, HEAD_DIM:] = _position_lanes(start, (nc, LANES))
    vo_ref[0] = tokens(vc_ref, pev_ref, w1v_ref, w2v_ref).T.astype(_BF)


def _nsa_compress(kc, vc, pek, pev, w1k, w1v, w2k, w2v):
    BH, nc, width = kc.shape
    half = CMP_STRIDE * HEAD_DIM
    const2 = lambda i: (0, 0)
    const3 = lambda i: (0, 0, 0)
    row = lambda w: pl.BlockSpec((1, nc, w), lambda i: (i, 0, 0))
    return pl.pallas_call(
        _compress_kernel,
        out_shape=(jax.ShapeDtypeStruct((BH, nc, 2 * HEAD_DIM), _BF),
                   jax.ShapeDtypeStruct((BH, HEAD_DIM, nc), _BF)),
        grid=(BH,),
        in_specs=[row(width), row(width),
                  pl.BlockSpec((2, half), const2), pl.BlockSpec((2, half), const2),
                  pl.BlockSpec((2, half, HEAD_DIM), const3), pl.BlockSpec((2, half, HEAD_DIM), const3),
                  pl.BlockSpec((HEAD_DIM, HEAD_DIM), const2), pl.BlockSpec((HEAD_DIM, HEAD_DIM), const2)],
        out_specs=(row(2 * HEAD_DIM), pl.BlockSpec((1, HEAD_DIM, nc), lambda i: (i, 0, 0))),
        compiler_params=pltpu.CompilerParams(
            dimension_semantics=("arbitrary",), vmem_limit_bytes=VMEM_LIMIT),
        name="nsa_compress",
    )(kc, vc, pek, pev, w1k, w1v, w2k, w2v)


def _mark_top_blocks(score, notsel, rounds):
    rows = lax.broadcasted_iota(jnp.int32, score.shape, 0).astype(_F32)
    for _ in range(rounds):
        mx = jnp.max(score, axis=0, keepdims=True)
        idx = jnp.min(jnp.where(score == mx, rows, float(LANES)), axis=0, keepdims=True)
        hit = rows == idx
        notsel = jnp.where(hit, 0.0, notsel)
        score = jnp.where(hit, -jnp.inf, score)
    return notsel


def _attn_kernel(qt_ref, gt_ref, kca_ref, vct_ref, ksa_ref, vst_ref, kwa_ref, vwt_ref, o_ref,
                 m_ref, l_ref, acc_ref, *, seq):
    h = pl.program_id(1)
    qb = pl.program_id(2)
    q0 = qb * Q_BLOCK
    nc = kca_ref.shape[1]
    n_sel = seq // SEL_BLOCK
    n_forced = 1 + N_LOCAL_FORCED
    k_top = min(N_SELECT, n_sel)
    G = Q_PER_KV
    W = G * Q_BLOCK

    lane_w = lax.broadcasted_iota(jnp.int32, (1, W), 1)
    slope = jnp.zeros((1, W), _F32)
    for g in range(G):
        sg = jnp.where(h == 0, _F32(2.0 ** -(g + 1)), _F32(2.0 ** -(G + g + 1)))
        slope = jnp.where(lane_w // Q_BLOCK == g, sg, slope)
    tok = q0 + lane_w % Q_BLOCK
    strips = [slice(g * Q_BLOCK, (g + 1) * Q_BLOCK) for g in range(G)]

    qt = qt_ref[0, 0]
    row_aug = lax.broadcasted_iota(jnp.int32, (LANES, W), 0)
    pos_rows = jnp.where(row_aug == 0, slope, jnp.where(row_aug == 1, slope * POS_RADIX, 0.0)).astype(_BF)
    lhs_pos = jnp.concatenate([qt, pos_rows], axis=0)

    s = _dot(kca_ref[0], lhs_pos)
    n_row = lax.broadcasted_iota(jnp.int32, (nc, Q_BLOCK), 0)
    n_last = (tok - (CMP_BLOCK - 1)) // CMP_STRIDE
    e_parts, inv_parts = [], []
    psum = None
    for cs in strips:
        sg = jnp.where(n_row <= n_last[:, cs], s[:, cs], NEG_INF)
        eg = jnp.exp(sg - jnp.max(sg, axis=0, keepdims=True))
        inv = jnp.where(n_last[:, cs] >= 0, 1.0 / jnp.sum(eg, axis=0, keepdims=True), 0.0)
        pg = eg * inv
        psum = pg if psum is None else psum + pg
        e_parts.append(eg.astype(_BF))
        inv_parts.append(inv)
    o_cmp = _dot(vct_ref[0], jnp.concatenate(e_parts, axis=1)) * jnp.concatenate(inv_parts, axis=1)

    mi = lax.broadcasted_iota(jnp.int32, (LANES, nc), 0)
    ni = lax.broadcasted_iota(jnp.int32, (LANES, nc), 1)
    overlap_t = jnp.where((ni * CMP_STRIDE + (CMP_BLOCK - 1) >= mi * SEL_BLOCK)
                          & (ni * CMP_STRIDE <= mi * SEL_BLOCK + (SEL_BLOCK - 1)), 1.0, 0.0).astype(_BF)
    p_hi = psum.astype(_BF)
    p_lo = (psum - p_hi.astype(_F32)).astype(_BF)
    imp_t = _dot(overlap_t, p_hi) + _dot(overlap_t, p_lo)
    mrow = lax.broadcasted_iota(jnp.int32, (LANES, Q_BLOCK), 0)
    tcol = lax.broadcasted_iota(jnp.int32, (LANES, Q_BLOCK), 1) + q0
    lag = tcol // SEL_BLOCK - mrow
    forced = (mrow == 0) | ((lag >= 0) & (lag < N_LOCAL_FORCED))
    score = jnp.where(forced | (lag < 0) | (mrow >= n_sel), -jnp.inf, imp_t)
    notsel_t = _mark_top_blocks(score, jnp.where(forced, 0.0, 1.0), k_top - n_forced)

    blocks_per_tile = SEL_TILE // SEL_BLOCK
    active = jnp.where(jnp.min(notsel_t.T, axis=0, keepdims=True) < 0.5, 1.0, 0.0)
    bi = lax.broadcasted_iota(jnp.int32, (LANES, LANES), 0)
    ti = lax.broadcasted_iota(jnp.int32, (LANES, LANES), 1)
    group = jnp.where(bi // blocks_per_tile == ti, 1.0, 0.0).astype(_BF)
    cnt = _dot(jnp.broadcast_to(active, (8, LANES)).astype(_BF), group)[0:1]
    lane = lax.broadcasted_iota(jnp.int32, (1, LANES), 1)
    weight = jnp.left_shift(1, lane & 15).astype(_F32)
    on = cnt > 0.5
    bits_lo = jnp.sum(jnp.where(on & (lane < 16), weight, 0.0)).astype(jnp.int32)
    bits_hi = jnp.sum(jnp.where(on & (lane >= 16) & (lane < 32), weight, 0.0)).astype(jnp.int32)
    tile_bits = bits_lo | (bits_hi << 16)

    sel_rows = jnp.where(row_aug == 0, slope, jnp.concatenate([notsel_t] * G, axis=1)).astype(_BF)
    lhs_sel = jnp.concatenate([qt, sel_rows], axis=0)

    key_row = lax.broadcasted_iota(jnp.int32, (SEL_TILE, Q_BLOCK), 0)

    def tile_scores(k0, n_tiles, causal_last):
        st = _dot(ksa_ref[0, 0, pl.ds(k0, n_tiles * SEL_TILE), :], lhs_sel)
        out = []
        for i in range(n_tiles):
            ki = k0 + i * SEL_TILE
            sti = st[i * SEL_TILE:(i + 1) * SEL_TILE]
            if causal_last and i == n_tiles - 1:
                sti = jnp.concatenate([jnp.where(ki + key_row > tok[:, cs], NEG_INF, sti[:, cs]) for cs in strips],
                                      axis=1)
            out.append((sti, slope * (q0 - ki).astype(_F32)))
        return out

    def tile_max(tiles):
        m = None
        for sti, shift in tiles:
            cm = jnp.max(sti, axis=0, keepdims=True) - shift
            m = cm if m is None else jnp.maximum(m, cm)
        return m

    def tile_probs(tiles, m):
        parts, l = [], None
        for sti, shift in tiles:
            sub = m + shift
            pt = jnp.concatenate([jnp.exp(sti[:, cs] - sub[:, cs]) for cs in strips], axis=1)
            li = jnp.sum(pt, axis=0, keepdims=True)
            l = li if l is None else l + li
            parts.append(pt.astype(_BF))
        return jnp.concatenate(parts, axis=0), l

    def sel_online(j, causal):
        k0 = pl.multiple_of(j * SEL_TILE, SEL_TILE)
        tiles = tile_scores(k0, 1, causal)
        m_old = m_ref[0:1]
        m_new = jnp.maximum(m_old, tile_max(tiles))
        alpha = jnp.exp(m_old - m_new)
        pt, li = tile_probs(tiles, m_new)
        l_ref[0:1] = alpha * l_ref[0:1] + li
        m_ref[0:1] = m_new
        acc_ref[...] = alpha * acc_ref[...] + _dot(vst_ref[0, 0, :, pl.ds(k0, SEL_TILE)], pt)

    def flagged_tiles(lo, hi):
        def loop_body(j, carry):
            @pl.when(((tile_bits >> j) & 1) == 1)
            def _():
                sel_online(j, False)
            return carry
        lax.fori_loop(lo, hi, loop_body, 0)

    diag = q0 // SEL_TILE

    @pl.when(diag >= LOCAL_TILES)
    def _():
        k_loc = pl.multiple_of((diag - (LOCAL_TILES - 1)) * SEL_TILE, SEL_TILE)
        tiles = tile_scores(0, 1, False) + tile_scores(k_loc, LOCAL_TILES, True)
        m = tile_max(tiles)
        pt, l = tile_probs(tiles, m)
        m_ref[0:1] = m
        l_ref[0:1] = l
        acc_ref[...] = (_dot(vst_ref[0, 0, :, 0:SEL_TILE], pt[:SEL_TILE])
                        + _dot(vst_ref[0, 0, :, pl.ds(k_loc, LOCAL_TILES * SEL_TILE)], pt[SEL_TILE:]))
        flagged_tiles(1, diag - (LOCAL_TILES - 1))

    @pl.when(diag < LOCAL_TILES)
    def _():
        m_ref[...] = jnp.full(m_ref.shape, NEG_INF, _F32)
        l_ref[...] = jnp.zeros(l_ref.shape, _F32)
        acc_ref[...] = jnp.zeros(acc_ref.shape, _F32)
        flagged_tiles(0, diag)
        sel_online(diag, True)

    wlen = WINDOW + Q_BLOCK
    w0 = pl.multiple_of(jnp.maximum(q0 - WINDOW, 0), Q_BLOCK)
    sw = _dot(kwa_ref[0, 0, pl.ds(w0, wlen), :], lhs_pos)
    w_row = w0 + lax.broadcasted_iota(jnp.int32, (wlen, Q_BLOCK), 0)
    e_parts, inv_parts = [], []
    for cs in strips:
        dist = tok[:, cs] - w_row
        sg = jnp.where(dist.astype(jnp.uint32) < WINDOW, sw[:, cs], NEG_INF)
        eg = jnp.exp(sg - jnp.max(sg, axis=0, keepdims=True))
        e_parts.append(eg.astype(_BF))
        inv_parts.append(1.0 / jnp.sum(eg, axis=0, keepdims=True))
    o_win = (_dot(vwt_ref[0, 0, :, pl.ds(w0, wlen)], jnp.concatenate(e_parts, axis=1))
             * jnp.concatenate(inv_parts, axis=1))
    o_sel = acc_ref[...] * (1.0 / l_ref[0:1])

    gates = gt_ref[0]
    for g in range(G):
        cs = slice(g * Q_BLOCK, (g + 1) * Q_BLOCK)
        out_t = (gates[3 * g:3 * g + 1] * o_cmp[:, cs] + gates[3 * g + 1:3 * g + 2] * o_sel[:, cs]
                 + gates[3 * g + 2:3 * g + 3] * o_win[:, cs])
        o_ref[:, g * HEAD_DIM:(g + 1) * HEAD_DIM] = out_t.T.astype(_BF)


def _nsa_attention(qt, gt, kca, vct, ksa, vst, kwa, vwt, B, S):
    nqb = S // Q_BLOCK
    T = B * S
    nc = kca.shape[1]
    gd = Q_PER_KV * HEAD_DIM
    W = Q_PER_KV * Q_BLOCK
    rows = pl.BlockSpec((1, 1, S, 2 * HEAD_DIM), lambda b, h, i: (b, h, 0, 0))
    cols = pl.BlockSpec((1, 1, HEAD_DIM, S), lambda b, h, i: (b, h, 0, 0))
    return pl.pallas_call(
        functools.partial(_attn_kernel, seq=S),
        out_shape=jax.ShapeDtypeStruct((T, N_KV_HEADS * gd), _BF),
        grid=(B, N_KV_HEADS, nqb),
        in_specs=[
            pl.BlockSpec((1, 1, HEAD_DIM, W), lambda b, h, i: (b * nqb + i, h, 0, 0)),
            pl.BlockSpec((1, GATE_ROWS, Q_BLOCK), lambda b, h, i: (h, 0, b * nqb + i)),
            pl.BlockSpec((1, nc, 2 * HEAD_DIM), lambda b, h, i: (b * N_KV_HEADS + h, 0, 0)),
            pl.BlockSpec((1, HEAD_DIM, nc), lambda b, h, i: (b * N_KV_HEADS + h, 0, 0)),
            rows, cols, rows, cols,
        ],
        out_specs=pl.BlockSpec((Q_BLOCK, gd), lambda b, h, i: (b * nqb + i, h)),
        scratch_shapes=[pltpu.VMEM((8, W), _F32), pltpu.VMEM((8, W), _F32), pltpu.VMEM((HEAD_DIM, W), _F32)],
        compiler_params=pltpu.CompilerParams(
            dimension_semantics=("arbitrary", "arbitrary", "arbitrary"), vmem_limit_bytes=VMEM_LIMIT),
        name="nsa_attention",
    )(qt, gt, kca, vct, ksa, vst, kwa, vwt)


def _merge_ffn_kernel(x_ref, gm_ref, yn_ref, wm1_ref, wpn_ref, wo_ref, g1_ref, b1_ref,
                      w1_ref, w2_ref, g2_ref, b2_ref, o_ref):
    x = x_ref[...]
    gate = _sigmoid(_dot(x.astype(_BF), wm1_ref[...]))
    merged = gm_ref[...] + gate * _dot(yn_ref[...], wpn_ref[...])
    mix = _dot(merged.astype(_BF), wo_ref[...])
    hid = _layer_norm(DEEPNORM_ALPHA * x + mix, g1_ref[...], b1_ref[...])
    hb = hid.astype(_BF)
    f = jnp.zeros(hid.shape, _F32)
    for c in range(D_FF // D_MODEL):
        a = jnp.maximum(_dot(hb, w1_ref[:, c * D_MODEL:(c + 1) * D_MODEL]), 0.0)
        f = f + _dot((a * a).astype(_BF), w2_ref[c * D_MODEL:(c + 1) * D_MODEL, :])
    o_ref[...] = _layer_norm(DEEPNORM_ALPHA * hid + f, g2_ref[...], b2_ref[...])


def _merge_ffn(x2, gm, yn, wm1, wpn, wo, g1, b1, w1, w2, g2, b2):
    T = x2.shape[0]
    tm = ROW_TILE
    rows = lambda w: pl.BlockSpec((tm, w), lambda i: (i, 0))
    const = lambda r, c: pl.BlockSpec((r, c), lambda i: (0, 0), pipeline_mode=pl.Buffered(1))
    return pl.pallas_call(
        _merge_ffn_kernel,
        out_shape=jax.ShapeDtypeStruct((T, D_MODEL), _F32),
        grid=(T // tm,),
        in_specs=[rows(D_MODEL), rows(D_MODEL), rows(D_MODEL),
                  const(D_MODEL, D_MODEL), const(D_MODEL, D_MODEL), const(D_MODEL, D_MODEL),
                  const(1, D_MODEL), const(1, D_MODEL),
                  const(D_MODEL, D_FF), const(D_FF, D_MODEL),
                  const(1, D_MODEL), const(1, D_MODEL)],
        out_specs=rows(D_MODEL),
        compiler_params=pltpu.CompilerParams(
            dimension_semantics=("arbitrary",), vmem_limit_bytes=VMEM_LIMIT),
        name="merge_ffn",
    )(x2, gm, yn, wm1, wpn, wo, g1, b1, w1, w2, g2, b2)


def _layer(x2, B, S, w_in, gm_ln_g, gm_ln_b, gm_w_s, gm_b_s, cmp_pe_k, cmp_w1_k, cmp_w2_k,
           cmp_pe_v, cmp_w1_v, cmp_w2_v, w_proj_gm, w_proj_nsa, w_out,
           ln1_g, ln1_b, w_ff1, w_ff2, ln2_g, ln2_b):
    gmw = 2 * D_MODEL
    aw = Q_PER_KV * N_KV_HEADS * HEAD_DIM
    kvw = N_KV_HEADS * HEAD_DIM
    o_q = gmw
    o_kv = o_q + aw
    o_g = o_kv + 6 * kvw
    n_gate = 3 * Q_PER_KV * N_KV_HEADS
    o_m = o_g + n_gate
    wb = w_in.astype(_BF)
    xb = x2.astype(_BF)
    row = lambda v: v.reshape(1, -1)

    wg = jnp.zeros((D_MODEL, N_KV_HEADS, LANES), _BF)
    wg = wg.at[:, :, :n_gate // N_KV_HEADS].set(wb[:, o_g:o_m].reshape(D_MODEL, N_KV_HEADS, -1))
    wg = wg.reshape(D_MODEL, N_KV_HEADS * LANES)

    gm = _gm_mixer(xb, wb[:, :gmw], wb[:, o_m:o_m + D_MODEL], row(gm_ln_g), row(gm_ln_b), gm_w_s,
                   jnp.broadcast_to(gm_b_s[:, :, None], (GM_GROUPS, GM_CHUNK, LANES)),
                   w_proj_gm.astype(_BF))

    qt, kc, vc, ksa, vst, kwa, vwt, gt = _qkv_proj(xb, wb[:, o_q:o_kv], wb[:, o_kv:o_g], wg, B, S)

    nc = S // CMP_STRIDE
    half = CMP_STRIDE * HEAD_DIM
    flat = lambda a: a.reshape(B * N_KV_HEADS, nc, half)
    kca, vct = _nsa_compress(
        flat(kc), flat(vc), cmp_pe_k.reshape(2, half), cmp_pe_v.reshape(2, half),
        cmp_w1_k.astype(_BF).reshape(2, half, HEAD_DIM), cmp_w1_v.astype(_BF).reshape(2, half, HEAD_DIM),
        cmp_w2_k.astype(_BF), cmp_w2_v.astype(_BF))

    yn = _nsa_attention(qt, gt, kca, vct, ksa, vst, kwa, vwt, B, S)

    return _merge_ffn(x2, gm, yn, wb[:, o_m + D_MODEL:o_m + 2 * D_MODEL], w_proj_nsa.astype(_BF),
                      w_out.astype(_BF), row(ln1_g), row(ln1_b), w_ff1.astype(_BF), w_ff2.astype(_BF),
                      row(ln2_g), row(ln2_b))


def kernel(x, w_in, gm_ln_g, gm_ln_b, gm_w_s, gm_b_s, cmp_pe_k, cmp_w1_k, cmp_w2_k, cmp_pe_v, cmp_w1_v, cmp_w2_v, w_proj_gm, w_proj_nsa, w_out, ln1_g, ln1_b, w_ff1, w_ff2, ln2_g, ln2_b):
    B, S, D = x.shape
    assert D == D_MODEL and S % ROW_TILE == 0 and WINDOW + Q_BLOCK <= S <= SEL_BLOCK * LANES
    assert S // POS_RADIX <= POS_RADIX and S // SEL_TILE <= 32
    h = x.reshape(B * S, D)
    for l in range(w_in.shape[0]):
        h = _layer(h, B, S, w_in[l], gm_ln_g[l], gm_ln_b[l], gm_w_s[l], gm_b_s[l],
                   cmp_pe_k[l], cmp_w1_k[l], cmp_w2_k[l], cmp_pe_v[l], cmp_w1_v[l], cmp_w2_v[l],
                   w_proj_gm[l], w_proj_nsa[l], w_out[l], ln1_g[l], ln1_b[l],
                   w_ff1[l], w_ff2[l], ln2_g[l], ln2_b[l])
    return h.reshape(B, S, D)
```

```python
import functools
import math

import jax
import jax.numpy as jnp
from jax import lax
from jax.experimental import pallas as pl
from jax.experimental.pallas import tpu as pltpu

D_MODEL = 1024
GM_GROUPS = 8
GM_CHUNK = 128
N_KV_HEADS = 2
Q_PER_KV = 4
HEAD_DIM = 128
CMP_BLOCK = 32
CMP_STRIDE = 16
SEL_BLOCK = 64
N_SELECT = 16
N_LOCAL_FORCED = 2
WINDOW = 512
Q_BLOCK = 128
D_FF = 4 * D_MODEL
DEEPNORM_ALPHA = 2.0 ** 0.25
LN_EPS = 1e-5
NEG_INF = -1e30

LANES = 128
SEL_TILE = 256
LOCAL_TILES = 6
BITS_PER_WORD = 16
POS_RADIX = 256
ROW_TILE = 512
GATE_ROWS = 16
VMEM_LIMIT = 56 * 1024 * 1024

_BF = jnp.bfloat16
_F32 = jnp.float32
_NT = (((1,), (1,)), ((), ()))


def _dot(a, b):
    return jnp.dot(a, b, preferred_element_type=_F32)


def _dot_nt(a, b):
    return lax.dot_general(a, b, _NT, preferred_element_type=_F32)


def _gelu(x):
    c = math.sqrt(2.0 / math.pi)
    return 0.5 * x * (1.0 + jnp.tanh(c * (x + 0.044715 * (x * x * x))))


def _sigmoid(x):
    return 1.0 / (1.0 + jnp.exp(-x))


def _layer_norm(x, g, b):
    mu = jnp.mean(x, axis=-1, keepdims=True)
    xc = x - mu
    var = jnp.mean(xc * xc, axis=-1, keepdims=True)
    return xc * lax.rsqrt(var + LN_EPS) * g + b


def _position_lanes(pos, shape):
    lane = lax.broadcasted_iota(jnp.int32, shape, 1)
    return jnp.where(lane == 0, (pos % POS_RADIX).astype(_F32),
                     jnp.where(lane == 1, (pos // POS_RADIX).astype(_F32), 0.0)).astype(_BF)


def _gm_kernel(x_ref, wgm_ref, wm0_ref, lng_ref, lnb_ref, ws_ref, bs_ref, wpg_ref, o_ref, vg_ref):
    tm = x_ref.shape[0]
    xb = x_ref[...]
    z = _gelu(_dot(xb, wgm_ref[...]))
    u = z[:, :D_MODEL]
    v = _layer_norm(z[:, D_MODEL:], lng_ref[...], lnb_ref[...]).astype(_BF)
    row = lax.broadcasted_iota(jnp.int32, (GM_CHUNK, GM_CHUNK), 0)
    col = lax.broadcasted_iota(jnp.int32, (GM_CHUNK, GM_CHUNK), 1)
    for gi in range(GM_GROUPS):
        w = jnp.where(row >= col, ws_ref[gi], 0.0).astype(_BF)
        for c in range(tm // GM_CHUNK):
            blk = v[c * GM_CHUNK:(c + 1) * GM_CHUNK, gi * LANES:(gi + 1) * LANES]
            vg_ref[c * GM_CHUNK:(c + 1) * GM_CHUNK, gi * LANES:(gi + 1) * LANES] = _dot(w, blk) + bs_ref[gi]
    y = (u * vg_ref[...]).astype(_BF)
    gate = _sigmoid(_dot(xb, wm0_ref[...]))
    o_ref[...] = gate * _dot(y, wpg_ref[...])


def _gm_mixer(xb, wgm, wm0, lng, lnb, ws, bs, wpg):
    T = xb.shape[0]
    tm = ROW_TILE
    const2 = lambda i: (0, 0)
    const3 = lambda i: (0, 0, 0)
    return pl.pallas_call(
        _gm_kernel,
        out_shape=jax.ShapeDtypeStruct((T, D_MODEL), _F32),
        grid=(T // tm,),
        in_specs=[
            pl.BlockSpec((tm, D_MODEL), lambda i: (i, 0)),
            pl.BlockSpec((D_MODEL, 2 * D_MODEL), const2),
            pl.BlockSpec((D_MODEL, D_MODEL), const2),
            pl.BlockSpec((1, D_MODEL), const2),
            pl.BlockSpec((1, D_MODEL), const2),
            pl.BlockSpec((GM_GROUPS, GM_CHUNK, GM_CHUNK), const3),
            pl.BlockSpec((GM_GROUPS, GM_CHUNK, LANES), const3),
            pl.BlockSpec((D_MODEL, D_MODEL), const2),
        ],
        out_specs=pl.BlockSpec((tm, D_MODEL), lambda i: (i, 0)),
        scratch_shapes=[pltpu.VMEM((tm, D_MODEL), _F32)],
        compiler_params=pltpu.CompilerParams(
            dimension_semantics=("arbitrary",), vmem_limit_bytes=VMEM_LIMIT),
        name="gm_mixer",
    )(xb, wgm, wm0, lng, lnb, ws, bs, wpg)


def _qkv_kernel(x_ref, wq_ref, wkv_ref, wg_ref, qt_ref, kc_ref, vc_ref, ksa_ref, vst_ref, kwa_ref, vwt_ref, gt_ref):
    tm = x_ref.shape[0]
    xb = x_ref[...]
    zq = _dot(xb, wq_ref[...]) * (HEAD_DIM ** -0.5)
    for tb in range(tm // Q_BLOCK):
        for h in range(N_KV_HEADS):
            for g in range(Q_PER_KV):
                c0 = (h * Q_PER_KV + g) * HEAD_DIM
                blk = zq[tb * Q_BLOCK:(tb + 1) * Q_BLOCK, c0:c0 + HEAD_DIM]
                qt_ref[tb, h, :, g * Q_BLOCK:(g + 1) * Q_BLOCK] = blk.T.astype(_BF)
    z = _dot(xb, wkv_ref[...])
    kpos = pl.program_id(1) * tm + lax.broadcasted_iota(jnp.int32, (tm, LANES), 0)
    blk_lane = lax.broadcasted_iota(jnp.int32, (tm, LANES), 1)
    sel_lanes = jnp.where(blk_lane == 0, (kpos % SEL_TILE).astype(_F32),
                          jnp.where(kpos // SEL_BLOCK == blk_lane, NEG_INF, 0.0)).astype(_BF)
    win_lanes = _position_lanes(kpos, (tm, LANES))
    for h in range(N_KV_HEADS):
        def col(j):
            return z[:, j * 2 * HEAD_DIM + h * HEAD_DIM: j * 2 * HEAD_DIM + (h + 1) * HEAD_DIM]
        kc_ref[0, h] = col(0)
        vc_ref[0, h] = col(1)
        ksa_ref[0, h, :, :HEAD_DIM] = col(2).astype(_BF)
        ksa_ref[0, h, :, HEAD_DIM:] = sel_lanes
        vst_ref[0, h] = col(3).T.astype(_BF)
        kwa_ref[0, h, :, :HEAD_DIM] = col(4).astype(_BF)
        kwa_ref[0, h, :, HEAD_DIM:] = win_lanes
        vwt_ref[0, h] = col(5).T.astype(_BF)
    zg = _sigmoid(_dot(xb, wg_ref[...]))
    for h in range(N_KV_HEADS):
        gt_ref[h] = zg[:, h * LANES:(h + 1) * LANES].T[:GATE_ROWS]


def _qkv_proj(xb, wq, wkv, wg, B, S):
    T = xb.shape[0]
    tm = ROW_TILE
    nsb = S // tm
    gd = Q_PER_KV * Q_BLOCK
    const2 = lambda b, s: (0, 0)
    rows_spec = lambda w: pl.BlockSpec((1, N_KV_HEADS, tm, w), lambda b, s: (b, 0, s, 0))
    rows_shape = lambda w, dt: jax.ShapeDtypeStruct((B, N_KV_HEADS, S, w), dt)
    cols_spec = pl.BlockSpec((1, N_KV_HEADS, HEAD_DIM, tm), lambda b, s: (b, 0, 0, s))
    cols_shape = jax.ShapeDtypeStruct((B, N_KV_HEADS, HEAD_DIM, S), _BF)
    return pl.pallas_call(
        _qkv_kernel,
        out_shape=(
            jax.ShapeDtypeStruct((T // Q_BLOCK, N_KV_HEADS, HEAD_DIM, gd), _BF),
            rows_shape(HEAD_DIM, _F32), rows_shape(HEAD_DIM, _F32),
            rows_shape(2 * HEAD_DIM, _BF), cols_shape,
            rows_shape(2 * HEAD_DIM, _BF), cols_shape,
            jax.ShapeDtypeStruct((N_KV_HEADS, GATE_ROWS, T), _F32),
        ),
        grid=(B, nsb),
        in_specs=[
            pl.BlockSpec((tm, D_MODEL), lambda b, s: (b * nsb + s, 0)),
            pl.BlockSpec((D_MODEL, D_MODEL), const2),
            pl.BlockSpec((D_MODEL, 6 * N_KV_HEADS * HEAD_DIM), const2),
            pl.BlockSpec((D_MODEL, N_KV_HEADS * LANES), const2),
        ],
        out_specs=(
            pl.BlockSpec((tm // Q_BLOCK, N_KV_HEADS, HEAD_DIM, gd), lambda b, s: (b * nsb + s, 0, 0, 0)),
            rows_spec(HEAD_DIM), rows_spec(HEAD_DIM), rows_spec(2 * HEAD_DIM), cols_spec,
            rows_spec(2 * HEAD_DIM), cols_spec,
            pl.BlockSpec((N_KV_HEADS, GATE_ROWS, tm), lambda b, s: (0, 0, b * nsb + s)),
        ),
        compiler_params=pltpu.CompilerParams(
            dimension_semantics=("arbitrary", "arbitrary"), vmem_limit_bytes=VMEM_LIMIT),
        name="qkv_proj",
    )(xb, wq, wkv, wg)


def _compress_kernel(kc_ref, vc_ref, pek_ref, pev_ref, w1k_ref, w1v_ref, w2k_ref, w2v_ref, ko_ref, vo_ref):
    nc = kc_ref.shape[1]

    def tokens(src, pe, w1, w2):
        ch = src[0]
        first = _dot((ch + pe[0:1, :]).astype(_BF), w1[0])
        second = _dot((ch + pe[1:2, :]).astype(_BF), w1[1])
        pre = first + pltpu.roll(second, nc - 1, 0)
        return _dot(_gelu(pre).astype(_BF), w2[...])

    ko_ref[0, :, :HEAD_DIM] = tokens(kc_ref, pek_ref, w1k_ref, w2k_ref).astype(_BF)
    start = lax.broadcasted_iota(jnp.int32, (nc, LANES), 0) * CMP_STRIDE
    ko_ref[0, :, HEAD_DIM:] = _position_lanes(start, (nc, LANES))
    vo_ref[0] = tokens(vc_ref, pev_ref, w1v_ref, w2v_ref).T.astype(_BF)


def _nsa_compress(kc, vc, pek, pev, w1k, w1v, w2k, w2v):
    BH, nc, width = kc.shape
    half = CMP_STRIDE * HEAD_DIM
    const2 = lambda i: (0, 0)
    const3 = lambda i: (0, 0, 0)
    row = lambda w: pl.BlockSpec((1, nc, w), lambda i: (i, 0, 0))
    return pl.pallas_call(
        _compress_kernel,
        out_shape=(jax.ShapeDtypeStruct((BH, nc, 2 * HEAD_DIM), _BF),
                   jax.ShapeDtypeStruct((BH, HEAD_DIM, nc), _BF)),
        grid=(BH,),
        in_specs=[row(width), row(width),
                  pl.BlockSpec((2, half), const2), pl.BlockSpec((2, half), const2),
                  pl.BlockSpec((2, half, HEAD_DIM), const3), pl.BlockSpec((2, half, HEAD_DIM), const3),
                  pl.BlockSpec((HEAD_DIM, HEAD_DIM), const2), pl.BlockSpec((HEAD_DIM, HEAD_DIM), const2)],
        out_specs=(row(2 * HEAD_DIM), pl.BlockSpec((1, HEAD_DIM, nc), lambda i: (i, 0, 0))),
        compiler_params=pltpu.CompilerParams(
            dimension_semantics=("arbitrary",), vmem_limit_bytes=VMEM_LIMIT),
        name="nsa_compress",
    )(kc, vc, pek, pev, w1k, w1v, w2k, w2v)


def _mark_top_blocks(score, notsel, rounds):
    rows = lax.broadcasted_iota(jnp.int32, score.shape, 0).astype(_F32)
    for _ in range(rounds):
        mx = jnp.max(score, axis=0, keepdims=True)
        idx = jnp.min(jnp.where(score == mx, rows, float(LANES)), axis=0, keepdims=True)
        hit = rows == idx
        notsel = jnp.where(hit, 0.0, notsel)
        score = jnp.where(hit, -jnp.inf, score)
    return notsel


def _attn_kernel(qt_ref, gt_ref, kca_ref, vct_ref, ksa_ref, vst_ref, kwa_ref, vwt_ref, o_ref,
                 m_ref, l_ref, acc_ref, part_ref, bits_ref, *, seq):
    h = pl.program_id(1)
    qb = pl.program_id(2)
    q0 = qb * Q_BLOCK
    nc = kca_ref.shape[1]
    n_sel = seq // SEL_BLOCK
    n_forced = 1 + N_LOCAL_FORCED
    k_top = min(N_SELECT, n_sel)
    G = Q_PER_KV
    W = G * Q_BLOCK

    lane_w = lax.broadcasted_iota(jnp.int32, (1, W), 1)
    slope = jnp.zeros((1, W), _F32)
    for g in range(G):
        sg = jnp.where(h == 0, _F32(2.0 ** -(g + 1)), _F32(2.0 ** -(G + g + 1)))
        slope = jnp.where(lane_w // Q_BLOCK == g, sg, slope)
    tok = q0 + lane_w % Q_BLOCK
    strips = [slice(g * Q_BLOCK, (g + 1) * Q_BLOCK) for g in range(G)]

    qt = qt_ref[0, 0]
    row_aug = lax.broadcasted_iota(jnp.int32, (LANES, W), 0)
    pos_rows = jnp.where(row_aug == 0, slope, jnp.where(row_aug == 1, slope * POS_RADIX, 0.0)).astype(_BF)
    lhs_pos = jnp.concatenate([qt, pos_rows], axis=0)

    s = _dot(kca_ref[0], lhs_pos)
    n_row = lax.broadcasted_iota(jnp.int32, (nc, Q_BLOCK), 0)
    n_last = (tok - (CMP_BLOCK - 1)) // CMP_STRIDE
    e_parts, inv_parts = [], []
    psum = None
    for cs in strips:
        sg = jnp.where(n_row <= n_last[:, cs], s[:, cs], NEG_INF)
        eg = jnp.exp(sg - jnp.max(sg, axis=0, keepdims=True))
        inv = jnp.where(n_last[:, cs] >= 0, 1.0 / jnp.sum(eg, axis=0, keepdims=True), 0.0)
        pg = eg * inv
        psum = pg if psum is None else psum + pg
        e_parts.append(eg.astype(_BF))
        inv_parts.append(inv)
    o_cmp = _dot(vct_ref[0], jnp.concatenate(e_parts, axis=1)) * jnp.concatenate(inv_parts, axis=1)

    wlen = WINDOW + Q_BLOCK
    w0 = pl.multiple_of(jnp.maximum(q0 - WINDOW, 0), Q_BLOCK)
    sw = _dot(kwa_ref[0, 0, pl.ds(w0, wlen), :], lhs_pos)
    w_row = w0 + lax.broadcasted_iota(jnp.int32, (wlen, Q_BLOCK), 0)
    e_parts, inv_parts = [], []
    for cs in strips:
        dist = tok[:, cs] - w_row
        sg = jnp.where(dist.astype(jnp.uint32) < WINDOW, sw[:, cs], NEG_INF)
        eg = jnp.exp(sg - jnp.max(sg, axis=0, keepdims=True))
        e_parts.append(eg.astype(_BF))
        inv_parts.append(1.0 / jnp.sum(eg, axis=0, keepdims=True))
    o_win = (_dot(vwt_ref[0, 0, :, pl.ds(w0, wlen)], jnp.concatenate(e_parts, axis=1))
             * jnp.concatenate(inv_parts, axis=1))

    gates = gt_ref[0]

    def gate_row(branch):
        return jnp.concatenate([gates[3 * g + branch:3 * g + branch + 1] for g in range(G)], axis=1)

    part_ref[...] = gate_row(0) * o_cmp + gate_row(2) * o_win

    mi = lax.broadcasted_iota(jnp.int32, (LANES, nc), 0)
    ni = lax.broadcasted_iota(jnp.int32, (LANES, nc), 1)
    overlap_t = jnp.where((ni * CMP_STRIDE + (CMP_BLOCK - 1) >= mi * SEL_BLOCK)
                          & (ni * CMP_STRIDE <= mi * SEL_BLOCK + (SEL_BLOCK - 1)), 1.0, 0.0).astype(_BF)
    p_hi = psum.astype(_BF)
    p_lo = (psum - p_hi.astype(_F32)).astype(_BF)
    imp_t = _dot(overlap_t, p_hi) + _dot(overlap_t, p_lo)
    mrow = lax.broadcasted_iota(jnp.int32, (LANES, Q_BLOCK), 0)
    tcol = lax.broadcasted_iota(jnp.int32, (LANES, Q_BLOCK), 1) + q0
    lag = tcol // SEL_BLOCK - mrow
    forced = (mrow == 0) | ((lag >= 0) & (lag < N_LOCAL_FORCED))
    score = jnp.where(forced | (lag < 0) | (mrow >= n_sel), -jnp.inf, imp_t)
    notsel_t = _mark_top_blocks(score, jnp.where(forced, 0.0, 1.0), k_top - n_forced)

    sel_rows = jnp.where(row_aug == 0, slope, jnp.concatenate([notsel_t] * G, axis=1)).astype(_BF)
    lhs_sel = jnp.concatenate([qt, sel_rows], axis=0)

    key_row = lax.broadcasted_iota(jnp.int32, (SEL_TILE, Q_BLOCK), 0)
    diag = q0 // SEL_TILE

    def tile_scores(j, causal, live=None):
        k0 = j * SEL_TILE if isinstance(j, int) else pl.multiple_of(j * SEL_TILE, SEL_TILE)
        st = _dot(ksa_ref[0, 0, pl.ds(k0, SEL_TILE), :], lhs_sel)
        if causal:
            st = jnp.concatenate([jnp.where(k0 + key_row > tok[:, cs], NEG_INF, st[:, cs]) for cs in strips], axis=1)
        shift = slope * (q0 - k0).astype(_F32)
        if live is not None:
            shift = jnp.where(live, shift, -NEG_INF)
        return st, shift, k0

    def tile_max(tiles):
        m = None
        for st, shift, _ in tiles:
            cm = jnp.max(st, axis=0, keepdims=True) - shift
            m = cm if m is None else jnp.maximum(m, cm)
        return m

    def tile_sums(tiles, m):
        l, acc = None, None
        for st, shift, k0 in tiles:
            sub = m + shift
            pt = jnp.concatenate([jnp.exp(st[:, cs] - sub[:, cs]) for cs in strips], axis=1)
            li = jnp.sum(pt, axis=0, keepdims=True)
            ai = _dot(vst_ref[0, 0, :, pl.ds(k0, SEL_TILE)], pt.astype(_BF))
            l = li if l is None else l + li
            acc = ai if acc is None else acc + ai
        return l, acc

    first_local = diag - (LOCAL_TILES - 1)
    tiles = [tile_scores(0, False, live=first_local > 0)]
    for i in range(LOCAL_TILES):
        j = first_local + i
        last = i == LOCAL_TILES - 1
        tiles.append(tile_scores(jnp.maximum(j, 0), last, live=None if last else j >= 0))
    m = tile_max(tiles)
    l, acc = tile_sums(tiles, m)
    m_ref[0:1] = m
    l_ref[0:1] = l
    acc_ref[...] = acc

    blk_on = jnp.where(jnp.min(notsel_t, axis=1, keepdims=True) < 0.5, 1.0, 0.0)
    blk_bit = jnp.left_shift(1, lax.broadcasted_iota(jnp.int32, (LANES, 1), 0) % BITS_PER_WORD).astype(_F32)
    for k in range(LANES // BITS_PER_WORD):
        word = jnp.sum((blk_on * blk_bit)[k * BITS_PER_WORD:(k + 1) * BITS_PER_WORD])
        bits_ref[k] = word.astype(jnp.int32)

    blocks_per_tile = SEL_TILE // SEL_BLOCK
    tiles_per_word = BITS_PER_WORD // blocks_per_tile

    def far_tile(j, carry):
        tile_bits = (bits_ref[j // tiles_per_word] >> ((j % tiles_per_word) * blocks_per_tile)) & ((1 << blocks_per_tile) - 1)

        @pl.when(tile_bits != 0)
        def _():
            tile = [tile_scores(j, False)]
            m_old = m_ref[0:1]
            m_new = jnp.maximum(m_old, tile_max(tile))
            alpha = jnp.exp(m_old - m_new)
            li, ai = tile_sums(tile, m_new)
            l_ref[0:1] = alpha * l_ref[0:1] + li
            m_ref[0:1] = m_new
            acc_ref[...] = alpha * acc_ref[...] + ai
        return carry

    lax.fori_loop(1, first_local, far_tile, 0)

    out_t = part_ref[...] + acc_ref[...] * (gate_row(1) * (1.0 / l_ref[0:1]))
    for g, cs in enumerate(strips):
        o_ref[:, g * HEAD_DIM:(g + 1) * HEAD_DIM] = out_t[:, cs].T.astype(_BF)


def _nsa_attention(qt, gt, kca, vct, ksa, vst, kwa, vwt, B, S):
    nqb = S // Q_BLOCK
    T = B * S
    nc = kca.shape[1]
    gd = Q_PER_KV * HEAD_DIM
    W = Q_PER_KV * Q_BLOCK
    rows = pl.BlockSpec((1, 1, S, 2 * HEAD_DIM), lambda b, h, i: (b, h, 0, 0))
    cols = pl.BlockSpec((1, 1, HEAD_DIM, S), lambda b, h, i: (b, h, 0, 0))
    return pl.pallas_call(
        functools.partial(_attn_kernel, seq=S),
        out_shape=jax.ShapeDtypeStruct((T, N_KV_HEADS * gd), _BF),
        grid=(B, N_KV_HEADS, nqb),
        in_specs=[
            pl.BlockSpec((1, 1, HEAD_DIM, W), lambda b, h, i: (b * nqb + i, h, 0, 0)),
            pl.BlockSpec((1, GATE_ROWS, Q_BLOCK), lambda b, h, i: (h, 0, b * nqb + i)),
            pl.BlockSpec((1, nc, 2 * HEAD_DIM), lambda b, h, i: (b * N_KV_HEADS + h, 0, 0)),
            pl.BlockSpec((1, HEAD_DIM, nc), lambda b, h, i: (b * N_KV_HEADS + h, 0, 0)),
            rows, cols, rows, cols,
        ],
        out_specs=pl.BlockSpec((Q_BLOCK, gd), lambda b, h, i: (b * nqb + i, h)),
        scratch_shapes=[pltpu.VMEM((8, W), _F32), pltpu.VMEM((8, W), _F32), pltpu.VMEM((HEAD_DIM, W), _F32),
                        pltpu.VMEM((HEAD_DIM, W), _F32), pltpu.SMEM((LANES // BITS_PER_WORD,), jnp.int32)],
        compiler_params=pltpu.CompilerParams(
            dimension_semantics=("arbitrary", "arbitrary", "arbitrary"), vmem_limit_bytes=VMEM_LIMIT),
        name="nsa_attention",
    )(qt, gt, kca, vct, ksa, vst, kwa, vwt)


def _merge_ffn_kernel(x_ref, gm_ref, yn_ref, wm1_ref, wpn_ref, wo_ref, g1_ref, b1_ref,
                      w1_ref, w2_ref, g2_ref, b2_ref, o_ref):
    x = x_ref[...]
    gate = _sigmoid(_dot(x.astype(_BF), wm1_ref[...]))
    merged = gm_ref[...] + gate * _dot(yn_ref[...], wpn_ref[...])
    mix = _dot(merged.astype(_BF), wo_ref[...])
    hid = _layer_norm(DEEPNORM_ALPHA * x + mix, g1_ref[...], b1_ref[...])
    hb = hid.astype(_BF)
    f = jnp.zeros(hid.shape, _F32)
    for c in range(D_FF // D_MODEL):
        a = jnp.maximum(_dot(hb, w1_ref[:, c * D_MODEL:(c + 1) * D_MODEL]), 0.0)
        f = f + _dot((a * a).astype(_BF), w2_ref[c * D_MODEL:(c + 1) * D_MODEL, :])
    o_ref[...] = _layer_norm(DEEPNORM_ALPHA * hid + f, g2_ref[...], b2_ref[...])


def _merge_ffn(x2, gm, yn, wm1, wpn, wo, g1, b1, w1, w2, g2, b2):
    T = x2.shape[0]
    tm = ROW_TILE
    rows = lambda w: pl.BlockSpec((tm, w), lambda i: (i, 0))
    const = lambda r, c: pl.BlockSpec((r, c), lambda i: (0, 0), pipeline_mode=pl.Buffered(1))
    return pl.pallas_call(
        _merge_ffn_kernel,
        out_shape=jax.ShapeDtypeStruct((T, D_MODEL), _F32),
        grid=(T // tm,),
        in_specs=[rows(D_MODEL), rows(D_MODEL), rows(D_MODEL),
                  const(D_MODEL, D_MODEL), const(D_MODEL, D_MODEL), const(D_MODEL, D_MODEL),
                  const(1, D_MODEL), const(1, D_MODEL),
                  const(D_MODEL, D_FF), const(D_FF, D_MODEL),
                  const(1, D_MODEL), const(1, D_MODEL)],
        out_specs=rows(D_MODEL),
        compiler_params=pltpu.CompilerParams(
            dimension_semantics=("arbitrary",), vmem_limit_bytes=VMEM_LIMIT),
        name="merge_ffn",
    )(x2, gm, yn, wm1, wpn, wo, g1, b1, w1, w2, g2, b2)


def _layer(x2, B, S, w_in, gm_ln_g, gm_ln_b, gm_w_s, gm_b_s, cmp_pe_k, cmp_w1_k, cmp_w2_k,
           cmp_pe_v, cmp_w1_v, cmp_w2_v, w_proj_gm, w_proj_nsa, w_out,
           ln1_g, ln1_b, w_ff1, w_ff2, ln2_g, ln2_b):
    gmw = 2 * D_MODEL
    aw = Q_PER_KV * N_KV_HEADS * HEAD_DIM
    kvw = N_KV_HEADS * HEAD_DIM
    o_q = gmw
    o_kv = o_q + aw
    o_g = o_kv + 6 * kvw
    n_gate = 3 * Q_PER_KV * N_KV_HEADS
    o_m = o_g + n_gate
    wb = w_in.astype(_BF)
    xb = x2.astype(_BF)
    row = lambda v: v.reshape(1, -1)

    wg = jnp.zeros((D_MODEL, N_KV_HEADS, LANES), _BF)
    wg = wg.at[:, :, :n_gate // N_KV_HEADS].set(wb[:, o_g:o_m].reshape(D_MODEL, N_KV_HEADS, -1))
    wg = wg.reshape(D_MODEL, N_KV_HEADS * LANES)

    gm = _gm_mixer(xb, wb[:, :gmw], wb[:, o_m:o_m + D_MODEL], row(gm_ln_g), row(gm_ln_b), gm_w_s,
                   jnp.broadcast_to(gm_b_s[:, :, None], (GM_GROUPS, GM_CHUNK, LANES)),
                   w_proj_gm.astype(_BF))

    qt, kc, vc, ksa, vst, kwa, vwt, gt = _qkv_proj(xb, wb[:, o_q:o_kv], wb[:, o_kv:o_g], wg, B, S)

    nc = S // CMP_STRIDE
    half = CMP_STRIDE * HEAD_DIM
    flat = lambda a: a.reshape(B * N_KV_HEADS, nc, half)
    kca, vct = _nsa_compress(
        flat(kc), flat(vc), cmp_pe_k.reshape(2, half), cmp_pe_v.reshape(2, half),
        cmp_w1_k.astype(_BF).reshape(2, half, HEAD_DIM), cmp_w1_v.astype(_BF).reshape(2, half, HEAD_DIM),
        cmp_w2_k.astype(_BF), cmp_w2_v.astype(_BF))

    yn = _nsa_attention(qt, gt, kca, vct, ksa, vst, kwa, vwt, B, S)

    return _merge_ffn(x2, gm, yn, wb[:, o_m + D_MODEL:o_m + 2 * D_MODEL], w_proj_nsa.astype(_BF),
                      w_out.astype(_BF), row(ln1_g), row(ln1_b), w_ff1.astype(_BF), w_ff2.astype(_BF),
                      row(ln2_g), row(ln2_b))


def kernel(x, w_in, gm_ln_g, gm_ln_b, gm_w_s, gm_b_s, cmp_pe_k, cmp_w1_k, cmp_w2_k, cmp_pe_v, cmp_w1_v, cmp_w2_v, w_proj_gm, w_proj_nsa, w_out, ln1_g, ln1_b, w_ff1, w_ff2, ln2_g, ln2_b):
    B, S, D = x.shape
    assert D == D_MODEL and S % ROW_TILE == 0 and WINDOW + Q_BLOCK <= S <= SEL_BLOCK * LANES
    assert S // POS_RADIX <= POS_RADIX and S // SEL_TILE <= 32
    h = x.reshape(B * S, D)
    for l in range(w_in.shape[0]):
        h = _layer(h, B, S, w_in[l], gm_ln_g[l], gm_ln_b[l], gm_w_s[l], gm_b_s[l],
                   cmp_pe_k[l], cmp_w1_k[l], cmp_w2_k[l], cmp_pe_v[l], cmp_w1_v[l], cmp_w2_v[l],
                   w_proj_gm[l], w_proj_nsa[l], w_out[l], ln1_g[l], ln1_b[l],
                   w_ff1[l], w_ff2[l], ln2_g[l], ln2_b[l])
    return h.reshape(B, S, D)
```

```python
import functools
import math

import jax
import jax.numpy as jnp
from jax import lax
from jax.experimental import pallas as pl
from jax.experimental.pallas import tpu as pltpu

D_MODEL = 1024
GM_GROUPS = 8
GM_CHUNK = 128
N_KV_HEADS = 2
Q_PER_KV = 4
HEAD_DIM = 128
CMP_BLOCK = 32
CMP_STRIDE = 16
SEL_BLOCK = 64
N_SELECT = 16
N_LOCAL_FORCED = 2
WINDOW = 512
Q_BLOCK = 128
D_FF = 4 * D_MODEL
DEEPNORM_ALPHA = 2.0 ** 0.25
LN_EPS = 1e-5
NEG_INF = -1e30
LOG2_E = math.log2(math.e)

LANES = 128
SEL_TILE = 256
LOCAL_TILES = 6
BITS_PER_WORD = 16
POS_RADIX = 256
ROW_TILE = 512
VT_ROWS = HEAD_DIM + 16
GATE_ROWS = 16
VMEM_LIMIT = 56 * 1024 * 1024

_BF = jnp.bfloat16
_F32 = jnp.float32
_NT = (((1,), (1,)), ((), ()))


def _dot(a, b):
    return jnp.dot(a, b, preferred_element_type=_F32)


def _dot_nt(a, b):
    return lax.dot_general(a, b, _NT, preferred_element_type=_F32)


def _gelu(x):
    c = math.sqrt(2.0 / math.pi)
    return 0.5 * x * (1.0 + jnp.tanh(c * (x + 0.044715 * (x * x * x))))


def _sigmoid(x):
    return 1.0 / (1.0 + jnp.exp(-x))


def _layer_norm(x, g, b):
    mu = jnp.mean(x, axis=-1, keepdims=True)
    xc = x - mu
    var = jnp.mean(xc * xc, axis=-1, keepdims=True)
    return xc * lax.rsqrt(var + LN_EPS) * g + b


def _position_lanes(pos, shape):
    lane = lax.broadcasted_iota(jnp.int32, shape, 1)
    return jnp.where(lane == 0, (pos % POS_RADIX).astype(_F32),
                     jnp.where(lane == 1, (pos // POS_RADIX).astype(_F32), 0.0)).astype(_BF)


def _gm_kernel(x_ref, wgm_ref, wm0_ref, lng_ref, lnb_ref, ws_ref, bs_ref, wpg_ref, o_ref, vg_ref):
    tm = x_ref.shape[0]
    xb = x_ref[...]
    z = _gelu(_dot(xb, wgm_ref[...]))
    u = z[:, :D_MODEL]
    v = _layer_norm(z[:, D_MODEL:], lng_ref[...], lnb_ref[...]).astype(_BF)
    row = lax.broadcasted_iota(jnp.int32, (GM_CHUNK, GM_CHUNK), 0)
    col = lax.broadcasted_iota(jnp.int32, (GM_CHUNK, GM_CHUNK), 1)
    for gi in range(GM_GROUPS):
        w = jnp.where(row >= col, ws_ref[gi], 0.0).astype(_BF)
        for c in range(tm // GM_CHUNK):
            blk = v[c * GM_CHUNK:(c + 1) * GM_CHUNK, gi * LANES:(gi + 1) * LANES]
            vg_ref[c * GM_CHUNK:(c + 1) * GM_CHUNK, gi * LANES:(gi + 1) * LANES] = _dot(w, blk) + bs_ref[gi]
    y = (u * vg_ref[...]).astype(_BF)
    gate = _sigmoid(_dot(xb, wm0_ref[...]))
    o_ref[...] = gate * _dot(y, wpg_ref[...])


def _gm_mixer(xb, wgm, wm0, lng, lnb, ws, bs, wpg):
    T = xb.shape[0]
    tm = ROW_TILE
    const2 = lambda i: (0, 0)
    const3 = lambda i: (0, 0, 0)
    return pl.pallas_call(
        _gm_kernel,
        out_shape=jax.ShapeDtypeStruct((T, D_MODEL), _F32),
        grid=(T // tm,),
        in_specs=[
            pl.BlockSpec((tm, D_MODEL), lambda i: (i, 0)),
            pl.BlockSpec((D_MODEL, 2 * D_MODEL), const2),
            pl.BlockSpec((D_MODEL, D_MODEL), const2),
            pl.BlockSpec((1, D_MODEL), const2),
            pl.BlockSpec((1, D_MODEL), const2),
            pl.BlockSpec((GM_GROUPS, GM_CHUNK, GM_CHUNK), const3),
            pl.BlockSpec((GM_GROUPS, GM_CHUNK, LANES), const3),
            pl.BlockSpec((D_MODEL, D_MODEL), const2),
        ],
        out_specs=pl.BlockSpec((tm, D_MODEL), lambda i: (i, 0)),
        scratch_shapes=[pltpu.VMEM((tm, D_MODEL), _F32)],
        compiler_params=pltpu.CompilerParams(
            dimension_semantics=("arbitrary",), vmem_limit_bytes=VMEM_LIMIT),
        name="gm_mixer",
    )(xb, wgm, wm0, lng, lnb, ws, bs, wpg)


def _qkv_kernel(x_ref, wq_ref, wkv_ref, wg_ref, qt_ref, kc_ref, vc_ref, ksa_ref, vst_ref, kwa_ref, vwt_ref, gt_ref):
    tm = x_ref.shape[0]
    xb = x_ref[...]
    zq = _dot(xb, wq_ref[...]) * (HEAD_DIM ** -0.5 * LOG2_E)
    for tb in range(tm // Q_BLOCK):
        for h in range(N_KV_HEADS):
            for g in range(Q_PER_KV):
                c0 = (h * Q_PER_KV + g) * HEAD_DIM
                blk = zq[tb * Q_BLOCK:(tb + 1) * Q_BLOCK, c0:c0 + HEAD_DIM]
                qt_ref[tb, h, :, g * Q_BLOCK:(g + 1) * Q_BLOCK] = blk.T.astype(_BF)
    z = _dot(xb, wkv_ref[...])
    kpos = pl.program_id(1) * tm + lax.broadcasted_iota(jnp.int32, (tm, LANES), 0)
    blk_lane = lax.broadcasted_iota(jnp.int32, (tm, LANES), 1)
    sel_lanes = jnp.where(blk_lane == 0, (kpos % SEL_TILE).astype(_F32),
                          jnp.where(kpos // SEL_BLOCK == blk_lane, NEG_INF, 0.0)).astype(_BF)
    win_lanes = _position_lanes(kpos, (tm, LANES))
    ones = jnp.ones((VT_ROWS - HEAD_DIM, tm), _BF)
    for h in range(N_KV_HEADS):
        def col(j):
            return z[:, j * 2 * HEAD_DIM + h * HEAD_DIM: j * 2 * HEAD_DIM + (h + 1) * HEAD_DIM]
        kc_ref[0, h] = col(0)
        vc_ref[0, h] = col(1)
        ksa_ref[0, h, :, :HEAD_DIM] = col(2).astype(_BF)
        ksa_ref[0, h, :, HEAD_DIM:] = sel_lanes
        vst_ref[0, h, :HEAD_DIM] = col(3).T.astype(_BF)
        vst_ref[0, h, HEAD_DIM:] = ones
        kwa_ref[0, h, :, :HEAD_DIM] = col(4).astype(_BF)
        kwa_ref[0, h, :, HEAD_DIM:] = win_lanes
        vwt_ref[0, h, :HEAD_DIM] = col(5).T.astype(_BF)
        vwt_ref[0, h, HEAD_DIM:] = ones
    zg = _sigmoid(_dot(xb, wg_ref[...]))
    for h in range(N_KV_HEADS):
        gt_ref[h] = zg[:, h * LANES:(h + 1) * LANES].T[:GATE_ROWS]


def _qkv_proj(xb, wq, wkv, wg, B, S):
    T = xb.shape[0]
    tm = ROW_TILE
    nsb = S // tm
    gd = Q_PER_KV * Q_BLOCK
    const2 = lambda b, s: (0, 0)
    rows_spec = lambda w: pl.BlockSpec((1, N_KV_HEADS, tm, w), lambda b, s: (b, 0, s, 0))
    rows_shape = lambda w, dt: jax.ShapeDtypeStruct((B, N_KV_HEADS, S, w), dt)
    cols_spec = pl.BlockSpec((1, N_KV_HEADS, VT_ROWS, tm), lambda b, s: (b, 0, 0, s))
    cols_shape = jax.ShapeDtypeStruct((B, N_KV_HEADS, VT_ROWS, S), _BF)
    return pl.pallas_call(
        _qkv_kernel,
        out_shape=(
            jax.ShapeDtypeStruct((T // Q_BLOCK, N_KV_HEADS, HEAD_DIM, gd), _BF),
            rows_shape(HEAD_DIM, _F32), rows_shape(HEAD_DIM, _F32),
            rows_shape(2 * HEAD_DIM, _BF), cols_shape,
            rows_shape(2 * HEAD_DIM, _BF), cols_shape,
            jax.ShapeDtypeStruct((N_KV_HEADS, GATE_ROWS, T), _F32),
        ),
        grid=(B, nsb),
        in_specs=[
            pl.BlockSpec((tm, D_MODEL), lambda b, s: (b * nsb + s, 0)),
            pl.BlockSpec((D_MODEL, D_MODEL), const2),
            pl.BlockSpec((D_MODEL, 6 * N_KV_HEADS * HEAD_DIM), const2),
            pl.BlockSpec((D_MODEL, N_KV_HEADS * LANES), const2),
        ],
        out_specs=(
            pl.BlockSpec((tm // Q_BLOCK, N_KV_HEADS, HEAD_DIM, gd), lambda b, s: (b * nsb + s, 0, 0, 0)),
            rows_spec(HEAD_DIM), rows_spec(HEAD_DIM), rows_spec(2 * HEAD_DIM), cols_spec,
            rows_spec(2 * HEAD_DIM), cols_spec,
            pl.BlockSpec((N_KV_HEADS, GATE_ROWS, tm), lambda b, s: (0, 0, b * nsb + s)),
        ),
        compiler_params=pltpu.CompilerParams(
            dimension_semantics=("arbitrary", "arbitrary"), vmem_limit_bytes=VMEM_LIMIT),
        name="qkv_proj",
    )(xb, wq, wkv, wg)


def _compress_kernel(kc_ref, vc_ref, pek_ref, pev_ref, w1k_ref, w1v_ref, w2k_ref, w2v_ref, ko_ref, vo_ref):
    nc = kc_ref.shape[1]

    def tokens(src, pe, w1, w2):
        ch = src[0]
        first = _dot((ch + pe[0:1, :]).astype(_BF), w1[0])
        second = _dot((ch + pe[1:2, :]).astype(_BF), w1[1])
        pre = first + pltpu.roll(second, nc - 1, 0)
        return _dot(_gelu(pre).astype(_BF), w2[...])

    ko_ref[0, :, :HEAD_DIM] = tokens(kc_ref, pek_ref, w1k_ref, w2k_ref).astype(_BF)
    start = lax.broadcasted_iota(jnp.int32, (nc, LANES), 0) * CMP_STRIDE
    ko_ref[0, :, HEAD_DIM:] = _position_lanes(start, (nc, LANES))
    vo_ref[0] = tokens(vc_ref, pev_ref, w1v_ref, w2v_ref).T.astype(_BF)


def _nsa_compress(kc, vc, pek, pev, w1k, w1v, w2k, w2v):
    BH, nc, width = kc.shape
    half = CMP_STRIDE * HEAD_DIM
    const2 = lambda i: (0, 0)
    const3 = lambda i: (0, 0, 0)
    row = lambda w: pl.BlockSpec((1, nc, w), lambda i: (i, 0, 0))
    return pl.pallas_call(
        _compress_kernel,
        out_shape=(jax.ShapeDtypeStruct((BH, nc, 2 * HEAD_DIM), _BF),
                   jax.ShapeDtypeStruct((BH, HEAD_DIM, nc), _BF)),
        grid=(BH,),
        in_specs=[row(width), row(width),
                  pl.BlockSpec((2, half), const2), pl.BlockSpec((2, half), const2),
                  pl.BlockSpec((2, half, HEAD_DIM), const3), pl.BlockSpec((2, half, HEAD_DIM), const3),
                  pl.BlockSpec((HEAD_DIM, HEAD_DIM), const2), pl.BlockSpec((HEAD_DIM, HEAD_DIM), const2)],
        out_specs=(row(2 * HEAD_DIM), pl.BlockSpec((1, HEAD_DIM, nc), lambda i: (i, 0, 0))),
        compiler_params=pltpu.CompilerParams(
            dimension_semantics=("arbitrary",), vmem_limit_bytes=VMEM_LIMIT),
        name="nsa_compress",
    )(kc, vc, pek, pev, w1k, w1v, w2k, w2v)


def _mark_top_blocks(score, notsel, rounds):
    rows = lax.broadcasted_iota(jnp.int32, score.shape, 0).astype(_F32)
    for _ in range(rounds):
        mx = jnp.max(score, axis=0, keepdims=True)
        idx = jnp.min(jnp.where(score == mx, rows, float(LANES)), axis=0, keepdims=True)
        hit = rows == idx
        notsel = jnp.where(hit, 0.0, notsel)
        score = jnp.where(hit, -jnp.inf, score)
    return notsel


def _attn_kernel(qt_ref, gt_ref, kca_ref, vct_ref, ksa_ref, vst_ref, kwa_ref, vwt_ref, o_ref,
                 m_ref, acc_ref, part_ref, bits_ref, *, seq):
    h = pl.program_id(1)
    qb = pl.program_id(2)
    q0 = qb * Q_BLOCK
    nc = kca_ref.shape[1]
    n_sel = seq // SEL_BLOCK
    n_forced = 1 + N_LOCAL_FORCED
    k_top = min(N_SELECT, n_sel)
    G = Q_PER_KV
    W = G * Q_BLOCK

    lane_w = lax.broadcasted_iota(jnp.int32, (1, W), 1)
    slope = jnp.zeros((1, W), _F32)
    for g in range(G):
        sg = jnp.where(h == 0, _F32(2.0 ** -(g + 1)), _F32(2.0 ** -(G + g + 1)))
        slope = jnp.where(lane_w // Q_BLOCK == g, sg, slope)
    slope = (slope * LOG2_E).astype(_BF).astype(_F32)
    tok = q0 + lane_w % Q_BLOCK
    strips = [slice(g * Q_BLOCK, (g + 1) * Q_BLOCK) for g in range(G)]

    qt = qt_ref[0, 0]
    row_aug = lax.broadcasted_iota(jnp.int32, (LANES, W), 0)
    pos_rows = jnp.where(row_aug == 0, slope, jnp.where(row_aug == 1, slope * POS_RADIX, 0.0)).astype(_BF)
    lhs_pos = jnp.concatenate([qt, pos_rows], axis=0)

    s = _dot(kca_ref[0], lhs_pos)
    n_row = lax.broadcasted_iota(jnp.int32, (nc, Q_BLOCK), 0)
    n_last = (tok - (CMP_BLOCK - 1)) // CMP_STRIDE
    e_parts, inv_parts = [], []
    psum = None
    for cs in strips:
        sg = jnp.where(n_row <= n_last[:, cs], s[:, cs], NEG_INF)
        eg = jnp.exp2(sg - jnp.max(sg, axis=0, keepdims=True))
        inv = jnp.where(n_last[:, cs] >= 0, 1.0 / jnp.sum(eg, axis=0, keepdims=True), 0.0)
        pg = eg * inv
        psum = pg if psum is None else psum + pg
        e_parts.append(eg.astype(_BF))
        inv_parts.append(inv)
    o_cmp = _dot(vct_ref[0], jnp.concatenate(e_parts, axis=1)) * jnp.concatenate(inv_parts, axis=1)

    mi = lax.broadcasted_iota(jnp.int32, (LANES, nc), 0)
    ni = lax.broadcasted_iota(jnp.int32, (LANES, nc), 1)
    overlap_t = jnp.where((ni * CMP_STRIDE + (CMP_BLOCK - 1) >= mi * SEL_BLOCK)
                          & (ni * CMP_STRIDE <= mi * SEL_BLOCK + (SEL_BLOCK - 1)), 1.0, 0.0).astype(_BF)
    p_hi = psum.astype(_BF)
    p_lo = (psum - p_hi.astype(_F32)).astype(_BF)
    imp_t = _dot(overlap_t, p_hi) + _dot(overlap_t, p_lo)

    anchor = jnp.concatenate([jnp.minimum(imp_t[0:1], 0.0)] * G, axis=1)
    win_rows = jnp.where(row_aug == 0, slope + anchor, jnp.where(row_aug == 1, slope * POS_RADIX, 0.0)).astype(_BF)
    wlen = WINDOW + Q_BLOCK
    w0 = pl.multiple_of(jnp.maximum(q0 - WINDOW, 0), Q_BLOCK)
    sw = _dot(kwa_ref[0, 0, pl.ds(w0, wlen), :], jnp.concatenate([qt, win_rows], axis=0))
    w_row = w0 + lax.broadcasted_iota(jnp.int32, (wlen, Q_BLOCK), 0)
    e_parts = []
    for cs in strips:
        dist = tok[:, cs] - w_row
        sg = jnp.where(dist.astype(jnp.uint32) < WINDOW, sw[:, cs], NEG_INF)
        e_parts.append(jnp.exp2(sg - jnp.max(sg, axis=0, keepdims=True)).astype(_BF))
    win = _dot(vwt_ref[0, 0, :, pl.ds(w0, wlen)], jnp.concatenate(e_parts, axis=1))
    o_win = win[:HEAD_DIM] * (1.0 / win[HEAD_DIM:HEAD_DIM + 1])

    gates = gt_ref[0]

    def gate_row(branch):
        return jnp.concatenate([gates[3 * g + branch:3 * g + branch + 1] for g in range(G)], axis=1)

    part_ref[...] = gate_row(0) * o_cmp + gate_row(2) * o_win
    mrow = lax.broadcasted_iota(jnp.int32, (LANES, Q_BLOCK), 0)
    tcol = lax.broadcasted_iota(jnp.int32, (LANES, Q_BLOCK), 1) + q0
    lag = tcol // SEL_BLOCK - mrow
    forced = (mrow == 0) | ((lag >= 0) & (lag < N_LOCAL_FORCED))
    score = jnp.where(forced | (lag < 0) | (mrow >= n_sel), -jnp.inf, imp_t)
    notsel_t = _mark_top_blocks(score, jnp.where(forced, 0.0, 1.0), k_top - n_forced)

    sel_rows = jnp.where(row_aug == 0, slope, jnp.concatenate([notsel_t] * G, axis=1)).astype(_BF)
    lhs_sel = jnp.concatenate([qt, sel_rows], axis=0)

    key_row = lax.broadcasted_iota(jnp.int32, (SEL_TILE, Q_BLOCK), 0)
    diag = q0 // SEL_TILE

    def tile_scores(j, causal, live=None):
        k0 = j * SEL_TILE if isinstance(j, int) else pl.multiple_of(j * SEL_TILE, SEL_TILE)
        st = _dot(ksa_ref[0, 0, pl.ds(k0, SEL_TILE), :], lhs_sel)
        if causal:
            st = jnp.concatenate([jnp.where(k0 + key_row > tok[:, cs], NEG_INF, st[:, cs]) for cs in strips], axis=1)
        shift = slope * (q0 - k0).astype(_F32)
        if live is not None:
            shift = jnp.where(live, shift, -NEG_INF)
        return st, shift, k0

    def tile_max(tiles):
        m = None
        for st, shift, _ in tiles:
            cm = jnp.max(st, axis=0, keepdims=True) - shift
            m = cm if m is None else jnp.maximum(m, cm)
        return m

    def tile_sums(tiles, m):
        acc = None
        for st, shift, k0 in tiles:
            sub = m + shift
            pt = jnp.concatenate([jnp.exp2(st[:, cs] - sub[:, cs]) for cs in strips], axis=1)
            ai = _dot(vst_ref[0, 0, :, pl.ds(k0, SEL_TILE)], pt.astype(_BF))
            acc = ai if acc is None else acc + ai
        return acc

    first_local = diag - (LOCAL_TILES - 1)
    tiles = [tile_scores(0, False, live=first_local > 0)]
    for i in range(LOCAL_TILES):
        j = first_local + i
        last = i == LOCAL_TILES - 1
        tiles.append(tile_scores(jnp.maximum(j, 0), last, live=None if last else j >= 0))
    m = tile_max(tiles)
    m_ref[0:1] = m
    acc_ref[...] = tile_sums(tiles, m)

    blk_on = jnp.where(jnp.min(notsel_t, axis=1, keepdims=True) < 0.5, 1.0, 0.0)
    blk_bit = jnp.left_shift(1, lax.broadcasted_iota(jnp.int32, (LANES, 1), 0) % BITS_PER_WORD).astype(_F32)
    for k in range(LANES // BITS_PER_WORD):
        word = jnp.sum((blk_on * blk_bit)[k * BITS_PER_WORD:(k + 1) * BITS_PER_WORD])
        bits_ref[k] = word.astype(jnp.int32)

    blocks_per_tile = SEL_TILE // SEL_BLOCK
    tiles_per_word = BITS_PER_WORD // blocks_per_tile

    def far_tile(j, carry):
        tile_bits = (bits_ref[j // tiles_per_word] >> ((j % tiles_per_word) * blocks_per_tile)) & ((1 << blocks_per_tile) - 1)

        @pl.when(tile_bits != 0)
        def _():
            tile = [tile_scores(j, False)]
            m_old = m_ref[0:1]
            m_new = jnp.maximum(m_old, tile_max(tile))
            m_ref[0:1] = m_new
            acc_ref[...] = jnp.exp2(m_old - m_new) * acc_ref[...] + tile_sums(tile, m_new)
        return carry

    lax.fori_loop(1, first_local, far_tile, 0)

    out_t = part_ref[...] + acc_ref[0:HEAD_DIM] * (gate_row(1) * (1.0 / acc_ref[HEAD_DIM:HEAD_DIM + 1]))
    for g, cs in enumerate(strips):
        o_ref[:, g * HEAD_DIM:(g + 1) * HEAD_DIM] = out_t[:, cs].T.astype(_BF)


def _nsa_attention(qt, gt, kca, vct, ksa, vst, kwa, vwt, B, S):
    nqb = S // Q_BLOCK
    T = B * S
    nc = kca.shape[1]
    gd = Q_PER_KV * HEAD_DIM
    W = Q_PER_KV * Q_BLOCK
    rows = pl.BlockSpec((1, 1, S, 2 * HEAD_DIM), lambda b, h, i: (b, h, 0, 0))
    cols = pl.BlockSpec((1, 1, VT_ROWS, S), lambda b, h, i: (b, h, 0, 0))
    return pl.pallas_call(
        functools.partial(_attn_kernel, seq=S),
        out_shape=jax.ShapeDtypeStruct((T, N_KV_HEADS * gd), _BF),
        grid=(B, N_KV_HEADS, nqb),
        in_specs=[
            pl.BlockSpec((1, 1, HEAD_DIM, W), lambda b, h, i: (b * nqb + i, h, 0, 0)),
            pl.BlockSpec((1, GATE_ROWS, Q_BLOCK), lambda b, h, i: (h, 0, b * nqb + i)),
            pl.BlockSpec((1, nc, 2 * HEAD_DIM), lambda b, h, i: (b * N_KV_HEADS + h, 0, 0)),
            pl.BlockSpec((1, HEAD_DIM, nc), lambda b, h, i: (b * N_KV_HEADS + h, 0, 0)),
            rows, cols, rows, cols,
        ],
        out_specs=pl.BlockSpec((Q_BLOCK, gd), lambda b, h, i: (b * nqb + i, h)),
        scratch_shapes=[pltpu.VMEM((8, W), _F32), pltpu.VMEM((VT_ROWS, W), _F32),
                        pltpu.VMEM((HEAD_DIM, W), _F32), pltpu.SMEM((LANES // BITS_PER_WORD,), jnp.int32)],
        compiler_params=pltpu.CompilerParams(
            dimension_semantics=("arbitrary", "arbitrary", "arbitrary"), vmem_limit_bytes=VMEM_LIMIT),
        name="nsa_attention",
    )(qt, gt, kca, vct, ksa, vst, kwa, vwt)


def _merge_ffn_kernel(x_ref, gm_ref, yn_ref, wm1_ref, wpn_ref, wo_ref, g1_ref, b1_ref,
                      w1_ref, w2_ref, g2_ref, b2_ref, o_ref):
    x = x_ref[...]
    gate = _sigmoid(_dot(x.astype(_BF), wm1_ref[...]))
    merged = gm_ref[...] + gate * _dot(yn_ref[...], wpn_ref[...])
    mix = _dot(merged.astype(_BF), wo_ref[...])
    hid = _layer_norm(DEEPNORM_ALPHA * x + mix, g1_ref[...], b1_ref[...])
    hb = hid.astype(_BF)
    f = jnp.zeros(hid.shape, _F32)
    for c in range(D_FF // D_MODEL):
        a = jnp.maximum(_dot(hb, w1_ref[:, c * D_MODEL:(c + 1) * D_MODEL]), 0.0)
        f = f + _dot((a * a).astype(_BF), w2_ref[c * D_MODEL:(c + 1) * D_MODEL, :])
    o_ref[...] = _layer_norm(DEEPNORM_ALPHA * hid + f, g2_ref[...], b2_ref[...])


def _merge_ffn(x2, gm, yn, wm1, wpn, wo, g1, b1, w1, w2, g2, b2):
    T = x2.shape[0]
    tm = ROW_TILE
    rows = lambda w: pl.BlockSpec((tm, w), lambda i: (i, 0))
    const = lambda r, c: pl.BlockSpec((r, c), lambda i: (0, 0), pipeline_mode=pl.Buffered(1))
    return pl.pallas_call(
        _merge_ffn_kernel,
        out_shape=jax.ShapeDtypeStruct((T, D_MODEL), _F32),
        grid=(T // tm,),
        in_specs=[rows(D_MODEL), rows(D_MODEL), rows(D_MODEL),
                  const(D_MODEL, D_MODEL), const(D_MODEL, D_MODEL), const(D_MODEL, D_MODEL),
                  const(1, D_MODEL), const(1, D_MODEL),
                  const(D_MODEL, D_FF), const(D_FF, D_MODEL),
                  const(1, D_MODEL), const(1, D_MODEL)],
        out_specs=rows(D_MODEL),
        compiler_params=pltpu.CompilerParams(
            dimension_semantics=("arbitrary",), vmem_limit_bytes=VMEM_LIMIT),
        name="merge_ffn",
    )(x2, gm, yn, wm1, wpn, wo, g1, b1, w1, w2, g2, b2)


def _layer(x2, B, S, w_in, gm_ln_g, gm_ln_b, gm_w_s, gm_b_s, cmp_pe_k, cmp_w1_k, cmp_w2_k,
           cmp_pe_v, cmp_w1_v, cmp_w2_v, w_proj_gm, w_proj_nsa, w_out,
           ln1_g, ln1_b, w_ff1, w_ff2, ln2_g, ln2_b):
    gmw = 2 * D_MODEL
    aw = Q_PER_KV * N_KV_HEADS * HEAD_DIM
    kvw = N_KV_HEADS * HEAD_DIM
    o_q = gmw
    o_kv = o_q + aw
    o_g = o_kv + 6 * kvw
    n_gate = 3 * Q_PER_KV * N_KV_HEADS
    o_m = o_g + n_gate
    wb = w_in.astype(_BF)
    xb = x2.astype(_BF)
    row = lambda v: v.reshape(1, -1)

    wg = jnp.zeros((D_MODEL, N_KV_HEADS, LANES), _BF)
    wg = wg.at[:, :, :n_gate // N_KV_HEADS].set(wb[:, o_g:o_m].reshape(D_MODEL, N_KV_HEADS, -1))
    wg = wg.reshape(D_MODEL, N_KV_HEADS * LANES)

    gm = _gm_mixer(xb, wb[:, :gmw], wb[:, o_m:o_m + D_MODEL], row(gm_ln_g), row(gm_ln_b), gm_w_s,
                   jnp.broadcast_to(gm_b_s[:, :, None], (GM_GROUPS, GM_CHUNK, LANES)),
                   w_proj_gm.astype(_BF))

    qt, kc, vc, ksa, vst, kwa, vwt, gt = _qkv_proj(xb, wb[:, o_q:o_kv], wb[:, o_kv:o_g], wg, B, S)

    nc = S // CMP_STRIDE
    half = CMP_STRIDE * HEAD_DIM
    flat = lambda a: a.reshape(B * N_KV_HEADS, nc, half)
    kca, vct = _nsa_compress(
        flat(kc), flat(vc), cmp_pe_k.reshape(2, half), cmp_pe_v.reshape(2, half),
        cmp_w1_k.astype(_BF).reshape(2, half, HEAD_DIM), cmp_w1_v.astype(_BF).reshape(2, half, HEAD_DIM),
        cmp_w2_k.astype(_BF), cmp_w2_v.astype(_BF))

    yn = _nsa_attention(qt, gt, kca, vct, ksa, vst, kwa, vwt, B, S)

    return _merge_ffn(x2, gm, yn, wb[:, o_m + D_MODEL:o_m + 2 * D_MODEL], w_proj_nsa.astype(_BF),
                      w_out.astype(_BF), row(ln1_g), row(ln1_b), w_ff1.astype(_BF), w_ff2.astype(_BF),
                      row(ln2_g), row(ln2_b))


def kernel(x, w_in, gm_ln_g, gm_ln_b, gm_w_s, gm_b_s, cmp_pe_k, cmp_w1_k, cmp_w2_k, cmp_pe_v, cmp_w1_v, cmp_w2_v, w_proj_gm, w_proj_nsa, w_out, ln1_g, ln1_b, w_ff1, w_ff2, ln2_g, ln2_b):
    B, S, D = x.shape
    assert D == D_MODEL and S % ROW_TILE == 0 and WINDOW + Q_BLOCK <= S <= SEL_BLOCK * LANES
    assert S // POS_RADIX <= POS_RADIX and S // SEL_TILE <= 32
    h = x.reshape(B * S, D)
    for l in range(w_in.shape[0]):
        h = _layer(h, B, S, w_in[l], gm_ln_g[l], gm_ln_b[l], gm_w_s[l], gm_b_s[l],
                   cmp_pe_k[l], cmp_w1_k[l], cmp_w2_k[l], cmp_pe_v[l], cmp_w1_v[l], cmp_w2_v[l],
                   w_proj_gm[l], w_proj_nsa[l], w_out[l], ln1_g[l], ln1_b[l],
                   w_ff1[l], w_ff2[l], ln2_g[l], ln2_b[l])
    return h.reshape(B, S, D)
```

```python
import functools
import math

import jax
import jax.numpy as jnp
from jax import lax
from jax.experimental import pallas as pl
from jax.experimental.pallas import tpu as pltpu

D_MODEL = 1024
GM_GROUPS = 8
GM_CHUNK = 128
N_KV_HEADS = 2
Q_PER_KV = 4
HEAD_DIM = 128
CMP_BLOCK = 32
CMP_STRIDE = 16
SEL_BLOCK = 64
N_SELECT = 16
N_LOCAL_FORCED = 2
WINDOW = 512
Q_BLOCK = 128
D_FF = 4 * D_MODEL
DEEPNORM_ALPHA = 2.0 ** 0.25
LN_EPS = 1e-5
NEG_INF = -1e30
LOG2_E = math.log2(math.e)

LANES = 128
SEL_TILE = 256
LOCAL_TILES = 6
BITS_PER_WORD = 16
POS_RADIX = 256
ROW_TILE = 512
VT_ROWS = HEAD_DIM + 16
GATE_ROWS = 16
VMEM_LIMIT = 56 * 1024 * 1024

_BF = jnp.bfloat16
_F32 = jnp.float32
_NT = (((1,), (1,)), ((), ()))


def _dot(a, b):
    return jnp.dot(a, b, preferred_element_type=_F32)


def _dot_nt(a, b):
    return lax.dot_general(a, b, _NT, preferred_element_type=_F32)


def _gelu(x):
    c = math.sqrt(2.0 / math.pi)
    return 0.5 * x * (1.0 + jnp.tanh(c * (x + 0.044715 * (x * x * x))))


def _sigmoid(x):
    return 1.0 / (1.0 + jnp.exp(-x))


def _layer_norm(x, g, b):
    mu = jnp.mean(x, axis=-1, keepdims=True)
    xc = x - mu
    var = jnp.mean(xc * xc, axis=-1, keepdims=True)
    return xc * lax.rsqrt(var + LN_EPS) * g + b


def _position_lanes(pos, shape):
    lane = lax.broadcasted_iota(jnp.int32, shape, 1)
    return jnp.where(lane == 0, (pos % POS_RADIX).astype(_F32),
                     jnp.where(lane == 1, (pos // POS_RADIX).astype(_F32), 0.0)).astype(_BF)


def _stage_bf16(first_step, pairs):
    @pl.when(first_step)
    def _():
        for src, dst in pairs:
            dst[...] = src[...].astype(_BF)


def _resident(shape, index_map):
    return pl.BlockSpec(shape, index_map, pipeline_mode=pl.Buffered(1))


def _gm_kernel(x_ref, wgm32_ref, wm032_ref, lng_ref, lnb_ref, ws_ref, bs_ref, wpg32_ref, o_ref,
               vg_ref, wgm_ref, wm0_ref, wpg_ref):
    _stage_bf16(pl.program_id(0) == 0, [(wgm32_ref, wgm_ref), (wm032_ref, wm0_ref), (wpg32_ref, wpg_ref)])
    tm = x_ref.shape[0]
    xb = x_ref[...].astype(_BF)
    z = _gelu(_dot(xb, wgm_ref[...]))
    u = z[:, :D_MODEL]
    v = _layer_norm(z[:, D_MODEL:], lng_ref[...], lnb_ref[...]).astype(_BF)
    row = lax.broadcasted_iota(jnp.int32, (GM_CHUNK, GM_CHUNK), 0)
    col = lax.broadcasted_iota(jnp.int32, (GM_CHUNK, GM_CHUNK), 1)
    for gi in range(GM_GROUPS):
        w = jnp.where(row >= col, ws_ref[gi], 0.0).astype(_BF)
        for c in range(tm // GM_CHUNK):
            blk = v[c * GM_CHUNK:(c + 1) * GM_CHUNK, gi * LANES:(gi + 1) * LANES]
            vg_ref[c * GM_CHUNK:(c + 1) * GM_CHUNK, gi * LANES:(gi + 1) * LANES] = _dot(w, blk) + bs_ref[gi]
    y = (u * vg_ref[...]).astype(_BF)
    gate = _sigmoid(_dot(xb, wm0_ref[...]))
    o_ref[...] = gate * _dot(y, wpg_ref[...])


def _gm_mixer(x2, w_in, wm0, lng, lnb, ws, bs, wpg):
    T = x2.shape[0]
    tm = ROW_TILE
    const2 = lambda i: (0, 0)
    const3 = lambda i: (0, 0, 0)
    return pl.pallas_call(
        _gm_kernel,
        out_shape=jax.ShapeDtypeStruct((T, D_MODEL), _F32),
        grid=(T // tm,),
        in_specs=[
            pl.BlockSpec((tm, D_MODEL), lambda i: (i, 0)),
            _resident((D_MODEL, 2 * D_MODEL), const2),
            _resident((D_MODEL, D_MODEL), const2),
            pl.BlockSpec((1, D_MODEL), const2),
            pl.BlockSpec((1, D_MODEL), const2),
            pl.BlockSpec((GM_GROUPS, GM_CHUNK, GM_CHUNK), const3),
            pl.BlockSpec((GM_GROUPS, GM_CHUNK, LANES), const3),
            _resident((D_MODEL, D_MODEL), const2),
        ],
        out_specs=pl.BlockSpec((tm, D_MODEL), lambda i: (i, 0)),
        scratch_shapes=[pltpu.VMEM((tm, D_MODEL), _F32), pltpu.VMEM((D_MODEL, 2 * D_MODEL), _BF),
                        pltpu.VMEM((D_MODEL, D_MODEL), _BF), pltpu.VMEM((D_MODEL, D_MODEL), _BF)],
        compiler_params=pltpu.CompilerParams(
            dimension_semantics=("arbitrary",), vmem_limit_bytes=VMEM_LIMIT),
        name="gm_mixer",
    )(x2, w_in, wm0, lng, lnb, ws, bs, wpg)


def _qkv_kernel(x_ref, wq32_ref, wkv32_ref, wg32_ref, qt_ref, kc_ref, vc_ref, ksa_ref, vst_ref, kwa_ref, vwt_ref,
                gt_ref, wq_ref, wkv_ref, wg_ref):
    _stage_bf16((pl.program_id(0) == 0) & (pl.program_id(1) == 0),
                [(wq32_ref, wq_ref), (wkv32_ref, wkv_ref), (wg32_ref, wg_ref)])
    tm = x_ref.shape[0]
    xb = x_ref[...].astype(_BF)
    zq = _dot(xb, wq_ref[...]) * (HEAD_DIM ** -0.5 * LOG2_E)
    for tb in range(tm // Q_BLOCK):
        for h in range(N_KV_HEADS):
            for g in range(Q_PER_KV):
                c0 = (h * Q_PER_KV + g) * HEAD_DIM
                blk = zq[tb * Q_BLOCK:(tb + 1) * Q_BLOCK, c0:c0 + HEAD_DIM]
                qt_ref[tb, h, :, g * Q_BLOCK:(g + 1) * Q_BLOCK] = blk.T.astype(_BF)
    z = _dot(xb, wkv_ref[...])
    kpos = pl.program_id(1) * tm + lax.broadcasted_iota(jnp.int32, (tm, LANES), 0)
    blk_lane = lax.broadcasted_iota(jnp.int32, (tm, LANES), 1)
    sel_lanes = jnp.where(blk_lane == 0, (kpos % SEL_TILE).astype(_F32),
                          jnp.where(kpos // SEL_BLOCK == blk_lane, NEG_INF, 0.0)).astype(_BF)
    win_lanes = _position_lanes(kpos, (tm, LANES))
    ones = jnp.ones((VT_ROWS - HEAD_DIM, tm), _BF)
    for h in range(N_KV_HEADS):
        def col(j):
            return z[:, j * 2 * HEAD_DIM + h * HEAD_DIM: j * 2 * HEAD_DIM + (h + 1) * HEAD_DIM]
        kc_ref[0, h] = col(0)
        vc_ref[0, h] = col(1)
        ksa_ref[0, h, :, :HEAD_DIM] = col(2).astype(_BF)
        ksa_ref[0, h, :, HEAD_DIM:] = sel_lanes
        vst_ref[0, h, :HEAD_DIM] = col(3).T.astype(_BF)
        vst_ref[0, h, HEAD_DIM:] = ones
        kwa_ref[0, h, :, :HEAD_DIM] = col(4).astype(_BF)
        kwa_ref[0, h, :, HEAD_DIM:] = win_lanes
        vwt_ref[0, h, :HEAD_DIM] = col(5).T.astype(_BF)
        vwt_ref[0, h, HEAD_DIM:] = ones
    zg_t = _sigmoid(_dot(xb, wg_ref[...])).T
    per_head = 3 * Q_PER_KV
    for h in range(N_KV_HEADS):
        gt_ref[h] = zg_t[h * per_head:h * per_head + GATE_ROWS]


def _qkv_proj(x2, w_in, q_col, B, S):
    T = x2.shape[0]
    tm = ROW_TILE
    nsb = S // tm
    gd = Q_PER_KV * Q_BLOCK
    aw = N_KV_HEADS * Q_PER_KV * HEAD_DIM
    kvw = 6 * N_KV_HEADS * HEAD_DIM
    kv_col = q_col + aw
    gate_col = kv_col + kvw
    assert q_col % aw == 0 and kv_col % kvw == 0 and gate_col % LANES == 0
    rows_spec = lambda w: pl.BlockSpec((1, N_KV_HEADS, tm, w), lambda b, s: (b, 0, s, 0))
    rows_shape = lambda w, dt: jax.ShapeDtypeStruct((B, N_KV_HEADS, S, w), dt)
    cols_spec = pl.BlockSpec((1, N_KV_HEADS, VT_ROWS, tm), lambda b, s: (b, 0, 0, s))
    cols_shape = jax.ShapeDtypeStruct((B, N_KV_HEADS, VT_ROWS, S), _BF)
    return pl.pallas_call(
        _qkv_kernel,
        out_shape=(
            jax.ShapeDtypeStruct((T // Q_BLOCK, N_KV_HEADS, HEAD_DIM, gd), _BF),
            rows_shape(HEAD_DIM, _F32), rows_shape(HEAD_DIM, _F32),
            rows_shape(2 * HEAD_DIM, _BF), cols_shape,
            rows_shape(2 * HEAD_DIM, _BF), cols_shape,
            jax.ShapeDtypeStruct((N_KV_HEADS, GATE_ROWS, T), _F32),
        ),
        grid=(B, nsb),
        in_specs=[
            pl.BlockSpec((tm, D_MODEL), lambda b, s: (b * nsb + s, 0)),
            _resident((D_MODEL, aw), lambda b, s: (0, q_col // aw)),
            _resident((D_MODEL, kvw), lambda b, s: (0, kv_col // kvw)),
            _resident((D_MODEL, LANES), lambda b, s: (0, gate_col // LANES)),
        ],
        out_specs=(
            pl.BlockSpec((tm // Q_BLOCK, N_KV_HEADS, HEAD_DIM, gd), lambda b, s: (b * nsb + s, 0, 0, 0)),
            rows_spec(HEAD_DIM), rows_spec(HEAD_DIM), rows_spec(2 * HEAD_DIM), cols_spec,
            rows_spec(2 * HEAD_DIM), cols_spec,
            pl.BlockSpec((N_KV_HEADS, GATE_ROWS, tm), lambda b, s: (0, 0, b * nsb + s)),
        ),
        scratch_shapes=[pltpu.VMEM((D_MODEL, aw), _BF), pltpu.VMEM((D_MODEL, kvw), _BF),
                        pltpu.VMEM((D_MODEL, LANES), _BF)],
        compiler_params=pltpu.CompilerParams(
            dimension_semantics=("arbitrary", "arbitrary"), vmem_limit_bytes=VMEM_LIMIT),
        name="qkv_proj",
    )(x2, w_in, w_in, w_in)


def _compress_kernel(kc_ref, vc_ref, pek_ref, pev_ref, w1k_ref, w1v_ref, w2k_ref, w2v_ref, ko_ref, vo_ref):
    nc = ko_ref.shape[1]

    def tokens(src, pe, w1, w2):
        first, second = None, None
        for p in range(0, CMP_STRIDE, 2):
            rows = [src[0, pl.ds(p + d, nc, stride=CMP_STRIDE), :] for d in range(2)]
            lo = jnp.concatenate([rows[d] + pe[p + d:p + d + 1, :] for d in range(2)], axis=1)
            hi = jnp.concatenate([rows[d] + pe[CMP_STRIDE + p + d:CMP_STRIDE + p + d + 1, :] for d in range(2)], axis=1)
            a = _dot(lo.astype(_BF), w1[p // 2])
            b = _dot(hi.astype(_BF), w1[(CMP_STRIDE + p) // 2])
            first = a if first is None else first + a
            second = b if second is None else second + b
        pre = first + pltpu.roll(second, nc - 1, 0)
        return _dot(_gelu(pre).astype(_BF), w2[...])

    ko_ref[0, :, :HEAD_DIM] = tokens(kc_ref, pek_ref, w1k_ref, w2k_ref).astype(_BF)
    start = lax.broadcasted_iota(jnp.int32, (nc, LANES), 0) * CMP_STRIDE
    ko_ref[0, :, HEAD_DIM:] = _position_lanes(start, (nc, LANES))
    vo_ref[0] = tokens(vc_ref, pev_ref, w1v_ref, w2v_ref).T.astype(_BF)


def _nsa_compress(kc, vc, pek, pev, w1k, w1v, w2k, w2v):
    BH, S, _ = kc.shape
    nc = S // CMP_STRIDE
    pair = 2 * HEAD_DIM
    const2 = lambda i: (0, 0)
    const3 = lambda i: (0, 0, 0)
    seq_rows = pl.BlockSpec((1, S, HEAD_DIM), lambda i: (i, 0, 0))
    pe_spec = pl.BlockSpec((CMP_BLOCK, HEAD_DIM), const2)
    w1_spec = pl.BlockSpec((CMP_BLOCK // 2, pair, HEAD_DIM), const3)
    w2_spec = pl.BlockSpec((HEAD_DIM, HEAD_DIM), const2)
    return pl.pallas_call(
        _compress_kernel,
        out_shape=(jax.ShapeDtypeStruct((BH, nc, 2 * HEAD_DIM), _BF),
                   jax.ShapeDtypeStruct((BH, HEAD_DIM, nc), _BF)),
        grid=(BH,),
        in_specs=[seq_rows, seq_rows, pe_spec, pe_spec, w1_spec, w1_spec, w2_spec, w2_spec],
        out_specs=(pl.BlockSpec((1, nc, 2 * HEAD_DIM), lambda i: (i, 0, 0)),
                   pl.BlockSpec((1, HEAD_DIM, nc), lambda i: (i, 0, 0))),
        compiler_params=pltpu.CompilerParams(
            dimension_semantics=("arbitrary",), vmem_limit_bytes=VMEM_LIMIT),
        name="nsa_compress",
    )(kc, vc, pek, pev, w1k, w1v, w2k, w2v)


def _mark_top_blocks(score, notsel, rounds):
    rows = lax.broadcasted_iota(jnp.int32, score.shape, 0).astype(_F32)
    for _ in range(rounds):
        mx = jnp.max(score, axis=0, keepdims=True)
        idx = jnp.min(jnp.where(score == mx, rows, float(LANES)), axis=0, keepdims=True)
        hit = rows == idx
        notsel = jnp.where(hit, 0.0, notsel)
        score = jnp.where(hit, -jnp.inf, score)
    return notsel


def _attn_kernel(qt_ref, gt_ref, kca_ref, vct_ref, ksa_ref, vst_ref, kwa_ref, vwt_ref, o_ref,
                 m_ref, acc_ref, part_ref, bits_ref, *, seq):
    h = pl.program_id(1)
    qb = pl.program_id(2)
    q0 = qb * Q_BLOCK
    nc = kca_ref.shape[1]
    n_sel = seq // SEL_BLOCK
    n_forced = 1 + N_LOCAL_FORCED
    k_top = min(N_SELECT, n_sel)
    G = Q_PER_KV
    W = G * Q_BLOCK

    lane_w = lax.broadcasted_iota(jnp.int32, (1, W), 1)
    slope = jnp.zeros((1, W), _F32)
    for g in range(G):
        sg = jnp.where(h == 0, _F32(2.0 ** -(g + 1)), _F32(2.0 ** -(G + g + 1)))
        slope = jnp.where(lane_w // Q_BLOCK == g, sg, slope)
    slope = (slope * LOG2_E).astype(_BF).astype(_F32)
    tok = q0 + lane_w % Q_BLOCK
    strips = [slice(g * Q_BLOCK, (g + 1) * Q_BLOCK) for g in range(G)]

    qt = qt_ref[0, 0]
    row_aug = lax.broadcasted_iota(jnp.int32, (LANES, W), 0)
    pos_rows = jnp.where(row_aug == 0, slope, jnp.where(row_aug == 1, slope * POS_RADIX, 0.0)).astype(_BF)
    lhs_pos = jnp.concatenate([qt, pos_rows], axis=0)

    s = _dot(kca_ref[0], lhs_pos)
    n_row = lax.broadcasted_iota(jnp.int32, (nc, Q_BLOCK), 0)
    n_last = (tok - (CMP_BLOCK - 1)) // CMP_STRIDE
    e_parts, inv_parts = [], []
    psum = None
    for cs in strips:
        sg = jnp.where(n_row <= n_last[:, cs], s[:, cs], NEG_INF)
        eg = jnp.exp2(sg - jnp.max(sg, axis=0, keepdims=True))
        inv = jnp.where(n_last[:, cs] >= 0, 1.0 / jnp.sum(eg, axis=0, keepdims=True), 0.0)
        pg = eg * inv
        psum = pg if psum is None else psum + pg
        e_parts.append(eg.astype(_BF))
        inv_parts.append(inv)
    o_cmp = _dot(vct_ref[0], jnp.concatenate(e_parts, axis=1)) * jnp.concatenate(inv_parts, axis=1)

    mi = lax.broadcasted_iota(jnp.int32, (LANES, nc), 0)
    ni = lax.broadcasted_iota(jnp.int32, (LANES, nc), 1)
    overlap_t = jnp.where((ni * CMP_STRIDE + (CMP_BLOCK - 1) >= mi * SEL_BLOCK)
                          & (ni * CMP_STRIDE <= mi * SEL_BLOCK + (SEL_BLOCK - 1)), 1.0, 0.0).astype(_BF)
    p_hi = psum.astype(_BF)
    p_lo = (psum - p_hi.astype(_F32)).astype(_BF)
    imp_t = _dot(overlap_t, p_hi) + _dot(overlap_t, p_lo)

    anchor = jnp.concatenate([jnp.minimum(imp_t[0:1], 0.0)] * G, axis=1)
    win_rows = jnp.where(row_aug == 0, slope + anchor, jnp.where(row_aug == 1, slope * POS_RADIX, 0.0)).astype(_BF)
    wlen = WINDOW + Q_BLOCK
    w0 = pl.multiple_of(jnp.maximum(q0 - WINDOW, 0), Q_BLOCK)
    sw = _dot(kwa_ref[0, 0, pl.ds(w0, wlen), :], jnp.concatenate([qt, win_rows], axis=0))
    w_row = w0 + lax.broadcasted_iota(jnp.int32, (wlen, Q_BLOCK), 0)
    e_parts = []
    for cs in strips:
        dist = tok[:, cs] - w_row
        sg = jnp.where(dist.astype(jnp.uint32) < WINDOW, sw[:, cs], NEG_INF)
        e_parts.append(jnp.exp2(sg - jnp.max(sg, axis=0, keepdims=True)).astype(_BF))
    win = _dot(vwt_ref[0, 0, :, pl.ds(w0, wlen)], jnp.concatenate(e_parts, axis=1))
    o_win = win[:HEAD_DIM] * (1.0 / win[HEAD_DIM:HEAD_DIM + 1])

    gates = gt_ref[0]

    def gate_row(branch):
        return jnp.concatenate([gates[3 * g + branch:3 * g + branch + 1] for g in range(G)], axis=1)

    part_ref[...] = gate_row(0) * o_cmp + gate_row(2) * o_win
    mrow = lax.broadcasted_iota(jnp.int32, (LANES, Q_BLOCK), 0)
    tcol = lax.broadcasted_iota(jnp.int32, (LANES, Q_BLOCK), 1) + q0
    lag = tcol // SEL_BLOCK - mrow
    forced = (mrow == 0) | ((lag >= 0) & (lag < N_LOCAL_FORCED))
    score = jnp.where(forced | (lag < 0) | (mrow >= n_sel), -jnp.inf, imp_t)
    notsel_t = _mark_top_blocks(score, jnp.where(forced, 0.0, 1.0), k_top - n_forced)

    sel_rows = jnp.where(row_aug == 0, slope, jnp.concatenate([notsel_t] * G, axis=1)).astype(_BF)
    lhs_sel = jnp.concatenate([qt, sel_rows], axis=0)

    key_row = lax.broadcasted_iota(jnp.int32, (SEL_TILE, Q_BLOCK), 0)
    diag = q0 // SEL_TILE

    def tile_scores(j, causal, live=None):
        k0 = j * SEL_TILE if isinstance(j, int) else pl.multiple_of(j * SEL_TILE, SEL_TILE)
        st = _dot(ksa_ref[0, 0, pl.ds(k0, SEL_TILE), :], lhs_sel)
        if causal:
            st = jnp.concatenate([jnp.where(k0 + key_row > tok[:, cs], NEG_INF, st[:, cs]) for cs in strips], axis=1)
        shift = slope * (q0 - k0).astype(_F32)
        if live is not None:
            shift = jnp.where(live, shift, -NEG_INF)
        return st, shift, k0

    def tile_max(tiles):
        m = None
        for st, shift, _ in tiles:
            cm = jnp.max(st, axis=0, keepdims=True) - shift
            m = cm if m is None else jnp.maximum(m, cm)
        return m

    def tile_sums(tiles, m):
        acc = None
        for st, shift, k0 in tiles:
            sub = m + shift
            pt = jnp.concatenate([jnp.exp2(st[:, cs] - sub[:, cs]) for cs in strips], axis=1)
            ai = _dot(vst_ref[0, 0, :, pl.ds(k0, SEL_TILE)], pt.astype(_BF))
            acc = ai if acc is None else acc + ai
        return acc

    first_local = diag - (LOCAL_TILES - 1)
    tiles = [tile_scores(0, False, live=first_local > 0)]
    for i in range(LOCAL_TILES):
        j = first_local + i
        last = i == LOCAL_TILES - 1
        tiles.append(tile_scores(jnp.maximum(j, 0), last, live=None if last else j >= 0))
    m = tile_max(tiles)
    m_ref[0:1] = m
    acc_ref[...] = tile_sums(tiles, m)

    blk_on = jnp.where(jnp.min(notsel_t, axis=1, keepdims=True) < 0.5, 1.0, 0.0)
    blk_bit = jnp.left_shift(1, lax.broadcasted_iota(jnp.int32, (LANES, 1), 0) % BITS_PER_WORD).astype(_F32)
    for k in range(LANES // BITS_PER_WORD):
        word = jnp.sum((blk_on * blk_bit)[k * BITS_PER_WORD:(k + 1) * BITS_PER_WORD])
        bits_ref[k] = word.astype(jnp.int32)

    blocks_per_tile = SEL_TILE // SEL_BLOCK
    tiles_per_word = BITS_PER_WORD // blocks_per_tile

    def far_tile(j, carry):
        tile_bits = (bits_ref[j // tiles_per_word] >> ((j % tiles_per_word) * blocks_per_tile)) & ((1 << blocks_per_tile) - 1)

        @pl.when(tile_bits != 0)
        def _():
            tile = [tile_scores(j, False)]
            m_old = m_ref[0:1]
            m_new = jnp.maximum(m_old, tile_max(tile))
            m_ref[0:1] = m_new
            acc_ref[...] = jnp.exp2(m_old - m_new) * acc_ref[...] + tile_sums(tile, m_new)
        return carry

    lax.fori_loop(1, first_local, far_tile, 0)

    out_t = part_ref[...] + acc_ref[0:HEAD_DIM] * (gate_row(1) * (1.0 / acc_ref[HEAD_DIM:HEAD_DIM + 1]))
    for g, cs in enumerate(strips):
        o_ref[:, g * HEAD_DIM:(g + 1) * HEAD_DIM] = out_t[:, cs].T.astype(_BF)


def _nsa_attention(qt, gt, kca, vct, ksa, vst, kwa, vwt, B, S):
    nqb = S // Q_BLOCK
    T = B * S
    nc = kca.shape[1]
    gd = Q_PER_KV * HEAD_DIM
    W = Q_PER_KV * Q_BLOCK
    rows = pl.BlockSpec((1, 1, S, 2 * HEAD_DIM), lambda b, h, i: (b, h, 0, 0))
    cols = pl.BlockSpec((1, 1, VT_ROWS, S), lambda b, h, i: (b, h, 0, 0))
    return pl.pallas_call(
        functools.partial(_attn_kernel, seq=S),
        out_shape=jax.ShapeDtypeStruct((T, N_KV_HEADS * gd), _BF),
        grid=(B, N_KV_HEADS, nqb),
        in_specs=[
            pl.BlockSpec((1, 1, HEAD_DIM, W), lambda b, h, i: (b * nqb + i, h, 0, 0)),
            pl.BlockSpec((1, GATE_ROWS, Q_BLOCK), lambda b, h, i: (h, 0, b * nqb + i)),
            pl.BlockSpec((1, nc, 2 * HEAD_DIM), lambda b, h, i: (b * N_KV_HEADS + h, 0, 0)),
            pl.BlockSpec((1, HEAD_DIM, nc), lambda b, h, i: (b * N_KV_HEADS + h, 0, 0)),
            rows, cols, rows, cols,
        ],
        out_specs=pl.BlockSpec((Q_BLOCK, gd), lambda b, h, i: (b * nqb + i, h)),
        scratch_shapes=[pltpu.VMEM((8, W), _F32), pltpu.VMEM((VT_ROWS, W), _F32),
                        pltpu.VMEM((HEAD_DIM, W), _F32), pltpu.SMEM((LANES // BITS_PER_WORD,), jnp.int32)],
        compiler_params=pltpu.CompilerParams(
            dimension_semantics=("arbitrary", "arbitrary", "arbitrary"), vmem_limit_bytes=VMEM_LIMIT),
        name="nsa_attention",
    )(qt, gt, kca, vct, ksa, vst, kwa, vwt)


def _merge_ffn_kernel(x_ref, gm_ref, yn_ref, wm1_ref, wpn_ref, wo_ref, g1_ref, b1_ref,
                      w1_ref, w2_ref, g2_ref, b2_ref, o_ref):
    x = x_ref[...]
    gate = _sigmoid(_dot(x.astype(_BF), wm1_ref[...]))
    merged = gm_ref[...] + gate * _dot(yn_ref[...], wpn_ref[...])
    mix = _dot(merged.astype(_BF), wo_ref[...])
    hid = _layer_norm(DEEPNORM_ALPHA * x + mix, g1_ref[...], b1_ref[...])
    hb = hid.astype(_BF)
    f = jnp.zeros(hid.shape, _F32)
    for c in range(D_FF // D_MODEL):
        a = jnp.maximum(_dot(hb, w1_ref[:, c * D_MODEL:(c + 1) * D_MODEL]), 0.0)
        f = f + _dot((a * a).astype(_BF), w2_ref[c * D_MODEL:(c + 1) * D_MODEL, :])
    o_ref[...] = _layer_norm(DEEPNORM_ALPHA * hid + f, g2_ref[...], b2_ref[...])


def _merge_ffn(x2, gm, yn, wm1, wpn, wo, g1, b1, w1, w2, g2, b2):
    T = x2.shape[0]
    tm = ROW_TILE
    rows = lambda w: pl.BlockSpec((tm, w), lambda i: (i, 0))
    const = lambda r, c: pl.BlockSpec((r, c), lambda i: (0, 0), pipeline_mode=pl.Buffered(1))
    return pl.pallas_call(
        _merge_ffn_kernel,
        out_shape=jax.ShapeDtypeStruct((T, D_MODEL), _F32),
        grid=(T // tm,),
        in_specs=[rows(D_MODEL), rows(D_MODEL), rows(D_MODEL),
                  const(D_MODEL, D_MODEL), const(D_MODEL, D_MODEL), const(D_MODEL, D_MODEL),
                  const(1, D_MODEL), const(1, D_MODEL),
                  const(D_MODEL, D_FF), const(D_FF, D_MODEL),
                  const(1, D_MODEL), const(1, D_MODEL)],
        out_specs=rows(D_MODEL),
        compiler_params=pltpu.CompilerParams(
            dimension_semantics=("arbitrary",), vmem_limit_bytes=VMEM_LIMIT),
        name="merge_ffn",
    )(x2, gm, yn, wm1, wpn, wo, g1, b1, w1, w2, g2, b2)


def _layer(x2, B, S, w_in, gm_ln_g, gm_ln_b, gm_w_s, gm_b_s, cmp_pe_k, cmp_w1_k, cmp_w2_k,
           cmp_pe_v, cmp_w1_v, cmp_w2_v, w_proj_gm, w_proj_nsa, w_out,
           ln1_g, ln1_b, w_ff1, w_ff2, ln2_g, ln2_b):
    o_q = 2 * D_MODEL
    o_m = o_q + (Q_PER_KV + 6) * N_KV_HEADS * HEAD_DIM + 3 * Q_PER_KV * N_KV_HEADS
    row = lambda v: v.reshape(1, -1)

    gm = _gm_mixer(x2, w_in, w_in[:, o_m:o_m + D_MODEL], row(gm_ln_g), row(gm_ln_b), gm_w_s,
                   jnp.broadcast_to(gm_b_s[:, :, None], (GM_GROUPS, GM_CHUNK, LANES)), w_proj_gm)

    qt, kc, vc, ksa, vst, kwa, vwt, gt = _qkv_proj(x2, w_in, o_q, B, S)

    heads = lambda a: a.reshape(B * N_KV_HEADS, S, HEAD_DIM)
    pairs = lambda w: w.astype(_BF).reshape(CMP_BLOCK // 2, 2 * HEAD_DIM, HEAD_DIM)
    kca, vct = _nsa_compress(heads(kc), heads(vc), cmp_pe_k, cmp_pe_v, pairs(cmp_w1_k), pairs(cmp_w1_v),
                             cmp_w2_k.astype(_BF), cmp_w2_v.astype(_BF))

    yn = _nsa_attention(qt, gt, kca, vct, ksa, vst, kwa, vwt, B, S)

    return _merge_ffn(x2, gm, yn, w_in[:, o_m + D_MODEL:o_m + 2 * D_MODEL].astype(_BF), w_proj_nsa.astype(_BF),
                      w_out.astype(_BF), row(ln1_g), row(ln1_b), w_ff1.astype(_BF), w_ff2.astype(_BF),
                      row(ln2_g), row(ln2_b))


def kernel(x, w_in, gm_ln_g, gm_ln_b, gm_w_s, gm_b_s, cmp_pe_k, cmp_w1_k, cmp_w2_k, cmp_pe_v, cmp_w1_v, cmp_w2_v, w_proj_gm, w_proj_nsa, w_out, ln1_g, ln1_b, w_ff1, w_ff2, ln2_g, ln2_b):
    B, S, D = x.shape
    assert D == D_MODEL and S % ROW_TILE == 0 and WINDOW + Q_BLOCK <= S <= SEL_BLOCK * LANES
    assert S // POS_RADIX <= POS_RADIX and S // SEL_TILE <= 32
    h = x.reshape(B * S, D)
    for l in range(w_in.shape[0]):
        h = _layer(h, B, S, w_in[l], gm_ln_g[l], gm_ln_b[l], gm_w_s[l], gm_b_s[l],
                   cmp_pe_k[l], cmp_w1_k[l], cmp_w2_k[l], cmp_pe_v[l], cmp_w1_v[l], cmp_w2_v[l],
                   w_proj_gm[l], w_proj_nsa[l], w_out[l], ln1_g[l], ln1_b[l],
                   w_ff1[l], w_ff2[l], ln2_g[l], ln2_b[l])
    return h.reshape(B, S, D)
```

```python
import functools
import math

import jax
import jax.numpy as jnp
from jax import lax
from jax.experimental import pallas as pl
from jax.experimental.pallas import tpu as pltpu

D_MODEL = 1024
GM_GROUPS = 8
GM_CHUNK = 128
N_KV_HEADS = 2
Q_PER_KV = 4
HEAD_DIM = 128
CMP_BLOCK = 32
CMP_STRIDE = 16
SEL_BLOCK = 64
N_SELECT = 16
N_LOCAL_FORCED = 2
WINDOW = 512
Q_BLOCK = 128
D_FF = 4 * D_MODEL
DEEPNORM_ALPHA = 2.0 ** 0.25
LN_EPS = 1e-5
NEG_INF = -1e30
LOG2_E = math.log2(math.e)

LANES = 128
SEL_TILE = 256
LOCAL_TILES = 6
BITS_PER_WORD = 16
POS_RADIX = 256
ROW_TILE = 512
VT_ROWS = HEAD_DIM + 16
GATE_ROWS = 16
VMEM_LIMIT = 56 * 1024 * 1024

_BF = jnp.bfloat16
_F32 = jnp.float32
_NT = (((1,), (1,)), ((), ()))


def _dot(a, b):
    return jnp.dot(a, b, preferred_element_type=_F32)


def _dot_nt(a, b):
    return lax.dot_general(a, b, _NT, preferred_element_type=_F32)


def _gelu(x):
    c = math.sqrt(2.0 / math.pi)
    return 0.5 * x * (1.0 + jnp.tanh(c * (x + 0.044715 * (x * x * x))))


def _sigmoid(x):
    return 1.0 / (1.0 + jnp.exp(-x))


def _layer_norm(x, g, b):
    mu = jnp.mean(x, axis=-1, keepdims=True)
    xc = x - mu
    var = jnp.mean(xc * xc, axis=-1, keepdims=True)
    return xc * lax.rsqrt(var + LN_EPS) * g + b


def _position_lanes(pos, shape):
    lane = lax.broadcasted_iota(jnp.int32, shape, 1)
    return jnp.where(lane == 0, (pos % POS_RADIX).astype(_F32),
                     jnp.where(lane == 1, (pos // POS_RADIX).astype(_F32), 0.0)).astype(_BF)


def _stage_bf16(first_step, pairs):
    @pl.when(first_step)
    def _():
        for src, dst in pairs:
            dst[...] = src[...].astype(_BF)


def _resident(shape, index_map):
    return pl.BlockSpec(shape, index_map, pipeline_mode=pl.Buffered(1))


def _gm_kernel(x_ref, wgm32_ref, wm032_ref, lng_ref, lnb_ref, ws_ref, bs_ref, wpg32_ref, o_ref,
               vg_ref, wgm_ref, wm0_ref, wpg_ref):
    _stage_bf16(pl.program_id(0) == 0, [(wgm32_ref, wgm_ref), (wm032_ref, wm0_ref), (wpg32_ref, wpg_ref)])
    tm = x_ref.shape[0]
    xb = x_ref[...].astype(_BF)
    z = _gelu(_dot(xb, wgm_ref[...]))
    u = z[:, :D_MODEL]
    v = _layer_norm(z[:, D_MODEL:], lng_ref[...], lnb_ref[...]).astype(_BF)
    row = lax.broadcasted_iota(jnp.int32, (GM_CHUNK, GM_CHUNK), 0)
    col = lax.broadcasted_iota(jnp.int32, (GM_CHUNK, GM_CHUNK), 1)
    for gi in range(GM_GROUPS):
        w = jnp.where(row >= col, ws_ref[gi], 0.0).astype(_BF)
        for c in range(tm // GM_CHUNK):
            blk = v[c * GM_CHUNK:(c + 1) * GM_CHUNK, gi * LANES:(gi + 1) * LANES]
            vg_ref[c * GM_CHUNK:(c + 1) * GM_CHUNK, gi * LANES:(gi + 1) * LANES] = _dot(w, blk) + bs_ref[gi]
    y = (u * vg_ref[...]).astype(_BF)
    gate = _sigmoid(_dot(xb, wm0_ref[...]))
    o_ref[...] = gate * _dot(y, wpg_ref[...])


def _gm_mixer(x2, w_in, layer, wm0, lng, lnb, ws, bs, wpg):
    T = x2.shape[0]
    tm = ROW_TILE
    const2 = lambda i: (0, 0)
    const3 = lambda i: (0, 0, 0)
    return pl.pallas_call(
        _gm_kernel,
        out_shape=jax.ShapeDtypeStruct((T, D_MODEL), _F32),
        grid=(T // tm,),
        in_specs=[
            pl.BlockSpec((tm, D_MODEL), lambda i: (i, 0)),
            _resident((D_MODEL, 2 * D_MODEL), lambda i: (layer, 0)),
            _resident((D_MODEL, D_MODEL), const2),
            pl.BlockSpec((1, D_MODEL), const2),
            pl.BlockSpec((1, D_MODEL), const2),
            pl.BlockSpec((GM_GROUPS, GM_CHUNK, GM_CHUNK), const3),
            pl.BlockSpec((GM_GROUPS, GM_CHUNK, LANES), const3),
            _resident((D_MODEL, D_MODEL), const2),
        ],
        out_specs=pl.BlockSpec((tm, D_MODEL), lambda i: (i, 0)),
        scratch_shapes=[pltpu.VMEM((tm, D_MODEL), _F32), pltpu.VMEM((D_MODEL, 2 * D_MODEL), _BF),
                        pltpu.VMEM((D_MODEL, D_MODEL), _BF), pltpu.VMEM((D_MODEL, D_MODEL), _BF)],
        compiler_params=pltpu.CompilerParams(
            dimension_semantics=("arbitrary",), vmem_limit_bytes=VMEM_LIMIT),
        name="gm_mixer",
    )(x2, w_in, wm0, lng, lnb, ws, bs, wpg)


def _qkv_kernel(x_ref, wq32_ref, wkv32_ref, wg32_ref, qt_ref, kc_ref, vc_ref, ksa_ref, vst_ref, kwa_ref, vwt_ref,
                gt_ref, wq_ref, wkv_ref, wg_ref):
    _stage_bf16((pl.program_id(0) == 0) & (pl.program_id(1) == 0),
                [(wq32_ref, wq_ref), (wkv32_ref, wkv_ref), (wg32_ref, wg_ref)])
    tm = x_ref.shape[0]
    xb = x_ref[...].astype(_BF)
    zq = _dot(xb, wq_ref[...]) * (HEAD_DIM ** -0.5 * LOG2_E)
    for tb in range(tm // Q_BLOCK):
        for h in range(N_KV_HEADS):
            for g in range(Q_PER_KV):
                c0 = (h * Q_PER_KV + g) * HEAD_DIM
                blk = zq[tb * Q_BLOCK:(tb + 1) * Q_BLOCK, c0:c0 + HEAD_DIM]
                qt_ref[tb, h, :, g * Q_BLOCK:(g + 1) * Q_BLOCK] = blk.T.astype(_BF)
    z = _dot(xb, wkv_ref[...])
    kpos = pl.program_id(1) * tm + lax.broadcasted_iota(jnp.int32, (tm, LANES), 0)
    blk_lane = lax.broadcasted_iota(jnp.int32, (tm, LANES), 1)
    sel_lanes = jnp.where(blk_lane == 0, (kpos % SEL_TILE).astype(_F32),
                          jnp.where(kpos // SEL_BLOCK == blk_lane, NEG_INF, 0.0)).astype(_BF)
    win_lanes = _position_lanes(kpos, (tm, LANES))
    ones = jnp.ones((VT_ROWS - HEAD_DIM, tm), _BF)
    for h in range(N_KV_HEADS):
        def col(j):
            return z[:, j * 2 * HEAD_DIM + h * HEAD_DIM: j * 2 * HEAD_DIM + (h + 1) * HEAD_DIM]
        kc_ref[0, h] = col(0)
        vc_ref[0, h] = col(1)
        ksa_ref[0, h, :, :HEAD_DIM] = col(2).astype(_BF)
        ksa_ref[0, h, :, HEAD_DIM:] = sel_lanes
        vst_ref[0, h, :HEAD_DIM] = col(3).T.astype(_BF)
        vst_ref[0, h, HEAD_DIM:] = ones
        kwa_ref[0, h, :, :HEAD_DIM] = col(4).astype(_BF)
        kwa_ref[0, h, :, HEAD_DIM:] = win_lanes
        vwt_ref[0, h, :HEAD_DIM] = col(5).T.astype(_BF)
        vwt_ref[0, h, HEAD_DIM:] = ones
    zg_t = _sigmoid(_dot(xb, wg_ref[...])).T
    per_head = 3 * Q_PER_KV
    for h in range(N_KV_HEADS):
        gt_ref[h] = zg_t[h * per_head:h * per_head + GATE_ROWS]


def _qkv_proj(x2, w_in, layer, q_col, B, S):
    T = x2.shape[0]
    tm = ROW_TILE
    nsb = S // tm
    gd = Q_PER_KV * Q_BLOCK
    aw = N_KV_HEADS * Q_PER_KV * HEAD_DIM
    kvw = 6 * N_KV_HEADS * HEAD_DIM
    kv_col = q_col + aw
    gate_col = kv_col + kvw
    assert q_col % aw == 0 and kv_col % kvw == 0 and gate_col % LANES == 0
    rows_spec = lambda w: pl.BlockSpec((1, N_KV_HEADS, tm, w), lambda b, s: (b, 0, s, 0))
    rows_shape = lambda w, dt: jax.ShapeDtypeStruct((B, N_KV_HEADS, S, w), dt)
    cols_spec = pl.BlockSpec((1, N_KV_HEADS, VT_ROWS, tm), lambda b, s: (b, 0, 0, s))
    cols_shape = jax.ShapeDtypeStruct((B, N_KV_HEADS, VT_ROWS, S), _BF)
    return pl.pallas_call(
        _qkv_kernel,
        out_shape=(
            jax.ShapeDtypeStruct((T // Q_BLOCK, N_KV_HEADS, HEAD_DIM, gd), _BF),
            rows_shape(HEAD_DIM, _F32), rows_shape(HEAD_DIM, _F32),
            rows_shape(2 * HEAD_DIM, _BF), cols_shape,
            rows_shape(2 * HEAD_DIM, _BF), cols_shape,
            jax.ShapeDtypeStruct((N_KV_HEADS, GATE_ROWS, T), _F32),
        ),
        grid=(B, nsb),
        in_specs=[
            pl.BlockSpec((tm, D_MODEL), lambda b, s: (b * nsb + s, 0)),
            _resident((D_MODEL, aw), lambda b, s: (layer, q_col // aw)),
            _resident((D_MODEL, kvw), lambda b, s: (layer, kv_col // kvw)),
            _resident((D_MODEL, LANES), lambda b, s: (layer, gate_col // LANES)),
        ],
        out_specs=(
            pl.BlockSpec((tm // Q_BLOCK, N_KV_HEADS, HEAD_DIM, gd), lambda b, s: (b * nsb + s, 0, 0, 0)),
            rows_spec(HEAD_DIM), rows_spec(HEAD_DIM), rows_spec(2 * HEAD_DIM), cols_spec,
            rows_spec(2 * HEAD_DIM), cols_spec,
            pl.BlockSpec((N_KV_HEADS, GATE_ROWS, tm), lambda b, s: (0, 0, b * nsb + s)),
        ),
        scratch_shapes=[pltpu.VMEM((D_MODEL, aw), _BF), pltpu.VMEM((D_MODEL, kvw), _BF),
                        pltpu.VMEM((D_MODEL, LANES), _BF)],
        compiler_params=pltpu.CompilerParams(
            dimension_semantics=("arbitrary", "arbitrary"), vmem_limit_bytes=VMEM_LIMIT),
        name="qkv_proj",
    )(x2, w_in, w_in, w_in)


def _compress_kernel(kc_ref, vc_ref, pek_ref, pev_ref, w1k_ref, w1v_ref, w2k_ref, w2v_ref, ko_ref, vo_ref):
    nc = ko_ref.shape[1]

    def tokens(src, pe, w1, w2):
        first, second = None, None
        for p in range(0, CMP_STRIDE, 2):
            rows = [src[0, pl.ds(p + d, nc, stride=CMP_STRIDE), :] for d in range(2)]
            lo = jnp.concatenate([rows[d] + pe[p + d:p + d + 1, :] for d in range(2)], axis=1)
            hi = jnp.concatenate([rows[d] + pe[CMP_STRIDE + p + d:CMP_STRIDE + p + d + 1, :] for d in range(2)], axis=1)
            a = _dot(lo.astype(_BF), w1[p // 2])
            b = _dot(hi.astype(_BF), w1[(CMP_STRIDE + p) // 2])
            first = a if first is None else first + a
            second = b if second is None else second + b
        pre = first + pltpu.roll(second, nc - 1, 0)
        return _dot(_gelu(pre).astype(_BF), w2[...])

    ko_ref[0, :, :HEAD_DIM] = tokens(kc_ref, pek_ref, w1k_ref, w2k_ref).astype(_BF)
    start = lax.broadcasted_iota(jnp.int32, (nc, LANES), 0) * CMP_STRIDE
    ko_ref[0, :, HEAD_DIM:] = _position_lanes(start, (nc, LANES))
    vo_ref[0] = tokens(vc_ref, pev_ref, w1v_ref, w2v_ref).T.astype(_BF)


def _nsa_compress(kc, vc, pek, pev, w1k, w1v, w2k, w2v):
    BH, S, _ = kc.shape
    nc = S // CMP_STRIDE
    pair = 2 * HEAD_DIM
    const2 = lambda i: (0, 0)
    const3 = lambda i: (0, 0, 0)
    seq_rows = pl.BlockSpec((1, S, HEAD_DIM), lambda i: (i, 0, 0))
    pe_spec = pl.BlockSpec((CMP_BLOCK, HEAD_DIM), const2)
    w1_spec = pl.BlockSpec((CMP_BLOCK // 2, pair, HEAD_DIM), const3)
    w2_spec = pl.BlockSpec((HEAD_DIM, HEAD_DIM), const2)
    return pl.pallas_call(
        _compress_kernel,
        out_shape=(jax.ShapeDtypeStruct((BH, nc, 2 * HEAD_DIM), _BF),
                   jax.ShapeDtypeStruct((BH, HEAD_DIM, nc), _BF)),
        grid=(BH,),
        in_specs=[seq_rows, seq_rows, pe_spec, pe_spec, w1_spec, w1_spec, w2_spec, w2_spec],
        out_specs=(pl.BlockSpec((1, nc, 2 * HEAD_DIM), lambda i: (i, 0, 0)),
                   pl.BlockSpec((1, HEAD_DIM, nc), lambda i: (i, 0, 0))),
        compiler_params=pltpu.CompilerParams(
            dimension_semantics=("arbitrary",), vmem_limit_bytes=VMEM_LIMIT),
        name="nsa_compress",
    )(kc, vc, pek, pev, w1k, w1v, w2k, w2v)


def _mark_top_blocks(score, notsel, rounds):
    rows = lax.broadcasted_iota(jnp.int32, score.shape, 0).astype(_F32)
    for _ in range(rounds):
        mx = jnp.max(score, axis=0, keepdims=True)
        idx = jnp.min(jnp.where(score == mx, rows, float(LANES)), axis=0, keepdims=True)
        hit = rows == idx
        notsel = jnp.where(hit, 0.0, notsel)
        score = jnp.where(hit, -jnp.inf, score)
    return notsel


def _attn_kernel(qt_ref, gt_ref, kca_ref, vct_ref, ksa_ref, vst_ref, kwa_ref, vwt_ref, o_ref,
                 m_ref, acc_ref, part_ref, bits_ref, *, seq):
    h = pl.program_id(1)
    qb = pl.program_id(2)
    q0 = qb * Q_BLOCK
    nc = kca_ref.shape[1]
    n_sel = seq // SEL_BLOCK
    n_forced = 1 + N_LOCAL_FORCED
    k_top = min(N_SELECT, n_sel)
    G = Q_PER_KV
    W = G * Q_BLOCK

    lane_w = lax.broadcasted_iota(jnp.int32, (1, W), 1)
    slope = jnp.zeros((1, W), _F32)
    for g in range(G):
        sg = jnp.where(h == 0, _F32(2.0 ** -(g + 1)), _F32(2.0 ** -(G + g + 1)))
        slope = jnp.where(lane_w // Q_BLOCK == g, sg, slope)
    slope = (slope * LOG2_E).astype(_BF).astype(_F32)
    tok = q0 + lane_w % Q_BLOCK
    strips = [slice(g * Q_BLOCK, (g + 1) * Q_BLOCK) for g in range(G)]

    qt = qt_ref[0, 0]
    row_aug = lax.broadcasted_iota(jnp.int32, (LANES, W), 0)
    pos_rows = jnp.where(row_aug == 0, slope, jnp.where(row_aug == 1, slope * POS_RADIX, 0.0)).astype(_BF)
    lhs_pos = jnp.concatenate([qt, pos_rows], axis=0)

    s = _dot(kca_ref[0], lhs_pos)
    n_row = lax.broadcasted_iota(jnp.int32, (nc, Q_BLOCK), 0)
    n_last = (tok - (CMP_BLOCK - 1)) // CMP_STRIDE
    e_parts, inv_parts = [], []
    psum = None
    for cs in strips:
        sg = jnp.where(n_row <= n_last[:, cs], s[:, cs], NEG_INF)
        eg = jnp.exp2(sg - jnp.max(sg, axis=0, keepdims=True))
        inv = jnp.where(n_last[:, cs] >= 0, 1.0 / jnp.sum(eg, axis=0, keepdims=True), 0.0)
        pg = eg * inv
        psum = pg if psum is None else psum + pg
        e_parts.append(eg.astype(_BF))
        inv_parts.append(inv)
    o_cmp = _dot(vct_ref[0], jnp.concatenate(e_parts, axis=1)) * jnp.concatenate(inv_parts, axis=1)

    mi = lax.broadcasted_iota(jnp.int32, (LANES, nc), 0)
    ni = lax.broadcasted_iota(jnp.int32, (LANES, nc), 1)
    overlap_t = jnp.where((ni * CMP_STRIDE + (CMP_BLOCK - 1) >= mi * SEL_BLOCK)
                          & (ni * CMP_STRIDE <= mi * SEL_BLOCK + (SEL_BLOCK - 1)), 1.0, 0.0).astype(_BF)
    p_hi = psum.astype(_BF)
    p_lo = (psum - p_hi.astype(_F32)).astype(_BF)
    imp_t = _dot(overlap_t, p_hi) + _dot(overlap_t, p_lo)

    anchor = jnp.concatenate([jnp.minimum(imp_t[0:1], 0.0)] * G, axis=1)
    win_rows = jnp.where(row_aug == 0, slope + anchor, jnp.where(row_aug == 1, slope * POS_RADIX, 0.0)).astype(_BF)
    wlen = WINDOW + Q_BLOCK
    w0 = pl.multiple_of(jnp.maximum(q0 - WINDOW, 0), Q_BLOCK)
    sw = _dot(kwa_ref[0, 0, pl.ds(w0, wlen), :], jnp.concatenate([qt, win_rows], axis=0))
    w_row = w0 + lax.broadcasted_iota(jnp.int32, (wlen, Q_BLOCK), 0)
    e_parts = []
    for cs in strips:
        dist = tok[:, cs] - w_row
        sg = jnp.where(dist.astype(jnp.uint32) < WINDOW, sw[:, cs], NEG_INF)
        e_parts.append(jnp.exp2(sg - jnp.max(sg, axis=0, keepdims=True)).astype(_BF))
    win = _dot(vwt_ref[0, 0, :, pl.ds(w0, wlen)], jnp.concatenate(e_parts, axis=1))
    o_win = win[:HEAD_DIM] * (1.0 / win[HEAD_DIM:HEAD_DIM + 1])

    gates = gt_ref[0]

    def gate_row(branch):
        return jnp.concatenate([gates[3 * g + branch:3 * g + branch + 1] for g in range(G)], axis=1)

    part_ref[...] = gate_row(0) * o_cmp + gate_row(2) * o_win
    mrow = lax.broadcasted_iota(jnp.int32, (LANES, Q_BLOCK), 0)
    tcol = lax.broadcasted_iota(jnp.int32, (LANES, Q_BLOCK), 1) + q0
    lag = tcol // SEL_BLOCK - mrow
    forced = (mrow == 0) | ((lag >= 0) & (lag < N_LOCAL_FORCED))
    score = jnp.where(forced | (lag < 0) | (mrow >= n_sel), -jnp.inf, imp_t)
    notsel_t = _mark_top_blocks(score, jnp.where(forced, 0.0, 1.0), k_top - n_forced)

    sel_rows = jnp.where(row_aug == 0, slope, jnp.concatenate([notsel_t] * G, axis=1)).astype(_BF)
    lhs_sel = jnp.concatenate([qt, sel_rows], axis=0)

    key_row = lax.broadcasted_iota(jnp.int32, (SEL_TILE, Q_BLOCK), 0)
    diag = q0 // SEL_TILE

    def tile_scores(j, causal, live=None):
        k0 = j * SEL_TILE if isinstance(j, int) else pl.multiple_of(j * SEL_TILE, SEL_TILE)
        st = _dot(ksa_ref[0, 0, pl.ds(k0, SEL_TILE), :], lhs_sel)
        if causal:
            st = jnp.concatenate([jnp.where(k0 + key_row > tok[:, cs], NEG_INF, st[:, cs]) for cs in strips], axis=1)
        shift = slope * (q0 - k0).astype(_F32)
        if live is not None:
            shift = jnp.where(live, shift, -NEG_INF)
        return st, shift, k0

    def tile_max(tiles):
        m = None
        for st, shift, _ in tiles:
            cm = jnp.max(st, axis=0, keepdims=True) - shift
            m = cm if m is None else jnp.maximum(m, cm)
        return m

    def tile_sums(tiles, m):
        acc = None
        for st, shift, k0 in tiles:
            sub = m + shift
            pt = jnp.concatenate([jnp.exp2(st[:, cs] - sub[:, cs]) for cs in strips], axis=1)
            ai = _dot(vst_ref[0, 0, :, pl.ds(k0, SEL_TILE)], pt.astype(_BF))
            acc = ai if acc is None else acc + ai
        return acc

    first_local = diag - (LOCAL_TILES - 1)
    tiles = [tile_scores(0, False, live=first_local > 0)]
    for i in range(LOCAL_TILES):
        j = first_local + i
        last = i == LOCAL_TILES - 1
        tiles.append(tile_scores(jnp.maximum(j, 0), last, live=None if last else j >= 0))
    m = tile_max(tiles)
    m_ref[0:1] = m
    acc_ref[...] = tile_sums(tiles, m)

    blk_on = jnp.where(jnp.min(notsel_t, axis=1, keepdims=True) < 0.5, 1.0, 0.0)
    blk_bit = jnp.left_shift(1, lax.broadcasted_iota(jnp.int32, (LANES, 1), 0) % BITS_PER_WORD).astype(_F32)
    for k in range(LANES // BITS_PER_WORD):
        word = jnp.sum((blk_on * blk_bit)[k * BITS_PER_WORD:(k + 1) * BITS_PER_WORD])
        bits_ref[k] = word.astype(jnp.int32)

    blocks_per_tile = SEL_TILE // SEL_BLOCK
    tiles_per_word = BITS_PER_WORD // blocks_per_tile

    def far_tile(j, carry):
        tile_bits = (bits_ref[j // tiles_per_word] >> ((j % tiles_per_word) * blocks_per_tile)) & ((1 << blocks_per_tile) - 1)

        @pl.when(tile_bits != 0)
        def _():
            tile = [tile_scores(j, False)]
            m_old = m_ref[0:1]
            m_new = jnp.maximum(m_old, tile_max(tile))
            m_ref[0:1] = m_new
            acc_ref[...] = jnp.exp2(m_old - m_new) * acc_ref[...] + tile_sums(tile, m_new)
        return carry

    lax.fori_loop(1, first_local, far_tile, 0)

    out_t = part_ref[...] + acc_ref[0:HEAD_DIM] * (gate_row(1) * (1.0 / acc_ref[HEAD_DIM:HEAD_DIM + 1]))
    for g, cs in enumerate(strips):
        o_ref[:, g * HEAD_DIM:(g + 1) * HEAD_DIM] = out_t[:, cs].T.astype(_BF)


def _nsa_attention(qt, gt, kca, vct, ksa, vst, kwa, vwt, B, S):
    nqb = S // Q_BLOCK
    T = B * S
    nc = kca.shape[1]
    gd = Q_PER_KV * HEAD_DIM
    W = Q_PER_KV * Q_BLOCK
    rows = pl.BlockSpec((1, 1, S, 2 * HEAD_DIM), lambda b, h, i: (b, h, 0, 0))
    cols = pl.BlockSpec((1, 1, VT_ROWS, S), lambda b, h, i: (b, h, 0, 0))
    return pl.pallas_call(
        functools.partial(_attn_kernel, seq=S),
        out_shape=jax.ShapeDtypeStruct((T, N_KV_HEADS * gd), _BF),
        grid=(B, N_KV_HEADS, nqb),
        in_specs=[
            pl.BlockSpec((1, 1, HEAD_DIM, W), lambda b, h, i: (b * nqb + i, h, 0, 0)),
            pl.BlockSpec((1, GATE_ROWS, Q_BLOCK), lambda b, h, i: (h, 0, b * nqb + i)),
            pl.BlockSpec((1, nc, 2 * HEAD_DIM), lambda b, h, i: (b * N_KV_HEADS + h, 0, 0)),
            pl.BlockSpec((1, HEAD_DIM, nc), lambda b, h, i: (b * N_KV_HEADS + h, 0, 0)),
            rows, cols, rows, cols,
        ],
        out_specs=pl.BlockSpec((Q_BLOCK, gd), lambda b, h, i: (b * nqb + i, h)),
        scratch_shapes=[pltpu.VMEM((8, W), _F32), pltpu.VMEM((VT_ROWS, W), _F32),
                        pltpu.VMEM((HEAD_DIM, W), _F32), pltpu.SMEM((LANES // BITS_PER_WORD,), jnp.int32)],
        compiler_params=pltpu.CompilerParams(
            dimension_semantics=("arbitrary", "arbitrary", "arbitrary"), vmem_limit_bytes=VMEM_LIMIT),
        name="nsa_attention",
    )(qt, gt, kca, vct, ksa, vst, kwa, vwt)


def _merge_ffn_kernel(x_ref, gm_ref, yn_ref, wm1_ref, wpn_ref, wo_ref, g1_ref, b1_ref,
                      w1_ref, w2_ref, g2_ref, b2_ref, o_ref):
    x = x_ref[...]
    gate = _sigmoid(_dot(x.astype(_BF), wm1_ref[...]))
    merged = gm_ref[...] + gate * _dot(yn_ref[...], wpn_ref[...])
    mix = _dot(merged.astype(_BF), wo_ref[...])
    hid = _layer_norm(DEEPNORM_ALPHA * x + mix, g1_ref[...], b1_ref[...])
    hb = hid.astype(_BF)
    f = jnp.zeros(hid.shape, _F32)
    for c in range(D_FF // D_MODEL):
        a = jnp.maximum(_dot(hb, w1_ref[:, c * D_MODEL:(c + 1) * D_MODEL]), 0.0)
        f = f + _dot((a * a).astype(_BF), w2_ref[c * D_MODEL:(c + 1) * D_MODEL, :])
    o_ref[...] = _layer_norm(DEEPNORM_ALPHA * hid + f, g2_ref[...], b2_ref[...])


def _merge_ffn(x2, gm, yn, wm1, wpn, wo, g1, b1, w1, w2, g2, b2):
    T = x2.shape[0]
    tm = ROW_TILE
    rows = lambda w: pl.BlockSpec((tm, w), lambda i: (i, 0))
    const = lambda r, c: pl.BlockSpec((r, c), lambda i: (0, 0), pipeline_mode=pl.Buffered(1))
    return pl.pallas_call(
        _merge_ffn_kernel,
        out_shape=jax.ShapeDtypeStruct((T, D_MODEL), _F32),
        grid=(T // tm,),
        in_specs=[rows(D_MODEL), rows(D_MODEL), rows(D_MODEL),
                  const(D_MODEL, D_MODEL), const(D_MODEL, D_MODEL), const(D_MODEL, D_MODEL),
                  const(1, D_MODEL), const(1, D_MODEL),
                  const(D_MODEL, D_FF), const(D_FF, D_MODEL),
                  const(1, D_MODEL), const(1, D_MODEL)],
        out_specs=rows(D_MODEL),
        compiler_params=pltpu.CompilerParams(
            dimension_semantics=("arbitrary",), vmem_limit_bytes=VMEM_LIMIT),
        name="merge_ffn",
    )(x2, gm, yn, wm1, wpn, wo, g1, b1, w1, w2, g2, b2)


def _layer(x2, B, S, w_in, layer, gm_ln_g, gm_ln_b, gm_w_s, gm_b_s, cmp_pe_k, cmp_w1_k, cmp_w2_k,
           cmp_pe_v, cmp_w1_v, cmp_w2_v, w_proj_gm, w_proj_nsa, w_out,
           ln1_g, ln1_b, w_ff1, w_ff2, ln2_g, ln2_b):
    o_q = 2 * D_MODEL
    o_m = o_q + (Q_PER_KV + 6) * N_KV_HEADS * HEAD_DIM + 3 * Q_PER_KV * N_KV_HEADS
    row = lambda v: v.reshape(1, -1)
    merge_cols = lambda j: w_in[layer * D_MODEL:(layer + 1) * D_MODEL, o_m + j * D_MODEL:o_m + (j + 1) * D_MODEL]

    gm = _gm_mixer(x2, w_in, layer, merge_cols(0), row(gm_ln_g), row(gm_ln_b), gm_w_s,
                   jnp.broadcast_to(gm_b_s[:, :, None], (GM_GROUPS, GM_CHUNK, LANES)), w_proj_gm)

    qt, kc, vc, ksa, vst, kwa, vwt, gt = _qkv_proj(x2, w_in, layer, o_q, B, S)

    heads = lambda a: a.reshape(B * N_KV_HEADS, S, HEAD_DIM)
    pairs = lambda w: w.astype(_BF).reshape(CMP_BLOCK // 2, 2 * HEAD_DIM, HEAD_DIM)
    kca, vct = _nsa_compress(heads(kc), heads(vc), cmp_pe_k, cmp_pe_v, pairs(cmp_w1_k), pairs(cmp_w1_v),
                             cmp_w2_k.astype(_BF), cmp_w2_v.astype(_BF))

    yn = _nsa_attention(qt, gt, kca, vct, ksa, vst, kwa, vwt, B, S)

    return _merge_ffn(x2, gm, yn, merge_cols(1).astype(_BF), w_proj_nsa.astype(_BF),
                      w_out.astype(_BF), row(ln1_g), row(ln1_b), w_ff1.astype(_BF), w_ff2.astype(_BF),
                      row(ln2_g), row(ln2_b))


def kernel(x, w_in, gm_ln_g, gm_ln_b, gm_w_s, gm_b_s, cmp_pe_k, cmp_w1_k, cmp_w2_k, cmp_pe_v, cmp_w1_v, cmp_w2_v, w_proj_gm, w_proj_nsa, w_out, ln1_g, ln1_b, w_ff1, w_ff2, ln2_g, ln2_b):
    B, S, D = x.shape
    assert D == D_MODEL and S % ROW_TILE == 0 and WINDOW + Q_BLOCK <= S <= SEL_BLOCK * LANES
    assert S // POS_RADIX <= POS_RADIX and S // SEL_TILE <= 32
    h = x.reshape(B * S, D)
    depth, _, d_in = w_in.shape
    w_rows = w_in.reshape(depth * D, d_in)
    for l in range(depth):
        h = _layer(h, B, S, w_rows, l, gm_ln_g[l], gm_ln_b[l], gm_w_s[l], gm_b_s[l],
                   cmp_pe_k[l], cmp_w1_k[l], cmp_w2_k[l], cmp_pe_v[l], cmp_w1_v[l], cmp_w2_v[l],
                   w_proj_gm[l], w_proj_nsa[l], w_out[l], ln1_g[l], ln1_b[l],
                   w_ff1[l], w_ff2[l], ln2_g[l], ln2_b[l])
    return h.reshape(B, S, D)
```

```python
import functools
import math

import jax
import jax.numpy as jnp
from jax import lax
from jax.experimental import pallas as pl
from jax.experimental.pallas import tpu as pltpu

D_MODEL = 1024
GM_GROUPS = 8
GM_CHUNK = 128
N_KV_HEADS = 2
Q_PER_KV = 4
HEAD_DIM = 128
CMP_BLOCK = 32
CMP_STRIDE = 16
SEL_BLOCK = 64
N_SELECT = 16
N_LOCAL_FORCED = 2
WINDOW = 512
Q_BLOCK = 128
D_FF = 4 * D_MODEL
DEEPNORM_ALPHA = 2.0 ** 0.25
LN_EPS = 1e-5
NEG_INF = -1e30
LOG2_E = math.log2(math.e)

LANES = 128
SEL_TILE = 256
LOCAL_TILES = 6
BITS_PER_WORD = 16
POS_RADIX = 256
ROW_TILE = 512
VT_ROWS = HEAD_DIM + 16
GATE_ROWS = 16
VMEM_LIMIT = 56 * 1024 * 1024

_BF = jnp.bfloat16
_F32 = jnp.float32
_NT = (((1,), (1,)), ((), ()))


def _dot(a, b):
    return jnp.dot(a, b, preferred_element_type=_F32)


def _dot_nt(a, b):
    return lax.dot_general(a, b, _NT, preferred_element_type=_F32)


def _gelu(x):
    c = math.sqrt(2.0 / math.pi)
    return 0.5 * x * (1.0 + jnp.tanh(c * (x + 0.044715 * (x * x * x))))


def _sigmoid(x):
    return 1.0 / (1.0 + jnp.exp(-x))


def _layer_norm(x, g, b):
    mu = jnp.mean(x, axis=-1, keepdims=True)
    xc = x - mu
    var = jnp.mean(xc * xc, axis=-1, keepdims=True)
    return xc * lax.rsqrt(var + LN_EPS) * g + b


def _position_lanes(pos, shape):
    lane = lax.broadcasted_iota(jnp.int32, shape, 1)
    return jnp.where(lane == 0, (pos % POS_RADIX).astype(_F32),
                     jnp.where(lane == 1, (pos // POS_RADIX).astype(_F32), 0.0)).astype(_BF)


def _stage_bf16(first_step, pairs):
    @pl.when(first_step)
    def _():
        for src, dst in pairs:
            dst[...] = src[...].astype(_BF)


def _resident(shape, index_map):
    return pl.BlockSpec(shape, index_map, pipeline_mode=pl.Buffered(1))


def _gm_kernel(x_ref, wgm32_ref, wm032_ref, lng_ref, lnb_ref, ws_ref, bs_ref, wpg32_ref, o_ref,
               vg_ref, wgm_ref, wm0_ref, wpg_ref):
    _stage_bf16(pl.program_id(0) == 0, [(wgm32_ref, wgm_ref), (wm032_ref, wm0_ref), (wpg32_ref, wpg_ref)])
    tm = x_ref.shape[0]
    xb = x_ref[...].astype(_BF)
    z = _gelu(_dot(xb, wgm_ref[...]))
    u = z[:, :D_MODEL]
    v = _layer_norm(z[:, D_MODEL:], lng_ref[...], lnb_ref[...]).astype(_BF)
    row = lax.broadcasted_iota(jnp.int32, (GM_CHUNK, GM_CHUNK), 0)
    col = lax.broadcasted_iota(jnp.int32, (GM_CHUNK, GM_CHUNK), 1)
    for gi in range(GM_GROUPS):
        w = jnp.where(row >= col, ws_ref[gi], 0.0).astype(_BF)
        for c in range(tm // GM_CHUNK):
            blk = v[c * GM_CHUNK:(c + 1) * GM_CHUNK, gi * LANES:(gi + 1) * LANES]
            vg_ref[c * GM_CHUNK:(c + 1) * GM_CHUNK, gi * LANES:(gi + 1) * LANES] = _dot(w, blk) + bs_ref[gi]
    y = (u * vg_ref[...]).astype(_BF)
    gate = _sigmoid(_dot(xb, wm0_ref[...]))
    o_ref[...] = gate * _dot(y, wpg_ref[...])


def _gm_mixer(x2, w_in, layer, wm0, lng, lnb, ws, bs, wpg):
    T = x2.shape[0]
    tm = ROW_TILE
    const2 = lambda i: (0, 0)
    const3 = lambda i: (0, 0, 0)
    return pl.pallas_call(
        _gm_kernel,
        out_shape=jax.ShapeDtypeStruct((T, D_MODEL), _F32),
        grid=(T // tm,),
        in_specs=[
            pl.BlockSpec((tm, D_MODEL), lambda i: (i, 0)),
            _resident((D_MODEL, 2 * D_MODEL), lambda i: (layer, 0)),
            _resident((D_MODEL, D_MODEL), const2),
            pl.BlockSpec((1, D_MODEL), const2),
            pl.BlockSpec((1, D_MODEL), const2),
            pl.BlockSpec((GM_GROUPS, GM_CHUNK, GM_CHUNK), const3),
            pl.BlockSpec((GM_GROUPS, GM_CHUNK, LANES), const3),
            _resident((D_MODEL, D_MODEL), const2),
        ],
        out_specs=pl.BlockSpec((tm, D_MODEL), lambda i: (i, 0)),
        scratch_shapes=[pltpu.VMEM((tm, D_MODEL), _F32), pltpu.VMEM((D_MODEL, 2 * D_MODEL), _BF),
                        pltpu.VMEM((D_MODEL, D_MODEL), _BF), pltpu.VMEM((D_MODEL, D_MODEL), _BF)],
        compiler_params=pltpu.CompilerParams(
            dimension_semantics=("arbitrary",), vmem_limit_bytes=VMEM_LIMIT),
        name="gm_mixer",
    )(x2, w_in, wm0, lng, lnb, ws, bs, wpg)


def _qkv_kernel(x_ref, wq32_ref, wkv32_ref, wg32_ref, qt_ref, kc_ref, vc_ref, ksa_ref, vst_ref, kwa_ref, vwt_ref,
                gt_ref, wq_ref, wkv_ref, wg_ref):
    _stage_bf16((pl.program_id(0) == 0) & (pl.program_id(1) == 0),
                [(wq32_ref, wq_ref), (wkv32_ref, wkv_ref), (wg32_ref, wg_ref)])
    tm = x_ref.shape[0]
    xb = x_ref[...].astype(_BF)
    zq = _dot(xb, wq_ref[...]) * (HEAD_DIM ** -0.5 * LOG2_E)
    for tb in range(tm // Q_BLOCK):
        for h in range(N_KV_HEADS):
            for g in range(Q_PER_KV):
                c0 = (h * Q_PER_KV + g) * HEAD_DIM
                blk = zq[tb * Q_BLOCK:(tb + 1) * Q_BLOCK, c0:c0 + HEAD_DIM]
                qt_ref[tb, h, :, g * Q_BLOCK:(g + 1) * Q_BLOCK] = blk.T.astype(_BF)
    z = _dot(xb, wkv_ref[...])
    kpos = pl.program_id(1) * tm + lax.broadcasted_iota(jnp.int32, (tm, LANES), 0)
    blk_lane = lax.broadcasted_iota(jnp.int32, (tm, LANES), 1)
    sel_lanes = jnp.where(blk_lane == 0, (kpos % SEL_TILE).astype(_F32),
                          jnp.where(kpos // SEL_BLOCK == blk_lane, NEG_INF, 0.0)).astype(_BF)
    win_lanes = _position_lanes(kpos, (tm, LANES))
    ones = jnp.ones((VT_ROWS - HEAD_DIM, tm), _BF)
    for h in range(N_KV_HEADS):
        def col(j):
            return z[:, j * 2 * HEAD_DIM + h * HEAD_DIM: j * 2 * HEAD_DIM + (h + 1) * HEAD_DIM]
        kc_ref[0, h] = col(0)
        vc_ref[0, h] = col(1)
        ksa_ref[0, h, :, :HEAD_DIM] = col(2).astype(_BF)
        ksa_ref[0, h, :, HEAD_DIM:] = sel_lanes
        vst_ref[0, h, :HEAD_DIM] = col(3).T.astype(_BF)
        vst_ref[0, h, HEAD_DIM:] = ones
        kwa_ref[0, h, :, :HEAD_DIM] = col(4).astype(_BF)
        kwa_ref[0, h, :, HEAD_DIM:] = win_lanes
        vwt_ref[0, h, :HEAD_DIM] = col(5).T.astype(_BF)
        vwt_ref[0, h, HEAD_DIM:] = ones
    zg_t = _sigmoid(_dot(xb, wg_ref[...])).T
    per_head = 3 * Q_PER_KV
    for h in range(N_KV_HEADS):
        gt_ref[h] = zg_t[h * per_head:h * per_head + GATE_ROWS]


def _qkv_proj(x2, w_in, layer, q_col, B, S):
    T = x2.shape[0]
    tm = ROW_TILE
    nsb = S // tm
    gd = Q_PER_KV * Q_BLOCK
    aw = N_KV_HEADS * Q_PER_KV * HEAD_DIM
    kvw = 6 * N_KV_HEADS * HEAD_DIM
    kv_col = q_col + aw
    gate_col = kv_col + kvw
    assert q_col % aw == 0 and kv_col % kvw == 0 and gate_col % LANES == 0
    rows_spec = lambda w: pl.BlockSpec((1, N_KV_HEADS, tm, w), lambda b, s: (b, 0, s, 0))
    rows_shape = lambda w, dt: jax.ShapeDtypeStruct((B, N_KV_HEADS, S, w), dt)
    cols_spec = pl.BlockSpec((1, N_KV_HEADS, VT_ROWS, tm), lambda b, s: (b, 0, 0, s))
    cols_shape = jax.ShapeDtypeStruct((B, N_KV_HEADS, VT_ROWS, S), _BF)
    return pl.pallas_call(
        _qkv_kernel,
        out_shape=(
            jax.ShapeDtypeStruct((T // Q_BLOCK, N_KV_HEADS, HEAD_DIM, gd), _BF),
            rows_shape(HEAD_DIM, _F32), rows_shape(HEAD_DIM, _F32),
            rows_shape(2 * HEAD_DIM, _BF), cols_shape,
            rows_shape(2 * HEAD_DIM, _BF), cols_shape,
            jax.ShapeDtypeStruct((N_KV_HEADS, GATE_ROWS, T), _F32),
        ),
        grid=(B, nsb),
        in_specs=[
            pl.BlockSpec((tm, D_MODEL), lambda b, s: (b * nsb + s, 0)),
            _resident((D_MODEL, aw), lambda b, s: (layer, q_col // aw)),
            _resident((D_MODEL, kvw), lambda b, s: (layer, kv_col // kvw)),
            _resident((D_MODEL, LANES), lambda b, s: (layer, gate_col // LANES)),
        ],
        out_specs=(
            pl.BlockSpec((tm // Q_BLOCK, N_KV_HEADS, HEAD_DIM, gd), lambda b, s: (b * nsb + s, 0, 0, 0)),
            rows_spec(HEAD_DIM), rows_spec(HEAD_DIM), rows_spec(2 * HEAD_DIM), cols_spec,
            rows_spec(2 * HEAD_DIM), cols_spec,
            pl.BlockSpec((N_KV_HEADS, GATE_ROWS, tm), lambda b, s: (0, 0, b * nsb + s)),
        ),
        scratch_shapes=[pltpu.VMEM((D_MODEL, aw), _BF), pltpu.VMEM((D_MODEL, kvw), _BF),
                        pltpu.VMEM((D_MODEL, LANES), _BF)],
        compiler_params=pltpu.CompilerParams(
            dimension_semantics=("arbitrary", "arbitrary"), vmem_limit_bytes=VMEM_LIMIT),
        name="qkv_proj",
    )(x2, w_in, w_in, w_in)


def _compress_kernel(kc_ref, vc_ref, pek_ref, pev_ref, w1k_ref, w1v_ref, w2k_ref, w2v_ref, ko_ref, vo_ref):
    nc = ko_ref.shape[1]

    def tokens(src, pe, w1, w2):
        first, second = None, None
        for p in range(0, CMP_STRIDE, 2):
            rows = [src[0, pl.ds(p + d, nc, stride=CMP_STRIDE), :] for d in range(2)]
            lo = jnp.concatenate([rows[d] + pe[p + d:p + d + 1, :] for d in range(2)], axis=1)
            hi = jnp.concatenate([rows[d] + pe[CMP_STRIDE + p + d:CMP_STRIDE + p + d + 1, :] for d in range(2)], axis=1)
            a = _dot(lo.astype(_BF), w1[p // 2])
            b = _dot(hi.astype(_BF), w1[(CMP_STRIDE + p) // 2])
            first = a if first is None else first + a
            second = b if second is None else second + b
        pre = first + pltpu.roll(second, nc - 1, 0)
        return _dot(_gelu(pre).astype(_BF), w2[...])

    ko_ref[0, :, :HEAD_DIM] = tokens(kc_ref, pek_ref, w1k_ref, w2k_ref).astype(_BF)
    start = lax.broadcasted_iota(jnp.int32, (nc, LANES), 0) * CMP_STRIDE
    ko_ref[0, :, HEAD_DIM:] = _position_lanes(start, (nc, LANES))
    vo_ref[0] = tokens(vc_ref, pev_ref, w1v_ref, w2v_ref).T.astype(_BF)


def _nsa_compress(kc, vc, pek, pev, w1k, w1v, w2k, w2v):
    BH, S, _ = kc.shape
    nc = S // CMP_STRIDE
    pair = 2 * HEAD_DIM
    const2 = lambda i: (0, 0)
    const3 = lambda i: (0, 0, 0)
    seq_rows = pl.BlockSpec((1, S, HEAD_DIM), lambda i: (i, 0, 0))
    pe_spec = pl.BlockSpec((CMP_BLOCK, HEAD_DIM), const2)
    w1_spec = pl.BlockSpec((CMP_BLOCK // 2, pair, HEAD_DIM), const3)
    w2_spec = pl.BlockSpec((HEAD_DIM, HEAD_DIM), const2)
    return pl.pallas_call(
        _compress_kernel,
        out_shape=(jax.ShapeDtypeStruct((BH, nc, 2 * HEAD_DIM), _BF),
                   jax.ShapeDtypeStruct((BH, HEAD_DIM, nc), _BF)),
        grid=(BH,),
        in_specs=[seq_rows, seq_rows, pe_spec, pe_spec, w1_spec, w1_spec, w2_spec, w2_spec],
        out_specs=(pl.BlockSpec((1, nc, 2 * HEAD_DIM), lambda i: (i, 0, 0)),
                   pl.BlockSpec((1, HEAD_DIM, nc), lambda i: (i, 0, 0))),
        compiler_params=pltpu.CompilerParams(
            dimension_semantics=("arbitrary",), vmem_limit_bytes=VMEM_LIMIT),
        name="nsa_compress",
    )(kc, vc, pek, pev, w1k, w1v, w2k, w2v)


def _mark_top_blocks(score, notsel, rounds):
    rows = lax.broadcasted_iota(jnp.int32, score.shape, 0).astype(_F32)
    for _ in range(rounds):
        mx = jnp.max(score, axis=0, keepdims=True)
        idx = jnp.min(jnp.where(score == mx, rows, float(LANES)), axis=0, keepdims=True)
        hit = rows == idx
        notsel = jnp.where(hit, 0.0, notsel)
        score = jnp.where(hit, -jnp.inf, score)
    return notsel


def _attn_kernel(qt_ref, gt_ref, kca_ref, vct_ref, ksa_ref, vst_ref, kwa_ref, vwt_ref, o_ref,
                 m_ref, acc_ref, lhs_ref, part_ref, gate_ref, bits_ref, *, seq):
    h = pl.program_id(1)
    step = pl.program_id(2)
    nqb = seq // Q_BLOCK
    nc = kca_ref.shape[1]
    n_sel = seq // SEL_BLOCK
    n_forced = 1 + N_LOCAL_FORCED
    k_top = min(N_SELECT, n_sel)
    G = Q_PER_KV
    W = G * Q_BLOCK
    words = LANES // BITS_PER_WORD
    put = step % 2
    get = 1 - put

    @pl.when(step == 0)
    def _():
        lhs_ref[1] = jnp.zeros(lhs_ref.shape[1:], _BF)
        part_ref[1] = jnp.zeros(part_ref.shape[1:], _F32)
        gate_ref[1] = jnp.zeros(gate_ref.shape[1:], _F32)
        for k in range(words):
            bits_ref[words + k] = 0

    lane_w = lax.broadcasted_iota(jnp.int32, (1, W), 1)
    slope = jnp.zeros((1, W), _F32)
    for g in range(G):
        sg = jnp.where(h == 0, _F32(2.0 ** -(g + 1)), _F32(2.0 ** -(G + g + 1)))
        slope = jnp.where(lane_w // Q_BLOCK == g, sg, slope)
    slope = (slope * LOG2_E).astype(_BF).astype(_F32)
    strips = [slice(g * Q_BLOCK, (g + 1) * Q_BLOCK) for g in range(G)]
    row_aug = lax.broadcasted_iota(jnp.int32, (LANES, W), 0)

    q0 = jnp.minimum(step, nqb - 1) * Q_BLOCK
    tok = q0 + lane_w % Q_BLOCK
    qt = qt_ref[0, 0]
    pos_rows = jnp.where(row_aug == 0, slope, jnp.where(row_aug == 1, slope * POS_RADIX, 0.0)).astype(_BF)
    lhs_pos = jnp.concatenate([qt, pos_rows], axis=0)

    s = _dot(kca_ref[0], lhs_pos)

    p0 = jnp.maximum(step - 1, 0) * Q_BLOCK
    ptok = p0 + lane_w % Q_BLOCK
    lhs_sel = lhs_ref[get]
    key_row = lax.broadcasted_iota(jnp.int32, (SEL_TILE, Q_BLOCK), 0)
    diag = p0 // SEL_TILE

    def tile_scores(j, causal, live=None):
        k0 = j * SEL_TILE if isinstance(j, int) else pl.multiple_of(j * SEL_TILE, SEL_TILE)
        st = _dot(ksa_ref[0, 0, pl.ds(k0, SEL_TILE), :], lhs_sel)
        if causal:
            st = jnp.concatenate([jnp.where(k0 + key_row > ptok[:, cs], NEG_INF, st[:, cs]) for cs in strips], axis=1)
        shift = slope * (p0 - k0).astype(_F32)
        if live is not None:
            shift = jnp.where(live, shift, -NEG_INF)
        return st, shift, k0

    def tile_max(tiles):
        m = None
        for st, shift, _ in tiles:
            cm = jnp.max(st, axis=0, keepdims=True) - shift
            m = cm if m is None else jnp.maximum(m, cm)
        return m

    def tile_sums(tiles, m):
        acc = None
        for st, shift, k0 in tiles:
            sub = m + shift
            pt = jnp.concatenate([jnp.exp2(st[:, cs] - sub[:, cs]) for cs in strips], axis=1)
            ai = _dot(vst_ref[0, 0, :, pl.ds(k0, SEL_TILE)], pt.astype(_BF))
            acc = ai if acc is None else acc + ai
        return acc

    first_local = diag - (LOCAL_TILES - 1)
    tiles = [tile_scores(0, False, live=first_local > 0)]
    for i in range(LOCAL_TILES):
        j = first_local + i
        last = i == LOCAL_TILES - 1
        tiles.append(tile_scores(jnp.maximum(j, 0), last, live=None if last else j >= 0))

    n_row = lax.broadcasted_iota(jnp.int32, (nc, Q_BLOCK), 0)
    n_last = (tok - (CMP_BLOCK - 1)) // CMP_STRIDE
    e_parts, inv_parts = [], []
    psum = None
    for cs in strips:
        sg = jnp.where(n_row <= n_last[:, cs], s[:, cs], NEG_INF)
        eg = jnp.exp2(sg - jnp.max(sg, axis=0, keepdims=True))
        inv = jnp.where(n_last[:, cs] >= 0, 1.0 / jnp.sum(eg, axis=0, keepdims=True), 0.0)
        pg = eg * inv
        psum = pg if psum is None else psum + pg
        e_parts.append(eg.astype(_BF))
        inv_parts.append(inv)
    o_cmp = _dot(vct_ref[0], jnp.concatenate(e_parts, axis=1)) * jnp.concatenate(inv_parts, axis=1)

    mi = lax.broadcasted_iota(jnp.int32, (LANES, nc), 0)
    ni = lax.broadcasted_iota(jnp.int32, (LANES, nc), 1)
    overlap_t = jnp.where((ni * CMP_STRIDE + (CMP_BLOCK - 1) >= mi * SEL_BLOCK)
                          & (ni * CMP_STRIDE <= mi * SEL_BLOCK + (SEL_BLOCK - 1)), 1.0, 0.0).astype(_BF)
    p_hi = psum.astype(_BF)
    p_lo = (psum - p_hi.astype(_F32)).astype(_BF)
    imp_t = _dot(overlap_t, p_hi) + _dot(overlap_t, p_lo)

    anchor = jnp.concatenate([jnp.minimum(imp_t[0:1], 0.0)] * G, axis=1)
    win_rows = jnp.where(row_aug == 0, slope + anchor, jnp.where(row_aug == 1, slope * POS_RADIX, 0.0)).astype(_BF)
    wlen = WINDOW + Q_BLOCK
    w0 = pl.multiple_of(jnp.maximum(q0 - WINDOW, 0), Q_BLOCK)
    sw = _dot(kwa_ref[0, 0, pl.ds(w0, wlen), :], jnp.concatenate([qt, win_rows], axis=0))

    m = tile_max(tiles)
    m_ref[0:1] = m
    acc_ref[...] = tile_sums(tiles, m)

    w_row = w0 + lax.broadcasted_iota(jnp.int32, (wlen, Q_BLOCK), 0)
    e_parts = []
    for cs in strips:
        dist = tok[:, cs] - w_row
        sg = jnp.where(dist.astype(jnp.uint32) < WINDOW, sw[:, cs], NEG_INF)
        e_parts.append(jnp.exp2(sg - jnp.max(sg, axis=0, keepdims=True)).astype(_BF))
    win = _dot(vwt_ref[0, 0, :, pl.ds(w0, wlen)], jnp.concatenate(e_parts, axis=1))
    o_win = win[:HEAD_DIM] * (1.0 / win[HEAD_DIM:HEAD_DIM + 1])

    gates = gt_ref[0]

    def gate_row(branch):
        return jnp.concatenate([gates[3 * g + branch:3 * g + branch + 1] for g in range(G)], axis=1)

    mrow =lax.broadcasted_iota(jnp.int32, (LANES, Q_BLOCK), 0)
    tcol = lax.broadcasted_iota(jnp.int32, (LANES, Q_BLOCK), 1) + q0
    lag = tcol // SEL_BLOCK - mrow
    forced = (mrow == 0) | ((lag >= 0) & (lag < N_LOCAL_FORCED))
    score = jnp.where(forced | (lag < 0) | (mrow >= n_sel), -jnp.inf, imp_t)
    notsel_t = _mark_top_blocks(score, jnp.where(forced, 0.0, 1.0), k_top - n_forced)

    sel_rows = jnp.where(row_aug == 0, slope, jnp.concatenate([notsel_t] * G, axis=1)).astype(_BF)

    blk_on = jnp.where(jnp.min(notsel_t, axis=1, keepdims=True) < 0.5, 1.0, 0.0)
    blk_bit = jnp.left_shift(1, lax.broadcasted_iota(jnp.int32, (LANES, 1), 0) % BITS_PER_WORD).astype(_F32)

    lhs_ref[put] = jnp.concatenate([qt, sel_rows], axis=0)
    part_ref[put] = gate_row(0) * o_cmp + gate_row(2) * o_win
    gate_ref[put] = jnp.broadcast_to(gate_row(1), gate_ref.shape[1:])
    for k in range(words):
        word = jnp.sum((blk_on * blk_bit)[k * BITS_PER_WORD:(k + 1) * BITS_PER_WORD])
        bits_ref[put * words + k] = word.astype(jnp.int32)

    blocks_per_tile = SEL_TILE // SEL_BLOCK
    tiles_per_word = BITS_PER_WORD // blocks_per_tile

    def far_tile(j, carry):
        word = bits_ref[get * words + j // tiles_per_word]
        tile_bits = (word >> ((j % tiles_per_word) * blocks_per_tile)) & ((1 << blocks_per_tile) - 1)

        @pl.when(tile_bits != 0)
        def _():
            tile = [tile_scores(j, False)]
            m_old = m_ref[0:1]
            m_new = jnp.maximum(m_old, tile_max(tile))
            m_ref[0:1] = m_new
            acc_ref[...] = jnp.exp2(m_old - m_new) * acc_ref[...] + tile_sums(tile, m_new)
        return carry

    lax.fori_loop(1, first_local, far_tile, 0)

    out_t = part_ref[get] + acc_ref[0:HEAD_DIM] * (gate_ref[get][0:1] * (1.0 / acc_ref[HEAD_DIM:HEAD_DIM + 1]))
    for g, cs in enumerate(strips):
        o_ref[:, g * HEAD_DIM:(g + 1) * HEAD_DIM] = out_t[:, cs].T.astype(_BF)


def _nsa_attention(qt, gt, kca, vct, ksa, vst, kwa, vwt, B, S):
    nqb = S // Q_BLOCK
    T = B * S
    nc = kca.shape[1]
    gd = Q_PER_KV * HEAD_DIM
    W = Q_PER_KV * Q_BLOCK
    rows = pl.BlockSpec((1, 1, S, 2 * HEAD_DIM), lambda b, h, i: (b, h, 0, 0))
    cols = pl.BlockSpec((1, 1, VT_ROWS, S), lambda b, h, i: (b, h, 0, 0))
    front = lambda i: jnp.minimum(i, nqb - 1)
    back = lambda i: jnp.maximum(i - 1, 0)
    return pl.pallas_call(
        functools.partial(_attn_kernel, seq=S),
        out_shape=jax.ShapeDtypeStruct((T, N_KV_HEADS * gd), _BF),
        grid=(B, N_KV_HEADS, nqb + 1),
        in_specs=[
            pl.BlockSpec((1, 1, HEAD_DIM, W), lambda b, h, i: (b * nqb + front(i), h, 0, 0)),
            pl.BlockSpec((1, GATE_ROWS, Q_BLOCK), lambda b, h, i: (h, 0, b * nqb + front(i))),
            pl.BlockSpec((1, nc, 2 * HEAD_DIM), lambda b, h, i: (b * N_KV_HEADS + h, 0, 0)),
            pl.BlockSpec((1, HEAD_DIM, nc), lambda b, h, i: (b * N_KV_HEADS + h, 0, 0)),
            rows, cols, rows, cols,
        ],
        out_specs=pl.BlockSpec((Q_BLOCK, gd), lambda b, h, i: (b * nqb + back(i), h)),
        scratch_shapes=[pltpu.VMEM((8, W), _F32), pltpu.VMEM((VT_ROWS, W), _F32),
                        pltpu.VMEM((2, 2 * HEAD_DIM, W), _BF), pltpu.VMEM((2, HEAD_DIM, W), _F32),
                        pltpu.VMEM((2, 8, W), _F32), pltpu.SMEM((2 * (LANES // BITS_PER_WORD),), jnp.int32)],
        compiler_params=pltpu.CompilerParams(
            dimension_semantics=("arbitrary", "arbitrary", "arbitrary"), vmem_limit_bytes=VMEM_LIMIT),
        name="nsa_attention",
    )(qt, gt, kca, vct, ksa, vst, kwa, vwt)


def _merge_ffn_kernel(x_ref, gm_ref, yn_ref, wm1_ref, wpn_ref, wo_ref, g1_ref, b1_ref,
                      w1_ref, w2_ref, g2_ref, b2_ref, o_ref):
    x = x_ref[...]
    gate = _sigmoid(_dot(x.astype(_BF), wm1_ref[...]))
    merged = gm_ref[...] + gate * _dot(yn_ref[...], wpn_ref[...])
    mix = _dot(merged.astype(_BF), wo_ref[...])
    hid = _layer_norm(DEEPNORM_ALPHA * x + mix, g1_ref[...], b1_ref[...])
    hb = hid.astype(_BF)
    f = jnp.zeros(hid.shape, _F32)
    for c in range(D_FF // D_MODEL):
        a = jnp.maximum(_dot(hb, w1_ref[:, c * D_MODEL:(c + 1) * D_MODEL]), 0.0)
        f = f + _dot((a * a).astype(_BF), w2_ref[c * D_MODEL:(c + 1) * D_MODEL, :])
    o_ref[...] = _layer_norm(DEEPNORM_ALPHA * hid + f, g2_ref[...], b2_ref[...])


def _merge_ffn(x2, gm, yn, wm1, wpn, wo, g1, b1, w1, w2, g2, b2):
    T = x2.shape[0]
    tm = ROW_TILE
    rows = lambda w: pl.BlockSpec((tm, w), lambda i: (i, 0))
    const = lambda r, c: pl.BlockSpec((r, c), lambda i: (0, 0), pipeline_mode=pl.Buffered(1))
    return pl.pallas_call(
        _merge_ffn_kernel,
        out_shape=jax.ShapeDtypeStruct((T, D_MODEL), _F32),
        grid=(T // tm,),
        in_specs=[rows(D_MODEL), rows(D_MODEL), rows(D_MODEL),
                  const(D_MODEL, D_MODEL), const(D_MODEL, D_MODEL), const(D_MODEL, D_MODEL),
                  const(1, D_MODEL), const(1, D_MODEL),
                  const(D_MODEL, D_FF), const(D_FF, D_MODEL),
                  const(1, D_MODEL), const(1, D_MODEL)],
        out_specs=rows(D_MODEL),
        compiler_params=pltpu.CompilerParams(
            dimension_semantics=("arbitrary",), vmem_limit_bytes=VMEM_LIMIT),
        name="merge_ffn",
    )(x2, gm, yn, wm1, wpn, wo, g1, b1, w1, w2, g2, b2)


def _layer(x2, B, S, w_in, layer, gm_ln_g, gm_ln_b, gm_w_s, gm_b_s, cmp_pe_k, cmp_w1_k, cmp_w2_k,
           cmp_pe_v, cmp_w1_v, cmp_w2_v, w_proj_gm, w_proj_nsa, w_out,
           ln1_g, ln1_b, w_ff1, w_ff2, ln2_g, ln2_b):
    o_q = 2 * D_MODEL
    o_m = o_q + (Q_PER_KV + 6) * N_KV_HEADS * HEAD_DIM + 3 * Q_PER_KV * N_KV_HEADS
    row = lambda v: v.reshape(1, -1)
    merge_cols = lambda j: w_in[layer * D_MODEL:(layer + 1) * D_MODEL, o_m + j * D_MODEL:o_m + (j + 1) * D_MODEL]

    gm = _gm_mixer(x2, w_in, layer, merge_cols(0), row(gm_ln_g), row(gm_ln_b), gm_w_s,
                   jnp.broadcast_to(gm_b_s[:, :, None], (GM_GROUPS, GM_CHUNK, LANES)), w_proj_gm)

    qt, kc, vc, ksa, vst, kwa, vwt, gt = _qkv_proj(x2, w_in, layer, o_q, B, S)

    heads = lambda a: a.reshape(B * N_KV_HEADS, S, HEAD_DIM)
    pairs = lambda w: w.astype(_BF).reshape(CMP_BLOCK // 2, 2 * HEAD_DIM, HEAD_DIM)
    kca, vct = _nsa_compress(heads(kc), heads(vc), cmp_pe_k, cmp_pe_v, pairs(cmp_w1_k), pairs(cmp_w1_v),
                             cmp_w2_k.astype(_BF), cmp_w2_v.astype(_BF))

    yn = _nsa_attention(qt, gt, kca, vct, ksa, vst, kwa, vwt, B, S)

    return _merge_ffn(x2, gm, yn, merge_cols(1).astype(_BF), w_proj_nsa.astype(_BF),
                      w_out.astype(_BF), row(ln1_g), row(ln1_b), w_ff1.astype(_BF), w_ff2.astype(_BF),
                      row(ln2_g), row(ln2_b))


def kernel(x, w_in, gm_ln_g, gm_ln_b, gm_w_s, gm_b_s, cmp_pe_k, cmp_w1_k, cmp_w2_k, cmp_pe_v, cmp_w1_v, cmp_w2_v, w_proj_gm, w_proj_nsa, w_out, ln1_g, ln1_b, w_ff1, w_ff2, ln2_g, ln2_b):
    B, S, D = x.shape
    assert D == D_MODEL and S % ROW_TILE == 0 and WINDOW + Q_BLOCK <= S <= SEL_BLOCK * LANES
    assert S // POS_RADIX <= POS_RADIX and S // SEL_TILE <= 32
    h = x.reshape(B * S, D)
    depth, _, d_in = w_in.shape
    w_rows = w_in.reshape(depth * D, d_in)
    for l in range(depth):
        h = _layer(h, B, S, w_rows, l, gm_ln_g[l], gm_ln_b[l], gm_w_s[l], gm_b_s[l],
                   cmp_pe_k[l], cmp_w1_k[l], cmp_w2_k[l], cmp_pe_v[l], cmp_w1_v[l], cmp_w2_v[l],
                   w_proj_gm[l], w_proj_nsa[l], w_out[l], ln1_g[l], ln1_b[l],
                   w_ff1[l], w_ff2[l], ln2_g[l], ln2_b[l])
    return h.reshape(B, S, D)
```

```python
import functools
import math

import jax
import jax.numpy as jnp
from jax import lax
from jax.experimental import pallas as pl
from jax.experimental.pallas import tpu as pltpu

D_MODEL = 1024
GM_GROUPS = 8
GM_CHUNK = 128
N_KV_HEADS = 2
Q_PER_KV = 4
HEAD_DIM = 128
CMP_BLOCK = 32
CMP_STRIDE = 16
SEL_BLOCK = 64
N_SELECT = 16
N_LOCAL_FORCED = 2
WINDOW = 512
Q_BLOCK = 128
D_FF = 4 * D_MODEL
DEEPNORM_ALPHA = 2.0 ** 0.25
LN_EPS = 1e-5
NEG_INF = -1e30
LOG2_E = math.log2(math.e)

LANES = 128
SEL_TILE = 128
LOCAL_TILES = 10
BITS_PER_WORD = 16
POS_RADIX = 256
ROW_TILE = 512
VT_ROWS = HEAD_DIM + 16
GATE_ROWS = 16
VMEM_LIMIT = 56 * 1024 * 1024

_BF = jnp.bfloat16
_F32 = jnp.float32


def _dot(a, b):
    return jnp.dot(a, b, preferred_element_type=_F32)


def _gelu(x):
    c = math.sqrt(2.0 / math.pi)
    return 0.5 * x * (1.0 + jnp.tanh(c * (x + 0.044715 * (x * x * x))))


def _sigmoid(x):
    return 1.0 / (1.0 + jnp.exp(-x))


def _layer_norm(x, g, b):
    mu = jnp.mean(x, axis=-1, keepdims=True)
    xc = x - mu
    var = jnp.mean(xc * xc, axis=-1, keepdims=True)
    return xc * lax.rsqrt(var + LN_EPS) * g + b


def _position_lanes(pos, shape):
    lane = lax.broadcasted_iota(jnp.int32, shape, 1)
    return jnp.where(lane == 0, (pos % POS_RADIX).astype(_F32),
                     jnp.where(lane == 1, (pos // POS_RADIX).astype(_F32), 0.0)).astype(_BF)


def _stage_bf16(first_step, pairs):
    @pl.when(first_step)
    def _():
        for src, dst in pairs:
            dst[...] = src[...].astype(_BF)


def _resident(shape, index_map):
    return pl.BlockSpec(shape, index_map, pipeline_mode=pl.Buffered(1))


def _gm_kernel(x_ref, wgm32_ref, wm032_ref, lng_ref, lnb_ref, ws_ref, bs_ref, wpg32_ref, o_ref,
               vg_ref, wgm_ref, wm0_ref, wpg_ref):
    _stage_bf16(pl.program_id(0) == 0, [(wgm32_ref, wgm_ref), (wm032_ref, wm0_ref), (wpg32_ref, wpg_ref)])
    tm = x_ref.shape[0]
    xb = x_ref[...].astype(_BF)
    z = _gelu(_dot(xb, wgm_ref[...]))
    u = z[:, :D_MODEL]
    v = _layer_norm(z[:, D_MODEL:], lng_ref[...], lnb_ref[...]).astype(_BF)
    row = lax.broadcasted_iota(jnp.int32, (GM_CHUNK, GM_CHUNK), 0)
    col = lax.broadcasted_iota(jnp.int32, (GM_CHUNK, GM_CHUNK), 1)
    for gi in range(GM_GROUPS):
        w = jnp.where(row >= col, ws_ref[gi], 0.0).astype(_BF)
        for c in range(tm // GM_CHUNK):
            blk = v[c * GM_CHUNK:(c + 1) * GM_CHUNK, gi * LANES:(gi + 1) * LANES]
            vg_ref[c * GM_CHUNK:(c + 1) * GM_CHUNK, gi * LANES:(gi + 1) * LANES] = _dot(w, blk) + bs_ref[gi]
    y = (u * vg_ref[...]).astype(_BF)
    gate = _sigmoid(_dot(xb, wm0_ref[...]))
    o_ref[...] = gate * _dot(y, wpg_ref[...])


def _gm_mixer(x2, w_in, layer, wm0, lng, lnb, ws, bs, wpg):
    T = x2.shape[0]
    tm = ROW_TILE
    const2 = lambda i: (0, 0)
    const3 = lambda i: (0, 0, 0)
    return pl.pallas_call(
        _gm_kernel,
        out_shape=jax.ShapeDtypeStruct((T, D_MODEL), _F32),
        grid=(T // tm,),
        in_specs=[
            pl.BlockSpec((tm, D_MODEL), lambda i: (i, 0)),
            _resident((D_MODEL, 2 * D_MODEL), lambda i: (layer, 0)),
            _resident((D_MODEL, D_MODEL), const2),
            pl.BlockSpec((1, D_MODEL), const2),
            pl.BlockSpec((1, D_MODEL), const2),
            pl.BlockSpec((GM_GROUPS, GM_CHUNK, GM_CHUNK), const3),
            pl.BlockSpec((GM_GROUPS, GM_CHUNK, LANES), const3),
            _resident((D_MODEL, D_MODEL), const2),
        ],
        out_specs=pl.BlockSpec((tm, D_MODEL), lambda i: (i, 0)),
        scratch_shapes=[pltpu.VMEM((tm, D_MODEL), _F32), pltpu.VMEM((D_MODEL, 2 * D_MODEL), _BF),
                        pltpu.VMEM((D_MODEL, D_MODEL), _BF), pltpu.VMEM((D_MODEL, D_MODEL), _BF)],
        compiler_params=pltpu.CompilerParams(
            dimension_semantics=("arbitrary",), vmem_limit_bytes=VMEM_LIMIT),
        name="gm_mixer",
    )(x2, w_in, wm0, lng, lnb, ws, bs, wpg)


def _qkv_kernel(x_ref, wq32_ref, wkv32_ref, wg32_ref, qt_ref, kc_ref, vc_ref, ksa_ref, vst_ref, kwa_ref, vwt_ref,
                gt_ref, wq_ref, wkv_ref, wg_ref):
    _stage_bf16((pl.program_id(0) == 0) & (pl.program_id(1) == 0),
                [(wq32_ref, wq_ref), (wkv32_ref, wkv_ref), (wg32_ref, wg_ref)])
    tm = x_ref.shape[0]
    xb = x_ref[...].astype(_BF)
    zq = _dot(xb, wq_ref[...]) * (HEAD_DIM ** -0.5 * LOG2_E)
    for tb in range(tm // Q_BLOCK):
        for h in range(N_KV_HEADS):
            for g in range(Q_PER_KV):
                c0 = (h * Q_PER_KV + g) * HEAD_DIM
                blk = zq[tb * Q_BLOCK:(tb + 1) * Q_BLOCK, c0:c0 + HEAD_DIM]
                qt_ref[tb, h, :, g * Q_BLOCK:(g + 1) * Q_BLOCK] = blk.T.astype(_BF)
    z = _dot(xb, wkv_ref[...])
    kpos = pl.program_id(1) * tm + lax.broadcasted_iota(jnp.int32, (tm, LANES), 0)
    blk_lane = lax.broadcasted_iota(jnp.int32, (tm, LANES), 1)
    sel_lanes = jnp.where(blk_lane == 0, (kpos % SEL_TILE).astype(_F32),
                          jnp.where(kpos // SEL_BLOCK == blk_lane, NEG_INF, 0.0)).astype(_BF)
    win_lanes = _position_lanes(kpos, (tm, LANES))
    ones = jnp.ones((VT_ROWS - HEAD_DIM, tm), _BF)
    for h in range(N_KV_HEADS):
        def col(j):
            return z[:, j * 2 * HEAD_DIM + h * HEAD_DIM: j * 2 * HEAD_DIM + (h + 1) * HEAD_DIM]
        kc_ref[0, h] = col(0)
        vc_ref[0, h] = col(1)
        ksa_ref[0, h, :, :HEAD_DIM] = col(2).astype(_BF)
        ksa_ref[0, h, :, HEAD_DIM:] = sel_lanes
        vst_ref[0, h, :HEAD_DIM] = col(3).T.astype(_BF)
        vst_ref[0, h, HEAD_DIM:] = ones
        kwa_ref[0, h, :, :HEAD_DIM] = col(4).astype(_BF)
        kwa_ref[0, h, :, HEAD_DIM:] = win_lanes
        vwt_ref[0, h, :HEAD_DIM] = col(5).T.astype(_BF)
        vwt_ref[0, h, HEAD_DIM:] = ones
    zg_t = _sigmoid(_dot(xb, wg_ref[...])).T
    per_head = 3 * Q_PER_KV
    for h in range(N_KV_HEADS):
        gt_ref[h] = zg_t[h * per_head:h * per_head + GATE_ROWS]


def _qkv_proj(x2, w_in, layer, q_col, B, S):
    T = x2.shape[0]
    tm = ROW_TILE
    nsb = S // tm
    gd = Q_PER_KV * Q_BLOCK
    aw = N_KV_HEADS * Q_PER_KV * HEAD_DIM
    kvw = 6 * N_KV_HEADS * HEAD_DIM
    kv_col = q_col + aw
    gate_col = kv_col + kvw
    assert q_col % aw == 0 and kv_col % kvw == 0 and gate_col % LANES == 0
    rows_spec = lambda w: pl.BlockSpec((1, N_KV_HEADS, tm, w), lambda b, s: (b, 0, s, 0))
    rows_shape = lambda w, dt: jax.ShapeDtypeStruct((B, N_KV_HEADS, S, w), dt)
    cols_spec = pl.BlockSpec((1, N_KV_HEADS, VT_ROWS, tm), lambda b, s: (b, 0, 0, s))
    cols_shape = jax.ShapeDtypeStruct((B, N_KV_HEADS, VT_ROWS, S), _BF)
    return pl.pallas_call(
        _qkv_kernel,
        out_shape=(
            jax.ShapeDtypeStruct((T // Q_BLOCK, N_KV_HEADS, HEAD_DIM, gd), _BF),
            rows_shape(HEAD_DIM, _F32), rows_shape(HEAD_DIM, _F32),
            rows_shape(2 * HEAD_DIM, _BF), cols_shape,
            rows_shape(2 * HEAD_DIM, _BF), cols_shape,
            jax.ShapeDtypeStruct((N_KV_HEADS, GATE_ROWS, T), _F32),
        ),
        grid=(B, nsb),
        in_specs=[
            pl.BlockSpec((tm, D_MODEL), lambda b, s: (b * nsb + s, 0)),
            _resident((D_MODEL, aw), lambda b, s: (layer, q_col // aw)),
            _resident((D_MODEL, kvw), lambda b, s: (layer, kv_col // kvw)),
            _resident((D_MODEL, LANES), lambda b, s: (layer, gate_col // LANES)),
        ],
        out_specs=(
            pl.BlockSpec((tm // Q_BLOCK, N_KV_HEADS, HEAD_DIM, gd), lambda b, s: (b * nsb + s, 0, 0, 0)),
            rows_spec(HEAD_DIM), rows_spec(HEAD_DIM), rows_spec(2 * HEAD_DIM), cols_spec,
            rows_spec(2 * HEAD_DIM), cols_spec,
            pl.BlockSpec((N_KV_HEADS, GATE_ROWS, tm), lambda b, s: (0, 0, b * nsb + s)),
        ),
        scratch_shapes=[pltpu.VMEM((D_MODEL, aw), _BF), pltpu.VMEM((D_MODEL, kvw), _BF),
                        pltpu.VMEM((D_MODEL, LANES), _BF)],
        compiler_params=pltpu.CompilerParams(
            dimension_semantics=("arbitrary", "arbitrary"), vmem_limit_bytes=VMEM_LIMIT),
        name="qkv_proj",
    )(x2, w_in, w_in, w_in)


def _compress_kernel(kc_ref, vc_ref, pek_ref, pev_ref, w1k_ref, w1v_ref, w2k_ref, w2v_ref, ko_ref, vo_ref):
    nc = ko_ref.shape[1]

    def tokens(src, pe, w1, w2):
        first, second = None, None
        for p in range(0, CMP_STRIDE, 2):
            rows = [src[0, pl.ds(p + d, nc, stride=CMP_STRIDE), :] for d in range(2)]
            lo = jnp.concatenate([rows[d] + pe[p + d:p + d + 1, :] for d in range(2)], axis=1)
            hi = jnp.concatenate([rows[d] + pe[CMP_STRIDE + p + d:CMP_STRIDE + p + d + 1, :] for d in range(2)], axis=1)
            a = _dot(lo.astype(_BF), w1[p // 2])
            b = _dot(hi.astype(_BF), w1[(CMP_STRIDE + p) // 2])
            first = a if first is None else first + a
            second = b if second is None else second + b
        pre = first + pltpu.roll(second, nc - 1, 0)
        return _dot(_gelu(pre).astype(_BF), w2[...])

    ko_ref[0, :, :HEAD_DIM] = tokens(kc_ref, pek_ref, w1k_ref, w2k_ref).astype(_BF)
    start = lax.broadcasted_iota(jnp.int32, (nc, LANES), 0) * CMP_STRIDE
    ko_ref[0, :, HEAD_DIM:] = _position_lanes(start, (nc, LANES))
    vo_ref[0] = tokens(vc_ref, pev_ref, w1v_ref, w2v_ref).T.astype(_BF)


def _nsa_compress(kc, vc, pek, pev, w1k, w1v, w2k, w2v):
    BH, S, _ = kc.shape
    nc = S // CMP_STRIDE
    pair = 2 * HEAD_DIM
    const2 = lambda i: (0, 0)
    const3 = lambda i: (0, 0, 0)
    seq_rows = pl.BlockSpec((1, S, HEAD_DIM), lambda i: (i, 0, 0))
    pe_spec = pl.BlockSpec((CMP_BLOCK, HEAD_DIM), const2)
    w1_spec = pl.BlockSpec((CMP_BLOCK // 2, pair, HEAD_DIM), const3)
    w2_spec = pl.BlockSpec((HEAD_DIM, HEAD_DIM), const2)
    return pl.pallas_call(
        _compress_kernel,
        out_shape=(jax.ShapeDtypeStruct((BH, nc, 2 * HEAD_DIM), _BF),
                   jax.ShapeDtypeStruct((BH, HEAD_DIM, nc), _BF)),
        grid=(BH,),
        in_specs=[seq_rows, seq_rows, pe_spec, pe_spec, w1_spec, w1_spec, w2_spec, w2_spec],
        out_specs=(pl.BlockSpec((1, nc, 2 * HEAD_DIM), lambda i: (i, 0, 0)),
                   pl.BlockSpec((1, HEAD_DIM, nc), lambda i: (i, 0, 0))),
        compiler_params=pltpu.CompilerParams(
            dimension_semantics=("arbitrary",), vmem_limit_bytes=VMEM_LIMIT),
        name="nsa_compress",
    )(kc, vc, pek, pev, w1k, w1v, w2k, w2v)


def _mark_top_blocks(score, notsel, rounds):
    rows = lax.broadcasted_iota(jnp.int32, score.shape, 0).astype(_F32)
    for _ in range(rounds):
        mx = jnp.max(score, axis=0, keepdims=True)
        idx = jnp.min(jnp.where(score == mx, rows, float(LANES)), axis=0, keepdims=True)
        hit = rows == idx
        notsel = jnp.where(hit, 0.0, notsel)
        score = jnp.where(hit, -jnp.inf, score)
    return notsel


def _attn_kernel(qt_ref, gt_ref, kca_ref, vct_ref, ksa_ref, vst_ref, kwa_ref, vwt_ref, o_ref,
                 m_ref, acc_ref, lhs_ref, part_ref, gate_ref, bits_ref, *, seq):
    h = pl.program_id(1)
    step = pl.program_id(2)
    nqb = seq // Q_BLOCK
    nc = kca_ref.shape[1]
    n_sel = seq // SEL_BLOCK
    n_forced = 1 + N_LOCAL_FORCED
    k_top = min(N_SELECT, n_sel)
    G = Q_PER_KV
    W = G * Q_BLOCK
    words = LANES // BITS_PER_WORD
    put = step % 2
    get = 1 - put

    @pl.when(step == 0)
    def _():
        lhs_ref[1] = jnp.zeros(lhs_ref.shape[1:], _BF)
        part_ref[1] = jnp.zeros(part_ref.shape[1:], _F32)
        gate_ref[1] = jnp.zeros(gate_ref.shape[1:], _F32)
        for k in range(words):
            bits_ref[words + k] = 0

    lane_w = lax.broadcasted_iota(jnp.int32, (1, W), 1)
    slope = jnp.zeros((1, W), _F32)
    for g in range(G):
        sg = jnp.where(h == 0, _F32(2.0 ** -(g + 1)), _F32(2.0 ** -(G + g + 1)))
        slope = jnp.where(lane_w // Q_BLOCK == g, sg, slope)
    slope = (slope * LOG2_E).astype(_BF).astype(_F32)
    strips = [slice(g * Q_BLOCK, (g + 1) * Q_BLOCK) for g in range(G)]
    row_aug = lax.broadcasted_iota(jnp.int32, (LANES, W), 0)

    q0 = jnp.minimum(step, nqb - 1) * Q_BLOCK
    tok = q0 + lane_w % Q_BLOCK
    qt = qt_ref[0, 0]
    pos_rows = jnp.where(row_aug == 0, slope, jnp.where(row_aug == 1, slope * POS_RADIX, 0.0)).astype(_BF)
    lhs_pos = jnp.concatenate([qt, pos_rows], axis=0)

    s = _dot(kca_ref[0], lhs_pos)

    p0 = jnp.maximum(step - 1, 0) * Q_BLOCK
    ptok = p0 + lane_w % Q_BLOCK
    lhs_sel = lhs_ref[get]
    key_row = lax.broadcasted_iota(jnp.int32, (SEL_TILE, Q_BLOCK), 0)
    diag = p0 // SEL_TILE

    def tile_scores(j, causal, live=None):
        k0 = j * SEL_TILE if isinstance(j, int) else pl.multiple_of(j * SEL_TILE, SEL_TILE)
        st = _dot(ksa_ref[0, 0, pl.ds(k0, SEL_TILE), :], lhs_sel)
        if causal:
            st = jnp.concatenate([jnp.where(k0 + key_row > ptok[:, cs], NEG_INF, st[:, cs]) for cs in strips], axis=1)
        shift = slope * (p0 - k0).astype(_F32)
        if live is not None:
            shift = jnp.where(live, shift, -NEG_INF)
        return st, shift, k0

    def tile_max(tiles):
        m = None
        for st, shift, _ in tiles:
            cm = jnp.max(st, axis=0, keepdims=True) - shift
            m = cm if m is None else jnp.maximum(m, cm)
        return m

    def tile_sums(tiles, m):
        acc = None
        for st, shift, k0 in tiles:
            sub = m + shift
            pt = jnp.concatenate([jnp.exp2(st[:, cs] - sub[:, cs]) for cs in strips], axis=1)
            ai = _dot(vst_ref[0, 0, :, pl.ds(k0, SEL_TILE)], pt.astype(_BF))
            acc = ai if acc is None else acc + ai
        return acc

    first_local = diag - (LOCAL_TILES - 1)
    tiles = [tile_scores(0, False, live=first_local > 0)]
    for i in range(LOCAL_TILES):
        j = first_local + i
        last = i == LOCAL_TILES - 1
        tiles.append(tile_scores(jnp.maximum(j, 0), last, live=None if last else j >= 0))

    n_row = lax.broadcasted_iota(jnp.int32, (nc, Q_BLOCK), 0)
    n_last = (tok - (CMP_BLOCK - 1)) // CMP_STRIDE
    e_parts, inv_parts = [], []
    psum = None
    for cs in strips:
        sg = jnp.where(n_row <= n_last[:, cs], s[:, cs], NEG_INF)
        eg = jnp.exp2(sg - jnp.max(sg, axis=0, keepdims=True))
        inv = jnp.where(n_last[:, cs] >= 0, 1.0 / jnp.sum(eg, axis=0, keepdims=True), 0.0)
        pg = eg * inv
        psum = pg if psum is None else psum + pg
        e_parts.append(eg.astype(_BF))
        inv_parts.append(inv)
    o_cmp = _dot(vct_ref[0], jnp.concatenate(e_parts, axis=1)) * jnp.concatenate(inv_parts, axis=1)

    mi = lax.broadcasted_iota(jnp.int32, (LANES, nc), 0)
    ni = lax.broadcasted_iota(jnp.int32, (LANES, nc), 1)
    overlap_t = jnp.where((ni * CMP_STRIDE + (CMP_BLOCK - 1) >= mi * SEL_BLOCK)
                          & (ni * CMP_STRIDE <= mi * SEL_BLOCK + (SEL_BLOCK - 1)), 1.0, 0.0).astype(_BF)
    p_hi = psum.astype(_BF)
    p_lo = (psum - p_hi.astype(_F32)).astype(_BF)
    imp_t = _dot(overlap_t, p_hi) + _dot(overlap_t, p_lo)

    anchor = jnp.concatenate([jnp.minimum(imp_t[0:1], 0.0)] * G, axis=1)
    win_rows = jnp.where(row_aug == 0, slope + anchor, jnp.where(row_aug == 1, slope * POS_RADIX, 0.0)).astype(_BF)
    wlen = WINDOW + Q_BLOCK
    w0 = pl.multiple_of(jnp.maximum(q0 - WINDOW, 0), Q_BLOCK)
    sw = _dot(kwa_ref[0, 0, pl.ds(w0, wlen), :], jnp.concatenate([qt, win_rows], axis=0))

    m = tile_max(tiles)
    m_ref[0:1] = m
    acc_ref[...] = tile_sums(tiles, m)

    w_row = w0 + lax.broadcasted_iota(jnp.int32, (wlen, Q_BLOCK), 0)
    e_parts = []
    for cs in strips:
        dist = tok[:, cs] - w_row
        sg = jnp.where(dist.astype(jnp.uint32) < WINDOW, sw[:, cs], NEG_INF)
        e_parts.append(jnp.exp2(sg - jnp.max(sg, axis=0, keepdims=True)).astype(_BF))
    win = _dot(vwt_ref[0, 0, :, pl.ds(w0, wlen)], jnp.concatenate(e_parts, axis=1))
    o_win = win[:HEAD_DIM] * (1.0 / win[HEAD_DIM:HEAD_DIM + 1])

    gates = gt_ref[0]

    def gate_row(branch):
        return jnp.concatenate([gates[3 * g + branch:3 * g + branch + 1] for g in range(G)], axis=1)

    mrow = lax.broadcasted_iota(jnp.int32, (LANES, Q_BLOCK), 0)
    tcol = lax.broadcasted_iota(jnp.int32, (LANES, Q_BLOCK), 1) + q0
    lag = tcol // SEL_BLOCK - mrow
    forced = (mrow == 0) | ((lag >= 0) & (lag < N_LOCAL_FORCED))
    score = jnp.where(forced | (lag < 0) | (mrow >= n_sel), -jnp.inf, imp_t)
    notsel_t = _mark_top_blocks(score, jnp.where(forced, 0.0, 1.0), k_top - n_forced)

    sel_rows = jnp.where(row_aug == 0, slope, jnp.concatenate([notsel_t] * G, axis=1)).astype(_BF)

    blk_on = jnp.where(jnp.min(notsel_t, axis=1, keepdims=True) < 0.5, 1.0, 0.0)
    blk_bit = jnp.left_shift(1, lax.broadcasted_iota(jnp.int32, (LANES, 1), 0) % BITS_PER_WORD).astype(_F32)

    lhs_ref[put] = jnp.concatenate([qt, sel_rows], axis=0)
    part_ref[put] = gate_row(0) * o_cmp + gate_row(2) * o_win
    gate_ref[put] = jnp.broadcast_to(gate_row(1), gate_ref.shape[1:])
    for k in range(words):
        word = jnp.sum((blk_on * blk_bit)[k * BITS_PER_WORD:(k + 1) * BITS_PER_WORD])
        bits_ref[put * words + k] = word.astype(jnp.int32)

    blocks_per_tile = SEL_TILE // SEL_BLOCK
    tiles_per_word = BITS_PER_WORD // blocks_per_tile

    def far_tile(j, carry):
        word = bits_ref[get * words + j // tiles_per_word]
        tile_bits = (word >> ((j % tiles_per_word) * blocks_per_tile)) & ((1 << blocks_per_tile) - 1)

        @pl.when(tile_bits != 0)
        def _():
            tile = [tile_scores(j, False)]
            m_old = m_ref[0:1]
            m_new = jnp.maximum(m_old, tile_max(tile))
            m_ref[0:1] = m_new
            acc_ref[...] = jnp.exp2(m_old - m_new) * acc_ref[...] + tile_sums(tile, m_new)
        return carry

    lax.fori_loop(1, first_local, far_tile, 0)

    out_t = part_ref[get] + acc_ref[0:HEAD_DIM] * (gate_ref[get][0:1] * (1.0 / acc_ref[HEAD_DIM:HEAD_DIM + 1]))
    for g, cs in enumerate(strips):
        o_ref[:, g * HEAD_DIM:(g + 1) * HEAD_DIM] = out_t[:, cs].T.astype(_BF)


def _nsa_attention(qt, gt, kca, vct, ksa, vst, kwa, vwt, B, S):
    nqb = S // Q_BLOCK
    T = B * S
    nc = kca.shape[1]
    gd = Q_PER_KV * HEAD_DIM
    W = Q_PER_KV * Q_BLOCK
    rows = pl.BlockSpec((1, 1, S, 2 * HEAD_DIM), lambda b, h, i: (b, h, 0, 0))
    cols = pl.BlockSpec((1, 1, VT_ROWS, S), lambda b, h, i: (b, h, 0, 0))
    front = lambda i: jnp.minimum(i, nqb - 1)
    back = lambda i: jnp.maximum(i - 1, 0)
    return pl.pallas_call(
        functools.partial(_attn_kernel, seq=S),
        out_shape=jax.ShapeDtypeStruct((T, N_KV_HEADS * gd), _BF),
        grid=(B, N_KV_HEADS, nqb + 1),
        in_specs=[
            pl.BlockSpec((1, 1, HEAD_DIM, W), lambda b, h, i: (b * nqb + front(i), h, 0, 0)),
            pl.BlockSpec((1, GATE_ROWS, Q_BLOCK), lambda b, h, i: (h, 0, b * nqb + front(i))),
            pl.BlockSpec((1, nc, 2 * HEAD_DIM), lambda b, h, i: (b * N_KV_HEADS + h, 0, 0)),
            pl.BlockSpec((1, HEAD_DIM, nc), lambda b, h, i: (b * N_KV_HEADS + h, 0, 0)),
            rows, cols, rows, cols,
        ],
        out_specs=pl.BlockSpec((Q_BLOCK, gd), lambda b, h, i: (b * nqb + back(i), h)),
        scratch_shapes=[pltpu.VMEM((8, W), _F32), pltpu.VMEM((VT_ROWS, W), _F32),
                        pltpu.VMEM((2, 2 * HEAD_DIM, W), _BF), pltpu.VMEM((2, HEAD_DIM, W), _F32),
                        pltpu.VMEM((2, 8, W), _F32), pltpu.SMEM((2 * (LANES // BITS_PER_WORD),), jnp.int32)],
        compiler_params=pltpu.CompilerParams(
            dimension_semantics=("arbitrary", "arbitrary", "arbitrary"), vmem_limit_bytes=VMEM_LIMIT),
        name="nsa_attention",
    )(qt, gt, kca, vct, ksa, vst, kwa, vwt)


def _merge_ffn_kernel(x_ref, gm_ref, yn_ref, wm1_ref, wpn_ref, wo_ref, g1_ref, b1_ref,
                      w1_ref, w2_ref, g2_ref, b2_ref, o_ref):
    x = x_ref[...]
    gate = _sigmoid(_dot(x.astype(_BF), wm1_ref[...]))
    merged = gm_ref[...] + gate * _dot(yn_ref[...], wpn_ref[...])
    mix = _dot(merged.astype(_BF), wo_ref[...])
    hid = _layer_norm(DEEPNORM_ALPHA * x + mix, g1_ref[...], b1_ref[...])
    hb = hid.astype(_BF)
    f = jnp.zeros(hid.shape, _F32)
    for c in range(D_FF // D_MODEL):
        a = jnp.maximum(_dot(hb, w1_ref[:, c * D_MODEL:(c + 1) * D_MODEL]), 0.0)
        f = f + _dot((a * a).astype(_BF), w2_ref[c * D_MODEL:(c + 1) * D_MODEL, :])
    o_ref[...] = _layer_norm(DEEPNORM_ALPHA * hid + f, g2_ref[...], b2_ref[...])


def _merge_ffn(x2, gm, yn, wm1, wpn, wo, g1, b1, w1, w2, g2, b2):
    T = x2.shape[0]
    tm = ROW_TILE
    rows = lambda w: pl.BlockSpec((tm, w), lambda i: (i, 0))
    const = lambda r, c: _resident((r, c), lambda i: (0, 0))
    return pl.pallas_call(
        _merge_ffn_kernel,
        out_shape=jax.ShapeDtypeStruct((T, D_MODEL), _F32),
        grid=(T // tm,),
        in_specs=[rows(D_MODEL), rows(D_MODEL), rows(D_MODEL),
                  const(D_MODEL, D_MODEL), const(D_MODEL, D_MODEL), const(D_MODEL, D_MODEL),
                  const(1, D_MODEL), const(1, D_MODEL),
                  const(D_MODEL, D_FF), const(D_FF, D_MODEL),
                  const(1, D_MODEL), const(1, D_MODEL)],
        out_specs=rows(D_MODEL),
        compiler_params=pltpu.CompilerParams(
            dimension_semantics=("arbitrary",), vmem_limit_bytes=VMEM_LIMIT),
        name="merge_ffn",
    )(x2, gm, yn, wm1, wpn, wo, g1, b1, w1, w2, g2, b2)


def _layer(x2, B, S, w_in, layer, gm_ln_g, gm_ln_b, gm_w_s, gm_b_s, cmp_pe_k, cmp_w1_k, cmp_w2_k,
           cmp_pe_v, cmp_w1_v, cmp_w2_v, w_proj_gm, w_proj_nsa, w_out,
           ln1_g, ln1_b, w_ff1, w_ff2, ln2_g, ln2_b):
    o_q = 2 * D_MODEL
    o_m = o_q + (Q_PER_KV + 6) * N_KV_HEADS * HEAD_DIM + 3 * Q_PER_KV * N_KV_HEADS
    row = lambda v: v.reshape(1, -1)
    merge_cols = lambda j: w_in[layer * D_MODEL:(layer + 1) * D_MODEL, o_m + j * D_MODEL:o_m + (j + 1) * D_MODEL]

    gm = _gm_mixer(x2, w_in, layer, merge_cols(0), row(gm_ln_g), row(gm_ln_b), gm_w_s,
                   jnp.broadcast_to(gm_b_s[:, :, None], (GM_GROUPS, GM_CHUNK, LANES)), w_proj_gm)

    qt, kc, vc, ksa, vst, kwa, vwt, gt = _qkv_proj(x2, w_in, layer, o_q, B, S)

    heads = lambda a: a.reshape(B * N_KV_HEADS, S, HEAD_DIM)
    pairs = lambda w: w.astype(_BF).reshape(CMP_BLOCK // 2, 2 * HEAD_DIM, HEAD_DIM)
    kca, vct = _nsa_compress(heads(kc), heads(vc), cmp_pe_k, cmp_pe_v, pairs(cmp_w1_k), pairs(cmp_w1_v),
                             cmp_w2_k.astype(_BF), cmp_w2_v.astype(_BF))

    yn = _nsa_attention(qt, gt, kca, vct, ksa, vst, kwa, vwt, B, S)

    return _merge_ffn(x2, gm, yn, merge_cols(1).astype(_BF), w_proj_nsa.astype(_BF),
                      w_out.astype(_BF), row(ln1_g), row(ln1_b), w_ff1.astype(_BF), w_ff2.astype(_BF),
                      row(ln2_g), row(ln2_b))


def kernel(x, w_in, gm_ln_g, gm_ln_b, gm_w_s, gm_b_s, cmp_pe_k, cmp_w1_k, cmp_w2_k, cmp_pe_v, cmp_w1_v, cmp_w2_v, w_proj_gm, w_proj_nsa, w_out, ln1_g, ln1_b, w_ff1, w_ff2, ln2_g, ln2_b):
    B, S, D = x.shape
    assert D == D_MODEL and S % ROW_TILE == 0 and WINDOW + Q_BLOCK <= S <= SEL_BLOCK * LANES
    assert S // POS_RADIX <= POS_RADIX
    h = x.reshape(B * S, D)
    depth, _, d_in = w_in.shape
    w_rows = w_in.reshape(depth * D, d_in)
    for l in range(depth):
        h = _layer(h, B, S, w_rows, l, gm_ln_g[l], gm_ln_b[l], gm_w_s[l], gm_b_s[l],
                   cmp_pe_k[l], cmp_w1_k[l], cmp_w2_k[l], cmp_pe_v[l], cmp_w1_v[l], cmp_w2_v[l],
                   w_proj_gm[l], w_proj_nsa[l], w_out[l], ln1_g[l], ln1_b[l],
                   w_ff1[l], w_ff2[l], ln2_g[l], ln2_b[l])
    return h.reshape(B, S, D)
```

```python
import functools
import math

import jax
import jax.numpy as jnp
from jax import lax
from jax.experimental import pallas as pl
from jax.experimental.pallas import tpu as pltpu

D_MODEL = 1024
GM_GROUPS = 8
GM_CHUNK = 128
N_KV_HEADS = 2
Q_PER_KV = 4
HEAD_DIM = 128
CMP_BLOCK = 32
CMP_STRIDE = 16
SEL_BLOCK = 64
N_SELECT = 16
N_LOCAL_FORCED = 2
WINDOW = 512
Q_BLOCK = 128
D_FF = 4 * D_MODEL
DEEPNORM_ALPHA = 2.0 ** 0.25
LN_EPS = 1e-5
NEG_INF = -1e30
LOG2_E = math.log2(math.e)

LANES = 128
SEL_TILE = 128
LOCAL_TILES = 11
BITS_PER_WORD = 16
POS_RADIX = 256
ROW_TILE = 512
VT_ROWS = HEAD_DIM + 16
GATE_ROWS = 16
VMEM_LIMIT = 56 * 1024 * 1024

_BF = jnp.bfloat16
_F32 = jnp.float32


def _dot(a, b):
    return jnp.dot(a, b, preferred_element_type=_F32)


def _gelu(x):
    c = math.sqrt(2.0 / math.pi)
    return 0.5 * x * (1.0 + jnp.tanh(c * (x + 0.044715 * (x * x * x))))


def _sigmoid(x):
    return 1.0 / (1.0 + jnp.exp(-x))


def _layer_norm(x, g, b):
    mu = jnp.mean(x, axis=-1, keepdims=True)
    xc = x - mu
    var = jnp.mean(xc * xc, axis=-1, keepdims=True)
    return xc * lax.rsqrt(var + LN_EPS) * g + b


def _position_lanes(pos, shape):
    lane = lax.broadcasted_iota(jnp.int32, shape, 1)
    return jnp.where(lane == 0, (pos % POS_RADIX).astype(_F32),
                     jnp.where(lane == 1, (pos // POS_RADIX).astype(_F32), 0.0)).astype(_BF)


def _stage_bf16(first_step, pairs):
    @pl.when(first_step)
    def _():
        for src, dst in pairs:
            dst[...] = src[...].astype(_BF)


def _resident(shape, index_map):
    return pl.BlockSpec(shape, index_map, pipeline_mode=pl.Buffered(1))


def _gm_kernel(x_ref, wgm32_ref, wm032_ref, lng_ref, lnb_ref, ws_ref, bs_ref, wpg32_ref, o_ref,
               vg_ref, wgm_ref, wm0_ref, wpg_ref):
    _stage_bf16(pl.program_id(0) == 0, [(wgm32_ref, wgm_ref), (wm032_ref, wm0_ref), (wpg32_ref, wpg_ref)])
    tm = x_ref.shape[0]
    xb = x_ref[...].astype(_BF)
    z = _gelu(_dot(xb, wgm_ref[...]))
    u = z[:, :D_MODEL]
    v = _layer_norm(z[:, D_MODEL:], lng_ref[...], lnb_ref[...]).astype(_BF)
    row = lax.broadcasted_iota(jnp.int32, (GM_CHUNK, GM_CHUNK), 0)
    col = lax.broadcasted_iota(jnp.int32, (GM_CHUNK, GM_CHUNK), 1)
    for gi in range(GM_GROUPS):
        w = jnp.where(row >= col, ws_ref[gi], 0.0).astype(_BF)
        for c in range(tm // GM_CHUNK):
            blk = v[c * GM_CHUNK:(c + 1) * GM_CHUNK, gi * LANES:(gi + 1) * LANES]
            vg_ref[c * GM_CHUNK:(c + 1) * GM_CHUNK, gi * LANES:(gi + 1) * LANES] = _dot(w, blk) + bs_ref[gi]
    y = (u * vg_ref[...]).astype(_BF)
    gate = _sigmoid(_dot(xb, wm0_ref[...]))
    o_ref[...] = gate * _dot(y, wpg_ref[...])


def _gm_mixer(x2, w_in, layer, wm0, lng, lnb, ws, bs, wpg):
    T = x2.shape[0]
    tm = ROW_TILE
    const2 = lambda i: (0, 0)
    const3 = lambda i: (0, 0, 0)
    return pl.pallas_call(
        _gm_kernel,
        out_shape=jax.ShapeDtypeStruct((T, D_MODEL), _F32),
        grid=(T // tm,),
        in_specs=[
            pl.BlockSpec((tm, D_MODEL), lambda i: (i, 0)),
            _resident((D_MODEL, 2 * D_MODEL), lambda i: (layer, 0)),
            _resident((D_MODEL, D_MODEL), const2),
            pl.BlockSpec((1, D_MODEL), const2),
            pl.BlockSpec((1, D_MODEL), const2),
            pl.BlockSpec((GM_GROUPS, GM_CHUNK, GM_CHUNK), const3),
            pl.BlockSpec((GM_GROUPS, GM_CHUNK, LANES), const3),
            _resident((D_MODEL, D_MODEL), const2),
        ],
        out_specs=pl.BlockSpec((tm, D_MODEL), lambda i: (i, 0)),
        scratch_shapes=[pltpu.VMEM((tm, D_MODEL), _F32), pltpu.VMEM((D_MODEL, 2 * D_MODEL), _BF),
                        pltpu.VMEM((D_MODEL, D_MODEL), _BF), pltpu.VMEM((D_MODEL, D_MODEL), _BF)],
        compiler_params=pltpu.CompilerParams(
            dimension_semantics=("arbitrary",), vmem_limit_bytes=VMEM_LIMIT),
        name="gm_mixer",
    )(x2, w_in, wm0, lng, lnb, ws, bs, wpg)


def _qkv_kernel(x_ref, wq32_ref, wkv32_ref, wg32_ref, qt_ref, kc_ref, vc_ref, ksa_ref, vst_ref, kwa_ref, vwt_ref,
                gt_ref, wq_ref, wkv_ref, wg_ref):
    _stage_bf16((pl.program_id(0) == 0) & (pl.program_id(1) == 0),
                [(wq32_ref, wq_ref), (wkv32_ref, wkv_ref), (wg32_ref, wg_ref)])
    tm = x_ref.shape[0]
    xb = x_ref[...].astype(_BF)
    zq = _dot(xb, wq_ref[...]) * (HEAD_DIM ** -0.5 * LOG2_E)
    for tb in range(tm // Q_BLOCK):
        for h in range(N_KV_HEADS):
            for g in range(Q_PER_KV):
                c0 = (h * Q_PER_KV + g) * HEAD_DIM
                blk = zq[tb * Q_BLOCK:(tb + 1) * Q_BLOCK, c0:c0 + HEAD_DIM]
                qt_ref[tb, h, :, g * Q_BLOCK:(g + 1) * Q_BLOCK] = blk.T.astype(_BF)
    z = _dot(xb, wkv_ref[...])
    kpos = pl.program_id(1) * tm + lax.broadcasted_iota(jnp.int32, (tm, LANES), 0)
    blk_lane = lax.broadcasted_iota(jnp.int32, (tm, LANES), 1)
    sel_lanes = jnp.where(blk_lane == 0, (kpos % SEL_TILE).astype(_F32),
                          jnp.where(kpos // SEL_BLOCK == blk_lane, NEG_INF, 0.0)).astype(_BF)
    win_lanes = _position_lanes(kpos, (tm, LANES))
    ones = jnp.ones((VT_ROWS - HEAD_DIM, tm), _BF)
    for h in range(N_KV_HEADS):
        def col(j):
            return z[:, j * 2 * HEAD_DIM + h * HEAD_DIM: j * 2 * HEAD_DIM + (h + 1) * HEAD_DIM]
        kc_ref[0, h] = col(0)
        vc_ref[0, h] = col(1)
        ksa_ref[0, h, :, :HEAD_DIM] = col(2).astype(_BF)
        ksa_ref[0, h, :, HEAD_DIM:] = sel_lanes
        vst_ref[0, h, :HEAD_DIM] = col(3).T.astype(_BF)
        vst_ref[0, h, HEAD_DIM:] = ones
        kwa_ref[0, h, :, :HEAD_DIM] = col(4).astype(_BF)
        kwa_ref[0, h, :, HEAD_DIM:] = win_lanes
        vwt_ref[0, h, :HEAD_DIM] = col(5).T.astype(_BF)
        vwt_ref[0, h, HEAD_DIM:] = ones
    zg_t = _sigmoid(_dot(xb, wg_ref[...])).T
    per_head = 3 * Q_PER_KV
    for h in range(N_KV_HEADS):
        gt_ref[h] = zg_t[h * per_head:h * per_head + GATE_ROWS]


def _qkv_proj(x2, w_in, layer, q_col, B, S):
    T = x2.shape[0]
    tm = ROW_TILE
    nsb = S // tm
    gd = Q_PER_KV * Q_BLOCK
    aw = N_KV_HEADS * Q_PER_KV * HEAD_DIM
    kvw = 6 * N_KV_HEADS * HEAD_DIM
    kv_col = q_col + aw
    gate_col = kv_col + kvw
    assert q_col % aw == 0 and kv_col % kvw == 0 and gate_col % LANES == 0
    rows_spec = lambda w: pl.BlockSpec((1, N_KV_HEADS, tm, w), lambda b, s: (b, 0, s, 0))
    rows_shape = lambda w, dt: jax.ShapeDtypeStruct((B, N_KV_HEADS, S, w), dt)
    cols_spec = pl.BlockSpec((1, N_KV_HEADS, VT_ROWS, tm), lambda b, s: (b, 0, 0, s))
    cols_shape = jax.ShapeDtypeStruct((B, N_KV_HEADS, VT_ROWS, S), _BF)
    return pl.pallas_call(
        _qkv_kernel,
        out_shape=(
            jax.ShapeDtypeStruct((T // Q_BLOCK, N_KV_HEADS, HEAD_DIM, gd), _BF),
            rows_shape(HEAD_DIM, _F32), rows_shape(HEAD_DIM, _F32),
            rows_shape(2 * HEAD_DIM, _BF), cols_shape,
            rows_shape(2 * HEAD_DIM, _BF), cols_shape,
            jax.ShapeDtypeStruct((N_KV_HEADS, GATE_ROWS, T), _F32),
        ),
        grid=(B, nsb),
        in_specs=[
            pl.BlockSpec((tm, D_MODEL), lambda b, s: (b * nsb + s, 0)),
            _resident((D_MODEL, aw), lambda b, s: (layer, q_col // aw)),
            _resident((D_MODEL, kvw), lambda b, s: (layer, kv_col // kvw)),
            _resident((D_MODEL, LANES), lambda b, s: (layer, gate_col // LANES)),
        ],
        out_specs=(
            pl.BlockSpec((tm // Q_BLOCK, N_KV_HEADS, HEAD_DIM, gd), lambda b, s: (b * nsb + s, 0, 0, 0)),
            rows_spec(HEAD_DIM), rows_spec(HEAD_DIM), rows_spec(2 * HEAD_DIM), cols_spec,
            rows_spec(2 * HEAD_DIM), cols_spec,
            pl.BlockSpec((N_KV_HEADS, GATE_ROWS, tm), lambda b, s: (0, 0, b * nsb + s)),
        ),
        scratch_shapes=[pltpu.VMEM((D_MODEL, aw), _BF), pltpu.VMEM((D_MODEL, kvw), _BF),
                        pltpu.VMEM((D_MODEL, LANES), _BF)],
        compiler_params=pltpu.CompilerParams(
            dimension_semantics=("arbitrary", "arbitrary"), vmem_limit_bytes=VMEM_LIMIT),
        name="qkv_proj",
    )(x2, w_in, w_in, w_in)


def _compress_kernel(kc_ref, vc_ref, pek_ref, pev_ref, w1k_ref, w1v_ref, w2k_ref, w2v_ref, ko_ref, vo_ref):
    nc = ko_ref.shape[1]

    def tokens(src, pe, w1, w2):
        first, second = None, None
        for p in range(0, CMP_STRIDE, 2):
            rows = [src[0, pl.ds(p + d, nc, stride=CMP_STRIDE), :] for d in range(2)]
            lo = jnp.concatenate([rows[d] + pe[p + d:p + d + 1, :] for d in range(2)], axis=1)
            hi = jnp.concatenate([rows[d] + pe[CMP_STRIDE + p + d:CMP_STRIDE + p + d + 1, :] for d in range(2)], axis=1)
            a = _dot(lo.astype(_BF), w1[p // 2])
            b = _dot(hi.astype(_BF), w1[(CMP_STRIDE + p) // 2])
            first = a if first is None else first + a
            second = b if second is None else second + b
        pre = first + pltpu.roll(second, nc - 1, 0)
        return _dot(_gelu(pre).astype(_BF), w2[...])

    ko_ref[0, :, :HEAD_DIM] = tokens(kc_ref, pek_ref, w1k_ref, w2k_ref).astype(_BF)
    start = lax.broadcasted_iota(jnp.int32, (nc, LANES), 0) * CMP_STRIDE
    ko_ref[0, :, HEAD_DIM:] = _position_lanes(start, (nc, LANES))
    vo_ref[0] = tokens(vc_ref, pev_ref, w1v_ref, w2v_ref).T.astype(_BF)


def _nsa_compress(kc, vc, pek, pev, w1k, w1v, w2k, w2v):
    BH, S, _ = kc.shape
    nc = S // CMP_STRIDE
    pair = 2 * HEAD_DIM
    const2 = lambda i: (0, 0)
    const3 = lambda i: (0, 0, 0)
    seq_rows = pl.BlockSpec((1, S, HEAD_DIM), lambda i: (i, 0, 0))
    pe_spec = pl.BlockSpec((CMP_BLOCK, HEAD_DIM), const2)
    w1_spec = pl.BlockSpec((CMP_BLOCK // 2, pair, HEAD_DIM), const3)
    w2_spec = pl.BlockSpec((HEAD_DIM, HEAD_DIM), const2)
    return pl.pallas_call(
        _compress_kernel,
        out_shape=(jax.ShapeDtypeStruct((BH, nc, 2 * HEAD_DIM), _BF),
                   jax.ShapeDtypeStruct((BH, HEAD_DIM, nc), _BF)),
        grid=(BH,),
        in_specs=[seq_rows, seq_rows, pe_spec, pe_spec, w1_spec, w1_spec, w2_spec, w2_spec],
        out_specs=(pl.BlockSpec((1, nc, 2 * HEAD_DIM), lambda i: (i, 0, 0)),
                   pl.BlockSpec((1, HEAD_DIM, nc), lambda i: (i, 0, 0))),
        compiler_params=pltpu.CompilerParams(
            dimension_semantics=("arbitrary",), vmem_limit_bytes=VMEM_LIMIT),
        name="nsa_compress",
    )(kc, vc, pek, pev, w1k, w1v, w2k, w2v)


def _mark_top_blocks(score, notsel, rounds):
    rows = lax.broadcasted_iota(jnp.int32, score.shape, 0).astype(_F32)
    for _ in range(rounds):
        mx = jnp.max(score, axis=0, keepdims=True)
        idx = jnp.min(jnp.where(score == mx, rows, float(LANES)), axis=0, keepdims=True)
        hit = rows == idx
        notsel = jnp.where(hit, 0.0, notsel)
        score = jnp.where(hit, -jnp.inf, score)
    return notsel


def _attn_kernel(qt_ref, gt_ref, kca_ref, vct_ref, ksa_ref, vst_ref, kwa_ref, vwt_ref, o_ref,
                 m_ref, acc_ref, lhs_ref, part_ref, gate_ref, bits_ref, *, seq):
    h = pl.program_id(1)
    step = pl.program_id(2)
    nqb = seq // Q_BLOCK
    nc = kca_ref.shape[1]
    n_sel = seq // SEL_BLOCK
    n_forced = 1 + N_LOCAL_FORCED
    k_top = min(N_SELECT, n_sel)
    G = Q_PER_KV
    W = G * Q_BLOCK
    words = LANES // BITS_PER_WORD
    put = step % 2
    get = 1 - put

    @pl.when(step == 0)
    def _():
        lhs_ref[1] = jnp.zeros(lhs_ref.shape[1:], _BF)
        part_ref[1] = jnp.zeros(part_ref.shape[1:], _F32)
        gate_ref[1] = jnp.zeros(gate_ref.shape[1:], _F32)
        for k in range(words + 1):
            bits_ref[words + 1 + k] = 0

    lane_w = lax.broadcasted_iota(jnp.int32, (1, W), 1)
    slope = jnp.zeros((1, W), _F32)
    for g in range(G):
        sg = jnp.where(h == 0, _F32(2.0 ** -(g + 1)), _F32(2.0 ** -(G + g + 1)))
        slope = jnp.where(lane_w // Q_BLOCK == g, sg, slope)
    slope = (slope * LOG2_E).astype(_BF).astype(_F32)
    strips = [slice(g * Q_BLOCK, (g + 1) * Q_BLOCK) for g in range(G)]
    row_aug = lax.broadcasted_iota(jnp.int32, (LANES, W), 0)

    q0 = jnp.minimum(step, nqb - 1) * Q_BLOCK
    tok = q0 + lane_w % Q_BLOCK
    qt = qt_ref[0, 0]
    pos_rows = jnp.where(row_aug == 0, slope, jnp.where(row_aug == 1, slope * POS_RADIX, 0.0)).astype(_BF)
    lhs_pos = jnp.concatenate([qt, pos_rows], axis=0)

    s = _dot(kca_ref[0], lhs_pos)

    p0 = jnp.maximum(step - 1, 0) * Q_BLOCK
    ptok = p0 + lane_w % Q_BLOCK
    lhs_sel = lhs_ref[get]
    key_row = lax.broadcasted_iota(jnp.int32, (SEL_TILE, Q_BLOCK), 0)
    diag = p0 // SEL_TILE

    def tile_scores(j, causal, live=None):
        k0 = j * SEL_TILE if isinstance(j, int) else pl.multiple_of(j * SEL_TILE, SEL_TILE)
        st = _dot(ksa_ref[0, 0, pl.ds(k0, SEL_TILE), :], lhs_sel)
        if causal:
            st = jnp.concatenate([jnp.where(k0 + key_row > ptok[:, cs], NEG_INF, st[:, cs]) for cs in strips], axis=1)
        shift = slope * (p0 - k0).astype(_F32)
        if live is not None:
            shift = jnp.where(live, shift, -NEG_INF)
        return st, shift, k0

    def tile_max(tiles):
        m = None
        for st, shift, _ in tiles:
            cm = jnp.max(st, axis=0, keepdims=True) - shift
            m = cm if m is None else jnp.maximum(m, cm)
        return m

    def tile_sums(tiles, m):
        acc = None
        for st, shift, k0 in tiles:
            sub = m + shift
            pt = jnp.concatenate([jnp.exp2(st[:, cs] - sub[:, cs]) for cs in strips], axis=1)
            ai = _dot(vst_ref[0, 0, :, pl.ds(k0, SEL_TILE)], pt.astype(_BF))
            acc = ai if acc is None else acc + ai
        return acc

    first_local = diag - (LOCAL_TILES - 1)
    tiles = [tile_scores(0, False, live=first_local > 0)]
    for i in range(LOCAL_TILES):
        j = first_local + i
        last = i == LOCAL_TILES - 1
        tiles.append(tile_scores(jnp.maximum(j, 0), last, live=None if last else j >= 0))

    n_row = lax.broadcasted_iota(jnp.int32, (nc, Q_BLOCK), 0)
    n_last = (tok - (CMP_BLOCK - 1)) // CMP_STRIDE
    e_parts, inv_parts = [], []
    psum = None
    for cs in strips:
        sg = jnp.where(n_row <= n_last[:, cs], s[:, cs], NEG_INF)
        eg = jnp.exp2(sg - jnp.max(sg, axis=0, keepdims=True))
        inv = jnp.where(n_last[:, cs] >= 0, 1.0 / jnp.sum(eg, axis=0, keepdims=True), 0.0)
        pg = eg * inv
        psum = pg if psum is None else psum + pg
        e_parts.append(eg.astype(_BF))
        inv_parts.append(inv)
    o_cmp = _dot(vct_ref[0], jnp.concatenate(e_parts, axis=1)) * jnp.concatenate(inv_parts, axis=1)

    mi = lax.broadcasted_iota(jnp.int32, (LANES, nc), 0)
    ni = lax.broadcasted_iota(jnp.int32, (LANES, nc), 1)
    overlap_t = jnp.where((ni * CMP_STRIDE + (CMP_BLOCK - 1) >= mi * SEL_BLOCK)
                          & (ni * CMP_STRIDE <= mi * SEL_BLOCK + (SEL_BLOCK - 1)), 1.0, 0.0).astype(_BF)
    p_hi = psum.astype(_BF)
    p_lo = (psum - p_hi.astype(_F32)).astype(_BF)
    imp_t = _dot(overlap_t, p_hi) + _dot(overlap_t, p_lo)

    anchor = jnp.concatenate([jnp.minimum(imp_t[0:1], 0.0)] * G, axis=1)
    win_rows = jnp.where(row_aug == 0, slope + anchor, jnp.where(row_aug == 1, slope * POS_RADIX, 0.0)).astype(_BF)
    wlen = WINDOW + Q_BLOCK
    w0 = pl.multiple_of(jnp.maximum(q0 - WINDOW, 0), Q_BLOCK)
    sw = _dot(kwa_ref[0, 0, pl.ds(w0, wlen), :], jnp.concatenate([qt, win_rows], axis=0))

    m = tile_max(tiles)
    m_ref[0:1] = m
    acc_ref[...] = tile_sums(tiles, m)

    w_row = w0 + lax.broadcasted_iota(jnp.int32, (wlen, Q_BLOCK), 0)
    e_parts = []
    for cs in strips:
        dist = tok[:, cs] - w_row
        sg = jnp.where(dist.astype(jnp.uint32) < WINDOW, sw[:, cs], NEG_INF)
        e_parts.append(jnp.exp2(sg - jnp.max(sg, axis=0, keepdims=True)).astype(_BF))
    win = _dot(vwt_ref[0, 0, :, pl.ds(w0, wlen)], jnp.concatenate(e_parts, axis=1))
    o_win = win[:HEAD_DIM] * (1.0 / win[HEAD_DIM:HEAD_DIM + 1])

    gates = gt_ref[0]

    def gate_row(branch):
        return jnp.concatenate([gates[3 * g + branch:3 * g + branch + 1] for g in range(G)], axis=1)

    mrow = lax.broadcasted_iota(jnp.int32, (LANES, Q_BLOCK), 0)
    tcol = lax.broadcasted_iota(jnp.int32, (LANES, Q_BLOCK), 1) + q0
    lag = tcol // SEL_BLOCK - mrow
    forced = (mrow == 0) | ((lag >= 0) & (lag < N_LOCAL_FORCED))
    score = jnp.where(forced | (lag < 0) | (mrow >= n_sel), -jnp.inf, imp_t)
    notsel_t = _mark_top_blocks(score, jnp.where(forced, 0.0, 1.0), k_top - n_forced)

    sel_rows = jnp.where(row_aug == 0, slope, jnp.concatenate([notsel_t] * G, axis=1)).astype(_BF)

    blocks_per_tile = SEL_TILE // SEL_BLOCK
    blk_row = lax.broadcasted_iota(jnp.int32, (LANES, 1), 0)
    far_end = (q0 // SEL_TILE - (LOCAL_TILES - 1)) * blocks_per_tile
    blk_on = jnp.where((jnp.min(notsel_t, axis=1, keepdims=True) < 0.5)
                       & (blk_row >= blocks_per_tile) & (blk_row < far_end), 1.0, 0.0)
    blk_bit = jnp.left_shift(1, blk_row % BITS_PER_WORD).astype(_F32)

    lhs_ref[put] = jnp.concatenate([qt, sel_rows], axis=0)
    part_ref[put] = gate_row(0) * o_cmp + gate_row(2) * o_win
    gate_ref[put] = jnp.broadcast_to(gate_row(1), gate_ref.shape[1:])
    for k in range(words):
        word = jnp.sum((blk_on * blk_bit)[k * BITS_PER_WORD:(k + 1) * BITS_PER_WORD])
        bits_ref[put * (words + 1) + k] = word.astype(jnp.int32)
    bits_ref[put * (words + 1) + words] = jnp.sum(blk_on).astype(jnp.int32)

    tiles_per_word = BITS_PER_WORD // blocks_per_tile

    def far_tile(j, carry):
        word = bits_ref[get * (words + 1) + j // tiles_per_word]
        tile_bits = (word >> ((j % tiles_per_word) * blocks_per_tile)) & ((1 << blocks_per_tile) - 1)

        @pl.when(tile_bits != 0)
        def _():
            tile = [tile_scores(j, False)]
            m_old = m_ref[0:1]
            m_new = jnp.maximum(m_old, tile_max(tile))
            m_ref[0:1] = m_new
            acc_ref[...] = jnp.exp2(m_old - m_new) * acc_ref[...] + tile_sums(tile, m_new)
        return carry

    @pl.when(bits_ref[get * (words + 1) + words] != 0)
    def _():
        lax.fori_loop(1, first_local, far_tile, 0)

    out_t = part_ref[get] + acc_ref[0:HEAD_DIM] * (gate_ref[get][0:1] * (1.0 / acc_ref[HEAD_DIM:HEAD_DIM + 1]))
    for g, cs in enumerate(strips):
        o_ref[:, g * HEAD_DIM:(g + 1) * HEAD_DIM] = out_t[:, cs].T.astype(_BF)


def _nsa_attention(qt, gt, kca, vct, ksa, vst, kwa, vwt, B, S):
    nqb = S // Q_BLOCK
    T = B * S
    nc = kca.shape[1]
    gd = Q_PER_KV * HEAD_DIM
    W = Q_PER_KV * Q_BLOCK
    rows = pl.BlockSpec((1, 1, S, 2 * HEAD_DIM), lambda b, h, i: (b, h, 0, 0))
    cols = pl.BlockSpec((1, 1, VT_ROWS, S), lambda b, h, i: (b, h, 0, 0))
    front = lambda i: jnp.minimum(i, nqb - 1)
    back = lambda i: jnp.maximum(i - 1, 0)
    return pl.pallas_call(
        functools.partial(_attn_kernel, seq=S),
        out_shape=jax.ShapeDtypeStruct((T, N_KV_HEADS * gd), _BF),
        grid=(B, N_KV_HEADS, nqb + 1),
        in_specs=[
            pl.BlockSpec((1, 1, HEAD_DIM, W), lambda b, h, i: (b * nqb + front(i), h, 0, 0)),
            pl.BlockSpec((1, GATE_ROWS, Q_BLOCK), lambda b, h, i: (h, 0, b * nqb + front(i))),
            pl.BlockSpec((1, nc, 2 * HEAD_DIM), lambda b, h, i: (b * N_KV_HEADS + h, 0, 0)),
            pl.BlockSpec((1, HEAD_DIM, nc), lambda b, h, i: (b * N_KV_HEADS + h, 0, 0)),
            rows, cols, rows, cols,
        ],
        out_specs=pl.BlockSpec((Q_BLOCK, gd), lambda b, h, i: (b * nqb + back(i), h)),
        scratch_shapes=[pltpu.VMEM((8, W), _F32), pltpu.VMEM((VT_ROWS, W), _F32),
                        pltpu.VMEM((2, 2 * HEAD_DIM, W), _BF), pltpu.VMEM((2, HEAD_DIM, W), _F32),
                        pltpu.VMEM((2, 8, W), _F32), pltpu.SMEM((2 * (LANES // BITS_PER_WORD + 1),), jnp.int32)],
        compiler_params=pltpu.CompilerParams(
            dimension_semantics=("arbitrary", "arbitrary", "arbitrary"), vmem_limit_bytes=VMEM_LIMIT),
        name="nsa_attention",
    )(qt, gt, kca, vct, ksa, vst, kwa, vwt)


def _merge_ffn_kernel(x_ref, gm_ref, yn_ref, wm1_ref, wpn_ref, wo_ref, g1_ref, b1_ref,
                      w1_ref, w2_ref, g2_ref, b2_ref, o_ref):
    x = x_ref[...]
    gate = _sigmoid(_dot(x.astype(_BF), wm1_ref[...]))
    merged = gm_ref[...] + gate * _dot(yn_ref[...], wpn_ref[...])
    mix = _dot(merged.astype(_BF), wo_ref[...])
    hid = _layer_norm(DEEPNORM_ALPHA * x + mix, g1_ref[...], b1_ref[...])
    hb = hid.astype(_BF)
    f = jnp.zeros(hid.shape, _F32)
    for c in range(D_FF // D_MODEL):
        a = jnp.maximum(_dot(hb, w1_ref[:, c * D_MODEL:(c + 1) * D_MODEL]), 0.0)
        f = f + _dot((a * a).astype(_BF), w2_ref[c * D_MODEL:(c + 1) * D_MODEL, :])
    o_ref[...] = _layer_norm(DEEPNORM_ALPHA * hid + f, g2_ref[...], b2_ref[...])


def _merge_ffn(x2, gm, yn, wm1, wpn, wo, g1, b1, w1, w2, g2, b2):
    T = x2.shape[0]
    tm = ROW_TILE
    rows = lambda w: pl.BlockSpec((tm, w), lambda i: (i, 0))
    const = lambda r, c: _resident((r, c), lambda i: (0, 0))
    return pl.pallas_call(
        _merge_ffn_kernel,
        out_shape=jax.ShapeDtypeStruct((T, D_MODEL), _F32),
        grid=(T // tm,),
        in_specs=[rows(D_MODEL), rows(D_MODEL), rows(D_MODEL),
                  const(D_MODEL, D_MODEL), const(D_MODEL, D_MODEL), const(D_MODEL, D_MODEL),
                  const(1, D_MODEL), const(1, D_MODEL),
                  const(D_MODEL, D_FF), const(D_FF, D_MODEL),
                  const(1, D_MODEL), const(1, D_MODEL)],
        out_specs=rows(D_MODEL),
        compiler_params=pltpu.CompilerParams(
            dimension_semantics=("arbitrary",), vmem_limit_bytes=VMEM_LIMIT),
        name="merge_ffn",
    )(x2, gm, yn, wm1, wpn, wo, g1, b1, w1, w2, g2, b2)


def _layer(x2, B, S, w_in, layer, gm_ln_g, gm_ln_b, gm_w_s, gm_b_s, cmp_pe_k, cmp_w1_k, cmp_w2_k,
           cmp_pe_v, cmp_w1_v, cmp_w2_v, w_proj_gm, w_proj_nsa, w_out,
           ln1_g, ln1_b, w_ff1, w_ff2, ln2_g, ln2_b):
    o_q = 2 * D_MODEL
    o_m = o_q + (Q_PER_KV + 6) * N_KV_HEADS * HEAD_DIM + 3 * Q_PER_KV * N_KV_HEADS
    row = lambda v: v.reshape(1, -1)
    merge_cols = lambda j: w_in[layer * D_MODEL:(layer + 1) * D_MODEL, o_m + j * D_MODEL:o_m + (j + 1) * D_MODEL]

    gm = _gm_mixer(x2, w_in, layer, merge_cols(0), row(gm_ln_g), row(gm_ln_b), gm_w_s,
                   jnp.broadcast_to(gm_b_s[:, :, None], (GM_GROUPS, GM_CHUNK, LANES)), w_proj_gm)

    qt, kc, vc, ksa, vst, kwa, vwt, gt = _qkv_proj(x2, w_in, layer, o_q, B, S)

    heads = lambda a: a.reshape(B * N_KV_HEADS, S, HEAD_DIM)
    pairs = lambda w: w.astype(_BF).reshape(CMP_BLOCK // 2, 2 * HEAD_DIM, HEAD_DIM)
    kca, vct = _nsa_compress(heads(kc), heads(vc), cmp_pe_k, cmp_pe_v, pairs(cmp_w1_k), pairs(cmp_w1_v),
                             cmp_w2_k.astype(_BF), cmp_w2_v.astype(_BF))

    yn = _nsa_attention(qt, gt, kca, vct, ksa, vst, kwa, vwt, B, S)

    return _merge_ffn(x2, gm, yn, merge_cols(1).astype(_BF), w_proj_nsa.astype(_BF),
                      w_out.astype(_BF), row(ln1_g), row(ln1_b), w_ff1.astype(_BF), w_ff2.astype(_BF),
                      row(ln2_g), row(ln2_b))


def kernel(x, w_in, gm_ln_g, gm_ln_b, gm_w_s, gm_b_s, cmp_pe_k, cmp_w1_k, cmp_w2_k, cmp_pe_v, cmp_w1_v, cmp_w2_v, w_proj_gm, w_proj_nsa, w_out, ln1_g, ln1_b, w_ff1, w_ff2, ln2_g, ln2_b):
    B, S, D = x.shape
    assert D == D_MODEL and S % ROW_TILE == 0 and WINDOW + Q_BLOCK <= S <= SEL_BLOCK * LANES
    assert S // POS_RADIX <= POS_RADIX
    h = x.reshape(B * S, D)
    depth, _, d_in = w_in.shape
    w_rows = w_in.reshape(depth * D, d_in)
    for l in range(depth):
        h = _layer(h, B, S, w_rows, l, gm_ln_g[l], gm_ln_b[l], gm_w_s[l], gm_b_s[l],
                   cmp_pe_k[l], cmp_w1_k[l], cmp_w2_k[l], cmp_pe_v[l], cmp_w1_v[l], cmp_w2_v[l],
                   w_proj_gm[l], w_proj_nsa[l], w_out[l], ln1_g[l], ln1_b[l],
                   w_ff1[l], w_ff2[l], ln2_g[l], ln2_b[l])
    return h.reshape(B, S, D)
```

```python
import functools
import math

import jax
import jax.numpy as jnp
from jax import lax
from jax.experimental import pallas as pl
from jax.experimental.pallas import tpu as pltpu

D_MODEL = 1024
GM_GROUPS = 8
GM_CHUNK = 128
N_KV_HEADS = 2
Q_PER_KV = 4
HEAD_DIM = 128
CMP_BLOCK = 32
CMP_STRIDE = 16
SEL_BLOCK = 64
N_SELECT = 16
N_LOCAL_FORCED = 2
WINDOW = 512
Q_BLOCK = 128
D_FF = 4 * D_MODEL
DEEPNORM_ALPHA = 2.0 ** 0.25
LN_EPS = 1e-5
NEG_INF = -1e30
LOG2_E = math.log2(math.e)

LANES = 128
SEL_TILE = 128
LOCAL_TILES = 11
BITS_PER_WORD = 16
POS_RADIX = 256
ROW_TILE = 512
VT_ROWS = HEAD_DIM + 16
GATE_ROWS = 16
VMEM_LIMIT = 56 * 1024 * 1024

_BF = jnp.bfloat16
_F32 = jnp.float32


def _dot(a, b):
    return jnp.dot(a, b, preferred_element_type=_F32)


def _gelu(x):
    c = math.sqrt(2.0 / math.pi)
    return 0.5 * x * (1.0 + jnp.tanh(c * (x + 0.044715 * (x * x * x))))


def _sigmoid(x):
    return 1.0 / (1.0 + jnp.exp(-x))


def _layer_norm(x, g, b):
    mu = jnp.mean(x, axis=-1, keepdims=True)
    xc = x - mu
    var = jnp.mean(xc * xc, axis=-1, keepdims=True)
    return xc * lax.rsqrt(var + LN_EPS) * g + b


def _position_lanes(pos, shape):
    lane = lax.broadcasted_iota(jnp.int32, shape, 1)
    return jnp.where(lane == 0, (pos % POS_RADIX).astype(_F32),
                     jnp.where(lane == 1, (pos // POS_RADIX).astype(_F32), 0.0)).astype(_BF)


def _stage_bf16(first_step, pairs):
    @pl.when(first_step)
    def _():
        for src, dst in pairs:
            dst[...] = src[...].astype(_BF)


def _resident(shape, index_map):
    return pl.BlockSpec(shape, index_map, pipeline_mode=pl.Buffered(1))


def _gm_kernel(x_ref, wgm32_ref, wm032_ref, lng_ref, lnb_ref, ws_ref, bs_ref, wpg32_ref, o_ref,
               vg_ref, wgm_ref, wm0_ref, wpg_ref):
    _stage_bf16(pl.program_id(0) == 0, [(wgm32_ref, wgm_ref), (wm032_ref, wm0_ref), (wpg32_ref, wpg_ref)])
    tm = x_ref.shape[0]
    xb = x_ref[...].astype(_BF)
    z = _gelu(_dot(xb, wgm_ref[...]))
    u = z[:, :D_MODEL]
    v = _layer_norm(z[:, D_MODEL:], lng_ref[...], lnb_ref[...]).astype(_BF)
    row = lax.broadcasted_iota(jnp.int32, (GM_CHUNK, GM_CHUNK), 0)
    col = lax.broadcasted_iota(jnp.int32, (GM_CHUNK, GM_CHUNK), 1)
    for gi in range(GM_GROUPS):
        w = jnp.where(row >= col, ws_ref[gi], 0.0).astype(_BF)
        for c in range(tm // GM_CHUNK):
            blk = v[c * GM_CHUNK:(c + 1) * GM_CHUNK, gi * LANES:(gi + 1) * LANES]
            vg_ref[c * GM_CHUNK:(c + 1) * GM_CHUNK, gi * LANES:(gi + 1) * LANES] = _dot(w, blk) + bs_ref[gi]
    y = (u * vg_ref[...]).astype(_BF)
    gate = _sigmoid(_dot(xb, wm0_ref[...]))
    o_ref[...] = gate * _dot(y, wpg_ref[...])


def _gm_mixer(x2, w_in, layer, wm0, lng, lnb, ws, bs, wpg):
    T = x2.shape[0]
    tm = ROW_TILE
    const2 = lambda i: (0, 0)
    const3 = lambda i: (0, 0, 0)
    return pl.pallas_call(
        _gm_kernel,
        out_shape=jax.ShapeDtypeStruct((T, D_MODEL), _F32),
        grid=(T // tm,),
        in_specs=[
            pl.BlockSpec((tm, D_MODEL), lambda i: (i, 0)),
            _resident((D_MODEL, 2 * D_MODEL), lambda i: (layer, 0)),
            _resident((D_MODEL, D_MODEL), const2),
            pl.BlockSpec((1, D_MODEL), const2),
            pl.BlockSpec((1, D_MODEL), const2),
            pl.BlockSpec((GM_GROUPS, GM_CHUNK, GM_CHUNK), const3),
            pl.BlockSpec((GM_GROUPS, GM_CHUNK, LANES), const3),
            _resident((D_MODEL, D_MODEL), const2),
        ],
        out_specs=pl.BlockSpec((tm, D_MODEL), lambda i: (i, 0)),
        scratch_shapes=[pltpu.VMEM((tm, D_MODEL), _F32), pltpu.VMEM((D_MODEL, 2 * D_MODEL), _BF),
                        pltpu.VMEM((D_MODEL, D_MODEL), _BF), pltpu.VMEM((D_MODEL, D_MODEL), _BF)],
        compiler_params=pltpu.CompilerParams(
            dimension_semantics=("arbitrary",), vmem_limit_bytes=VMEM_LIMIT),
        name="gm_mixer",
    )(x2, w_in, wm0, lng, lnb, ws, bs, wpg)


def _qkv_kernel(x_ref, wq32_ref, wkv32_ref, wg32_ref, qt_ref, kc_ref, vc_ref, ksa_ref, vst_ref, kwa_ref, vwt_ref,
                gt_ref, wq_ref, wkv_ref, wg_ref):
    _stage_bf16((pl.program_id(0) == 0) & (pl.program_id(1) == 0),
                [(wq32_ref, wq_ref), (wkv32_ref, wkv_ref), (wg32_ref, wg_ref)])
    tm = x_ref.shape[0]
    xb = x_ref[...].astype(_BF)
    zq = _dot(xb, wq_ref[...]) * (HEAD_DIM ** -0.5 * LOG2_E)
    for tb in range(tm // Q_BLOCK):
        for h in range(N_KV_HEADS):
            for g in range(Q_PER_KV):
                c0 = (h * Q_PER_KV + g) * HEAD_DIM
                blk = zq[tb * Q_BLOCK:(tb + 1) * Q_BLOCK, c0:c0 + HEAD_DIM]
                qt_ref[tb, h, :, g * Q_BLOCK:(g + 1) * Q_BLOCK] = blk.T.astype(_BF)
    z = _dot(xb, wkv_ref[...])
    kpos = pl.program_id(1) * tm + lax.broadcasted_iota(jnp.int32, (tm, LANES), 0)
    blk_lane = lax.broadcasted_iota(jnp.int32, (tm, LANES), 1)
    sel_lanes = jnp.where(blk_lane == 0, (kpos % SEL_TILE).astype(_F32),
                          jnp.where(kpos // SEL_BLOCK == blk_lane, NEG_INF, 0.0)).astype(_BF)
    win_lanes = _position_lanes(kpos, (tm, LANES))
    ones = jnp.ones((VT_ROWS - HEAD_DIM, tm), _BF)
    for h in range(N_KV_HEADS):
        def col(j):
            return z[:, j * 2 * HEAD_DIM + h * HEAD_DIM: j * 2 * HEAD_DIM + (h + 1) * HEAD_DIM]
        kc_ref[0, h] = col(0)
        vc_ref[0, h] = col(1)
        ksa_ref[0, h, :, :HEAD_DIM] = col(2).astype(_BF)
        ksa_ref[0, h, :, HEAD_DIM:] = sel_lanes
        vst_ref[0, h, :HEAD_DIM] = col(3).T.astype(_BF)
        vst_ref[0, h, HEAD_DIM:] = ones
        kwa_ref[0, h, :, :HEAD_DIM] = col(4).astype(_BF)
        kwa_ref[0, h, :, HEAD_DIM:] = win_lanes
        vwt_ref[0, h, :HEAD_DIM] = col(5).T.astype(_BF)
        vwt_ref[0, h, HEAD_DIM:] = ones
    zg_t = _sigmoid(_dot(xb, wg_ref[...])).T
    per_head = 3 * Q_PER_KV
    for h in range(N_KV_HEADS):
        gt_ref[h] = zg_t[h * per_head:h * per_head + GATE_ROWS]


def _qkv_proj(x2, w_in, layer, q_col, B, S):
    T = x2.shape[0]
    tm = ROW_TILE
    nsb = S // tm
    gd = Q_PER_KV * Q_BLOCK
    aw = N_KV_HEADS * Q_PER_KV * HEAD_DIM
    kvw = 6 * N_KV_HEADS * HEAD_DIM
    kv_col = q_col + aw
    gate_col = kv_col + kvw
    assert q_col % aw == 0 and kv_col % kvw == 0 and gate_col % LANES == 0
    rows_spec = lambda w: pl.BlockSpec((1, N_KV_HEADS, tm, w), lambda b, s: (b, 0, s, 0))
    rows_shape = lambda w, dt: jax.ShapeDtypeStruct((B, N_KV_HEADS, S, w), dt)
    cols_spec = pl.BlockSpec((1, N_KV_HEADS, VT_ROWS, tm), lambda b, s: (b, 0, 0, s))
    cols_shape = jax.ShapeDtypeStruct((B, N_KV_HEADS, VT_ROWS, S), _BF)
    return pl.pallas_call(
        _qkv_kernel,
        out_shape=(
            jax.ShapeDtypeStruct((T // Q_BLOCK, N_KV_HEADS, HEAD_DIM, gd), _BF),
            rows_shape(HEAD_DIM, _F32), rows_shape(HEAD_DIM, _F32),
            rows_shape(2 * HEAD_DIM, _BF), cols_shape,
            rows_shape(2 * HEAD_DIM, _BF), cols_shape,
            jax.ShapeDtypeStruct((N_KV_HEADS, GATE_ROWS, T), _F32),
        ),
        grid=(B, nsb),
        in_specs=[
            pl.BlockSpec((tm, D_MODEL), lambda b, s: (b * nsb + s, 0)),
            _resident((D_MODEL, aw), lambda b, s: (layer, q_col // aw)),
            _resident((D_MODEL, kvw), lambda b, s: (layer, kv_col // kvw)),
            _resident((D_MODEL, LANES), lambda b, s: (layer, gate_col // LANES)),
        ],
        out_specs=(
            pl.BlockSpec((tm // Q_BLOCK, N_KV_HEADS, HEAD_DIM, gd), lambda b, s: (b * nsb + s, 0, 0, 0)),
            rows_spec(HEAD_DIM), rows_spec(HEAD_DIM), rows_spec(2 * HEAD_DIM), cols_spec,
            rows_spec(2 * HEAD_DIM), cols_spec,
            pl.BlockSpec((N_KV_HEADS, GATE_ROWS, tm), lambda b, s: (0, 0, b * nsb + s)),
        ),
        scratch_shapes=[pltpu.VMEM((D_MODEL, aw), _BF), pltpu.VMEM((D_MODEL, kvw), _BF),
                        pltpu.VMEM((D_MODEL, LANES), _BF)],
        compiler_params=pltpu.CompilerParams(
            dimension_semantics=("arbitrary", "arbitrary"), vmem_limit_bytes=VMEM_LIMIT),
        name="qkv_proj",
    )(x2, w_in, w_in, w_in)


def _compress_kernel(kc_ref, vc_ref, pek_ref, pev_ref, w1k_ref, w1v_ref, w2k_ref, w2v_ref, ko_ref, vo_ref):
    nc = ko_ref.shape[1]

    def tokens(src, pe, w1, w2):
        first, second = None, None
        for p in range(0, CMP_STRIDE, 2):
            rows = [src[0, pl.ds(p + d, nc, stride=CMP_STRIDE), :] for d in range(2)]
            lo = jnp.concatenate([rows[d] + pe[p + d:p + d + 1, :] for d in range(2)], axis=1)
            hi = jnp.concatenate([rows[d] + pe[CMP_STRIDE + p + d:CMP_STRIDE + p + d + 1, :] for d in range(2)], axis=1)
            a = _dot(lo.astype(_BF), w1[p // 2])
            b = _dot(hi.astype(_BF), w1[(CMP_STRIDE + p) // 2])
            first = a if first is None else first + a
            second = b if second is None else second + b
        pre = first + pltpu.roll(second, nc - 1, 0)
        return _dot(_gelu(pre).astype(_BF), w2[...])

    ko_ref[0, :, :HEAD_DIM] = tokens(kc_ref, pek_ref, w1k_ref, w2k_ref).astype(_BF)
    start = lax.broadcasted_iota(jnp.int32, (nc, LANES), 0) * CMP_STRIDE
    ko_ref[0, :, HEAD_DIM:] = _position_lanes(start, (nc, LANES))
    vo_ref[0] = tokens(vc_ref, pev_ref, w1v_ref, w2v_ref).T.astype(_BF)


def _nsa_compress(kc, vc, pek, pev, w1k, w1v, w2k, w2v):
    BH, S, _ = kc.shape
    nc = S // CMP_STRIDE
    pair = 2 * HEAD_DIM
    const2 = lambda i: (0, 0)
    const3 = lambda i: (0, 0, 0)
    seq_rows = pl.BlockSpec((1, S, HEAD_DIM), lambda i: (i, 0, 0))
    pe_spec = pl.BlockSpec((CMP_BLOCK, HEAD_DIM), const2)
    w1_spec = pl.BlockSpec((CMP_BLOCK // 2, pair, HEAD_DIM), const3)
    w2_spec = pl.BlockSpec((HEAD_DIM, HEAD_DIM), const2)
    return pl.pallas_call(
        _compress_kernel,
        out_shape=(jax.ShapeDtypeStruct((BH, nc, 2 * HEAD_DIM), _BF),
                   jax.ShapeDtypeStruct((BH, HEAD_DIM, nc), _BF)),
        grid=(BH,),
        in_specs=[seq_rows, seq_rows, pe_spec, pe_spec, w1_spec, w1_spec, w2_spec, w2_spec],
        out_specs=(pl.BlockSpec((1, nc, 2 * HEAD_DIM), lambda i: (i, 0, 0)),
                   pl.BlockSpec((1, HEAD_DIM, nc), lambda i: (i, 0, 0))),
        compiler_params=pltpu.CompilerParams(
            dimension_semantics=("arbitrary",), vmem_limit_bytes=VMEM_LIMIT),
        name="nsa_compress",
    )(kc, vc, pek, pev, w1k, w1v, w2k, w2v)


def _mark_top_blocks(score, notsel, rounds):
    rows = lax.broadcasted_iota(jnp.int32, score.shape, 0).astype(_F32)
    for _ in range(rounds):
        mx = jnp.max(score, axis=0, keepdims=True)
        idx = jnp.min(jnp.where(score == mx, rows, float(LANES)), axis=0, keepdims=True)
        hit = rows == idx
        notsel = jnp.where(hit, 0.0, notsel)
        score = jnp.where(hit, -jnp.inf, score)
    return notsel


def _attn_kernel(qt_ref, gt_ref, kca_ref, vct_ref, ksa_ref, vst_ref, kwa_ref, vwt_ref, o_ref,
                 m_ref, acc_ref, lhs_ref, part_ref, gate_ref, bits_ref, *, seq):
    h = pl.program_id(1)
    step = pl.program_id(2)
    nqb = seq // Q_BLOCK
    nc = kca_ref.shape[1]
    n_sel = seq // SEL_BLOCK
    n_forced = 1 + N_LOCAL_FORCED
    k_top = min(N_SELECT, n_sel)
    G = Q_PER_KV
    W = G * Q_BLOCK
    words = LANES // BITS_PER_WORD
    put = step % 2
    get = 1 - put
    put3 = step % 3
    get3 = (step + 1) % 3

    @pl.when(step == 0)
    def _():
        lhs_ref[1] = jnp.zeros(lhs_ref.shape[1:], _BF)
        part_ref[...] = jnp.zeros(part_ref.shape, _F32)
        gate_ref[...] = jnp.zeros(gate_ref.shape, _F32)
        acc_ref[...] = jnp.ones(acc_ref.shape, _F32)
        for k in range(words + 1):
            bits_ref[words + 1 + k] = 0

    lane_w = lax.broadcasted_iota(jnp.int32, (1, W), 1)
    slope = jnp.zeros((1, W), _F32)
    for g in range(G):
        sg = jnp.where(h == 0, _F32(2.0 ** -(g + 1)), _F32(2.0 ** -(G + g + 1)))
        slope = jnp.where(lane_w // Q_BLOCK == g, sg, slope)
    slope = (slope * LOG2_E).astype(_BF).astype(_F32)
    strips = [slice(g * Q_BLOCK, (g + 1) * Q_BLOCK) for g in range(G)]
    row_aug = lax.broadcasted_iota(jnp.int32, (LANES, W), 0)

    out_t = part_ref[get3] + acc_ref[0:HEAD_DIM] * (gate_ref[get3][0:1] * (1.0 / acc_ref[HEAD_DIM:HEAD_DIM + 1]))
    for g, cs in enumerate(strips):
        o_ref[:, g * HEAD_DIM:(g + 1) * HEAD_DIM] = out_t[:, cs].T.astype(_BF)

    q0 = jnp.minimum(step, nqb - 1) * Q_BLOCK
    tok = q0 + lane_w % Q_BLOCK
    qt = qt_ref[0, 0]
    pos_rows = jnp.where(row_aug == 0, slope, jnp.where(row_aug == 1, slope * POS_RADIX, 0.0)).astype(_BF)
    lhs_pos = jnp.concatenate([qt, pos_rows], axis=0)

    s = _dot(kca_ref[0], lhs_pos)

    p0 = jnp.clip(step - 1, 0, nqb - 1) * Q_BLOCK
    ptok = p0 + lane_w % Q_BLOCK
    lhs_sel = lhs_ref[get]
    key_row = lax.broadcasted_iota(jnp.int32, (SEL_TILE, Q_BLOCK), 0)
    diag = p0 // SEL_TILE

    def tile_scores(j, causal, live=None):
        k0 = j * SEL_TILE if isinstance(j, int) else pl.multiple_of(j * SEL_TILE, SEL_TILE)
        st = _dot(ksa_ref[0, 0, pl.ds(k0, SEL_TILE), :], lhs_sel)
        if causal:
            ahead = jnp.where(k0 + key_row > ptok[:, :Q_BLOCK], NEG_INF, 0.0)
            st = jnp.concatenate([st[:, cs] + ahead for cs in strips], axis=1)
        shift = slope * (p0 - k0).astype(_F32)
        if live is not None:
            shift = jnp.where(live, shift, -NEG_INF)
        return st, shift, k0

    def tile_max(tiles):
        m = None
        for st, shift, _ in tiles:
            cm = jnp.max(st, axis=0, keepdims=True) - shift
            m = cm if m is None else jnp.maximum(m, cm)
        return m

    def tile_sums(tiles, m):
        acc = None
        for st, shift, k0 in tiles:
            sub = m + shift
            pt = jnp.concatenate([jnp.exp2(st[:, cs] - sub[:, cs]) for cs in strips], axis=1)
            ai = _dot(vst_ref[0, 0, :, pl.ds(k0, SEL_TILE)], pt.astype(_BF))
            acc = ai if acc is None else acc + ai
        return acc

    first_local = diag - (LOCAL_TILES - 1)
    tiles = [tile_scores(0, False, live=first_local > 0)]
    for i in range(LOCAL_TILES):
        j = first_local + i
        last = i == LOCAL_TILES - 1
        tiles.append(tile_scores(jnp.maximum(j, 0), last, live=None if last else j >= 0))

    n_row = lax.broadcasted_iota(jnp.int32, (nc, Q_BLOCK), 0)
    n_last = (tok - (CMP_BLOCK - 1)) // CMP_STRIDE
    mask_c = jnp.where(n_row <= n_last[:, :Q_BLOCK], 0.0, NEG_INF)
    e_parts, inv_parts = [], []
    psum = None
    for cs in strips:
        sg = s[:, cs] + mask_c
        eg = jnp.exp2(sg - jnp.max(sg, axis=0, keepdims=True))
        inv = jnp.where(n_last[:, cs] >= 0, 1.0 / jnp.sum(eg, axis=0, keepdims=True), 0.0)
        pg = eg * inv
        psum = pg if psum is None else psum + pg
        e_parts.append(eg.astype(_BF))
        inv_parts.append(inv)
    o_cmp = _dot(vct_ref[0], jnp.concatenate(e_parts, axis=1)) * jnp.concatenate(inv_parts, axis=1)

    mi = lax.broadcasted_iota(jnp.int32, (LANES, nc), 0)
    ni = lax.broadcasted_iota(jnp.int32, (LANES, nc), 1)
    overlap_t = jnp.where((ni * CMP_STRIDE + (CMP_BLOCK - 1) >= mi * SEL_BLOCK)
                          & (ni * CMP_STRIDE <= mi * SEL_BLOCK + (SEL_BLOCK - 1)), 1.0, 0.0).astype(_BF)
    p_hi = psum.astype(_BF)
    p_lo = (psum - p_hi.astype(_F32)).astype(_BF)
    imp_t = _dot(overlap_t, p_hi) + _dot(overlap_t, p_lo)

    anchor = jnp.concatenate([jnp.minimum(imp_t[0:1], 0.0)] * G, axis=1)
    win_rows = jnp.where(row_aug == 0, slope + anchor, jnp.where(row_aug == 1, slope * POS_RADIX, 0.0)).astype(_BF)
    wlen = WINDOW + Q_BLOCK
    w0 = pl.multiple_of(jnp.maximum(q0 - WINDOW, 0), Q_BLOCK)
    sw = _dot(kwa_ref[0, 0, pl.ds(w0, wlen), :], jnp.concatenate([qt, win_rows], axis=0))

    m = tile_max(tiles)
    m_ref[0:1] = m
    acc_ref[...] = tile_sums(tiles, m)

    w_row = w0 + lax.broadcasted_iota(jnp.int32, (wlen, Q_BLOCK), 0)
    dist = tok[:, :Q_BLOCK] - w_row
    mask_w = jnp.where(dist.astype(jnp.uint32) < WINDOW, 0.0, NEG_INF)
    e_parts = []
    for cs in strips:
        sg = sw[:, cs] + mask_w
        e_parts.append(jnp.exp2(sg - jnp.max(sg, axis=0, keepdims=True)).astype(_BF))
    win = _dot(vwt_ref[0, 0, :, pl.ds(w0, wlen)], jnp.concatenate(e_parts, axis=1))
    o_win = win[:HEAD_DIM] * (1.0 / win[HEAD_DIM:HEAD_DIM + 1])

    gates = gt_ref[0]

    def gate_row(branch):
        return jnp.concatenate([gates[3 * g + branch:3 * g + branch + 1] for g in range(G)], axis=1)

    mrow = lax.broadcasted_iota(jnp.int32, (LANES, Q_BLOCK), 0)
    tcol = lax.broadcasted_iota(jnp.int32, (LANES, Q_BLOCK), 1) + q0
    lag = tcol // SEL_BLOCK - mrow
    forced = (mrow == 0) | ((lag >= 0) & (lag < N_LOCAL_FORCED))
    score = jnp.where(forced | (lag < 0) | (mrow >= n_sel), -jnp.inf, imp_t)
    notsel_t = _mark_top_blocks(score, jnp.where(forced, 0.0, 1.0), k_top - n_forced)

    sel_rows = jnp.where(row_aug == 0, slope, jnp.concatenate([notsel_t] * G, axis=1)).astype(_BF)

    blocks_per_tile = SEL_TILE // SEL_BLOCK
    blk_row = lax.broadcasted_iota(jnp.int32, (LANES, 1), 0)
    far_end = (q0 // SEL_TILE - (LOCAL_TILES - 1)) * blocks_per_tile
    blk_on = jnp.where((jnp.min(notsel_t, axis=1, keepdims=True) < 0.5)
                       & (blk_row >= blocks_per_tile) & (blk_row < far_end), 1.0, 0.0)
    blk_bit = jnp.left_shift(1, blk_row % BITS_PER_WORD).astype(_F32)

    lhs_ref[put] = jnp.concatenate([qt, sel_rows], axis=0)
    part_ref[put3] = gate_row(0) * o_cmp + gate_row(2) * o_win
    gate_ref[put3] = jnp.broadcast_to(gate_row(1), gate_ref.shape[1:])
    for k in range(words):
        word = jnp.sum((blk_on * blk_bit)[k * BITS_PER_WORD:(k + 1) * BITS_PER_WORD])
        bits_ref[put * (words + 1) + k] = word.astype(jnp.int32)
    bits_ref[put * (words + 1) + words] = jnp.sum(blk_on).astype(jnp.int32)

    tiles_per_word = BITS_PER_WORD // blocks_per_tile

    def far_tile(j, carry):
        word = bits_ref[get * (words + 1) + j // tiles_per_word]
        tile_bits = (word >> ((j % tiles_per_word) * blocks_per_tile)) & ((1 << blocks_per_tile) - 1)

        @pl.when(tile_bits != 0)
        def _():
            tile = [tile_scores(j, False)]
            m_old = m_ref[0:1]
            m_new = jnp.maximum(m_old, tile_max(tile))
            m_ref[0:1] = m_new
            acc_ref[...] = jnp.exp2(m_old - m_new) * acc_ref[...] + tile_sums(tile, m_new)
        return carry

    @pl.when(bits_ref[get * (words + 1) + words] != 0)
    def _():
        lax.fori_loop(1, first_local, far_tile, 0)


def _nsa_attention(qt, gt, kca, vct, ksa, vst, kwa, vwt, B, S):
    nqb = S // Q_BLOCK
    T = B * S
    nc = kca.shape[1]
    gd = Q_PER_KV * HEAD_DIM
    W = Q_PER_KV * Q_BLOCK
    rows = pl.BlockSpec((1, 1, S, 2 * HEAD_DIM), lambda b, h, i: (b, h, 0, 0))
    cols = pl.BlockSpec((1, 1, VT_ROWS, S), lambda b, h, i: (b, h, 0, 0))
    front = lambda i: jnp.minimum(i, nqb - 1)
    back = lambda i: jnp.maximum(i - 2, 0)
    return pl.pallas_call(
        functools.partial(_attn_kernel, seq=S),
        out_shape=jax.ShapeDtypeStruct((T, N_KV_HEADS * gd), _BF),
        grid=(B, N_KV_HEADS, nqb + 2),
        in_specs=[
            pl.BlockSpec((1, 1, HEAD_DIM, W), lambda b, h, i: (b * nqb + front(i), h, 0, 0)),
            pl.BlockSpec((1, GATE_ROWS, Q_BLOCK), lambda b, h, i: (h, 0, b * nqb + front(i))),
            pl.BlockSpec((1, nc, 2 * HEAD_DIM), lambda b, h, i: (b * N_KV_HEADS + h, 0, 0)),
            pl.BlockSpec((1, HEAD_DIM, nc), lambda b, h, i: (b * N_KV_HEADS + h, 0, 0)),
            rows, cols, rows, cols,
        ],
        out_specs=pl.BlockSpec((Q_BLOCK, gd), lambda b, h, i: (b * nqb + back(i), h)),
        scratch_shapes=[pltpu.VMEM((8, W), _F32), pltpu.VMEM((VT_ROWS, W), _F32),
                        pltpu.VMEM((2, 2 * HEAD_DIM, W), _BF), pltpu.VMEM((3, HEAD_DIM, W), _F32),
                        pltpu.VMEM((3, 8, W), _F32), pltpu.SMEM((2 * (LANES // BITS_PER_WORD + 1),), jnp.int32)],
        compiler_params=pltpu.CompilerParams(
            dimension_semantics=("arbitrary", "arbitrary", "arbitrary"), vmem_limit_bytes=VMEM_LIMIT),
        name="nsa_attention",
    )(qt, gt, kca, vct, ksa, vst, kwa, vwt)


def _merge_ffn_kernel(x_ref, gm_ref, yn_ref, wm1_ref, wpn_ref, wo_ref, g1_ref, b1_ref,
                      w1_ref, w2_ref, g2_ref, b2_ref, o_ref):
    x = x_ref[...]
    gate = _sigmoid(_dot(x.astype(_BF), wm1_ref[...]))
    merged = gm_ref[...] + gate * _dot(yn_ref[...], wpn_ref[...])
    mix = _dot(merged.astype(_BF), wo_ref[...])
    hid = _layer_norm(DEEPNORM_ALPHA * x + mix, g1_ref[...], b1_ref[...])
    hb = hid.astype(_BF)
    f = jnp.zeros(hid.shape, _F32)
    for c in range(D_FF // D_MODEL):
        a = jnp.maximum(_dot(hb, w1_ref[:, c * D_MODEL:(c + 1) * D_MODEL]), 0.0)
        f = f + _dot((a * a).astype(_BF), w2_ref[c * D_MODEL:(c + 1) * D_MODEL, :])
    o_ref[...] = _layer_norm(DEEPNORM_ALPHA * hid + f, g2_ref[...], b2_ref[...])


def _merge_ffn(x2, gm, yn, wm1, wpn, wo, g1, b1, w1, w2, g2, b2):
    T = x2.shape[0]
    tm = ROW_TILE
    rows = lambda w: pl.BlockSpec((tm, w), lambda i: (i, 0))
    const = lambda r, c: _resident((r, c), lambda i: (0, 0))
    return pl.pallas_call(
        _merge_ffn_kernel,
        out_shape=jax.ShapeDtypeStruct((T, D_MODEL), _F32),
        grid=(T // tm,),
        in_specs=[rows(D_MODEL), rows(D_MODEL), rows(D_MODEL),
                  const(D_MODEL, D_MODEL), const(D_MODEL, D_MODEL), const(D_MODEL, D_MODEL),
                  const(1, D_MODEL), const(1, D_MODEL),
                  const(D_MODEL, D_FF), const(D_FF, D_MODEL),
                  const(1, D_MODEL), const(1, D_MODEL)],
        out_specs=rows(D_MODEL),
        compiler_params=pltpu.CompilerParams(
            dimension_semantics=("arbitrary",), vmem_limit_bytes=VMEM_LIMIT),
        name="merge_ffn",
    )(x2, gm, yn, wm1, wpn, wo, g1, b1, w1, w2, g2, b2)


def _layer(x2, B, S, w_in, layer, gm_ln_g, gm_ln_b, gm_w_s, gm_b_s, cmp_pe_k, cmp_w1_k, cmp_w2_k,
           cmp_pe_v, cmp_w1_v, cmp_w2_v, w_proj_gm, w_proj_nsa, w_out,
           ln1_g, ln1_b, w_ff1, w_ff2, ln2_g, ln2_b):
    o_q = 2 * D_MODEL
    o_m = o_q + (Q_PER_KV + 6) * N_KV_HEADS * HEAD_DIM + 3 * Q_PER_KV * N_KV_HEADS
    row = lambda v: v.reshape(1, -1)
    merge_cols = lambda j: w_in[layer * D_MODEL:(layer + 1) * D_MODEL, o_m + j * D_MODEL:o_m + (j + 1) * D_MODEL]

    gm = _gm_mixer(x2, w_in, layer, merge_cols(0), row(gm_ln_g), row(gm_ln_b), gm_w_s,
                   jnp.broadcast_to(gm_b_s[:, :, None], (GM_GROUPS, GM_CHUNK, LANES)), w_proj_gm)

    qt, kc, vc, ksa, vst, kwa, vwt, gt = _qkv_proj(x2, w_in, layer, o_q, B, S)

    heads = lambda a: a.reshape(B * N_KV_HEADS, S, HEAD_DIM)
    pairs = lambda w: w.astype(_BF).reshape(CMP_BLOCK // 2, 2 * HEAD_DIM, HEAD_DIM)
    kca, vct = _nsa_compress(heads(kc), heads(vc), cmp_pe_k, cmp_pe_v, pairs(cmp_w1_k), pairs(cmp_w1_v),
                             cmp_w2_k.astype(_BF), cmp_w2_v.astype(_BF))

    yn = _nsa_attention(qt, gt, kca, vct, ksa, vst, kwa, vwt, B, S)

    return _merge_ffn(x2, gm, yn, merge_cols(1).astype(_BF), w_proj_nsa.astype(_BF),
                      w_out.astype(_BF), row(ln1_g), row(ln1_b), w_ff1.astype(_BF), w_ff2.astype(_BF),
                      row(ln2_g), row(ln2_b))


def kernel(x, w_in, gm_ln_g, gm_ln_b, gm_w_s, gm_b_s, cmp_pe_k, cmp_w1_k, cmp_w2_k, cmp_pe_v, cmp_w1_v, cmp_w2_v, w_proj_gm, w_proj_nsa, w_out, ln1_g, ln1_b, w_ff1, w_ff2, ln2_g, ln2_b):
    B, S, D = x.shape
    assert D == D_MODEL and S % ROW_TILE == 0 and WINDOW + Q_BLOCK <= S <= SEL_BLOCK * LANES
    assert S // POS_RADIX <= POS_RADIX
    h = x.reshape(B * S, D)
    depth, _, d_in = w_in.shape
    w_rows = w_in.reshape(depth * D, d_in)
    for l in range(depth):
        h = _layer(h, B, S, w_rows, l, gm_ln_g[l], gm_ln_b[l], gm_w_s[l], gm_b_s[l],
                   cmp_pe_k[l], cmp_w1_k[l], cmp_w2_k[l], cmp_pe_v[l], cmp_w1_v[l], cmp_w2_v[l],
                   w_proj_gm[l], w_proj_nsa[l], w_out[l], ln1_g[l], ln1_b[l],
                   w_ff1[l], w_ff2[l], ln2_g[l], ln2_b[l])
    return h.reshape(B, S, D)
```

```python
import functools
import math

import jax
import jax.numpy as jnp
from jax import lax
from jax.experimental import pallas as pl
from jax.experimental.pallas import tpu as pltpu

D_MODEL = 1024
GM_GROUPS = 8
GM_CHUNK = 128
N_KV_HEADS = 2
Q_PER_KV = 4
HEAD_DIM = 128
CMP_BLOCK = 32
CMP_STRIDE = 16
SEL_BLOCK = 64
N_SELECT = 16
N_LOCAL_FORCED = 2
WINDOW = 512
Q_BLOCK = 128
D_FF = 4 * D_MODEL
DEEPNORM_ALPHA = 2.0 ** 0.25
LN_EPS = 1e-5
NEG_INF = -1e30
LOG2_E = math.log2(math.e)

LANES = 128
SEL_TILE = 128
LOCAL_TILES = 11
BITS_PER_WORD = 16
POS_RADIX = 256
ROW_TILE = 512
VT_ROWS = HEAD_DIM + 16
GATE_ROWS = 16
VMEM_LIMIT = 56 * 1024 * 1024

_BF = jnp.bfloat16
_F32 = jnp.float32


def _dot(a, b):
    return jnp.dot(a, b, preferred_element_type=_F32)


def _gelu(x):
    c = math.sqrt(2.0 / math.pi)
    return 0.5 * x * (1.0 + jnp.tanh(c * (x + 0.044715 * (x * x * x))))


def _sigmoid(x):
    return 1.0 / (1.0 + jnp.exp(-x))


def _layer_norm(x, g, b):
    mu = jnp.mean(x, axis=-1, keepdims=True)
    xc = x - mu
    var = jnp.mean(xc * xc, axis=-1, keepdims=True)
    return xc * lax.rsqrt(var + LN_EPS) * g + b


def _position_lanes(pos, shape):
    lane = lax.broadcasted_iota(jnp.int32, shape, 1)
    return jnp.where(lane == 0, (pos % POS_RADIX).astype(_F32),
                     jnp.where(lane == 1, (pos // POS_RADIX).astype(_F32), 0.0)).astype(_BF)


def _stage_bf16(first_step, pairs):
    @pl.when(first_step)
    def _():
        for src, dst in pairs:
            dst[...] = src[...].astype(_BF)


def _resident(shape, index_map):
    return pl.BlockSpec(shape, index_map, pipeline_mode=pl.Buffered(1))


def _gm_kernel(x_ref, wgm32_ref, wm032_ref, lng_ref, lnb_ref, ws_ref, bs_ref, wpg32_ref, o_ref,
               vg_ref, wgm_ref, wm0_ref, wpg_ref):
    _stage_bf16(pl.program_id(0) == 0, [(wgm32_ref, wgm_ref), (wm032_ref, wm0_ref), (wpg32_ref, wpg_ref)])
    tm = x_ref.shape[0]
    xb = x_ref[...].astype(_BF)
    z = _gelu(_dot(xb, wgm_ref[...]))
    u = z[:, :D_MODEL]
    v = _layer_norm(z[:, D_MODEL:], lng_ref[...], lnb_ref[...]).astype(_BF)
    row = lax.broadcasted_iota(jnp.int32, (GM_CHUNK, GM_CHUNK), 0)
    col = lax.broadcasted_iota(jnp.int32, (GM_CHUNK, GM_CHUNK), 1)
    for gi in range(GM_GROUPS):
        w = jnp.where(row >= col, ws_ref[gi], 0.0).astype(_BF)
        for c in range(tm // GM_CHUNK):
            blk = v[c * GM_CHUNK:(c + 1) * GM_CHUNK, gi * LANES:(gi + 1) * LANES]
            vg_ref[c * GM_CHUNK:(c + 1) * GM_CHUNK, gi * LANES:(gi + 1) * LANES] = _dot(w, blk) + bs_ref[gi]
    y = (u * vg_ref[...]).astype(_BF)
    gate = _sigmoid(_dot(xb, wm0_ref[...]))
    o_ref[...] = gate * _dot(y, wpg_ref[...])


def _gm_mixer(x2, w_in, layer, wm0, lng, lnb, ws, bs, wpg):
    T = x2.shape[0]
    tm = ROW_TILE
    const2 = lambda i: (0, 0)
    const3 = lambda i: (0, 0, 0)
    return pl.pallas_call(
        _gm_kernel,
        out_shape=jax.ShapeDtypeStruct((T, D_MODEL), _F32),
        grid=(T // tm,),
        in_specs=[
            pl.BlockSpec((tm, D_MODEL), lambda i: (i, 0)),
            _resident((D_MODEL, 2 * D_MODEL), lambda i: (layer, 0)),
            _resident((D_MODEL, D_MODEL), const2),
            pl.BlockSpec((1, D_MODEL), const2),
            pl.BlockSpec((1, D_MODEL), const2),
            pl.BlockSpec((GM_GROUPS, GM_CHUNK, GM_CHUNK), const3),
            pl.BlockSpec((GM_GROUPS, GM_CHUNK, LANES), const3),
            _resident((D_MODEL, D_MODEL), const2),
        ],
        out_specs=pl.BlockSpec((tm, D_MODEL), lambda i: (i, 0)),
        scratch_shapes=[pltpu.VMEM((tm, D_MODEL), _F32), pltpu.VMEM((D_MODEL, 2 * D_MODEL), _BF),
                        pltpu.VMEM((D_MODEL, D_MODEL), _BF), pltpu.VMEM((D_MODEL, D_MODEL), _BF)],
        compiler_params=pltpu.CompilerParams(
            dimension_semantics=("arbitrary",), vmem_limit_bytes=VMEM_LIMIT),
        name="gm_mixer",
    )(x2, w_in, wm0, lng, lnb, ws, bs, wpg)


def _qkv_kernel(x_ref, wq32_ref, wkv32_ref, wg32_ref, qt_ref, kc_ref, vc_ref, ksa_ref, vst_ref, kwa_ref, vwt_ref,
                gt_ref, wq_ref, wkv_ref, wg_ref):
    _stage_bf16((pl.program_id(0) == 0) & (pl.program_id(1) == 0),
                [(wq32_ref, wq_ref), (wkv32_ref, wkv_ref), (wg32_ref, wg_ref)])
    tm = x_ref.shape[0]
    xb = x_ref[...].astype(_BF)
    zq = _dot(xb, wq_ref[...]) * (HEAD_DIM ** -0.5 * LOG2_E)
    for tb in range(tm // Q_BLOCK):
        for h in range(N_KV_HEADS):
            for g in range(Q_PER_KV):
                c0 = (h * Q_PER_KV + g) * HEAD_DIM
                blk = zq[tb * Q_BLOCK:(tb + 1) * Q_BLOCK, c0:c0 + HEAD_DIM]
                qt_ref[tb, h, :, g * Q_BLOCK:(g + 1) * Q_BLOCK] = blk.T.astype(_BF)
    z = _dot(xb, wkv_ref[...])
    kpos = pl.program_id(1) * tm + lax.broadcasted_iota(jnp.int32, (tm, LANES), 0)
    blk_lane = lax.broadcasted_iota(jnp.int32, (tm, LANES), 1)
    sel_lanes = jnp.where(blk_lane == 0, (kpos % SEL_TILE).astype(_F32),
                          jnp.where(kpos // SEL_BLOCK == blk_lane, NEG_INF, 0.0)).astype(_BF)
    win_lanes = _position_lanes(kpos, (tm, LANES))
    ones = jnp.ones((VT_ROWS - HEAD_DIM, tm), _BF)
    for h in range(N_KV_HEADS):
        def col(j):
            return z[:, j * 2 * HEAD_DIM + h * HEAD_DIM: j * 2 * HEAD_DIM + (h + 1) * HEAD_DIM]
        kc_ref[0, h] = col(0)
        vc_ref[0, h] = col(1)
        ksa_ref[0, h, :, :HEAD_DIM] = col(2).astype(_BF)
        ksa_ref[0, h, :, HEAD_DIM:] = sel_lanes
        vst_ref[0, h, :HEAD_DIM] = col(3).T.astype(_BF)
        vst_ref[0, h, HEAD_DIM:] = ones
        kwa_ref[0, h, :, :HEAD_DIM] = col(4).astype(_BF)
        kwa_ref[0, h, :, HEAD_DIM:] = win_lanes
        vwt_ref[0, h, :HEAD_DIM] = col(5).T.astype(_BF)
        vwt_ref[0, h, HEAD_DIM:] = ones
    zg_t = _sigmoid(_dot(xb, wg_ref[...])).T
    per_head = 3 * Q_PER_KV
    for h in range(N_KV_HEADS):
        gt_ref[h] = zg_t[h * per_head:h * per_head + GATE_ROWS]


def _qkv_proj(x2, w_in, layer, q_col, B, S):
    T = x2.shape[0]
    tm = ROW_TILE
    nsb = S // tm
    gd = Q_PER_KV * Q_BLOCK
    aw = N_KV_HEADS * Q_PER_KV * HEAD_DIM
    kvw = 6 * N_KV_HEADS * HEAD_DIM
    kv_col = q_col + aw
    gate_col = kv_col + kvw
    assert q_col % aw == 0 and kv_col % kvw == 0 and gate_col % LANES == 0
    rows_spec = lambda w: pl.BlockSpec((1, N_KV_HEADS, tm, w), lambda b, s: (b, 0, s, 0))
    rows_shape = lambda w, dt: jax.ShapeDtypeStruct((B, N_KV_HEADS, S, w), dt)
    cols_spec = pl.BlockSpec((1, N_KV_HEADS, VT_ROWS, tm), lambda b, s: (b, 0, 0, s))
    cols_shape = jax.ShapeDtypeStruct((B, N_KV_HEADS, VT_ROWS, S), _BF)
    return pl.pallas_call(
        _qkv_kernel,
        out_shape=(
            jax.ShapeDtypeStruct((T // Q_BLOCK, N_KV_HEADS, HEAD_DIM, gd), _BF),
            rows_shape(HEAD_DIM, _F32), rows_shape(HEAD_DIM, _F32),
            rows_shape(2 * HEAD_DIM, _BF), cols_shape,
            rows_shape(2 * HEAD_DIM, _BF), cols_shape,
            jax.ShapeDtypeStruct((N_KV_HEADS, GATE_ROWS, T), _F32),
        ),
        grid=(B, nsb),
        in_specs=[
            pl.BlockSpec((tm, D_MODEL), lambda b, s: (b * nsb + s, 0)),
            _resident((D_MODEL, aw), lambda b, s: (layer, q_col // aw)),
            _resident((D_MODEL, kvw), lambda b, s: (layer, kv_col // kvw)),
            _resident((D_MODEL, LANES), lambda b, s: (layer, gate_col // LANES)),
        ],
        out_specs=(
            pl.BlockSpec((tm // Q_BLOCK, N_KV_HEADS, HEAD_DIM, gd), lambda b, s: (b * nsb + s, 0, 0, 0)),
            rows_spec(HEAD_DIM), rows_spec(HEAD_DIM), rows_spec(2 * HEAD_DIM), cols_spec,
            rows_spec(2 * HEAD_DIM), cols_spec,
            pl.BlockSpec((N_KV_HEADS, GATE_ROWS, tm), lambda b, s: (0, 0, b * nsb + s)),
        ),
        scratch_shapes=[pltpu.VMEM((D_MODEL, aw), _BF), pltpu.VMEM((D_MODEL, kvw), _BF),
                        pltpu.VMEM((D_MODEL, LANES), _BF)],
        compiler_params=pltpu.CompilerParams(
            dimension_semantics=("arbitrary", "arbitrary"), vmem_limit_bytes=VMEM_LIMIT),
        name="qkv_proj",
    )(x2, w_in, w_in, w_in)


def _compress_kernel(kc_ref, vc_ref, pek_ref, pev_ref, w1k_ref, w1v_ref, w2k_ref, w2v_ref, ko_ref, vo_ref):
    nc = ko_ref.shape[1]

    def tokens(src, pe, w1, w2):
        first, second = None, None
        for p in range(0, CMP_STRIDE, 2):
            rows = [src[0, pl.ds(p + d, nc, stride=CMP_STRIDE), :] for d in range(2)]
            lo = jnp.concatenate([rows[d] + pe[p + d:p + d + 1, :] for d in range(2)], axis=1)
            hi = jnp.concatenate([rows[d] + pe[CMP_STRIDE + p + d:CMP_STRIDE + p + d + 1, :] for d in range(2)], axis=1)
            a = _dot(lo.astype(_BF), w1[p // 2])
            b = _dot(hi.astype(_BF), w1[(CMP_STRIDE + p) // 2])
            first = a if first is None else first + a
            second = b if second is None else second + b
        pre = first + pltpu.roll(second, nc - 1, 0)
        return _dot(_gelu(pre).astype(_BF), w2[...])

    ko_ref[0, :, :HEAD_DIM] = tokens(kc_ref, pek_ref, w1k_ref, w2k_ref).astype(_BF)
    start = lax.broadcasted_iota(jnp.int32, (nc, LANES), 0) * CMP_STRIDE
    ko_ref[0, :, HEAD_DIM:] = _position_lanes(start, (nc, LANES))
    vo_ref[0, :HEAD_DIM] = tokens(vc_ref, pev_ref, w1v_ref, w2v_ref).T.astype(_BF)
    mi = lax.broadcasted_iota(jnp.int32, (LANES, nc), 0)
    ni = lax.broadcasted_iota(jnp.int32, (LANES, nc), 1)
    vo_ref[0, HEAD_DIM:] = jnp.where((ni * CMP_STRIDE + (CMP_BLOCK - 1) >= mi * SEL_BLOCK)
                                     & (ni * CMP_STRIDE <= mi * SEL_BLOCK + (SEL_BLOCK - 1)), 1.0, 0.0).astype(_BF)


def _nsa_compress(kc, vc, pek, pev, w1k, w1v, w2k, w2v):
    BH, S, _ = kc.shape
    nc = S // CMP_STRIDE
    pair = 2 * HEAD_DIM
    const2 = lambda i: (0, 0)
    const3 = lambda i: (0, 0, 0)
    seq_rows = pl.BlockSpec((1, S, HEAD_DIM), lambda i: (i, 0, 0))
    pe_spec = pl.BlockSpec((CMP_BLOCK, HEAD_DIM), const2)
    w1_spec = pl.BlockSpec((CMP_BLOCK // 2, pair, HEAD_DIM), const3)
    w2_spec = pl.BlockSpec((HEAD_DIM, HEAD_DIM), const2)
    return pl.pallas_call(
        _compress_kernel,
        out_shape=(jax.ShapeDtypeStruct((BH, nc, 2 * HEAD_DIM), _BF),
                   jax.ShapeDtypeStruct((BH, HEAD_DIM + LANES, nc), _BF)),
        grid=(BH,),
        in_specs=[seq_rows, seq_rows, pe_spec, pe_spec, w1_spec, w1_spec, w2_spec, w2_spec],
        out_specs=(pl.BlockSpec((1, nc, 2 * HEAD_DIM), lambda i: (i, 0, 0)),
                   pl.BlockSpec((1, HEAD_DIM + LANES, nc), lambda i: (i, 0, 0))),
        compiler_params=pltpu.CompilerParams(
            dimension_semantics=("arbitrary",), vmem_limit_bytes=VMEM_LIMIT),
        name="nsa_compress",
    )(kc, vc, pek, pev, w1k, w1v, w2k, w2v)


def _mark_top_blocks(score, notsel, rounds):
    rows = lax.broadcasted_iota(jnp.int32, score.shape, 0).astype(_F32)
    for _ in range(rounds):
        mx = jnp.max(score, axis=0, keepdims=True)
        idx = jnp.min(jnp.where(score == mx, rows, float(LANES)), axis=0, keepdims=True)
        hit = rows == idx
        notsel = jnp.where(hit, 0.0, notsel)
        score = jnp.where(hit, -jnp.inf, score)
    return notsel


def _attn_kernel(qt_ref, gt_ref, kca_ref, vct_ref, ksa_ref, vst_ref, kwa_ref, vwt_ref, o_ref,
                 m_ref, acc_ref, lhs_ref, part_ref, gate_ref, bits_ref, *, seq):
    h = pl.program_id(1)
    step = pl.program_id(2)
    nqb = seq // Q_BLOCK
    nc = kca_ref.shape[1]
    n_sel = seq // SEL_BLOCK
    n_forced = 1 + N_LOCAL_FORCED
    k_top = min(N_SELECT, n_sel)
    G = Q_PER_KV
    W = G * Q_BLOCK
    words = LANES // BITS_PER_WORD
    put = step % 2
    get = 1 - put
    put3 = step % 3
    get3 = (step + 1) % 3

    @pl.when(step == 0)
    def _():
        lhs_ref[1] = jnp.zeros(lhs_ref.shape[1:], _BF)
        part_ref[...] = jnp.zeros(part_ref.shape, _F32)
        gate_ref[...] = jnp.zeros(gate_ref.shape, _F32)
        acc_ref[...] = jnp.ones(acc_ref.shape, _F32)
        for k in range(words + 1):
            bits_ref[words + 1 + k] = 0

    lane_w = lax.broadcasted_iota(jnp.int32, (1, W), 1)
    slope = jnp.zeros((1, W), _F32)
    for g in range(G):
        sg = jnp.where(h == 0, _F32(2.0 ** -(g + 1)), _F32(2.0 ** -(G + g + 1)))
        slope = jnp.where(lane_w // Q_BLOCK == g, sg, slope)
    slope = (slope * LOG2_E).astype(_BF).astype(_F32)
    strips = [slice(g * Q_BLOCK, (g + 1) * Q_BLOCK) for g in range(G)]
    row_aug = lax.broadcasted_iota(jnp.int32, (LANES, W), 0)

    out_t = part_ref[get3] + acc_ref[0:HEAD_DIM] * (gate_ref[get3][0:1] * (1.0 / acc_ref[HEAD_DIM:HEAD_DIM + 1]))
    for g, cs in enumerate(strips):
        o_ref[:, g * HEAD_DIM:(g + 1) * HEAD_DIM] = out_t[:, cs].T.astype(_BF)

    q0 = jnp.minimum(step, nqb - 1) * Q_BLOCK
    tok = q0 + lane_w % Q_BLOCK
    qt = qt_ref[0, 0]
    pos_rows = jnp.where(row_aug == 0, slope, jnp.where(row_aug == 1, slope * POS_RADIX, 0.0)).astype(_BF)
    lhs_pos = jnp.concatenate([qt, pos_rows], axis=0)

    s = _dot(kca_ref[0], lhs_pos)

    p0 = jnp.clip(step - 1, 0, nqb - 1) * Q_BLOCK
    ptok = p0 + lane_w % Q_BLOCK
    lhs_sel = lhs_ref[get]
    key_row = lax.broadcasted_iota(jnp.int32, (SEL_TILE, Q_BLOCK), 0)
    diag = p0 // SEL_TILE

    def tile_scores(j, causal, live=None):
        k0 = j * SEL_TILE if isinstance(j, int) else pl.multiple_of(j * SEL_TILE, SEL_TILE)
        st = _dot(ksa_ref[0, 0, pl.ds(k0, SEL_TILE), :], lhs_sel)
        if causal:
            ahead = jnp.where(k0 + key_row > ptok[:, :Q_BLOCK], NEG_INF, 0.0)
            st = jnp.concatenate([st[:, cs] + ahead for cs in strips], axis=1)
        shift = slope * (p0 - k0).astype(_F32)
        if live is not None:
            shift = jnp.where(live, shift, -NEG_INF)
        return st, shift, k0

    def tile_max(tiles):
        m = None
        for st, shift, _ in tiles:
            cm = jnp.max(st, axis=0, keepdims=True) - shift
            m = cm if m is None else jnp.maximum(m, cm)
        return m

    def tile_sums(tiles, m):
        acc = None
        for st, shift, k0 in tiles:
            sub = m + shift
            pt = jnp.concatenate([jnp.exp2(st[:, cs] - sub[:, cs]) for cs in strips], axis=1)
            ai = _dot(vst_ref[0, 0, :, pl.ds(k0, SEL_TILE)], pt.astype(_BF))
            acc = ai if acc is None else acc + ai
        return acc

    first_local = diag - (LOCAL_TILES - 1)
    tiles = [tile_scores(0, False, live=first_local > 0)]
    for i in range(LOCAL_TILES):
        j = first_local + i
        last = i == LOCAL_TILES - 1
        tiles.append(tile_scores(jnp.maximum(j, 0), last, live=None if last else j >= 0))

    n_row = lax.broadcasted_iota(jnp.int32, (nc, Q_BLOCK), 0)
    n_last = (tok - (CMP_BLOCK - 1)) // CMP_STRIDE
    mask_c = jnp.where(n_row <= n_last[:, :Q_BLOCK], 0.0, NEG_INF)
    e_parts, inv_parts = [], []
    for cs in strips:
        sg = s[:, cs] + mask_c
        eg = jnp.exp2(sg - jnp.max(sg, axis=0, keepdims=True))
        e_parts.append(eg.astype(_BF))
        inv_parts.append(jnp.where(n_last[:, cs] >= 0, 1.0 / jnp.sum(eg, axis=0, keepdims=True), 0.0))
    both = _dot(vct_ref[0], jnp.concatenate(e_parts, axis=1)) * jnp.concatenate(inv_parts, axis=1)
    o_cmp = both[:HEAD_DIM]

    imp_t = both[HEAD_DIM:, strips[0]]
    for cs in strips[1:]:
        imp_t = imp_t + both[HEAD_DIM:, cs]

    anchor = jnp.concatenate([jnp.minimum(imp_t[0:1], 0.0)] * G, axis=1)
    win_rows = jnp.where(row_aug == 0, slope + anchor, jnp.where(row_aug == 1, slope * POS_RADIX, 0.0)).astype(_BF)
    wlen = WINDOW + Q_BLOCK
    w0 = pl.multiple_of(jnp.maximum(q0 - WINDOW, 0), Q_BLOCK)
    sw = _dot(kwa_ref[0, 0, pl.ds(w0, wlen), :], jnp.concatenate([qt, win_rows], axis=0))

    m = tile_max(tiles)
    m_ref[0:1] = m
    acc_ref[...] = tile_sums(tiles, m)

    w_row = w0 + lax.broadcasted_iota(jnp.int32, (wlen, Q_BLOCK), 0)
    dist = tok[:, :Q_BLOCK] - w_row
    mask_w = jnp.where(dist.astype(jnp.uint32) < WINDOW, 0.0, NEG_INF)
    e_parts = []
    for cs in strips:
        sg = sw[:, cs] + mask_w
        e_parts.append(jnp.exp2(sg - jnp.max(sg, axis=0, keepdims=True)).astype(_BF))
    win = _dot(vwt_ref[0, 0, :, pl.ds(w0, wlen)], jnp.concatenate(e_parts, axis=1))
    o_win = win[:HEAD_DIM] * (1.0 / win[HEAD_DIM:HEAD_DIM + 1])

    gates = gt_ref[0]

    def gate_row(branch):
        return jnp.concatenate([gates[3 * g + branch:3 * g + branch + 1] for g in range(G)], axis=1)

    mrow = lax.broadcasted_iota(jnp.int32, (LANES, Q_BLOCK), 0)
    tcol = lax.broadcasted_iota(jnp.int32, (LANES, Q_BLOCK), 1) + q0
    lag = tcol // SEL_BLOCK - mrow
    forced = (mrow == 0) | ((lag >= 0) & (lag < N_LOCAL_FORCED))
    score = jnp.where(forced | (lag < 0) | (mrow >= n_sel), -jnp.inf, imp_t)
    notsel_t = _mark_top_blocks(score, jnp.where(forced, 0.0, 1.0), k_top - n_forced)

    sel_rows = jnp.where(row_aug == 0, slope, jnp.concatenate([notsel_t] * G, axis=1)).astype(_BF)

    blocks_per_tile = SEL_TILE // SEL_BLOCK
    blk_row = lax.broadcasted_iota(jnp.int32, (LANES, 1), 0)
    far_end = (q0 // SEL_TILE - (LOCAL_TILES - 1)) * blocks_per_tile
    blk_on = jnp.where((jnp.min(notsel_t, axis=1, keepdims=True) < 0.5)
                       & (blk_row >= blocks_per_tile) & (blk_row < far_end), 1.0, 0.0)
    blk_bit = jnp.left_shift(1, blk_row % BITS_PER_WORD).astype(_F32)

    lhs_ref[put] = jnp.concatenate([qt, sel_rows], axis=0)
    part_ref[put3] = gate_row(0) * o_cmp + gate_row(2) * o_win
    gate_ref[put3] = jnp.broadcast_to(gate_row(1), gate_ref.shape[1:])
    for k in range(words):
        word = jnp.sum((blk_on * blk_bit)[k * BITS_PER_WORD:(k + 1) * BITS_PER_WORD])
        bits_ref[put * (words + 1) + k] = word.astype(jnp.int32)
    bits_ref[put * (words + 1) + words] = jnp.sum(blk_on).astype(jnp.int32)

    tiles_per_word = BITS_PER_WORD // blocks_per_tile

    def far_tile(j, carry):
        word = bits_ref[get * (words + 1) + j // tiles_per_word]
        tile_bits = (word >> ((j % tiles_per_word) * blocks_per_tile)) & ((1 << blocks_per_tile) - 1)

        @pl.when(tile_bits != 0)
        def _():
            tile = [tile_scores(j, False)]
            m_old = m_ref[0:1]
            m_new = jnp.maximum(m_old, tile_max(tile))
            m_ref[0:1] = m_new
            acc_ref[...] = jnp.exp2(m_old - m_new) * acc_ref[...] + tile_sums(tile, m_new)
        return carry

    @pl.when(bits_ref[get * (words + 1) + words] != 0)
    def _():
        lax.fori_loop(1, first_local, far_tile, 0)


def _nsa_attention(qt, gt, kca, vct, ksa, vst, kwa, vwt, B, S):
    nqb = S // Q_BLOCK
    T = B * S
    nc = kca.shape[1]
    gd = Q_PER_KV * HEAD_DIM
    W = Q_PER_KV * Q_BLOCK
    rows = pl.BlockSpec((1, 1, S, 2 * HEAD_DIM), lambda b, h, i: (b, h, 0, 0))
    cols = pl.BlockSpec((1, 1, VT_ROWS, S), lambda b, h, i: (b, h, 0, 0))
    front = lambda i: jnp.minimum(i, nqb - 1)
    back = lambda i: jnp.maximum(i - 2, 0)
    return pl.pallas_call(
        functools.partial(_attn_kernel, seq=S),
        out_shape=jax.ShapeDtypeStruct((T, N_KV_HEADS * gd), _BF),
        grid=(B, N_KV_HEADS, nqb + 2),
        in_specs=[
            pl.BlockSpec((1, 1, HEAD_DIM, W), lambda b, h, i: (b * nqb + front(i), h, 0, 0)),
            pl.BlockSpec((1, GATE_ROWS, Q_BLOCK), lambda b, h, i: (h, 0, b * nqb + front(i))),
            pl.BlockSpec((1, nc, 2 * HEAD_DIM), lambda b, h, i: (b * N_KV_HEADS + h, 0, 0)),
            pl.BlockSpec((1, HEAD_DIM + LANES, nc), lambda b, h, i: (b * N_KV_HEADS + h, 0, 0)),
            rows, cols, rows, cols,
        ],
        out_specs=pl.BlockSpec((Q_BLOCK, gd), lambda b, h, i: (b * nqb + back(i), h)),
        scratch_shapes=[pltpu.VMEM((8, W), _F32), pltpu.VMEM((VT_ROWS, W), _F32),
                        pltpu.VMEM((2, 2 * HEAD_DIM, W), _BF), pltpu.VMEM((3, HEAD_DIM, W), _F32),
                        pltpu.VMEM((3, 8, W), _F32), pltpu.SMEM((2 * (LANES // BITS_PER_WORD + 1),), jnp.int32)],
        compiler_params=pltpu.CompilerParams(
            dimension_semantics=("arbitrary", "arbitrary", "arbitrary"), vmem_limit_bytes=VMEM_LIMIT),
        name="nsa_attention",
    )(qt, gt, kca, vct, ksa, vst, kwa, vwt)


def _merge_ffn_kernel(x_ref, gm_ref, yn_ref, wm1_ref, wpn_ref, wo_ref, g1_ref, b1_ref,
                      w1_ref, w2_ref, g2_ref, b2_ref, o_ref):
    x = x_ref[...]
    gate = _sigmoid(_dot(x.astype(_BF), wm1_ref[...]))
    merged = gm_ref[...] + gate * _dot(yn_ref[...], wpn_ref[...])
    mix = _dot(merged.astype(_BF), wo_ref[...])
    hid = _layer_norm(DEEPNORM_ALPHA * x + mix, g1_ref[...], b1_ref[...])
    hb = hid.astype(_BF)
    f = jnp.zeros(hid.shape, _F32)
    for c in range(D_FF // D_MODEL):
        a = jnp.maximum(_dot(hb, w1_ref[:, c * D_MODEL:(c + 1) * D_MODEL]), 0.0)
        f = f + _dot((a * a).astype(_BF), w2_ref[c * D_MODEL:(c + 1) * D_MODEL, :])
    o_ref[...] = _layer_norm(DEEPNORM_ALPHA * hid + f, g2_ref[...], b2_ref[...])


def _merge_ffn(x2, gm, yn, wm1, wpn, wo, g1, b1, w1, w2, g2, b2):
    T = x2.shape[0]
    tm = ROW_TILE
    rows = lambda w: pl.BlockSpec((tm, w), lambda i: (i, 0))
    const = lambda r, c: _resident((r, c), lambda i: (0, 0))
    return pl.pallas_call(
        _merge_ffn_kernel,
        out_shape=jax.ShapeDtypeStruct((T, D_MODEL), _F32),
        grid=(T // tm,),
        in_specs=[rows(D_MODEL), rows(D_MODEL), rows(D_MODEL),
                  const(D_MODEL, D_MODEL), const(D_MODEL, D_MODEL), const(D_MODEL, D_MODEL),
                  const(1, D_MODEL), const(1, D_MODEL),
                  const(D_MODEL, D_FF), const(D_FF, D_MODEL),
                  const(1, D_MODEL), const(1, D_MODEL)],
        out_specs=rows(D_MODEL),
        compiler_params=pltpu.CompilerParams(
            dimension_semantics=("arbitrary",), vmem_limit_bytes=VMEM_LIMIT),
        name="merge_ffn",
    )(x2, gm, yn, wm1, wpn, wo, g1, b1, w1, w2, g2, b2)


def _layer(x2, B, S, w_in, layer, gm_ln_g, gm_ln_b, gm_w_s, gm_b_s, cmp_pe_k, cmp_w1_k, cmp_w2_k,
           cmp_pe_v, cmp_w1_v, cmp_w2_v, w_proj_gm, w_proj_nsa, w_out,
           ln1_g, ln1_b, w_ff1, w_ff2, ln2_g, ln2_b):
    o_q = 2 * D_MODEL
    o_m = o_q + (Q_PER_KV + 6) * N_KV_HEADS * HEAD_DIM + 3 * Q_PER_KV * N_KV_HEADS
    row = lambda v: v.reshape(1, -1)
    merge_cols = lambda j: w_in[layer * D_MODEL:(layer + 1) * D_MODEL, o_m + j * D_MODEL:o_m + (j + 1) * D_MODEL]

    gm = _gm_mixer(x2, w_in, layer, merge_cols(0), row(gm_ln_g), row(gm_ln_b), gm_w_s,
                   jnp.broadcast_to(gm_b_s[:, :, None], (GM_GROUPS, GM_CHUNK, LANES)), w_proj_gm)

    qt, kc, vc, ksa, vst, kwa, vwt, gt = _qkv_proj(x2, w_in, layer, o_q, B, S)

    heads = lambda a: a.reshape(B * N_KV_HEADS, S, HEAD_DIM)
    pairs = lambda w: w.astype(_BF).reshape(CMP_BLOCK // 2, 2 * HEAD_DIM, HEAD_DIM)
    kca, vct = _nsa_compress(heads(kc), heads(vc), cmp_pe_k, cmp_pe_v, pairs(cmp_w1_k), pairs(cmp_w1_v),
                             cmp_w2_k.astype(_BF), cmp_w2_v.astype(_BF))

    yn = _nsa_attention(qt, gt, kca, vct, ksa, vst, kwa, vwt, B, S)

    return _merge_ffn(x2, gm, yn, merge_cols(1).astype(_BF), w_proj_nsa.astype(_BF),
                      w_out.astype(_BF), row(ln1_g), row(ln1_b), w_ff1.astype(_BF), w_ff2.astype(_BF),
                      row(ln2_g), row(ln2_b))


def kernel(x, w_in, gm_ln_g, gm_ln_b, gm_w_s, gm_b_s, cmp_pe_k, cmp_w1_k, cmp_w2_k, cmp_pe_v, cmp_w1_v, cmp_w2_v, w_proj_gm, w_proj_nsa, w_out, ln1_g, ln1_b, w_ff1, w_ff2, ln2_g, ln2_b):
    B, S, D = x.shape
    assert D == D_MODEL and S % ROW_TILE == 0 and WINDOW + Q_BLOCK <= S <= SEL_BLOCK * LANES
    assert S // POS_RADIX <= POS_RADIX
    h = x.reshape(B * S, D)
    depth, _, d_in = w_in.shape
    w_rows = w_in.reshape(depth * D, d_in)
    for l in range(depth):
        h = _layer(h, B, S, w_rows, l, gm_ln_g[l], gm_ln_b[l], gm_w_s[l], gm_b_s[l],
                   cmp_pe_k[l], cmp_w1_k[l], cmp_w2_k[l], cmp_pe_v[l], cmp_w1_v[l], cmp_w2_v[l],
                   w_proj_gm[l], w_proj_nsa[l], w_out[l], ln1_g[l], ln1_b[l],
                   w_ff1[l], w_ff2[l], ln2_g[l], ln2_b[l])
    return h.reshape(B, S, D)
```

```python
import functools
import math

import jax
import jax.numpy as jnp
from jax import lax
from jax.experimental import pallas as pl
from jax.experimental.pallas import tpu as pltpu

D_MODEL = 1024
GM_GROUPS = 8
GM_CHUNK = 128
N_KV_HEADS = 2
Q_PER_KV = 4
HEAD_DIM = 128
CMP_BLOCK = 32
CMP_STRIDE = 16
SEL_BLOCK = 64
N_SELECT = 16
N_LOCAL_FORCED = 2
WINDOW = 512
Q_BLOCK = 128
D_FF = 4 * D_MODEL
DEEPNORM_ALPHA = 2.0 ** 0.25
LN_EPS = 1e-5
NEG_INF = -1e30
LOG2_E = math.log2(math.e)

LANES = 128
SEL_TILE = 128
LOCAL_TILES = 11
BITS_PER_WORD = 16
POS_RADIX = 256
ROW_TILE = 512
VT_ROWS = HEAD_DIM + 16
GATE_ROWS = 16
VMEM_LIMIT = 56 * 1024 * 1024

_BF = jnp.bfloat16
_F32 = jnp.float32


def _dot(a, b):
    return jnp.dot(a, b, preferred_element_type=_F32)


def _gelu(x):
    c = math.sqrt(2.0 / math.pi)
    return 0.5 * x * (1.0 + jnp.tanh(c * (x + 0.044715 * (x * x * x))))


def _sigmoid(x):
    return 1.0 / (1.0 + jnp.exp(-x))


def _layer_norm(x, g, b):
    mu = jnp.mean(x, axis=-1, keepdims=True)
    xc = x - mu
    var = jnp.mean(xc * xc, axis=-1, keepdims=True)
    return xc * lax.rsqrt(var + LN_EPS) * g + b


def _position_lanes(pos, shape):
    lane = lax.broadcasted_iota(jnp.int32, shape, 1)
    return jnp.where(lane == 0, (pos % POS_RADIX).astype(_F32),
                     jnp.where(lane == 1, (pos // POS_RADIX).astype(_F32), 0.0)).astype(_BF)


def _stage_bf16(first_step, pairs):
    @pl.when(first_step)
    def _():
        for src, dst in pairs:
            dst[...] = src[...].astype(_BF)


def _resident(shape, index_map):
    return pl.BlockSpec(shape, index_map, pipeline_mode=pl.Buffered(1))


def _gm_kernel(x_ref, wgm32_ref, wm032_ref, lng_ref, lnb_ref, ws_ref, bs_ref, wpg32_ref, o_ref,
               vg_ref, wgm_ref, wm0_ref, wpg_ref):
    _stage_bf16(pl.program_id(0) == 0, [(wgm32_ref, wgm_ref), (wm032_ref, wm0_ref), (wpg32_ref, wpg_ref)])
    tm = x_ref.shape[0]
    xb = x_ref[...].astype(_BF)
    z = _gelu(_dot(xb, wgm_ref[...]))
    gate_logits = _dot(xb, wm0_ref[...])
    u = z[:, :D_MODEL]
    v = _layer_norm(z[:, D_MODEL:], lng_ref[...], lnb_ref[...]).astype(_BF)
    row = lax.broadcasted_iota(jnp.int32, (GM_CHUNK, GM_CHUNK), 0)
    col = lax.broadcasted_iota(jnp.int32, (GM_CHUNK, GM_CHUNK), 1)
    for gi in range(GM_GROUPS):
        w = jnp.where(row >= col, ws_ref[gi], 0.0).astype(_BF)
        for c in range(tm // GM_CHUNK):
            blk = v[c * GM_CHUNK:(c + 1) * GM_CHUNK, gi * LANES:(gi + 1) * LANES]
            vg_ref[c * GM_CHUNK:(c + 1) * GM_CHUNK, gi * LANES:(gi + 1) * LANES] = _dot(w, blk) + bs_ref[gi]
    y = (u * vg_ref[...]).astype(_BF)
    o_ref[...] = _sigmoid(gate_logits) * _dot(y, wpg_ref[...])


def _gm_mixer(x2, w_in, layer, wm0, lng, lnb, ws, bs, wpg):
    T = x2.shape[0]
    tm = ROW_TILE
    const2 = lambda i: (0, 0)
    const3 = lambda i: (0, 0, 0)
    return pl.pallas_call(
        _gm_kernel,
        out_shape=jax.ShapeDtypeStruct((T, D_MODEL), _F32),
        grid=(T // tm,),
        in_specs=[
            pl.BlockSpec((tm, D_MODEL), lambda i: (i, 0)),
            _resident((D_MODEL, 2 * D_MODEL), lambda i: (layer, 0)),
            _resident((D_MODEL, D_MODEL), const2),
            pl.BlockSpec((1, D_MODEL), const2),
            pl.BlockSpec((1, D_MODEL), const2),
            pl.BlockSpec((GM_GROUPS, GM_CHUNK, GM_CHUNK), const3),
            pl.BlockSpec((GM_GROUPS, GM_CHUNK, LANES), const3),
            _resident((D_MODEL, D_MODEL), const2),
        ],
        out_specs=pl.BlockSpec((tm, D_MODEL), lambda i: (i, 0)),
        scratch_shapes=[pltpu.VMEM((tm, D_MODEL), _F32), pltpu.VMEM((D_MODEL, 2 * D_MODEL), _BF),
                        pltpu.VMEM((D_MODEL, D_MODEL), _BF), pltpu.VMEM((D_MODEL, D_MODEL), _BF)],
        compiler_params=pltpu.CompilerParams(
            dimension_semantics=("arbitrary",), vmem_limit_bytes=VMEM_LIMIT),
        name="gm_mixer",
    )(x2, w_in, wm0, lng, lnb, ws, bs, wpg)


def _qkv_kernel(x_ref, wq32_ref, wkv32_ref, wg32_ref, qt_ref, kc_ref, vc_ref, ksa_ref, vst_ref, kwa_ref, vwt_ref,
                gt_ref, wq_ref, wkv_ref, wg_ref):
    _stage_bf16((pl.program_id(0) == 0) & (pl.program_id(1) == 0),
                [(wq32_ref, wq_ref), (wkv32_ref, wkv_ref), (wg32_ref, wg_ref)])
    tm = x_ref.shape[0]
    xb = x_ref[...].astype(_BF)
    z = _dot(xb, wkv_ref[...])
    zq = _dot(xb, wq_ref[...]) * (HEAD_DIM ** -0.5 * LOG2_E)
    for tb in range(tm // Q_BLOCK):
        for h in range(N_KV_HEADS):
            for g in range(Q_PER_KV):
                c0 = (h * Q_PER_KV + g) * HEAD_DIM
                blk = zq[tb * Q_BLOCK:(tb + 1) * Q_BLOCK, c0:c0 + HEAD_DIM]
                qt_ref[tb, h, :, g * Q_BLOCK:(g + 1) * Q_BLOCK] = blk.T.astype(_BF)
    kpos = pl.program_id(1) * tm + lax.broadcasted_iota(jnp.int32, (tm, LANES), 0)
    blk_lane = lax.broadcasted_iota(jnp.int32, (tm, LANES), 1)
    sel_lanes = jnp.where(blk_lane == 0, (kpos % SEL_TILE).astype(_F32),
                          jnp.where(kpos // SEL_BLOCK == blk_lane, NEG_INF, 0.0)).astype(_BF)
    win_lanes = _position_lanes(kpos, (tm, LANES))
    ones = jnp.ones((VT_ROWS - HEAD_DIM, tm), _BF)
    for h in range(N_KV_HEADS):
        def col(j):
            return z[:, j * 2 * HEAD_DIM + h * HEAD_DIM: j * 2 * HEAD_DIM + (h + 1) * HEAD_DIM]
        kc_ref[0, h] = col(0)
        vc_ref[0, h] = col(1)
        ksa_ref[0, h, :, :HEAD_DIM] = col(2).astype(_BF)
        ksa_ref[0, h, :, HEAD_DIM:] = sel_lanes
        vst_ref[0, h, :HEAD_DIM] = col(3).T.astype(_BF)
        vst_ref[0, h, HEAD_DIM:] = ones
        kwa_ref[0, h, :, :HEAD_DIM] = col(4).astype(_BF)
        kwa_ref[0, h, :, HEAD_DIM:] = win_lanes
        vwt_ref[0, h, :HEAD_DIM] = col(5).T.astype(_BF)
        vwt_ref[0, h, HEAD_DIM:] = ones
    zg_t = _sigmoid(_dot(xb, wg_ref[...])).T
    per_head = 3 * Q_PER_KV
    for h in range(N_KV_HEADS):
        gt_ref[h] = zg_t[h * per_head:h * per_head + GATE_ROWS]


def _qkv_proj(x2, w_in, layer, q_col, B, S):
    T = x2.shape[0]
    tm = ROW_TILE
    nsb = S // tm
    gd = Q_PER_KV * Q_BLOCK
    aw = N_KV_HEADS * Q_PER_KV * HEAD_DIM
    kvw = 6 * N_KV_HEADS * HEAD_DIM
    kv_col = q_col + aw
    gate_col = kv_col + kvw
    assert q_col % aw == 0 and kv_col % kvw == 0 and gate_col % LANES == 0
    rows_spec = lambda w: pl.BlockSpec((1, N_KV_HEADS, tm, w), lambda b, s: (b, 0, s, 0))
    rows_shape = lambda w, dt: jax.ShapeDtypeStruct((B, N_KV_HEADS, S, w), dt)
    cols_spec = pl.BlockSpec((1, N_KV_HEADS, VT_ROWS, tm), lambda b, s: (b, 0, 0, s))
    cols_shape = jax.ShapeDtypeStruct((B, N_KV_HEADS, VT_ROWS, S), _BF)
    return pl.pallas_call(
        _qkv_kernel,
        out_shape=(
            jax.ShapeDtypeStruct((T // Q_BLOCK, N_KV_HEADS, HEAD_DIM, gd), _BF),
            rows_shape(HEAD_DIM, _F32), rows_shape(HEAD_DIM, _F32),
            rows_shape(2 * HEAD_DIM, _BF), cols_shape,
            rows_shape(2 * HEAD_DIM, _BF), cols_shape,
            jax.ShapeDtypeStruct((N_KV_HEADS, GATE_ROWS, T), _F32),
        ),
        grid=(B, nsb),
        in_specs=[
            pl.BlockSpec((tm, D_MODEL), lambda b, s: (b * nsb + s, 0)),
            _resident((D_MODEL, aw), lambda b, s: (layer, q_col // aw)),
            _resident((D_MODEL, kvw), lambda b, s: (layer, kv_col // kvw)),
            _resident((D_MODEL, LANES), lambda b, s: (layer, gate_col // LANES)),
        ],
        out_specs=(
            pl.BlockSpec((tm // Q_BLOCK, N_KV_HEADS, HEAD_DIM, gd), lambda b, s: (b * nsb + s, 0, 0, 0)),
            rows_spec(HEAD_DIM), rows_spec(HEAD_DIM), rows_spec(2 * HEAD_DIM), cols_spec,
            rows_spec(2 * HEAD_DIM), cols_spec,
            pl.BlockSpec((N_KV_HEADS, GATE_ROWS, tm), lambda b, s: (0, 0, b * nsb + s)),
        ),
        scratch_shapes=[pltpu.VMEM((D_MODEL, aw), _BF), pltpu.VMEM((D_MODEL, kvw), _BF),
                        pltpu.VMEM((D_MODEL, LANES), _BF)],
        compiler_params=pltpu.CompilerParams(
            dimension_semantics=("arbitrary", "arbitrary"), vmem_limit_bytes=VMEM_LIMIT),
        name="qkv_proj",
    )(x2, w_in, w_in, w_in)


def _compress_kernel(kc_ref, vc_ref, pek_ref, pev_ref, w1k_ref, w1v_ref, w2k_ref, w2v_ref, ko_ref, vo_ref):
    nc = ko_ref.shape[1]

    def tokens(src, pe, w1, w2):
        first, second = None, None
        for p in range(0, CMP_STRIDE, 2):
            rows = [src[0, pl.ds(p + d, nc, stride=CMP_STRIDE), :] for d in range(2)]
            lo = jnp.concatenate([rows[d] + pe[p + d:p + d + 1, :] for d in range(2)], axis=1)
            hi = jnp.concatenate([rows[d] + pe[CMP_STRIDE + p + d:CMP_STRIDE + p + d + 1, :] for d in range(2)], axis=1)
            a = _dot(lo.astype(_BF), w1[p // 2])
            b = _dot(hi.astype(_BF), w1[(CMP_STRIDE + p) // 2])
            first = a if first is None else first + a
            second = b if second is None else second + b
        pre = first + pltpu.roll(second, nc - 1, 0)
        return _dot(_gelu(pre).astype(_BF), w2[...])

    ko_ref[0, :, :HEAD_DIM] = tokens(kc_ref, pek_ref, w1k_ref, w2k_ref).astype(_BF)
    start = lax.broadcasted_iota(jnp.int32, (nc, LANES), 0) * CMP_STRIDE
    ko_ref[0, :, HEAD_DIM:] = _position_lanes(start, (nc, LANES))
    vo_ref[0, :HEAD_DIM] = tokens(vc_ref, pev_ref, w1v_ref, w2v_ref).T.astype(_BF)
    mi = lax.broadcasted_iota(jnp.int32, (LANES, nc), 0)
    ni = lax.broadcasted_iota(jnp.int32, (LANES, nc), 1)
    vo_ref[0, HEAD_DIM:] = jnp.where((ni * CMP_STRIDE + (CMP_BLOCK - 1) >= mi * SEL_BLOCK)
                                     & (ni * CMP_STRIDE <= mi * SEL_BLOCK + (SEL_BLOCK - 1)), 1.0, 0.0).astype(_BF)


def _nsa_compress(kc, vc, pek, pev, w1k, w1v, w2k, w2v):
    BH, S, _ = kc.shape
    nc = S // CMP_STRIDE
    pair = 2 * HEAD_DIM
    const2 = lambda i: (0, 0)
    const3 = lambda i: (0, 0, 0)
    seq_rows = pl.BlockSpec((1, S, HEAD_DIM), lambda i: (i, 0, 0))
    pe_spec = pl.BlockSpec((CMP_BLOCK, HEAD_DIM), const2)
    w1_spec = pl.BlockSpec((CMP_BLOCK // 2, pair, HEAD_DIM), const3)
    w2_spec = pl.BlockSpec((HEAD_DIM, HEAD_DIM), const2)
    return pl.pallas_call(
        _compress_kernel,
        out_shape=(jax.ShapeDtypeStruct((BH, nc, 2 * HEAD_DIM), _BF),
                   jax.ShapeDtypeStruct((BH, HEAD_DIM + LANES, nc), _BF)),
        grid=(BH,),
        in_specs=[seq_rows, seq_rows, pe_spec, pe_spec, w1_spec, w1_spec, w2_spec, w2_spec],
        out_specs=(pl.BlockSpec((1, nc, 2 * HEAD_DIM), lambda i: (i, 0, 0)),
                   pl.BlockSpec((1, HEAD_DIM + LANES, nc), lambda i: (i, 0, 0))),
        compiler_params=pltpu.CompilerParams(
            dimension_semantics=("arbitrary",), vmem_limit_bytes=VMEM_LIMIT),
        name="nsa_compress",
    )(kc, vc, pek, pev, w1k, w1v, w2k, w2v)


def _mark_top_blocks(score, notsel, rounds):
    rows = lax.broadcasted_iota(jnp.int32, score.shape, 0).astype(_F32)
    for _ in range(rounds):
        mx = jnp.max(score, axis=0, keepdims=True)
        idx = jnp.min(jnp.where(score == mx, rows, float(LANES)), axis=0, keepdims=True)
        hit = rows == idx
        notsel = jnp.where(hit, 0.0, notsel)
        score = jnp.where(hit, -jnp.inf, score)
    return notsel


def _attn_kernel(qt_ref, gt_ref, kca_ref, vct_ref, ksa_ref, vst_ref, kwa_ref, vwt_ref, o_ref,
                 m_ref, acc_ref, lhs_ref, part_ref, gate_ref, bits_ref, *, seq):
    h = pl.program_id(1)
    step = pl.program_id(2)
    nqb = seq // Q_BLOCK
    nc = kca_ref.shape[1]
    n_sel = seq // SEL_BLOCK
    n_forced = 1 + N_LOCAL_FORCED
    k_top = min(N_SELECT, n_sel)
    G = Q_PER_KV
    W = G * Q_BLOCK
    words = LANES // BITS_PER_WORD
    put = step % 2
    get = 1 - put
    put3 = step % 3
    get3 = (step + 1) % 3

    @pl.when(step == 0)
    def _():
        lhs_ref[1] = jnp.zeros(lhs_ref.shape[1:], _BF)
        part_ref[...] = jnp.zeros(part_ref.shape, _F32)
        gate_ref[...] = jnp.zeros(gate_ref.shape, _F32)
        acc_ref[...] = jnp.ones(acc_ref.shape, _F32)
        for k in range(words + 1):
            bits_ref[words + 1 + k] = 0

    lane_w = lax.broadcasted_iota(jnp.int32, (1, W), 1)
    slope = jnp.zeros((1, W), _F32)
    for g in range(G):
        sg = jnp.where(h == 0, _F32(2.0 ** -(g + 1)), _F32(2.0 ** -(G + g + 1)))
        slope = jnp.where(lane_w // Q_BLOCK == g, sg, slope)
    slope = (slope * LOG2_E).astype(_BF).astype(_F32)
    strips = [slice(g * Q_BLOCK, (g + 1) * Q_BLOCK) for g in range(G)]
    row_aug = lax.broadcasted_iota(jnp.int32, (LANES, W), 0)

    out_t = part_ref[get3] + acc_ref[0:HEAD_DIM] * (gate_ref[get3][0:1] * (1.0 / acc_ref[HEAD_DIM:HEAD_DIM + 1]))
    for g, cs in enumerate(strips):
        o_ref[:, g * HEAD_DIM:(g + 1) * HEAD_DIM] = out_t[:, cs].T.astype(_BF)

    q0 = jnp.minimum(step, nqb - 1) * Q_BLOCK
    tok = q0 + lane_w % Q_BLOCK
    qt = qt_ref[0, 0]
    pos_rows = jnp.where(row_aug == 0, slope, jnp.where(row_aug == 1, slope * POS_RADIX, 0.0)).astype(_BF)
    lhs_pos = jnp.concatenate([qt, pos_rows], axis=0)

    s = _dot(kca_ref[0], lhs_pos)

    p0 = jnp.clip(step - 1, 0, nqb - 1) * Q_BLOCK
    ptok = p0 + lane_w % Q_BLOCK
    lhs_sel = lhs_ref[get]
    key_row = lax.broadcasted_iota(jnp.int32, (SEL_TILE, Q_BLOCK), 0)
    diag = p0 // SEL_TILE

    def tile_scores(j, causal, live=None):
        k0 = j * SEL_TILE if isinstance(j, int) else pl.multiple_of(j * SEL_TILE, SEL_TILE)
        st = _dot(ksa_ref[0, 0, pl.ds(k0, SEL_TILE), :], lhs_sel)
        if causal:
            ahead = jnp.where(k0 + key_row > ptok[:, :Q_BLOCK], NEG_INF, 0.0)
            st = jnp.concatenate([st[:, cs] + ahead for cs in strips], axis=1)
        shift = slope * (p0 - k0).astype(_F32)
        if live is not None:
            shift = jnp.where(live, shift, -NEG_INF)
        return st, shift, k0

    def tile_max(tiles):
        m = None
        for st, shift, _ in tiles:
            cm = jnp.max(st, axis=0, keepdims=True) - shift
            m = cm if m is None else jnp.maximum(m, cm)
        return m

    def tile_sums(tiles, m):
        acc = None
        for st, shift, k0 in tiles:
            sub = m + shift
            pt = jnp.concatenate([jnp.exp2(st[:, cs] - sub[:, cs]) for cs in strips], axis=1)
            ai = _dot(vst_ref[0, 0, :, pl.ds(k0, SEL_TILE)], pt.astype(_BF))
            acc = ai if acc is None else acc + ai
        return acc

    first_local = diag - (LOCAL_TILES - 1)
    tiles = [tile_scores(0, False, live=first_local > 0)]
    for i in range(LOCAL_TILES):
        j = first_local + i
        last = i == LOCAL_TILES - 1
        tiles.append(tile_scores(jnp.maximum(j, 0), last, live=None if last else j >= 0))

    n_row = lax.broadcasted_iota(jnp.int32, (nc, Q_BLOCK), 0)
    n_last = (tok - (CMP_BLOCK - 1)) // CMP_STRIDE
    mask_c = jnp.where(n_row <= n_last[:, :Q_BLOCK], 0.0, NEG_INF)
    e_parts, inv_parts = [], []
    for cs in strips:
        sg = s[:, cs] + mask_c
        eg = jnp.exp2(sg - jnp.max(sg, axis=0, keepdims=True))
        e_parts.append(eg.astype(_BF))
        inv_parts.append(jnp.where(n_last[:, cs] >= 0, 1.0 / jnp.sum(eg, axis=0, keepdims=True), 0.0))
    both = _dot(vct_ref[0], jnp.concatenate(e_parts, axis=1)) * jnp.concatenate(inv_parts, axis=1)
    o_cmp = both[:HEAD_DIM]

    imp_t = both[HEAD_DIM:, strips[0]]
    for cs in strips[1:]:
        imp_t = imp_t + both[HEAD_DIM:, cs]

    anchor = jnp.concatenate([jnp.minimum(imp_t[0:1], 0.0)] * G, axis=1)
    win_rows = jnp.where(row_aug == 0, slope + anchor, jnp.where(row_aug == 1, slope * POS_RADIX, 0.0)).astype(_BF)
    wlen = WINDOW + Q_BLOCK
    w0 = pl.multiple_of(jnp.maximum(q0 - WINDOW, 0), Q_BLOCK)
    sw = _dot(kwa_ref[0, 0, pl.ds(w0, wlen), :], jnp.concatenate([qt, win_rows], axis=0))

    m = tile_max(tiles)
    m_ref[0:1] = m
    acc_ref[...] = tile_sums(tiles, m)

    w_row = w0 + lax.broadcasted_iota(jnp.int32, (wlen, Q_BLOCK), 0)
    dist = tok[:, :Q_BLOCK] - w_row
    mask_w = jnp.where(dist.astype(jnp.uint32) < WINDOW, 0.0, NEG_INF)
    e_parts = []
    for cs in strips:
        sg = sw[:, cs] + mask_w
        e_parts.append(jnp.exp2(sg - jnp.max(sg, axis=0, keepdims=True)).astype(_BF))
    win = _dot(vwt_ref[0, 0, :, pl.ds(w0, wlen)], jnp.concatenate(e_parts, axis=1))
    o_win = win[:HEAD_DIM] * (1.0 / win[HEAD_DIM:HEAD_DIM + 1])

    gates = gt_ref[0]

    def gate_row(branch):
        return jnp.concatenate([gates[3 * g + branch:3 * g + branch + 1] for g in range(G)], axis=1)

    mrow = lax.broadcasted_iota(jnp.int32, (LANES, Q_BLOCK), 0)
    tcol = lax.broadcasted_iota(jnp.int32, (LANES, Q_BLOCK), 1) + q0
    lag = tcol // SEL_BLOCK - mrow
    forced = (mrow == 0) | ((lag >= 0) & (lag < N_LOCAL_FORCED))
    score = jnp.where(forced | (lag < 0) | (mrow >= n_sel), -jnp.inf, imp_t)
    notsel_t = _mark_top_blocks(score, jnp.where(forced, 0.0, 1.0), k_top - n_forced)

    sel_rows = jnp.where(row_aug == 0, slope, jnp.concatenate([notsel_t] * G, axis=1)).astype(_BF)

    blocks_per_tile = SEL_TILE // SEL_BLOCK
    blk_row = lax.broadcasted_iota(jnp.int32, (LANES, 1), 0)
    far_end = (q0 // SEL_TILE - (LOCAL_TILES - 1)) * blocks_per_tile
    blk_on = jnp.where((jnp.min(notsel_t, axis=1, keepdims=True) < 0.5)
                       & (blk_row >= blocks_per_tile) & (blk_row < far_end), 1.0, 0.0)
    blk_bit = jnp.left_shift(1, blk_row % BITS_PER_WORD).astype(_F32)

    lhs_ref[put] = jnp.concatenate([qt, sel_rows], axis=0)
    part_ref[put3] = gate_row(0) * o_cmp + gate_row(2) * o_win
    gate_ref[put3] = jnp.broadcast_to(gate_row(1), gate_ref.shape[1:])
    for k in range(words):
        word = jnp.sum((blk_on * blk_bit)[k * BITS_PER_WORD:(k + 1) * BITS_PER_WORD])
        bits_ref[put * (words + 1) + k] = word.astype(jnp.int32)
    bits_ref[put * (words + 1) + words] = jnp.sum(blk_on).astype(jnp.int32)

    tiles_per_word = BITS_PER_WORD // blocks_per_tile

    def far_tile(j, carry):
        word = bits_ref[get * (words + 1) + j // tiles_per_word]
        tile_bits = (word >> ((j % tiles_per_word) * blocks_per_tile)) & ((1 << blocks_per_tile) - 1)

        @pl.when(tile_bits != 0)
        def _():
            tile = [tile_scores(j, False)]
            m_old = m_ref[0:1]
            m_new = jnp.maximum(m_old, tile_max(tile))
            m_ref[0:1] = m_new
            acc_ref[...] = jnp.exp2(m_old - m_new) * acc_ref[...] + tile_sums(tile, m_new)
        return carry

    @pl.when(bits_ref[get * (words + 1) + words] != 0)
    def _():
        lax.fori_loop(1, first_local, far_tile, 0)


def _nsa_attention(qt, gt, kca, vct, ksa, vst, kwa, vwt, B, S):
    nqb = S // Q_BLOCK
    T = B * S
    nc = kca.shape[1]
    gd = Q_PER_KV * HEAD_DIM
    W = Q_PER_KV * Q_BLOCK
    rows = pl.BlockSpec((1, 1, S, 2 * HEAD_DIM), lambda b, h, i: (b, h, 0, 0))
    cols = pl.BlockSpec((1, 1, VT_ROWS, S), lambda b, h, i: (b, h, 0, 0))
    front = lambda i: jnp.minimum(i, nqb - 1)
    back = lambda i: jnp.maximum(i - 2, 0)
    return pl.pallas_call(
        functools.partial(_attn_kernel, seq=S),
        out_shape=jax.ShapeDtypeStruct((T, N_KV_HEADS * gd), _BF),
        grid=(B, N_KV_HEADS, nqb + 2),
        in_specs=[
            pl.BlockSpec((1, 1, HEAD_DIM, W), lambda b, h, i: (b * nqb + front(i), h, 0, 0)),
            pl.BlockSpec((1, GATE_ROWS, Q_BLOCK), lambda b, h, i: (h, 0, b * nqb + front(i))),
            pl.BlockSpec((1, nc, 2 * HEAD_DIM), lambda b, h, i: (b * N_KV_HEADS + h, 0, 0)),
            pl.BlockSpec((1, HEAD_DIM + LANES, nc), lambda b, h, i: (b * N_KV_HEADS + h, 0, 0)),
            rows, cols, rows, cols,
        ],
        out_specs=pl.BlockSpec((Q_BLOCK, gd), lambda b, h, i: (b * nqb + back(i), h)),
        scratch_shapes=[pltpu.VMEM((8, W), _F32), pltpu.VMEM((VT_ROWS, W), _F32),
                        pltpu.VMEM((2, 2 * HEAD_DIM, W), _BF), pltpu.VMEM((3, HEAD_DIM, W), _F32),
                        pltpu.VMEM((3, 8, W), _F32), pltpu.SMEM((2 * (LANES // BITS_PER_WORD + 1),), jnp.int32)],
        compiler_params=pltpu.CompilerParams(
            dimension_semantics=("arbitrary", "arbitrary", "arbitrary"), vmem_limit_bytes=VMEM_LIMIT),
        name="nsa_attention",
    )(qt, gt, kca, vct, ksa, vst, kwa, vwt)


def _merge_ffn_kernel(x_ref, gm_ref, yn_ref, wm1_ref, wpn_ref, wo_ref, g1_ref, b1_ref,
                      w1_ref, w2_ref, g2_ref, b2_ref, o_ref):
    x = x_ref[...]
    gate = _sigmoid(_dot(x.astype(_BF), wm1_ref[...]))
    merged = gm_ref[...] + gate * _dot(yn_ref[...], wpn_ref[...])
    mix = _dot(merged.astype(_BF), wo_ref[...])
    hid = _layer_norm(DEEPNORM_ALPHA * x + mix, g1_ref[...], b1_ref[...])
    hb = hid.astype(_BF)
    f = jnp.zeros(hid.shape, _F32)
    for c in range(D_FF // D_MODEL):
        a = jnp.maximum(_dot(hb, w1_ref[:, c * D_MODEL:(c + 1) * D_MODEL]), 0.0)
        f = f + _dot((a * a).astype(_BF), w2_ref[c * D_MODEL:(c + 1) * D_MODEL, :])
    o_ref[...] = _layer_norm(DEEPNORM_ALPHA * hid + f, g2_ref[...], b2_ref[...])


def _merge_ffn(x2, gm, yn, wm1, wpn, wo, g1, b1, w1, w2, g2, b2):
    T = x2.shape[0]
    tm = ROW_TILE
    rows = lambda w: pl.BlockSpec((tm, w), lambda i: (i, 0))
    const = lambda r, c: _resident((r, c), lambda i: (0, 0))
    return pl.pallas_call(
        _merge_ffn_kernel,
        out_shape=jax.ShapeDtypeStruct((T, D_MODEL), _F32),
        grid=(T // tm,),
        in_specs=[rows(D_MODEL), rows(D_MODEL), rows(D_MODEL),
                  const(D_MODEL, D_MODEL), const(D_MODEL, D_MODEL), const(D_MODEL, D_MODEL),
                  const(1, D_MODEL), const(1, D_MODEL),
                  const(D_MODEL, D_FF), const(D_FF, D_MODEL),
                  const(1, D_MODEL), const(1, D_MODEL)],
        out_specs=rows(D_MODEL),
        compiler_params=pltpu.CompilerParams(
            dimension_semantics=("arbitrary",), vmem_limit_bytes=VMEM_LIMIT),
        name="merge_ffn",
    )(x2, gm, yn, wm1, wpn, wo, g1, b1, w1, w2, g2, b2)


def _layer(x2, B, S, w_in, layer, gm_ln_g, gm_ln_b, gm_w_s, gm_b_s, cmp_pe_k, cmp_w1_k, cmp_w2_k,
           cmp_pe_v, cmp_w1_v, cmp_w2_v, w_proj_gm, w_proj_nsa, w_out,
           ln1_g, ln1_b, w_ff1, w_ff2, ln2_g, ln2_b):
    o_q = 2 * D_MODEL
    o_m = o_q + (Q_PER_KV + 6) * N_KV_HEADS * HEAD_DIM + 3 * Q_PER_KV * N_KV_HEADS
    row = lambda v: v.reshape(1, -1)
    merge_cols = lambda j: w_in[layer * D_MODEL:(layer + 1) * D_MODEL, o_m + j * D_MODEL:o_m + (j + 1) * D_MODEL]

    gm = _gm_mixer(x2, w_in, layer, merge_cols(0), row(gm_ln_g), row(gm_ln_b), gm_w_s,
                   jnp.broadcast_to(gm_b_s[:, :, None], (GM_GROUPS, GM_CHUNK, LANES)), w_proj_gm)

    qt, kc, vc, ksa, vst, kwa, vwt, gt = _qkv_proj(x2, w_in, layer, o_q, B, S)

    heads = lambda a: a.reshape(B * N_KV_HEADS, S, HEAD_DIM)
    pairs = lambda w: w.astype(_BF).reshape(CMP_BLOCK // 2, 2 * HEAD_DIM, HEAD_DIM)
    kca, vct = _nsa_compress(heads(kc), heads(vc), cmp_pe_k, cmp_pe_v, pairs(cmp_w1_k), pairs(cmp_w1_v),
                             cmp_w2_k.astype(_BF), cmp_w2_v.astype(_BF))

    yn = _nsa_attention(qt, gt, kca, vct, ksa, vst, kwa, vwt, B, S)

    return _merge_ffn(x2, gm, yn, merge_cols(1).astype(_BF), w_proj_nsa.astype(_BF),
                      w_out.astype(_BF), row(ln1_g), row(ln1_b), w_ff1.astype(_BF), w_ff2.astype(_BF),
                      row(ln2_g), row(ln2_b))


def kernel(x, w_in, gm_ln_g, gm_ln_b, gm_w_s, gm_b_s, cmp_pe_k, cmp_w1_k, cmp_w2_k, cmp_pe_v, cmp_w1_v, cmp_w2_v, w_proj_gm, w_proj_nsa, w_out, ln1_g, ln1_b, w_ff1, w_ff2, ln2_g, ln2_b):
    B, S, D = x.shape
    assert D == D_MODEL and S % ROW_TILE == 0 and WINDOW + Q_BLOCK <= S <= SEL_BLOCK * LANES
    assert S // POS_RADIX <= POS_RADIX
    h = x.reshape(B * S, D)
    depth, _, d_in = w_in.shape
    w_rows = w_in.reshape(depth * D, d_in)
    for l in range(depth):
        h = _layer(h, B, S, w_rows, l, gm_ln_g[l], gm_ln_b[l], gm_w_s[l], gm_b_s[l],
                   cmp_pe_k[l], cmp_w1_k[l], cmp_w2_k[l], cmp_pe_v[l], cmp_w1_v[l], cmp_w2_v[l],
                   w_proj_gm[l], w_proj_nsa[l], w_out[l], ln1_g[l], ln1_b[l],
                   w_ff1[l], w_ff2[l], ln2_g[l], ln2_b[l])
    return h.reshape(B, S, D)
```

```python
import functools
import math

import jax
import jax.numpy as jnp
from jax import lax
from jax.experimental import pallas as pl
from jax.experimental.pallas import tpu as pltpu

D_MODEL = 1024
GM_GROUPS = 8
GM_CHUNK = 128
N_KV_HEADS = 2
Q_PER_KV = 4
HEAD_DIM = 128
CMP_BLOCK = 32
CMP_STRIDE = 16
SEL_BLOCK = 64
N_SELECT = 16
N_LOCAL_FORCED = 2
WINDOW = 512
Q_BLOCK = 128
D_FF = 4 * D_MODEL
DEEPNORM_ALPHA = 2.0 ** 0.25
LN_EPS = 1e-5
NEG_INF = -1e30
LOG2_E = math.log2(math.e)

LANES = 128
SEL_TILE = 128
LOCAL_TILES = 11
PAIR = 2
BITS_PER_WORD = 16
POS_RADIX = 256
ROW_TILE = 512
VT_ROWS = HEAD_DIM + 16
GATE_ROWS = 16
VMEM_LIMIT = 56 * 1024 * 1024

_BF = jnp.bfloat16
_F32 = jnp.float32


def _dot(a, b):
    return jnp.dot(a, b, preferred_element_type=_F32)


def _gelu(x):
    c = math.sqrt(2.0 / math.pi)
    return 0.5 * x * (1.0 + jnp.tanh(c * (x + 0.044715 * (x * x * x))))


def _sigmoid(x):
    return 1.0 / (1.0 + jnp.exp(-x))


def _layer_norm(x, g, b):
    mu = jnp.mean(x, axis=-1, keepdims=True)
    xc = x - mu
    var = jnp.mean(xc * xc, axis=-1, keepdims=True)
    return xc * lax.rsqrt(var + LN_EPS) * g + b


def _position_lanes(pos, shape):
    lane = lax.broadcasted_iota(jnp.int32, shape, 1)
    return jnp.where(lane == 0, (pos % POS_RADIX).astype(_F32),
                     jnp.where(lane == 1, (pos // POS_RADIX).astype(_F32), 0.0)).astype(_BF)


def _stage_bf16(first_step, pairs):
    @pl.when(first_step)
    def _():
        for src, dst in pairs:
            dst[...] = src[...].astype(_BF)


def _resident(shape, index_map):
    return pl.BlockSpec(shape, index_map, pipeline_mode=pl.Buffered(1))


def _gm_kernel(x_ref, wgm32_ref, wm032_ref, lng_ref, lnb_ref, ws_ref, bs_ref, wpg32_ref, o_ref,
               vg_ref, wgm_ref, wm0_ref, wpg_ref):
    _stage_bf16(pl.program_id(0) == 0, [(wgm32_ref, wgm_ref), (wm032_ref, wm0_ref), (wpg32_ref, wpg_ref)])
    tm = x_ref.shape[0]
    xb = x_ref[...].astype(_BF)
    z = _gelu(_dot(xb, wgm_ref[...]))
    gate_logits = _dot(xb, wm0_ref[...])
    u = z[:, :D_MODEL]
    v = _layer_norm(z[:, D_MODEL:], lng_ref[...], lnb_ref[...]).astype(_BF)
    row = lax.broadcasted_iota(jnp.int32, (GM_CHUNK, GM_CHUNK), 0)
    col = lax.broadcasted_iota(jnp.int32, (GM_CHUNK, GM_CHUNK), 1)
    for gi in range(GM_GROUPS):
        w = jnp.where(row >= col, ws_ref[gi], 0.0).astype(_BF)
        for c in range(tm // GM_CHUNK):
            blk = v[c * GM_CHUNK:(c + 1) * GM_CHUNK, gi * LANES:(gi + 1) * LANES]
            vg_ref[c * GM_CHUNK:(c + 1) * GM_CHUNK, gi * LANES:(gi + 1) * LANES] = _dot(w, blk) + bs_ref[gi]
    y = (u * vg_ref[...]).astype(_BF)
    o_ref[...] = _sigmoid(gate_logits) * _dot(y, wpg_ref[...])


def _gm_mixer(x2, w_in, layer, wm0, lng, lnb, ws, bs, wpg):
    T = x2.shape[0]
    tm = ROW_TILE
    const2 = lambda i: (0, 0)
    const3 = lambda i: (0, 0, 0)
    return pl.pallas_call(
        _gm_kernel,
        out_shape=jax.ShapeDtypeStruct((T, D_MODEL), _F32),
        grid=(T // tm,),
        in_specs=[
            pl.BlockSpec((tm, D_MODEL), lambda i: (i, 0)),
            _resident((D_MODEL, 2 * D_MODEL), lambda i: (layer, 0)),
            _resident((D_MODEL, D_MODEL), const2),
            pl.BlockSpec((1, D_MODEL), const2),
            pl.BlockSpec((1, D_MODEL), const2),
            pl.BlockSpec((GM_GROUPS, GM_CHUNK, GM_CHUNK), const3),
            pl.BlockSpec((GM_GROUPS, GM_CHUNK, LANES), const3),
            _resident((D_MODEL, D_MODEL), const2),
        ],
        out_specs=pl.BlockSpec((tm, D_MODEL), lambda i: (i, 0)),
        scratch_shapes=[pltpu.VMEM((tm, D_MODEL), _F32), pltpu.VMEM((D_MODEL, 2 * D_MODEL), _BF),
                        pltpu.VMEM((D_MODEL, D_MODEL), _BF), pltpu.VMEM((D_MODEL, D_MODEL), _BF)],
        compiler_params=pltpu.CompilerParams(
            dimension_semantics=("arbitrary",), vmem_limit_bytes=VMEM_LIMIT),
        name="gm_mixer",
    )(x2, w_in, wm0, lng, lnb, ws, bs, wpg)


def _qkv_kernel(x_ref, wq32_ref, wkv32_ref, wg32_ref, qt_ref, kc_ref, vc_ref, ksa_ref, vst_ref, kwa_ref, vwt_ref,
                gt_ref, wq_ref, wkv_ref, wg_ref):
    _stage_bf16((pl.program_id(0) == 0) & (pl.program_id(1) == 0),
                [(wq32_ref, wq_ref), (wkv32_ref, wkv_ref), (wg32_ref, wg_ref)])
    tm = x_ref.shape[0]
    xb = x_ref[...].astype(_BF)
    zq = _dot(xb, wq_ref[...]) * (HEAD_DIM ** -0.5 * LOG2_E)
    for tb in range(tm // Q_BLOCK):
        for h in range(N_KV_HEADS):
            for g in range(Q_PER_KV):
                c0 = (h * Q_PER_KV + g) * HEAD_DIM
                blk = zq[tb * Q_BLOCK:(tb + 1) * Q_BLOCK, c0:c0 + HEAD_DIM]
                qt_ref[tb, h, :, g * Q_BLOCK:(g + 1) * Q_BLOCK] = blk.T.astype(_BF)
    z = _dot(xb, wkv_ref[...])
    kpos = pl.program_id(1) * tm + lax.broadcasted_iota(jnp.int32, (tm, LANES), 0)
    blk_lane = lax.broadcasted_iota(jnp.int32, (tm, LANES), 1)
    sel_lanes = jnp.where(blk_lane == 0, (kpos % SEL_TILE).astype(_F32),
                          jnp.where(kpos // SEL_BLOCK == blk_lane, NEG_INF, 0.0)).astype(_BF)
    win_lanes = _position_lanes(kpos, (tm, LANES))
    ones = jnp.ones((VT_ROWS - HEAD_DIM, tm), _BF)
    for h in range(N_KV_HEADS):
        def col(j):
            return z[:, j * 2 * HEAD_DIM + h * HEAD_DIM: j * 2 * HEAD_DIM + (h + 1) * HEAD_DIM]
        kc_ref[0, h] = col(0)
        vc_ref[0, h] = col(1)
        ksa_ref[0, h, :, :HEAD_DIM] = col(2).astype(_BF)
        ksa_ref[0, h, :, HEAD_DIM:] = sel_lanes
        vst_ref[0, h, :HEAD_DIM] = col(3).T.astype(_BF)
        vst_ref[0, h, HEAD_DIM:] = ones
        kwa_ref[0, h, :, :HEAD_DIM] = col(4).astype(_BF)
        kwa_ref[0, h, :, HEAD_DIM:] = win_lanes
        vwt_ref[0, h, :HEAD_DIM] = col(5).T.astype(_BF)
        vwt_ref[0, h, HEAD_DIM:] = ones
    zg_t = _sigmoid(_dot(xb, wg_ref[...])).T
    per_head = 3 * Q_PER_KV
    for h in range(N_KV_HEADS):
        gt_ref[h] = zg_t[h * per_head:h * per_head + GATE_ROWS]


def _qkv_proj(x2, w_in, layer, q_col, B, S):
    T = x2.shape[0]
    tm = ROW_TILE
    nsb = S // tm
    gd = Q_PER_KV * Q_BLOCK
    aw = N_KV_HEADS * Q_PER_KV * HEAD_DIM
    kvw = 6 * N_KV_HEADS * HEAD_DIM
    kv_col = q_col + aw
    gate_col = kv_col + kvw
    assert q_col % aw == 0 and kv_col % kvw == 0 and gate_col % LANES == 0
    rows_spec = lambda w: pl.BlockSpec((1, N_KV_HEADS, tm, w), lambda b, s: (b, 0, s, 0))
    rows_shape = lambda w, dt: jax.ShapeDtypeStruct((B, N_KV_HEADS, S, w), dt)
    cols_spec = pl.BlockSpec((1, N_KV_HEADS, VT_ROWS, tm), lambda b, s: (b, 0, 0, s))
    cols_shape = jax.ShapeDtypeStruct((B, N_KV_HEADS, VT_ROWS, S), _BF)
    return pl.pallas_call(
        _qkv_kernel,
        out_shape=(
            jax.ShapeDtypeStruct((T // Q_BLOCK, N_KV_HEADS, HEAD_DIM, gd), _BF),
            rows_shape(HEAD_DIM, _F32), rows_shape(HEAD_DIM, _F32),
            rows_shape(2 * HEAD_DIM, _BF), cols_shape,
            rows_shape(2 * HEAD_DIM, _BF), cols_shape,
            jax.ShapeDtypeStruct((N_KV_HEADS, GATE_ROWS, T), _F32),
        ),
        grid=(B, nsb),
        in_specs=[
            pl.BlockSpec((tm, D_MODEL), lambda b, s: (b * nsb + s, 0)),
            _resident((D_MODEL, aw), lambda b, s: (layer, q_col // aw)),
            _resident((D_MODEL, kvw), lambda b, s: (layer, kv_col // kvw)),
            _resident((D_MODEL, LANES), lambda b, s: (layer, gate_col // LANES)),
        ],
        out_specs=(
            pl.BlockSpec((tm // Q_BLOCK, N_KV_HEADS, HEAD_DIM, gd), lambda b, s: (b * nsb + s, 0, 0, 0)),
            rows_spec(HEAD_DIM), rows_spec(HEAD_DIM), rows_spec(2 * HEAD_DIM), cols_spec,
            rows_spec(2 * HEAD_DIM), cols_spec,
            pl.BlockSpec((N_KV_HEADS, GATE_ROWS, tm), lambda b, s: (0, 0, b * nsb + s)),
        ),
        scratch_shapes=[pltpu.VMEM((D_MODEL, aw), _BF), pltpu.VMEM((D_MODEL, kvw), _BF),
                        pltpu.VMEM((D_MODEL, LANES), _BF)],
        compiler_params=pltpu.CompilerParams(
            dimension_semantics=("arbitrary", "arbitrary"), vmem_limit_bytes=VMEM_LIMIT),
        name="qkv_proj",
    )(x2, w_in, w_in, w_in)


def _compress_kernel(kc_ref, vc_ref, pek_ref, pev_ref, w1k_ref, w1v_ref, w2k_ref, w2v_ref, ko_ref, vo_ref):
    nc = ko_ref.shape[1]

    def tokens(src, pe, w1, w2):
        first, second = None, None
        for p in range(0, CMP_STRIDE, 2):
            rows = [src[0, pl.ds(p + d, nc, stride=CMP_STRIDE), :] for d in range(2)]
            lo = jnp.concatenate([rows[d] + pe[p + d:p + d + 1, :] for d in range(2)], axis=1)
            hi = jnp.concatenate([rows[d] + pe[CMP_STRIDE + p + d:CMP_STRIDE + p + d + 1, :] for d in range(2)], axis=1)
            a = _dot(lo.astype(_BF), w1[p // 2])
            b = _dot(hi.astype(_BF), w1[(CMP_STRIDE + p) // 2])
            first = a if first is None else first + a
            second = b if second is None else second + b
        pre = first + pltpu.roll(second, nc - 1, 0)
        return _dot(_gelu(pre).astype(_BF), w2[...])

    ko_ref[0, :, :HEAD_DIM] = tokens(kc_ref, pek_ref, w1k_ref, w2k_ref).astype(_BF)
    start = lax.broadcasted_iota(jnp.int32, (nc, LANES), 0) * CMP_STRIDE
    ko_ref[0, :, HEAD_DIM:] = _position_lanes(start, (nc, LANES))
    vo_ref[0, :HEAD_DIM] = tokens(vc_ref, pev_ref, w1v_ref, w2v_ref).T.astype(_BF)
    mi = lax.broadcasted_iota(jnp.int32, (LANES, nc), 0)
    ni = lax.broadcasted_iota(jnp.int32, (LANES, nc), 1)
    vo_ref[0, HEAD_DIM:] = jnp.where((ni * CMP_STRIDE + (CMP_BLOCK - 1) >= mi * SEL_BLOCK)
                                     & (ni * CMP_STRIDE <= mi * SEL_BLOCK + (SEL_BLOCK - 1)), 1.0, 0.0).astype(_BF)


def _nsa_compress(kc, vc, pek, pev, w1k, w1v, w2k, w2v):
    BH, S, _ = kc.shape
    nc = S // CMP_STRIDE
    pair = 2 * HEAD_DIM
    const2 = lambda i: (0, 0)
    const3 = lambda i: (0, 0, 0)
    seq_rows = pl.BlockSpec((1, S, HEAD_DIM), lambda i: (i, 0, 0))
    pe_spec = pl.BlockSpec((CMP_BLOCK, HEAD_DIM), const2)
    w1_spec = pl.BlockSpec((CMP_BLOCK // 2, pair, HEAD_DIM), const3)
    w2_spec = pl.BlockSpec((HEAD_DIM, HEAD_DIM), const2)
    return pl.pallas_call(
        _compress_kernel,
        out_shape=(jax.ShapeDtypeStruct((BH, nc, 2 * HEAD_DIM), _BF),
                   jax.ShapeDtypeStruct((BH, HEAD_DIM + LANES, nc), _BF)),
        grid=(BH,),
        in_specs=[seq_rows, seq_rows, pe_spec, pe_spec, w1_spec, w1_spec, w2_spec, w2_spec],
        out_specs=(pl.BlockSpec((1, nc, 2 * HEAD_DIM), lambda i: (i, 0, 0)),
                   pl.BlockSpec((1, HEAD_DIM + LANES, nc), lambda i: (i, 0, 0))),
        compiler_params=pltpu.CompilerParams(
            dimension_semantics=("arbitrary",), vmem_limit_bytes=VMEM_LIMIT),
        name="nsa_compress",
    )(kc, vc, pek, pev, w1k, w1v, w2k, w2v)


def _mark_top_blocks(score, notsel, rounds):
    rows = lax.broadcasted_iota(jnp.int32, score.shape, 0).astype(_F32)
    for _ in range(rounds):
        mx = jnp.max(score, axis=0, keepdims=True)
        idx = jnp.min(jnp.where(score == mx, rows, float(LANES)), axis=0, keepdims=True)
        hit = rows == idx
        notsel = jnp.where(hit, 0.0, notsel)
        score = jnp.where(hit, -jnp.inf, score)
    return notsel


def _attn_kernel(qt_ref, gt_ref, kca_ref, vct_ref, ksa_ref, vst_ref, kwa_ref, vwt_ref, o_ref,
                 m_ref, acc_ref, lhs_ref, part_ref, gate_ref, bits_ref, *, seq):
    h = pl.program_id(1)
    step = pl.program_id(2)
    n_pair = seq // (Q_BLOCK * PAIR)
    nc = kca_ref.shape[1]
    n_sel = seq // SEL_BLOCK
    n_forced = 1 + N_LOCAL_FORCED
    k_top = min(N_SELECT, n_sel)
    G = Q_PER_KV
    W = G * Q_BLOCK
    words = LANES // BITS_PER_WORD
    blocks_per_tile = SEL_TILE // SEL_BLOCK
    tiles_per_word = BITS_PER_WORD // blocks_per_tile
    put = step % 2
    get = 1 - put
    put3 = step % 3
    get3 = (step + 1) % 3
    units = range(PAIR)

    def flag_base(slot, u):
        return (slot * PAIR + u) * (words + 1)

    @pl.when(step == 0)
    def _():
        lhs_ref[1] = jnp.zeros(lhs_ref.shape[1:], _BF)
        part_ref[...] = jnp.zeros(part_ref.shape, _F32)
        gate_ref[...] = jnp.zeros(gate_ref.shape, _F32)
        acc_ref[...] = jnp.ones(acc_ref.shape, _F32)
        for k in range(PAIR * (words + 1)):
            bits_ref[flag_base(1, 0) + k] = 0

    lane_w = lax.broadcasted_iota(jnp.int32, (1, W), 1)
    slope = jnp.zeros((1, W), _F32)
    for g in range(G):
        sg = jnp.where(h == 0, _F32(2.0 ** -(g + 1)), _F32(2.0 ** -(G + g + 1)))
        slope = jnp.where(lane_w // Q_BLOCK == g, sg, slope)
    slope = (slope * LOG2_E).astype(_BF).astype(_F32)
    strips = [slice(g * Q_BLOCK, (g + 1) * Q_BLOCK) for g in range(G)]
    row_aug = lax.broadcasted_iota(jnp.int32, (LANES, W), 0)
    key_row = lax.broadcasted_iota(jnp.int32, (SEL_TILE, Q_BLOCK), 0)
    pos_rows = jnp.where(row_aug == 0, slope, jnp.where(row_aug == 1, slope * POS_RADIX, 0.0)).astype(_BF)

    for u in units:
        out_t = (part_ref[get3, u] + acc_ref[u, 0:HEAD_DIM]
                 * (gate_ref[get3, u][0:1] * (1.0 / acc_ref[u, HEAD_DIM:HEAD_DIM + 1])))
        for g, cs in enumerate(strips):
            o_ref[u * Q_BLOCK:(u + 1) * Q_BLOCK, g * HEAD_DIM:(g + 1) * HEAD_DIM] = out_t[:, cs].T.astype(_BF)

    front = [{} for _ in units]
    for u, f in zip(units, front):
        f["q0"] = (jnp.minimum(step, n_pair - 1) * PAIR + u) * Q_BLOCK
        f["tok"] = f["q0"] + lane_w % Q_BLOCK
        f["qt"] = qt_ref[u, 0]
        f["s"] = _dot(kca_ref[0], jnp.concatenate([f["qt"], pos_rows], axis=0))

    def tile_scores(bk, j, causal, live=None):
        k0 = j * SEL_TILE if isinstance(j, int) else pl.multiple_of(j * SEL_TILE, SEL_TILE)
        st = _dot(ksa_ref[0, 0, pl.ds(k0, SEL_TILE), :], bk["lhs"])
        if causal:
            ahead = jnp.where(k0 + key_row > bk["tok"][:, :Q_BLOCK], NEG_INF, 0.0)
            st = jnp.concatenate([st[:, cs] + ahead for cs in strips], axis=1)
        shift = slope * (bk["p0"] - k0).astype(_F32)
        if live is not None:
            shift = jnp.where(live, shift, -NEG_INF)
        return st, shift, k0

    def tile_max(tiles):
        m = None
        for st, shift, _ in tiles:
            cm = jnp.max(st, axis=0, keepdims=True) - shift
            m = cm if m is None else jnp.maximum(m, cm)
        return m

    def tile_sums(tiles, m):
        acc = None
        for st, shift, k0 in tiles:
            sub = m + shift
            pt = jnp.concatenate([jnp.exp2(st[:, cs] - sub[:, cs]) for cs in strips], axis=1)
            ai = _dot(vst_ref[0, 0, :, pl.ds(k0, SEL_TILE)], pt.astype(_BF))
            acc = ai if acc is None else acc + ai
        return acc

    back = [{} for _ in units]
    for u, bk in zip(units, back):
        bk["p0"] = (jnp.clip(step - 1, 0, n_pair - 1) * PAIR + u) * Q_BLOCK
        bk["tok"] = bk["p0"] + lane_w % Q_BLOCK
        bk["lhs"] = lhs_ref[get, u]
        bk["first_local"] = bk["p0"] // SEL_TILE - (LOCAL_TILES - 1)
        bk["tiles"] = [tile_scores(bk, 0, False, live=bk["first_local"] > 0)]
        for i in range(LOCAL_TILES):
            j = bk["first_local"] + i
            last = i == LOCAL_TILES - 1
            bk["tiles"].append(tile_scores(bk, jnp.maximum(j, 0), last, live=None if last else j >= 0))

    n_row = lax.broadcasted_iota(jnp.int32, (nc, Q_BLOCK), 0)
    for f in front:
        n_last = (f["tok"] - (CMP_BLOCK - 1)) // CMP_STRIDE
        mask_c = jnp.where(n_row <= n_last[:, :Q_BLOCK], 0.0, NEG_INF)
        e_parts, inv_parts = [], []
        for cs in strips:
            sg = f["s"][:, cs] + mask_c
            eg = jnp.exp2(sg - jnp.max(sg, axis=0, keepdims=True))
            e_parts.append(eg.astype(_BF))
            inv_parts.append(jnp.where(n_last[:, cs] >= 0, 1.0 / jnp.sum(eg, axis=0, keepdims=True), 0.0))
        both = _dot(vct_ref[0], jnp.concatenate(e_parts, axis=1)) * jnp.concatenate(inv_parts, axis=1)
        f["o_cmp"] = both[:HEAD_DIM]
        imp_t = both[HEAD_DIM:, strips[0]]
        for cs in strips[1:]:
            imp_t = imp_t + both[HEAD_DIM:, cs]
        f["imp_t"] = imp_t

    wlen = WINDOW + Q_BLOCK
    for f in front:
        anchor = jnp.concatenate([jnp.minimum(f["imp_t"][0:1], 0.0)] * G, axis=1)
        win_rows = jnp.where(row_aug == 0, slope + anchor,
                             jnp.where(row_aug == 1, slope * POS_RADIX, 0.0)).astype(_BF)
        f["w0"] = pl.multiple_of(jnp.maximum(f["q0"] - WINDOW, 0), Q_BLOCK)
        f["sw"] = _dot(kwa_ref[0, 0, pl.ds(f["w0"], wlen), :], jnp.concatenate([f["qt"], win_rows], axis=0))

    for u, bk in zip(units, back):
        m = tile_max(bk["tiles"])
        m_ref[u, 0:1] = m
        acc_ref[u] = tile_sums(bk["tiles"], m)

    for f in front:
        w_row = f["w0"] + lax.broadcasted_iota(jnp.int32, (wlen, Q_BLOCK), 0)
        dist = f["tok"][:, :Q_BLOCK] - w_row
        mask_w = jnp.where(dist.astype(jnp.uint32) < WINDOW, 0.0, NEG_INF)
        e_parts = []
        for cs in strips:
            sg = f["sw"][:, cs] + mask_w
            e_parts.append(jnp.exp2(sg - jnp.max(sg, axis=0, keepdims=True)).astype(_BF))
        win = _dot(vwt_ref[0, 0, :, pl.ds(f["w0"], wlen)], jnp.concatenate(e_parts, axis=1))
        f["o_win"] = win[:HEAD_DIM] * (1.0 / win[HEAD_DIM:HEAD_DIM + 1])

    mrow = lax.broadcasted_iota(jnp.int32, (LANES, Q_BLOCK), 0)
    blk_row = lax.broadcasted_iota(jnp.int32, (LANES, 1), 0)
    blk_bit = jnp.left_shift(1, blk_row % BITS_PER_WORD).astype(_F32)
    for u, f in zip(units, front):
        gates = gt_ref[0][:, u * Q_BLOCK:(u + 1) * Q_BLOCK]

        def gate_row(branch):
            return jnp.concatenate([gates[3 * g + branch:3 * g + branch + 1] for g in range(G)], axis=1)

        tcol = lax.broadcasted_iota(jnp.int32, (LANES, Q_BLOCK), 1) + f["q0"]
        lag = tcol // SEL_BLOCK - mrow
        forced = (mrow == 0) | ((lag >= 0) & (lag < N_LOCAL_FORCED))
        score = jnp.where(forced | (lag < 0) | (mrow >= n_sel), -jnp.inf, f["imp_t"])
        notsel_t = _mark_top_blocks(score, jnp.where(forced, 0.0, 1.0), k_top - n_forced)

        sel_rows = jnp.where(row_aug == 0, slope, jnp.concatenate([notsel_t] * G, axis=1)).astype(_BF)
        lhs_ref[put, u] = jnp.concatenate([f["qt"], sel_rows], axis=0)
        part_ref[put3, u] = gate_row(0) * f["o_cmp"] + gate_row(2) * f["o_win"]
        gate_ref[put3, u] = jnp.broadcast_to(gate_row(1), gate_ref.shape[2:])

        far_end = (f["q0"] // SEL_TILE - (LOCAL_TILES - 1)) * blocks_per_tile
        blk_on = jnp.where((jnp.min(notsel_t, axis=1, keepdims=True) < 0.5)
                           & (blk_row >= blocks_per_tile) & (blk_row < far_end), 1.0, 0.0)
        for k in range(words):
            word = jnp.sum((blk_on * blk_bit)[k * BITS_PER_WORD:(k + 1) * BITS_PER_WORD])
            bits_ref[flag_base(put, u) + k] = word.astype(jnp.int32)
        bits_ref[flag_base(put, u) + words] = jnp.sum(blk_on).astype(jnp.int32)

    for u, bk in zip(units, back):
        def far_tile(j, carry, u=u, bk=bk):
            word = bits_ref[flag_base(get, u) + j // tiles_per_word]
            tile_bits = (word >> ((j % tiles_per_word) * blocks_per_tile)) & ((1 << blocks_per_tile) - 1)

            @pl.when(tile_bits != 0)
            def _():
                tile = [tile_scores(bk, j, False)]
                m_old = m_ref[u, 0:1]
                m_new = jnp.maximum(m_old, tile_max(tile))
                m_ref[u, 0:1] = m_new
                acc_ref[u] = jnp.exp2(m_old - m_new) * acc_ref[u] + tile_sums(tile, m_new)
            return carry

        @pl.when(bits_ref[flag_base(get, u) + words] != 0)
        def _(far_tile=far_tile, bk=bk):
            lax.fori_loop(1, bk["first_local"], far_tile, 0)


def _nsa_attention(qt, gt, kca, vct, ksa, vst, kwa, vwt, B, S):
    n_pair = S // (Q_BLOCK * PAIR)
    T = B * S
    nc = kca.shape[1]
    gd = Q_PER_KV * HEAD_DIM
    W = Q_PER_KV * Q_BLOCK
    rows = pl.BlockSpec((1, 1, S, 2 * HEAD_DIM), lambda b, h, i: (b, h, 0, 0))
    cols = pl.BlockSpec((1, 1, VT_ROWS, S), lambda b, h, i: (b, h, 0, 0))
    front = lambda i: jnp.minimum(i, n_pair - 1)
    done = lambda i: jnp.maximum(i - 2, 0)
    return pl.pallas_call(
        functools.partial(_attn_kernel, seq=S),
        out_shape=jax.ShapeDtypeStruct((T, N_KV_HEADS * gd), _BF),
        grid=(B, N_KV_HEADS, n_pair + 2),
        in_specs=[
            pl.BlockSpec((PAIR, 1, HEAD_DIM, W), lambda b, h, i: (b * n_pair + front(i), h, 0, 0)),
            pl.BlockSpec((1, GATE_ROWS, PAIR * Q_BLOCK), lambda b, h, i: (h, 0, b * n_pair + front(i))),
            pl.BlockSpec((1, nc, 2 * HEAD_DIM), lambda b, h, i: (b * N_KV_HEADS + h, 0, 0)),
            pl.BlockSpec((1, HEAD_DIM + LANES, nc), lambda b, h, i: (b * N_KV_HEADS + h, 0, 0)),
            rows, cols, rows, cols,
        ],
        out_specs=pl.BlockSpec((PAIR * Q_BLOCK, gd), lambda b, h, i: (b * n_pair + done(i), h)),
        scratch_shapes=[pltpu.VMEM((PAIR, 8, W), _F32), pltpu.VMEM((PAIR, VT_ROWS, W), _F32),
                        pltpu.VMEM((2, PAIR, 2 * HEAD_DIM, W), _BF), pltpu.VMEM((3, PAIR, HEAD_DIM, W), _F32),
                        pltpu.VMEM((3, PAIR, 8, W), _F32),
                        pltpu.SMEM((2 * PAIR * (LANES // BITS_PER_WORD + 1),), jnp.int32)],
        compiler_params=pltpu.CompilerParams(
            dimension_semantics=("arbitrary", "arbitrary", "arbitrary"), vmem_limit_bytes=VMEM_LIMIT),
        name="nsa_attention",
    )(qt, gt, kca, vct, ksa, vst, kwa, vwt)


def _merge_ffn_kernel(x_ref, gm_ref, yn_ref, wm1_ref, wpn_ref, wo_ref, g1_ref, b1_ref,
                      w1_ref, w2_ref, g2_ref, b2_ref, o_ref):
    x = x_ref[...]
    gate = _sigmoid(_dot(x.astype(_BF), wm1_ref[...]))
    merged = gm_ref[...] + gate * _dot(yn_ref[...], wpn_ref[...])
    mix = _dot(merged.astype(_BF), wo_ref[...])
    hid = _layer_norm(DEEPNORM_ALPHA * x + mix, g1_ref[...], b1_ref[...])
    hb = hid.astype(_BF)
    f = jnp.zeros(hid.shape, _F32)
    for c in range(D_FF // D_MODEL):
        a = jnp.maximum(_dot(hb, w1_ref[:, c * D_MODEL:(c + 1) * D_MODEL]), 0.0)
        f = f + _dot((a * a).astype(_BF), w2_ref[c * D_MODEL:(c + 1) * D_MODEL, :])
    o_ref[...] = _layer_norm(DEEPNORM_ALPHA * hid + f, g2_ref[...], b2_ref[...])


def _merge_ffn(x2, gm, yn, wm1, wpn, wo, g1, b1, w1, w2, g2, b2):
    T = x2.shape[0]
    tm = ROW_TILE
    rows = lambda w: pl.BlockSpec((tm, w), lambda i: (i, 0))
    const = lambda r, c: _resident((r, c), lambda i: (0, 0))
    return pl.pallas_call(
        _merge_ffn_kernel,
        out_shape=jax.ShapeDtypeStruct((T, D_MODEL), _F32),
        grid=(T // tm,),
        in_specs=[rows(D_MODEL), rows(D_MODEL), rows(D_MODEL),
                  const(D_MODEL, D_MODEL), const(D_MODEL, D_MODEL), const(D_MODEL, D_MODEL),
                  const(1, D_MODEL), const(1, D_MODEL),
                  const(D_MODEL, D_FF), const(D_FF, D_MODEL),
                  const(1, D_MODEL), const(1, D_MODEL)],
        out_specs=rows(D_MODEL),
        compiler_params=pltpu.CompilerParams(
            dimension_semantics=("arbitrary",), vmem_limit_bytes=VMEM_LIMIT),
        name="merge_ffn",
    )(x2, gm, yn, wm1, wpn, wo, g1, b1, w1, w2, g2, b2)


def _layer(x2, B, S, w_in, layer, gm_ln_g, gm_ln_b, gm_w_s, gm_b_s, cmp_pe_k, cmp_w1_k, cmp_w2_k,
           cmp_pe_v, cmp_w1_v, cmp_w2_v, w_proj_gm, w_proj_nsa, w_out,
           ln1_g, ln1_b, w_ff1, w_ff2, ln2_g, ln2_b):
    o_q = 2 * D_MODEL
    o_m = o_q + (Q_PER_KV + 6) * N_KV_HEADS * HEAD_DIM + 3 * Q_PER_KV * N_KV_HEADS
    row = lambda v: v.reshape(1, -1)
    merge_cols = lambda j: w_in[layer * D_MODEL:(layer + 1) * D_MODEL, o_m + j * D_MODEL:o_m + (j + 1) * D_MODEL]

    gm = _gm_mixer(x2, w_in, layer, merge_cols(0), row(gm_ln_g), row(gm_ln_b), gm_w_s,
                   jnp.broadcast_to(gm_b_s[:, :, None], (GM_GROUPS, GM_CHUNK, LANES)), w_proj_gm)

    qt, kc, vc, ksa, vst, kwa, vwt, gt = _qkv_proj(x2, w_in, layer, o_q, B, S)

    heads = lambda a: a.reshape(B * N_KV_HEADS, S, HEAD_DIM)
    pairs = lambda w: w.astype(_BF).reshape(CMP_BLOCK // 2, 2 * HEAD_DIM, HEAD_DIM)
    kca, vct = _nsa_compress(heads(kc), heads(vc), cmp_pe_k, cmp_pe_v, pairs(cmp_w1_k), pairs(cmp_w1_v),
                             cmp_w2_k.astype(_BF), cmp_w2_v.astype(_BF))

    yn = _nsa_attention(qt, gt, kca, vct, ksa, vst, kwa, vwt, B, S)

    return _merge_ffn(x2, gm, yn, merge_cols(1).astype(_BF), w_proj_nsa.astype(_BF),
                      w_out.astype(_BF), row(ln1_g), row(ln1_b), w_ff1.astype(_BF), w_ff2.astype(_BF),
                      row(ln2_g), row(ln2_b))


def kernel(x, w_in, gm_ln_g, gm_ln_b, gm_w_s, gm_b_s, cmp_pe_k, cmp_w1_k, cmp_w2_k, cmp_pe_v, cmp_w1_v, cmp_w2_v, w_proj_gm, w_proj_nsa, w_out, ln1_g, ln1_b, w_ff1, w_ff2, ln2_g, ln2_b):
    B, S, D = x.shape
    assert D == D_MODEL and S % ROW_TILE == 0 and WINDOW + Q_BLOCK <= S <= SEL_BLOCK * LANES
    assert S // POS_RADIX <= POS_RADIX and S % (PAIR * Q_BLOCK) == 0
    h = x.reshape(B * S, D)
    depth, _, d_in = w_in.shape
    w_rows = w_in.reshape(depth * D, d_in)
    for l in range(depth):
        h = _layer(h, B, S, w_rows, l, gm_ln_g[l], gm_ln_b[l], gm_w_s[l], gm_b_s[l],
                   cmp_pe_k[l], cmp_w1_k[l], cmp_w2_k[l], cmp_pe_v[l], cmp_w1_v[l], cmp_w2_v[l],
                   w_proj_gm[l], w_proj_nsa[l], w_out[l], ln1_g[l], ln1_b[l],
                   w_ff1[l], w_ff2[l], ln2_g[l], ln2_b[l])
    return h.reshape(B, S, D)
```

```python
import functools
import math

import jax
import jax.numpy as jnp
from jax import lax
from jax.experimental import pallas as pl
from jax.experimental.pallas import tpu as pltpu

D_MODEL = 1024
GM_GROUPS = 8
GM_CHUNK = 128
N_KV_HEADS = 2
Q_PER_KV = 4
HEAD_DIM = 128
CMP_BLOCK = 32
CMP_STRIDE = 16
SEL_BLOCK = 64
N_SELECT = 16
N_LOCAL_FORCED = 2
WINDOW = 512
Q_BLOCK = 128
D_FF = 4 * D_MODEL
DEEPNORM_ALPHA = 2.0 ** 0.25
LN_EPS = 1e-5
NEG_INF = -1e30
LOG2_E = math.log2(math.e)

LANES = 128
SEL_TILE = 128
LOCAL_TILES = 11
PAIR = 2
BITS_PER_WORD = 16
POS_RADIX = 256
ROW_TILE = 512
VT_ROWS = HEAD_DIM + 16
GATE_ROWS = 16
VMEM_LIMIT = 56 * 1024 * 1024

_BF = jnp.bfloat16
_F32 = jnp.float32


def _dot(a, b):
    return jnp.dot(a, b, preferred_element_type=_F32)


def _gelu(x):
    c = math.sqrt(2.0 / math.pi)
    return 0.5 * x * (1.0 + jnp.tanh(c * (x + 0.044715 * (x * x * x))))


def _sigmoid(x):
    return 1.0 / (1.0 + jnp.exp(-x))


def _layer_norm(x, g, b):
    mu = jnp.mean(x, axis=-1, keepdims=True)
    xc = x - mu
    var = jnp.mean(xc * xc, axis=-1, keepdims=True)
    return xc * lax.rsqrt(var + LN_EPS) * g + b


def _position_lanes(pos, shape):
    lane = lax.broadcasted_iota(jnp.int32, shape, 1)
    return jnp.where(lane == 0, (pos % POS_RADIX).astype(_F32),
                     jnp.where(lane == 1, (pos // POS_RADIX).astype(_F32), 0.0)).astype(_BF)


def _stage_bf16(first_step, pairs):
    @pl.when(first_step)
    def _():
        for src, dst in pairs:
            dst[...] = src[...].astype(_BF)


def _resident(shape, index_map):
    return pl.BlockSpec(shape, index_map, pipeline_mode=pl.Buffered(1))


def _gm_kernel(x_ref, wgm32_ref, wm032_ref, lng_ref, lnb_ref, ws_ref, bs_ref, wpg32_ref, o_ref,
               vg_ref, wgm_ref, wm0_ref, wpg_ref):
    _stage_bf16(pl.program_id(0) == 0, [(wgm32_ref, wgm_ref), (wm032_ref, wm0_ref), (wpg32_ref, wpg_ref)])
    tm = x_ref.shape[0]
    xb = x_ref[...].astype(_BF)
    z = _gelu(_dot(xb, wgm_ref[...]))
    gate_logits = _dot(xb, wm0_ref[...])
    u = z[:, :D_MODEL]
    v = _layer_norm(z[:, D_MODEL:], lng_ref[...], lnb_ref[...]).astype(_BF)
    row = lax.broadcasted_iota(jnp.int32, (GM_CHUNK, GM_CHUNK), 0)
    col = lax.broadcasted_iota(jnp.int32, (GM_CHUNK, GM_CHUNK), 1)
    for gi in range(GM_GROUPS):
        w = jnp.where(row >= col, ws_ref[gi], 0.0).astype(_BF)
        for c in range(tm // GM_CHUNK):
            blk = v[c * GM_CHUNK:(c + 1) * GM_CHUNK, gi * LANES:(gi + 1) * LANES]
            vg_ref[c * GM_CHUNK:(c + 1) * GM_CHUNK, gi * LANES:(gi + 1) * LANES] = _dot(w, blk) + bs_ref[gi]
    y = (u * vg_ref[...]).astype(_BF)
    o_ref[...] = _sigmoid(gate_logits) * _dot(y, wpg_ref[...])


def _gm_mixer(x2, w_in, layer, wm0, lng, lnb, ws, bs, wpg):
    T = x2.shape[0]
    tm = ROW_TILE
    const2 = lambda i: (0, 0)
    const3 = lambda i: (0, 0, 0)
    return pl.pallas_call(
        _gm_kernel,
        out_shape=jax.ShapeDtypeStruct((T, D_MODEL), _F32),
        grid=(T // tm,),
        in_specs=[
            pl.BlockSpec((tm, D_MODEL), lambda i: (i, 0)),
            _resident((D_MODEL, 2 * D_MODEL), lambda i: (layer, 0)),
            _resident((D_MODEL, D_MODEL), const2),
            pl.BlockSpec((1, D_MODEL), const2),
            pl.BlockSpec((1, D_MODEL), const2),
            pl.BlockSpec((GM_GROUPS, GM_CHUNK, GM_CHUNK), const3),
            pl.BlockSpec((GM_GROUPS, GM_CHUNK, LANES), const3),
            _resident((D_MODEL, D_MODEL), const2),
        ],
        out_specs=pl.BlockSpec((tm, D_MODEL), lambda i: (i, 0)),
        scratch_shapes=[pltpu.VMEM((tm, D_MODEL), _F32), pltpu.VMEM((D_MODEL, 2 * D_MODEL), _BF),
                        pltpu.VMEM((D_MODEL, D_MODEL), _BF), pltpu.VMEM((D_MODEL, D_MODEL), _BF)],
        compiler_params=pltpu.CompilerParams(
            dimension_semantics=("arbitrary",), vmem_limit_bytes=VMEM_LIMIT),
        name="gm_mixer",
    )(x2, w_in, wm0, lng, lnb, ws, bs, wpg)


def _qkv_kernel(x_ref, wq32_ref, wkv32_ref, wg32_ref, qt_ref, kc_ref, vc_ref, ksa_ref, vst_ref, kwa_ref, vwt_ref,
                gt_ref, wq_ref, wkv_ref, wg_ref):
    _stage_bf16((pl.program_id(0) == 0) & (pl.program_id(1) == 0),
                [(wq32_ref, wq_ref), (wkv32_ref, wkv_ref), (wg32_ref, wg_ref)])
    tm = x_ref.shape[0]
    xb = x_ref[...].astype(_BF)
    zq = _dot(xb, wq_ref[...]) * (HEAD_DIM ** -0.5 * LOG2_E)
    for tb in range(tm // Q_BLOCK):
        for h in range(N_KV_HEADS):
            for g in range(Q_PER_KV):
                c0 = (h * Q_PER_KV + g) * HEAD_DIM
                blk = zq[tb * Q_BLOCK:(tb + 1) * Q_BLOCK, c0:c0 + HEAD_DIM]
                qt_ref[tb, h, :, g * Q_BLOCK:(g + 1) * Q_BLOCK] = blk.T.astype(_BF)
    z = _dot(xb, wkv_ref[...])
    kpos = pl.program_id(1) * tm + lax.broadcasted_iota(jnp.int32, (tm, LANES), 0)
    blk_lane = lax.broadcasted_iota(jnp.int32, (tm, LANES), 1)
    sel_lanes = jnp.where(blk_lane == 0, (kpos % SEL_TILE).astype(_F32),
                          jnp.where(kpos // SEL_BLOCK == blk_lane, NEG_INF, 0.0)).astype(_BF)
    win_lanes = _position_lanes(kpos, (tm, LANES))
    ones = jnp.ones((VT_ROWS - HEAD_DIM, tm), _BF)
    for h in range(N_KV_HEADS):
        def col(j):
            return z[:, j * 2 * HEAD_DIM + h * HEAD_DIM: j * 2 * HEAD_DIM + (h + 1) * HEAD_DIM]
        kc_ref[0, h] = col(0)
        vc_ref[0, h] = col(1)
        ksa_ref[0, h, :, :HEAD_DIM] = col(2).astype(_BF)
        ksa_ref[0, h, :, HEAD_DIM:] = sel_lanes
        vst_ref[0, h, :HEAD_DIM] = col(3).T.astype(_BF)
        vst_ref[0, h, HEAD_DIM:] = ones
        kwa_ref[0, h, :, :HEAD_DIM] = col(4).astype(_BF)
        kwa_ref[0, h, :, HEAD_DIM:] = win_lanes
        vwt_ref[0, h, :HEAD_DIM] = col(5).T.astype(_BF)
        vwt_ref[0, h, HEAD_DIM:] = ones
    zg_t = _sigmoid(_dot(xb, wg_ref[...])).T
    per_head = 3 * Q_PER_KV
    for h in range(N_KV_HEADS):
        gt_ref[h] = zg_t[h * per_head:h * per_head + GATE_ROWS]


def _qkv_proj(x2, w_in, layer, q_col, B, S):
    T = x2.shape[0]
    tm = ROW_TILE
    nsb = S // tm
    gd = Q_PER_KV * Q_BLOCK
    aw = N_KV_HEADS * Q_PER_KV * HEAD_DIM
    kvw = 6 * N_KV_HEADS * HEAD_DIM
    kv_col = q_col + aw
    gate_col = kv_col + kvw
    assert q_col % aw == 0 and kv_col % kvw == 0 and gate_col % LANES == 0
    rows_spec = lambda w: pl.BlockSpec((1, N_KV_HEADS, tm, w), lambda b, s: (b, 0, s, 0))
    rows_shape = lambda w, dt: jax.ShapeDtypeStruct((B, N_KV_HEADS, S, w), dt)
    cols_spec = pl.BlockSpec((1, N_KV_HEADS, VT_ROWS, tm), lambda b, s: (b, 0, 0, s))
    cols_shape = jax.ShapeDtypeStruct((B, N_KV_HEADS, VT_ROWS, S), _BF)
    return pl.pallas_call(
        _qkv_kernel,
        out_shape=(
            jax.ShapeDtypeStruct((T // Q_BLOCK, N_KV_HEADS, HEAD_DIM, gd), _BF),
            rows_shape(HEAD_DIM, _F32), rows_shape(HEAD_DIM, _F32),
            rows_shape(2 * HEAD_DIM, _BF), cols_shape,
            rows_shape(2 * HEAD_DIM, _BF), cols_shape,
            jax.ShapeDtypeStruct((N_KV_HEADS, GATE_ROWS, T), _F32),
        ),
        grid=(B, nsb),
        in_specs=[
            pl.BlockSpec((tm, D_MODEL), lambda b, s: (b * nsb + s, 0)),
            _resident((D_MODEL, aw), lambda b, s: (layer, q_col // aw)),
            _resident((D_MODEL, kvw), lambda b, s: (layer, kv_col // kvw)),
            _resident((D_MODEL, LANES), lambda b, s: (layer, gate_col // LANES)),
        ],
        out_specs=(
            pl.BlockSpec((tm // Q_BLOCK, N_KV_HEADS, HEAD_DIM, gd), lambda b, s: (b * nsb + s, 0, 0, 0)),
            rows_spec(HEAD_DIM), rows_spec(HEAD_DIM), rows_spec(2 * HEAD_DIM), cols_spec,
            rows_spec(2 * HEAD_DIM), cols_spec,
            pl.BlockSpec((N_KV_HEADS, GATE_ROWS, tm), lambda b, s: (0, 0, b * nsb + s)),
        ),
        scratch_shapes=[pltpu.VMEM((D_MODEL, aw), _BF), pltpu.VMEM((D_MODEL, kvw), _BF),
                        pltpu.VMEM((D_MODEL, LANES), _BF)],
        compiler_params=pltpu.CompilerParams(
            dimension_semantics=("arbitrary", "arbitrary"), vmem_limit_bytes=VMEM_LIMIT),
        name="qkv_proj",
    )(x2, w_in, w_in, w_in)


def _compress_kernel(kc_ref, vc_ref, pek_ref, pev_ref, w1k_ref, w1v_ref, w2k_ref, w2v_ref, ko_ref, vo_ref):
    nc = ko_ref.shape[1]

    def tokens(src, pe, w1, w2):
        first, second = None, None
        for p in range(0, CMP_STRIDE, 2):
            rows = [src[0, pl.ds(p + d, nc, stride=CMP_STRIDE), :] for d in range(2)]
            lo = jnp.concatenate([rows[d] + pe[p + d:p + d + 1, :] for d in range(2)], axis=1)
            hi = jnp.concatenate([rows[d] + pe[CMP_STRIDE + p + d:CMP_STRIDE + p + d + 1, :] for d in range(2)], axis=1)
            a = _dot(lo.astype(_BF), w1[p // 2])
            b = _dot(hi.astype(_BF), w1[(CMP_STRIDE + p) // 2])
            first = a if first is None else first + a
            second = b if second is None else second + b
        pre = first + pltpu.roll(second, nc - 1, 0)
        return _dot(_gelu(pre).astype(_BF), w2[...])

    ko_ref[0, :, :HEAD_DIM] = tokens(kc_ref, pek_ref, w1k_ref, w2k_ref).astype(_BF)
    start = lax.broadcasted_iota(jnp.int32, (nc, LANES), 0) * CMP_STRIDE
    ko_ref[0, :, HEAD_DIM:] = _position_lanes(start, (nc, LANES))
    vo_ref[0, :HEAD_DIM] = tokens(vc_ref, pev_ref, w1v_ref, w2v_ref).T.astype(_BF)
    mi = lax.broadcasted_iota(jnp.int32, (LANES, nc), 0)
    ni = lax.broadcasted_iota(jnp.int32, (LANES, nc), 1)
    vo_ref[0, HEAD_DIM:] = jnp.where((ni * CMP_STRIDE + (CMP_BLOCK - 1) >= mi * SEL_BLOCK)
                                     & (ni * CMP_STRIDE <= mi * SEL_BLOCK + (SEL_BLOCK - 1)), 1.0, 0.0).astype(_BF)


def _nsa_compress(kc, vc, pek, pev, w1k, w1v, w2k, w2v):
    BH, S, _ = kc.shape
    nc = S // CMP_STRIDE
    pair = 2 * HEAD_DIM
    const2 = lambda i: (0, 0)
    const3 = lambda i: (0, 0, 0)
    seq_rows = pl.BlockSpec((1, S, HEAD_DIM), lambda i: (i, 0, 0))
    pe_spec = pl.BlockSpec((CMP_BLOCK, HEAD_DIM), const2)
    w1_spec = pl.BlockSpec((CMP_BLOCK // 2, pair, HEAD_DIM), const3)
    w2_spec = pl.BlockSpec((HEAD_DIM, HEAD_DIM), const2)
    return pl.pallas_call(
        _compress_kernel,
        out_shape=(jax.ShapeDtypeStruct((BH, nc, 2 * HEAD_DIM), _BF),
                   jax.ShapeDtypeStruct((BH, HEAD_DIM + LANES, nc), _BF)),
        grid=(BH,),
        in_specs=[seq_rows, seq_rows, pe_spec, pe_spec, w1_spec, w1_spec, w2_spec, w2_spec],
        out_specs=(pl.BlockSpec((1, nc, 2 * HEAD_DIM), lambda i: (i, 0, 0)),
                   pl.BlockSpec((1, HEAD_DIM + LANES, nc), lambda i: (i, 0, 0))),
        compiler_params=pltpu.CompilerParams(
            dimension_semantics=("arbitrary",), vmem_limit_bytes=VMEM_LIMIT),
        name="nsa_compress",
    )(kc, vc, pek, pev, w1k, w1v, w2k, w2v)


def _mark_top_blocks(score, notsel, rounds):
    rows = lax.broadcasted_iota(jnp.int32, score.shape, 0).astype(_F32)
    for _ in range(rounds):
        mx = jnp.max(score, axis=0, keepdims=True)
        idx = jnp.min(jnp.where(score == mx, rows, float(LANES)), axis=0, keepdims=True)
        hit = rows == idx
        notsel = jnp.where(hit, 0.0, notsel)
        score = jnp.where(hit, -jnp.inf, score)
    return notsel


def _attn_kernel(qt_ref, gt_ref, kca_ref, vct_ref, ksa_ref, vst_ref, kwa_ref, vwt_ref, o_ref,
                 m_ref, acc_ref, lhs_ref, part_ref, gate_ref, bits_ref, *, seq, n_seq):
    step = pl.program_id(0)
    n_pair = seq // (Q_BLOCK * PAIR)
    last = n_seq * n_pair - 1
    front_pair = jnp.minimum(step, last)
    back_pair = jnp.clip(step - 1, 0, last)
    nc = kca_ref.shape[1]
    n_sel = seq // SEL_BLOCK
    n_forced = 1 + N_LOCAL_FORCED
    k_top = min(N_SELECT, n_sel)
    G = Q_PER_KV
    W = G * Q_BLOCK
    words = LANES // BITS_PER_WORD
    blocks_per_tile = SEL_TILE // SEL_BLOCK
    tiles_per_word = BITS_PER_WORD // blocks_per_tile
    put = step % 2
    get = 1 - put
    put3 = step % 3
    get3 = (step + 1) % 3
    units = range(PAIR)

    def flag_base(slot, u):
        return (slot * PAIR + u) * (words + 1)

    @pl.when(step == 0)
    def _():
        lhs_ref[1] = jnp.zeros(lhs_ref.shape[1:], _BF)
        part_ref[...] = jnp.zeros(part_ref.shape, _F32)
        gate_ref[...] = jnp.zeros(gate_ref.shape, _F32)
        acc_ref[...] = jnp.ones(acc_ref.shape, _F32)
        for k in range(PAIR * (words + 1)):
            bits_ref[flag_base(1, 0) + k] = 0

    lane_w = lax.broadcasted_iota(jnp.int32, (1, W), 1)

    def head_slopes(pair_index):
        h = (pair_index // n_pair) % N_KV_HEADS
        slope = jnp.zeros((1, W), _F32)
        for g in range(G):
            sg = jnp.where(h == 0, _F32(2.0 ** -(g + 1)), _F32(2.0 ** -(G + g + 1)))
            slope = jnp.where(lane_w // Q_BLOCK == g, sg, slope)
        return (slope * LOG2_E).astype(_BF).astype(_F32)

    slope = head_slopes(front_pair)
    slope_b = head_slopes(back_pair)
    strips =[slice(g * Q_BLOCK, (g + 1) * Q_BLOCK) for g in range(G)]
    row_aug = lax.broadcasted_iota(jnp.int32, (LANES, W), 0)
    key_row = lax.broadcasted_iota(jnp.int32, (SEL_TILE, Q_BLOCK), 0)
    pos_rows = jnp.where(row_aug == 0, slope, jnp.where(row_aug == 1, slope * POS_RADIX, 0.0)).astype(_BF)

    for u in units:
        out_t = (part_ref[get3, u] + acc_ref[u, 0:HEAD_DIM]
                 * (gate_ref[get3, u][0:1] * (1.0 / acc_ref[u, HEAD_DIM:HEAD_DIM + 1])))
        for g, cs in enumerate(strips):
            o_ref[u * Q_BLOCK:(u + 1) * Q_BLOCK, g * HEAD_DIM:(g + 1) * HEAD_DIM] = out_t[:, cs].T.astype(_BF)

    front = [{} for _ in units]
    for u, f in zip(units, front):
        f["q0"] = (front_pair % n_pair * PAIR + u) * Q_BLOCK
        f["tok"] = f["q0"] + lane_w % Q_BLOCK
        f["qt"] = qt_ref[u, 0]
        f["s"] = _dot(kca_ref[0], jnp.concatenate([f["qt"], pos_rows], axis=0))

    def tile_scores(bk, j, causal, live=None):
        k0 = j * SEL_TILE if isinstance(j, int) else pl.multiple_of(j * SEL_TILE, SEL_TILE)
        st = _dot(ksa_ref[0, 0, pl.ds(k0, SEL_TILE), :], bk["lhs"])
        if causal:
            ahead = jnp.where(k0 + key_row > bk["tok"][:, :Q_BLOCK], NEG_INF, 0.0)
            st = jnp.concatenate([st[:, cs] + ahead for cs in strips], axis=1)
        shift = slope_b * (bk["p0"] - k0).astype(_F32)
        if live is not None:
            shift = jnp.where(live, shift, -NEG_INF)
        return st, shift, k0

    def tile_max(tiles):
        m = None
        for st, shift, _ in tiles:
            cm = jnp.max(st, axis=0, keepdims=True) - shift
            m = cm if m is None else jnp.maximum(m, cm)
        return m

    def tile_sums(tiles, m):
        acc = None
        for st, shift, k0 in tiles:
            sub = m + shift
            pt = jnp.concatenate([jnp.exp2(st[:, cs] - sub[:, cs]) for cs in strips], axis=1)
            ai = _dot(vst_ref[0, 0, :, pl.ds(k0, SEL_TILE)], pt.astype(_BF))
            acc = ai if acc is None else acc + ai
        return acc

    back = [{} for _ in units]
    for u, bk in zip(units, back):
        bk["p0"] = (back_pair % n_pair * PAIR + u) * Q_BLOCK
        bk["tok"] = bk["p0"] + lane_w % Q_BLOCK
        bk["lhs"] = lhs_ref[get, u]
        bk["first_local"] = bk["p0"] // SEL_TILE - (LOCAL_TILES - 1)
        bk["tiles"] = [tile_scores(bk, 0, False, live=bk["first_local"] > 0)]
        for i in range(LOCAL_TILES):
            j = bk["first_local"] + i
            last = i == LOCAL_TILES - 1
            bk["tiles"].append(tile_scores(bk, jnp.maximum(j, 0), last, live=None if last else j >= 0))

    n_row = lax.broadcasted_iota(jnp.int32, (nc, Q_BLOCK), 0)
    for f in front:
        n_last = (f["tok"] - (CMP_BLOCK - 1)) // CMP_STRIDE
        mask_c = jnp.where(n_row <= n_last[:, :Q_BLOCK], 0.0, NEG_INF)
        e_parts, inv_parts = [], []
        for cs in strips:
            sg = f["s"][:, cs] + mask_c
            eg = jnp.exp2(sg - jnp.max(sg, axis=0, keepdims=True))
            e_parts.append(eg.astype(_BF))
            inv_parts.append(jnp.where(n_last[:, cs] >= 0, 1.0 / jnp.sum(eg, axis=0, keepdims=True), 0.0))
        both = _dot(vct_ref[0], jnp.concatenate(e_parts, axis=1)) * jnp.concatenate(inv_parts, axis=1)
        f["o_cmp"] = both[:HEAD_DIM]
        imp_t = both[HEAD_DIM:, strips[0]]
        for cs in strips[1:]:
            imp_t = imp_t + both[HEAD_DIM:, cs]
        f["imp_t"] = imp_t

    wlen = WINDOW + Q_BLOCK
    for f in front:
        anchor = jnp.concatenate([jnp.minimum(f["imp_t"][0:1], 0.0)] * G, axis=1)
        win_rows = jnp.where(row_aug == 0, slope + anchor,
                             jnp.where(row_aug == 1, slope * POS_RADIX, 0.0)).astype(_BF)
        f["w0"] = pl.multiple_of(jnp.maximum(f["q0"] - WINDOW, 0), Q_BLOCK)
        f["sw"] = _dot(kwa_ref[0, 0, pl.ds(f["w0"], wlen), :], jnp.concatenate([f["qt"], win_rows], axis=0))

    for u, bk in zip(units, back):
        m = tile_max(bk["tiles"])
        m_ref[u, 0:1] = m
        acc_ref[u] = tile_sums(bk["tiles"], m)

    for f in front:
        w_row = f["w0"] + lax.broadcasted_iota(jnp.int32, (wlen, Q_BLOCK), 0)
        dist = f["tok"][:, :Q_BLOCK] - w_row
        mask_w = jnp.where(dist.astype(jnp.uint32) < WINDOW, 0.0, NEG_INF)
        e_parts = []
        for cs in strips:
            sg = f["sw"][:, cs] + mask_w
            e_parts.append(jnp.exp2(sg - jnp.max(sg, axis=0, keepdims=True)).astype(_BF))
        win = _dot(vwt_ref[0, 0, :, pl.ds(f["w0"], wlen)], jnp.concatenate(e_parts, axis=1))
        f["o_win"] = win[:HEAD_DIM] * (1.0 / win[HEAD_DIM:HEAD_DIM + 1])

    mrow = lax.broadcasted_iota(jnp.int32, (LANES, Q_BLOCK), 0)
    blk_row = lax.broadcasted_iota(jnp.int32, (LANES, 1), 0)
    blk_bit = jnp.left_shift(1, blk_row % BITS_PER_WORD).astype(_F32)
    for u, f in zip(units, front):
        gates = gt_ref[0][:, u * Q_BLOCK:(u + 1) * Q_BLOCK]

        def gate_row(branch):
            return jnp.concatenate([gates[3 * g + branch:3 * g + branch + 1] for g in range(G)], axis=1)

        tcol = lax.broadcasted_iota(jnp.int32, (LANES, Q_BLOCK), 1) + f["q0"]
        lag = tcol // SEL_BLOCK - mrow
        forced = (mrow == 0) | ((lag >= 0) & (lag < N_LOCAL_FORCED))
        score = jnp.where(forced | (lag < 0) | (mrow >= n_sel), -jnp.inf, f["imp_t"])
        notsel_t = _mark_top_blocks(score, jnp.where(forced, 0.0, 1.0), k_top - n_forced)

        sel_rows = jnp.where(row_aug == 0, slope, jnp.concatenate([notsel_t] * G, axis=1)).astype(_BF)
        lhs_ref[put, u] = jnp.concatenate([f["qt"], sel_rows], axis=0)
        part_ref[put3, u] = gate_row(0) * f["o_cmp"] + gate_row(2) * f["o_win"]
        gate_ref[put3, u] = jnp.broadcast_to(gate_row(1), gate_ref.shape[2:])

        far_end = (f["q0"] // SEL_TILE - (LOCAL_TILES - 1)) * blocks_per_tile
        blk_on = jnp.where((jnp.min(notsel_t, axis=1, keepdims=True) < 0.5)
                           & (blk_row >= blocks_per_tile) & (blk_row < far_end), 1.0, 0.0)
        for k in range(words):
            word = jnp.sum((blk_on * blk_bit)[k * BITS_PER_WORD:(k + 1) * BITS_PER_WORD])
            bits_ref[flag_base(put, u) + k] = word.astype(jnp.int32)
        bits_ref[flag_base(put, u) + words] = jnp.sum(blk_on).astype(jnp.int32)

    for u, bk in zip(units, back):
        def far_tile(j, carry, u=u, bk=bk):
            word = bits_ref[flag_base(get, u) + j // tiles_per_word]
            tile_bits = (word >> ((j % tiles_per_word) * blocks_per_tile)) & ((1 << blocks_per_tile) - 1)

            @pl.when(tile_bits != 0)
            def _():
                tile = [tile_scores(bk, j, False)]
                m_old = m_ref[u, 0:1]
                m_new = jnp.maximum(m_old, tile_max(tile))
                m_ref[u, 0:1] = m_new
                acc_ref[u] = jnp.exp2(m_old - m_new) * acc_ref[u] + tile_sums(tile, m_new)
            return carry

        @pl.when(bits_ref[flag_base(get, u) + words] != 0)
        def _(far_tile=far_tile, bk=bk):
            lax.fori_loop(1, bk["first_local"], far_tile, 0)


def _nsa_attention(qt, gt, kca, vct, ksa, vst, kwa, vwt, B, S):
    n_pair = S // (Q_BLOCK * PAIR)
    T = B * S
    nc = kca.shape[1]
    gd = Q_PER_KV * HEAD_DIM
    W = Q_PER_KV * Q_BLOCK
    n_seq = B * N_KV_HEADS
    last = n_seq * n_pair - 1
    front = lambda i: jnp.minimum(i, last)
    back = lambda i: jnp.clip(i - 1, 0, last)
    done = lambda i: jnp.maximum(i - 2, 0)
    seq_of = lambda p: p // n_pair
    blk_of = lambda p: (seq_of(p) // N_KV_HEADS) * n_pair + p % n_pair
    head_of = lambda p: seq_of(p) % N_KV_HEADS
    rows = lambda stage: pl.BlockSpec((1, 1, S, 2 * HEAD_DIM),
                                      lambda i: (seq_of(stage(i)) // N_KV_HEADS, head_of(stage(i)), 0, 0))
    cols = lambda stage: pl.BlockSpec((1, 1, VT_ROWS, S),
                                      lambda i: (seq_of(stage(i)) // N_KV_HEADS, head_of(stage(i)), 0, 0))
    return pl.pallas_call(
        functools.partial(_attn_kernel, seq=S, n_seq=n_seq),
        out_shape=jax.ShapeDtypeStruct((T, N_KV_HEADS * gd), _BF),
        grid=(last + 3,),
        in_specs=[
            pl.BlockSpec((PAIR, 1, HEAD_DIM, W), lambda i: (blk_of(front(i)), head_of(front(i)), 0, 0)),
            pl.BlockSpec((1, GATE_ROWS, PAIR * Q_BLOCK), lambda i: (head_of(front(i)), 0, blk_of(front(i)))),
            pl.BlockSpec((1, nc, 2 * HEAD_DIM), lambda i: (seq_of(front(i)), 0, 0)),
            pl.BlockSpec((1, HEAD_DIM + LANES, nc), lambda i: (seq_of(front(i)), 0, 0)),
            rows(back), cols(back), rows(front), cols(front),
        ],
        out_specs=pl.BlockSpec((PAIR * Q_BLOCK, gd), lambda i: (blk_of(done(i)), head_of(done(i)))),
        scratch_shapes=[pltpu.VMEM((PAIR, 8, W), _F32), pltpu.VMEM((PAIR, VT_ROWS, W), _F32),
                        pltpu.VMEM((2, PAIR, 2 * HEAD_DIM, W), _BF), pltpu.VMEM((3, PAIR, HEAD_DIM, W), _F32),
                        pltpu.VMEM((3, PAIR, 8, W), _F32),
                        pltpu.SMEM((2 * PAIR * (LANES // BITS_PER_WORD + 1),), jnp.int32)],
        compiler_params=pltpu.CompilerParams(
            dimension_semantics=("arbitrary",), vmem_limit_bytes=VMEM_LIMIT),
        name="nsa_attention",
    )(qt, gt, kca, vct, ksa, vst, kwa, vwt)


def _merge_ffn_kernel(x_ref, gm_ref, yn_ref, wm1_ref, wpn_ref, wo_ref, g1_ref, b1_ref,
                      w1_ref, w2_ref, g2_ref, b2_ref, o_ref):
    x = x_ref[...]
    gate = _sigmoid(_dot(x.astype(_BF), wm1_ref[...]))
    merged = gm_ref[...] + gate * _dot(yn_ref[...], wpn_ref[...])
    mix = _dot(merged.astype(_BF), wo_ref[...])
    hid = _layer_norm(DEEPNORM_ALPHA * x + mix, g1_ref[...], b1_ref[...])
    hb = hid.astype(_BF)
    f = jnp.zeros(hid.shape, _F32)
    for c in range(D_FF // D_MODEL):
        a = jnp.maximum(_dot(hb, w1_ref[:, c * D_MODEL:(c + 1) * D_MODEL]), 0.0)
        f = f + _dot((a * a).astype(_BF), w2_ref[c * D_MODEL:(c + 1) * D_MODEL, :])
    o_ref[...] = _layer_norm(DEEPNORM_ALPHA * hid + f, g2_ref[...], b2_ref[...])


def _merge_ffn(x2, gm, yn, wm1, wpn, wo, g1, b1, w1, w2, g2, b2):
    T = x2.shape[0]
    tm = ROW_TILE
    rows = lambda w: pl.BlockSpec((tm, w), lambda i: (i, 0))
    const = lambda r, c: _resident((r, c), lambda i: (0, 0))
    return pl.pallas_call(
        _merge_ffn_kernel,
        out_shape=jax.ShapeDtypeStruct((T, D_MODEL), _F32),
        grid=(T // tm,),
        in_specs=[rows(D_MODEL), rows(D_MODEL), rows(D_MODEL),
                  const(D_MODEL, D_MODEL), const(D_MODEL, D_MODEL), const(D_MODEL, D_MODEL),
                  const(1, D_MODEL), const(1, D_MODEL),
                  const(D_MODEL, D_FF), const(D_FF, D_MODEL),
                  const(1, D_MODEL), const(1, D_MODEL)],
        out_specs=rows(D_MODEL),
        compiler_params=pltpu.CompilerParams(
            dimension_semantics=("arbitrary",), vmem_limit_bytes=VMEM_LIMIT),
        name="merge_ffn",
    )(x2, gm, yn, wm1, wpn, wo, g1, b1, w1, w2, g2, b2)


def _layer(x2, B, S, w_in, layer, gm_ln_g, gm_ln_b, gm_w_s, gm_b_s, cmp_pe_k, cmp_w1_k, cmp_w2_k,
           cmp_pe_v, cmp_w1_v, cmp_w2_v, w_proj_gm, w_proj_nsa, w_out,
           ln1_g, ln1_b, w_ff1, w_ff2, ln2_g, ln2_b):
    o_q = 2 * D_MODEL
    o_m = o_q + (Q_PER_KV + 6) * N_KV_HEADS * HEAD_DIM + 3 * Q_PER_KV * N_KV_HEADS
    row = lambda v: v.reshape(1, -1)
    merge_cols = lambda j: w_in[layer * D_MODEL:(layer + 1) * D_MODEL, o_m + j * D_MODEL:o_m + (j + 1) * D_MODEL]

    gm = _gm_mixer(x2, w_in, layer, merge_cols(0), row(gm_ln_g), row(gm_ln_b), gm_w_s,
                   jnp.broadcast_to(gm_b_s[:, :, None], (GM_GROUPS, GM_CHUNK, LANES)), w_proj_gm)

    qt, kc, vc, ksa, vst, kwa, vwt, gt = _qkv_proj(x2, w_in, layer, o_q, B, S)

    heads = lambda a: a.reshape(B * N_KV_HEADS, S, HEAD_DIM)
    pairs = lambda w: w.astype(_BF).reshape(CMP_BLOCK // 2, 2 * HEAD_DIM, HEAD_DIM)
    kca, vct = _nsa_compress(heads(kc), heads(vc), cmp_pe_k, cmp_pe_v, pairs(cmp_w1_k), pairs(cmp_w1_v),
                             cmp_w2_k.astype(_BF), cmp_w2_v.astype(_BF))

    yn = _nsa_attention(qt, gt, kca, vct, ksa, vst, kwa, vwt, B, S)

    return _merge_ffn(x2, gm, yn, merge_cols(1).astype(_BF), w_proj_nsa.astype(_BF),
                      w_out.astype(_BF), row(ln1_g), row(ln1_b), w_ff1.astype(_BF), w_ff2.astype(_BF),
                      row(ln2_g), row(ln2_b))


def kernel(x, w_in, gm_ln_g, gm_ln_b, gm_w_s, gm_b_s, cmp_pe_k, cmp_w1_k, cmp_w2_k, cmp_pe_v, cmp_w1_v, cmp_w2_v, w_proj_gm, w_proj_nsa, w_out, ln1_g, ln1_b, w_ff1, w_ff2, ln2_g, ln2_b):
    B, S, D = x.shape
    assert D == D_MODEL and S % ROW_TILE == 0 and WINDOW + Q_BLOCK <= S <= SEL_BLOCK * LANES
    assert S // POS_RADIX <= POS_RADIX and S % (PAIR * Q_BLOCK) == 0
    h = x.reshape(B * S, D)
    depth, _, d_in = w_in.shape
    w_rows = w_in.reshape(depth * D, d_in)
    for l in range(depth):
        h = _layer(h, B, S, w_rows, l, gm_ln_g[l], gm_ln_b[l], gm_w_s[l], gm_b_s[l],
                   cmp_pe_k[l], cmp_w1_k[l], cmp_w2_k[l], cmp_pe_v[l], cmp_w1_v[l], cmp_w2_v[l],
                   w_proj_gm[l], w_proj_nsa[l], w_out[l], ln1_g[l], ln1_b[l],
                   w_ff1[l], w_ff2[l], ln2_g[l], ln2_b[l])
    return h.reshape(B, S, D)
```

```python
import functools
import math

import jax
import jax.numpy as jnp
from jax import lax
from jax.experimental import pallas as pl
from jax.experimental.pallas import tpu as pltpu

D_MODEL = 1024
GM_GROUPS = 8
GM_CHUNK = 128
N_KV_HEADS = 2
Q_PER_KV = 4
HEAD_DIM = 128
CMP_BLOCK = 32
CMP_STRIDE = 16
SEL_BLOCK = 64
N_SELECT = 16
N_LOCAL_FORCED = 2
WINDOW = 512
Q_BLOCK = 128
D_FF = 4 * D_MODEL
DEEPNORM_ALPHA = 2.0 ** 0.25
LN_EPS = 1e-5
NEG_INF = -1e30
LOG2_E = math.log2(math.e)

LANES = 128
SEL_TILE = 128
LOCAL_TILES = 11
PAIR = 2
BITS_PER_WORD = 16
POS_RADIX = 256
ROW_TILE = 512
VT_ROWS = HEAD_DIM + 16
GATE_ROWS = 16
VMEM_LIMIT = 56 * 1024 * 1024

_BF = jnp.bfloat16
_F32 = jnp.float32


def _dot(a, b):
    return jnp.dot(a, b, preferred_element_type=_F32)


def _gelu(x):
    c = math.sqrt(2.0 / math.pi)
    return 0.5 * x * (1.0 + jnp.tanh(c * (x + 0.044715 * (x * x * x))))


def _sigmoid(x):
    return 1.0 / (1.0 + jnp.exp(-x))


def _layer_norm(x, g, b):
    mu = jnp.mean(x, axis=-1, keepdims=True)
    xc = x - mu
    var = jnp.mean(xc * xc, axis=-1, keepdims=True)
    return xc * lax.rsqrt(var + LN_EPS) * g + b


def _position_lanes(pos, shape):
    lane = lax.broadcasted_iota(jnp.int32, shape, 1)
    return jnp.where(lane == 0, (pos % POS_RADIX).astype(_F32),
                     jnp.where(lane == 1, (pos // POS_RADIX).astype(_F32), 0.0)).astype(_BF)


def _stage_bf16(first_step, pairs):
    @pl.when(first_step)
    def _():
        for src, dst in pairs:
            dst[...] = src[...].astype(_BF)


def _resident(shape, index_map):
    return pl.BlockSpec(shape, index_map, pipeline_mode=pl.Buffered(1))


def _gm_kernel(x_ref, wgm32_ref, wm032_ref, lng_ref, lnb_ref, ws_ref, bs_ref, wpg32_ref, o_ref,
               vg_ref, wgm_ref, wm0_ref, wpg_ref):
    _stage_bf16(pl.program_id(0) == 0, [(wgm32_ref, wgm_ref), (wm032_ref, wm0_ref), (wpg32_ref, wpg_ref)])
    tm = x_ref.shape[0]
    xb = x_ref[...].astype(_BF)
    z = _gelu(_dot(xb, wgm_ref[...]))
    gate_logits = _dot(xb, wm0_ref[...])
    u = z[:, :D_MODEL]
    v = _layer_norm(z[:, D_MODEL:], lng_ref[...], lnb_ref[...]).astype(_BF)
    row = lax.broadcasted_iota(jnp.int32, (GM_CHUNK, GM_CHUNK), 0)
    col = lax.broadcasted_iota(jnp.int32, (GM_CHUNK, GM_CHUNK), 1)
    for gi in range(GM_GROUPS):
        w = jnp.where(row >= col, ws_ref[gi], 0.0).astype(_BF)
        for c in range(tm // GM_CHUNK):
            blk = v[c * GM_CHUNK:(c + 1) * GM_CHUNK, gi * LANES:(gi + 1) * LANES]
            vg_ref[c * GM_CHUNK:(c + 1) * GM_CHUNK, gi * LANES:(gi + 1) * LANES] = _dot(w, blk) + bs_ref[gi]
    y = (u * vg_ref[...]).astype(_BF)
    o_ref[...] = _sigmoid(gate_logits) * _dot(y, wpg_ref[...])


def _gm_mixer(x2, w_in, layer, wm0, lng, lnb, ws, bs, wpg):
    T = x2.shape[0]
    tm = ROW_TILE
    const2 = lambda i: (0, 0)
    const3 = lambda i: (0, 0, 0)
    return pl.pallas_call(
        _gm_kernel,
        out_shape=jax.ShapeDtypeStruct((T, D_MODEL), _F32),
        grid=(T // tm,),
        in_specs=[
            pl.BlockSpec((tm, D_MODEL), lambda i: (i, 0)),
            _resident((D_MODEL, 2 * D_MODEL), lambda i: (layer, 0)),
            _resident((D_MODEL, D_MODEL), const2),
            pl.BlockSpec((1, D_MODEL), const2),
            pl.BlockSpec((1, D_MODEL), const2),
            pl.BlockSpec((GM_GROUPS, GM_CHUNK, GM_CHUNK), const3),
            pl.BlockSpec((GM_GROUPS, GM_CHUNK, LANES), const3),
            _resident((D_MODEL, D_MODEL), const2),
        ],
        out_specs=pl.BlockSpec((tm, D_MODEL), lambda i: (i, 0)),
        scratch_shapes=[pltpu.VMEM((tm, D_MODEL), _F32), pltpu.VMEM((D_MODEL, 2 * D_MODEL), _BF),
                        pltpu.VMEM((D_MODEL, D_MODEL), _BF), pltpu.VMEM((D_MODEL, D_MODEL), _BF)],
        compiler_params=pltpu.CompilerParams(
            dimension_semantics=("arbitrary",), vmem_limit_bytes=VMEM_LIMIT),
        name="gm_mixer",
    )(x2, w_in, wm0, lng, lnb, ws, bs, wpg)


def _qkv_kernel(x_ref, wq32_ref, wkv32_ref, wg32_ref, qt_ref, kc_ref, vc_ref, ksa_ref, vst_ref, kwa_ref, vwt_ref,
                gt_ref, wq_ref, wkv_ref, wg_ref):
    _stage_bf16((pl.program_id(0) == 0) & (pl.program_id(1) == 0),
                [(wq32_ref, wq_ref), (wkv32_ref, wkv_ref), (wg32_ref, wg_ref)])
    tm = x_ref.shape[0]
    xb = x_ref[...].astype(_BF)
    zq = _dot(xb, wq_ref[...]) * (HEAD_DIM ** -0.5 * LOG2_E)
    for tb in range(tm // Q_BLOCK):
        for h in range(N_KV_HEADS):
            for g in range(Q_PER_KV):
                c0 = (h * Q_PER_KV + g) * HEAD_DIM
                blk = zq[tb * Q_BLOCK:(tb + 1) * Q_BLOCK, c0:c0 + HEAD_DIM]
                qt_ref[tb, h, :, g * Q_BLOCK:(g + 1) * Q_BLOCK] = blk.T.astype(_BF)
    z = _dot(xb, wkv_ref[...])
    kpos = pl.program_id(1) * tm + lax.broadcasted_iota(jnp.int32, (tm, LANES), 0)
    blk_lane = lax.broadcasted_iota(jnp.int32, (tm, LANES), 1)
    sel_lanes = jnp.where(blk_lane == 0, (kpos % SEL_TILE).astype(_F32),
                          jnp.where(kpos // SEL_BLOCK == blk_lane, NEG_INF, 0.0)).astype(_BF)
    win_lanes = _position_lanes(kpos, (tm, LANES))
    ones = jnp.ones((VT_ROWS - HEAD_DIM, tm), _BF)
    for h in range(N_KV_HEADS):
        def col(j):
            return z[:, j * 2 * HEAD_DIM + h * HEAD_DIM: j * 2 * HEAD_DIM + (h + 1) * HEAD_DIM]
        kc_ref[0, h] = col(0)
        vc_ref[0, h] = col(1)
        ksa_ref[0, h, :, :HEAD_DIM] = col(2).astype(_BF)
        ksa_ref[0, h, :, HEAD_DIM:] = sel_lanes
        vst_ref[0, h, :HEAD_DIM] = col(3).T.astype(_BF)
        vst_ref[0, h, HEAD_DIM:] = ones
        kwa_ref[0, h, :, :HEAD_DIM] = col(4).astype(_BF)
        kwa_ref[0, h, :, HEAD_DIM:] = win_lanes
        vwt_ref[0, h, :HEAD_DIM] = col(5).T.astype(_BF)
        vwt_ref[0, h, HEAD_DIM:] = ones
    zg_t = _sigmoid(_dot(xb, wg_ref[...])).T
    per_head = 3 * Q_PER_KV
    for h in range(N_KV_HEADS):
        gt_ref[h] = zg_t[h * per_head:h * per_head + GATE_ROWS]


def _qkv_proj(x2, w_in, layer, q_col, B, S):
    T = x2.shape[0]
    tm = ROW_TILE
    nsb = S // tm
    gd = Q_PER_KV * Q_BLOCK
    aw = N_KV_HEADS * Q_PER_KV * HEAD_DIM
    kvw = 6 * N_KV_HEADS * HEAD_DIM
    kv_col = q_col + aw
    gate_col = kv_col + kvw
    assert q_col % aw == 0 and kv_col % kvw == 0 and gate_col % LANES == 0
    rows_spec = lambda w: pl.BlockSpec((1, N_KV_HEADS, tm, w), lambda b, s: (b, 0, s, 0))
    rows_shape = lambda w, dt: jax.ShapeDtypeStruct((B, N_KV_HEADS, S, w), dt)
    cols_spec = pl.BlockSpec((1, N_KV_HEADS, VT_ROWS, tm), lambda b, s: (b, 0, 0, s))
    cols_shape = jax.ShapeDtypeStruct((B, N_KV_HEADS, VT_ROWS, S), _BF)
    return pl.pallas_call(
        _qkv_kernel,
        out_shape=(
            jax.ShapeDtypeStruct((T // Q_BLOCK, N_KV_HEADS, HEAD_DIM, gd), _BF),
            rows_shape(HEAD_DIM, _F32), rows_shape(HEAD_DIM, _F32),
            rows_shape(2 * HEAD_DIM, _BF), cols_shape,
            rows_shape(2 * HEAD_DIM, _BF), cols_shape,
            jax.ShapeDtypeStruct((N_KV_HEADS, GATE_ROWS, T), _F32),
        ),
        grid=(B, nsb),
        in_specs=[
            pl.BlockSpec((tm, D_MODEL), lambda b, s: (b * nsb + s, 0)),
            _resident((D_MODEL, aw), lambda b, s: (layer, q_col // aw)),
            _resident((D_MODEL, kvw), lambda b, s: (layer, kv_col // kvw)),
            _resident((D_MODEL, LANES), lambda b, s: (layer, gate_col // LANES)),
        ],
        out_specs=(
            pl.BlockSpec((tm // Q_BLOCK, N_KV_HEADS, HEAD_DIM, gd), lambda b, s: (b * nsb + s, 0, 0, 0)),
            rows_spec(HEAD_DIM), rows_spec(HEAD_DIM), rows_spec(2 * HEAD_DIM), cols_spec,
            rows_spec(2 * HEAD_DIM), cols_spec,
            pl.BlockSpec((N_KV_HEADS, GATE_ROWS, tm), lambda b, s: (0, 0, b * nsb + s)),
        ),
        scratch_shapes=[pltpu.VMEM((D_MODEL, aw), _BF), pltpu.VMEM((D_MODEL, kvw), _BF),
                        pltpu.VMEM((D_MODEL, LANES), _BF)],
        compiler_params=pltpu.CompilerParams(
            dimension_semantics=("arbitrary", "arbitrary"), vmem_limit_bytes=VMEM_LIMIT),
        name="qkv_proj",
    )(x2, w_in, w_in, w_in)


def _compress_kernel(kc_ref, vc_ref, pek_ref, pev_ref, w1k_ref, w1v_ref, w2k_ref, w2v_ref, ko_ref, vo_ref):
    nc = ko_ref.shape[1]

    def tokens(src, pe, w1, w2):
        first, second = None, None
        for p in range(0, CMP_STRIDE, 2):
            rows = [src[0, pl.ds(p + d, nc, stride=CMP_STRIDE), :] for d in range(2)]
            lo = jnp.concatenate([rows[d] + pe[p + d:p + d + 1, :] for d in range(2)], axis=1)
            hi = jnp.concatenate([rows[d] + pe[CMP_STRIDE + p + d:CMP_STRIDE + p + d + 1, :] for d in range(2)], axis=1)
            a = _dot(lo.astype(_BF), w1[p // 2])
            b = _dot(hi.astype(_BF), w1[(CMP_STRIDE + p) // 2])
            first = a if first is None else first + a
            second = b if second is None else second + b
        pre = first + pltpu.roll(second, nc - 1, 0)
        return _dot(_gelu(pre).astype(_BF), w2[...])

    ko_ref[0, :, :HEAD_DIM] = tokens(kc_ref, pek_ref, w1k_ref, w2k_ref).astype(_BF)
    start = lax.broadcasted_iota(jnp.int32, (nc, LANES), 0) * CMP_STRIDE
    ko_ref[0, :, HEAD_DIM:] = _position_lanes(start, (nc, LANES))
    vo_ref[0, :HEAD_DIM] = tokens(vc_ref, pev_ref, w1v_ref, w2v_ref).T.astype(_BF)
    mi = lax.broadcasted_iota(jnp.int32, (LANES, nc), 0)
    ni = lax.broadcasted_iota(jnp.int32, (LANES, nc), 1)
    vo_ref[0, HEAD_DIM:] = jnp.where((ni * CMP_STRIDE + (CMP_BLOCK - 1) >= mi * SEL_BLOCK)
                                     & (ni * CMP_STRIDE <= mi * SEL_BLOCK + (SEL_BLOCK - 1)), 1.0, 0.0).astype(_BF)


def _nsa_compress(kc, vc, pek, pev, w1k, w1v, w2k, w2v):
    BH, S, _ = kc.shape
    nc = S // CMP_STRIDE
    pair = 2 * HEAD_DIM
    const2 = lambda i: (0, 0)
    const3 = lambda i: (0, 0, 0)
    seq_rows = pl.BlockSpec((1, S, HEAD_DIM), lambda i: (i, 0, 0))
    pe_spec = pl.BlockSpec((CMP_BLOCK, HEAD_DIM), const2)
    w1_spec = pl.BlockSpec((CMP_BLOCK // 2, pair, HEAD_DIM), const3)
    w2_spec = pl.BlockSpec((HEAD_DIM, HEAD_DIM), const2)
    return pl.pallas_call(
        _compress_kernel,
        out_shape=(jax.ShapeDtypeStruct((BH, nc, 2 * HEAD_DIM), _BF),
                   jax.ShapeDtypeStruct((BH, HEAD_DIM + LANES, nc), _BF)),
        grid=(BH,),
        in_specs=[seq_rows, seq_rows, pe_spec, pe_spec, w1_spec, w1_spec, w2_spec, w2_spec],
        out_specs=(pl.BlockSpec((1, nc, 2 * HEAD_DIM), lambda i: (i, 0, 0)),
                   pl.BlockSpec((1, HEAD_DIM + LANES, nc), lambda i: (i, 0, 0))),
        compiler_params=pltpu.CompilerParams(
            dimension_semantics=("arbitrary",), vmem_limit_bytes=VMEM_LIMIT),
        name="nsa_compress",
    )(kc, vc, pek, pev, w1k, w1v, w2k, w2v)


def _mark_top_blocks(score, notsel, rounds):
    rows = lax.broadcasted_iota(jnp.int32, score.shape, 0).astype(_F32)
    for _ in range(rounds):
        mx = jnp.max(score, axis=0, keepdims=True)
        idx = jnp.min(jnp.where(score == mx, rows, float(LANES)), axis=0, keepdims=True)
        hit = rows == idx
        notsel = jnp.where(hit, 0.0, notsel)
        score = jnp.where(hit, -jnp.inf, score)
    return notsel


def _attn_kernel(qt_ref, gt_ref, kca_ref, vct_ref, ksa_ref, vst_ref, kwa_ref, vwt_ref, o_ref,
                 m_ref, acc_ref, lhs_ref, part_ref, gate_ref, bits_ref, *, seq, n_seq):
    step = pl.program_id(0)
    n_pair = seq // (Q_BLOCK * PAIR)
    last = n_seq * n_pair - 1
    front_pair = jnp.minimum(step, last)
    back_pair = jnp.clip(step - 1, 0, last)
    nc = kca_ref.shape[1]
    n_sel = seq // SEL_BLOCK
    n_forced = 1 + N_LOCAL_FORCED
    k_top = min(N_SELECT, n_sel)
    G = Q_PER_KV
    W = G * Q_BLOCK
    words = LANES // BITS_PER_WORD
    blocks_per_tile = SEL_TILE // SEL_BLOCK
    tiles_per_word = BITS_PER_WORD // blocks_per_tile
    put = step % 2
    get = 1 - put
    put3 = step % 3
    get3 = (step + 1) % 3
    units = range(PAIR)

    def flag_base(slot, u):
        return (slot * PAIR + u) * (words + 1)

    @pl.when(step == 0)
    def _():
        lhs_ref[1] = jnp.zeros(lhs_ref.shape[1:], _BF)
        part_ref[...] = jnp.zeros(part_ref.shape, _F32)
        gate_ref[...] = jnp.zeros(gate_ref.shape, _F32)
        acc_ref[...] = jnp.ones(acc_ref.shape, _F32)
        for k in range(PAIR * (words + 1)):
            bits_ref[flag_base(1, 0) + k] = 0

    lane_w = lax.broadcasted_iota(jnp.int32, (1, W), 1)

    def head_slopes(pair_index):
        h = (pair_index // n_pair) % N_KV_HEADS
        slope = jnp.zeros((1, W), _F32)
        for g in range(G):
            sg = jnp.where(h == 0, _F32(2.0 ** -(g + 1)), _F32(2.0 ** -(G + g + 1)))
            slope = jnp.where(lane_w // Q_BLOCK == g, sg, slope)
        return (slope * LOG2_E).astype(_BF).astype(_F32)

    slope = head_slopes(front_pair)
    slope_b = head_slopes(back_pair)
    strips =[slice(g * Q_BLOCK, (g + 1) * Q_BLOCK) for g in range(G)]
    row_aug = lax.broadcasted_iota(jnp.int32, (LANES, W), 0)
    key_row = lax.broadcasted_iota(jnp.int32, (SEL_TILE, Q_BLOCK), 0)
    pos_rows = jnp.where(row_aug == 0, slope, jnp.where(row_aug == 1, slope * POS_RADIX, 0.0)).astype(_BF)

    for u in units:
        out_t = (part_ref[get3, u] + acc_ref[u, 0:HEAD_DIM]
                 * (gate_ref[get3, u][0:1] * (1.0 / acc_ref[u, HEAD_DIM:HEAD_DIM + 1])))
        for g, cs in enumerate(strips):
            o_ref[u * Q_BLOCK:(u + 1) * Q_BLOCK, g * HEAD_DIM:(g + 1) * HEAD_DIM] = out_t[:, cs].T.astype(_BF)

    front = [{} for _ in units]
    for u, f in zip(units, front):
        f["q0"] = (front_pair % n_pair * PAIR + u) * Q_BLOCK
        f["tok"] = f["q0"] + lane_w % Q_BLOCK
        f["qt"] = qt_ref[u, 0]
        f["s"] = _dot(kca_ref[0], jnp.concatenate([f["qt"], pos_rows], axis=0))

    def tile_scores(bk, j, causal, live=None):
        k0 = j * SEL_TILE if isinstance(j, int) else pl.multiple_of(j * SEL_TILE, SEL_TILE)
        st = _dot(ksa_ref[0, 0, pl.ds(k0, SEL_TILE), :], bk["lhs"])
        if causal:
            ahead = jnp.where(k0 + key_row > bk["tok"][:, :Q_BLOCK], NEG_INF, 0.0)
            st = jnp.concatenate([st[:, cs] + ahead for cs in strips], axis=1)
        shift = slope_b * (bk["p0"] - k0).astype(_F32)
        if live is not None:
            shift = jnp.where(live, shift, -NEG_INF)
        return st, shift, k0

    def tile_max(tiles):
        m = None
        for st, shift, _ in tiles:
            cm = jnp.max(st, axis=0, keepdims=True) - shift
            m = cm if m is None else jnp.maximum(m, cm)
        return m

    def tile_sums(tiles, m, paired=()):
        weights = []
        for st, shift, _ in tiles:
            sub = m + shift
            weights.append(jnp.concatenate([jnp.exp2(st[:, cs] - sub[:, cs]) for cs in strips], axis=1).astype(_BF))
        acc, t = None, 0
        while t < len(tiles):
            if t in paired:
                k0 = pl.multiple_of(jnp.maximum(tiles[t + 1][2] - SEL_TILE, 0), SEL_TILE)
                ai = _dot(vst_ref[0, 0, :, pl.ds(k0, 2 * SEL_TILE)], jnp.concatenate(weights[t:t + 2], axis=0))
                t += 2
            else:
                ai = _dot(vst_ref[0, 0, :, pl.ds(tiles[t][2], SEL_TILE)], weights[t])
                t += 1
            acc = ai if acc is None else acc + ai
        return acc

    back = [{} for _ in units]
    for u, bk in zip(units, back):
        bk["p0"] = (back_pair % n_pair * PAIR + u) * Q_BLOCK
        bk["tok"] = bk["p0"] + lane_w % Q_BLOCK
        bk["lhs"] = lhs_ref[get, u]
        bk["first_local"] = bk["p0"] // SEL_TILE - (LOCAL_TILES - 1)
        bk["tiles"] = [tile_scores(bk, 0, True)]
        for i in range(LOCAL_TILES):
            j = bk["first_local"] + i
            bk["tiles"].append(tile_scores(bk, jnp.maximum(j, 0), i == LOCAL_TILES - 1, live=j >= 1))

    n_row = lax.broadcasted_iota(jnp.int32, (nc, Q_BLOCK), 0)
    for f in front:
        n_last = (f["tok"] - (CMP_BLOCK - 1)) // CMP_STRIDE
        mask_c = jnp.where(n_row <= n_last[:, :Q_BLOCK], 0.0, NEG_INF)
        e_parts, inv_parts = [], []
        for cs in strips:
            sg = f["s"][:, cs] + mask_c
            eg = jnp.exp2(sg - jnp.max(sg, axis=0, keepdims=True))
            e_parts.append(eg.astype(_BF))
            inv_parts.append(jnp.where(n_last[:, cs] >= 0, 1.0 / jnp.sum(eg, axis=0, keepdims=True), 0.0))
        both = _dot(vct_ref[0], jnp.concatenate(e_parts, axis=1)) * jnp.concatenate(inv_parts, axis=1)
        f["o_cmp"] = both[:HEAD_DIM]
        imp_t = both[HEAD_DIM:, strips[0]]
        for cs in strips[1:]:
            imp_t = imp_t + both[HEAD_DIM:, cs]
        f["imp_t"] = imp_t

    wlen = WINDOW + Q_BLOCK
    for f in front:
        anchor = jnp.concatenate([jnp.minimum(f["imp_t"][0:1], 0.0)] * G, axis=1)
        win_rows = jnp.where(row_aug == 0, slope + anchor,
                             jnp.where(row_aug == 1, slope * POS_RADIX, 0.0)).astype(_BF)
        f["w0"] = pl.multiple_of(jnp.maximum(f["q0"] - WINDOW, 0), Q_BLOCK)
        f["sw"] = _dot(kwa_ref[0, 0, pl.ds(f["w0"], wlen), :], jnp.concatenate([f["qt"], win_rows], axis=0))

    for u, bk in zip(units, back):
        m = tile_max(bk["tiles"])
        m_ref[u, 0:1] = m
        acc_ref[u] = tile_sums(bk["tiles"], m, paired=range(1, LOCAL_TILES, 2))

    for f in front:
        w_row = f["w0"] + lax.broadcasted_iota(jnp.int32, (wlen, Q_BLOCK), 0)
        dist = f["tok"][:, :Q_BLOCK] - w_row
        mask_w = jnp.where(dist.astype(jnp.uint32) < WINDOW, 0.0, NEG_INF)
        e_parts = []
        for cs in strips:
            sg = f["sw"][:, cs] + mask_w
            e_parts.append(jnp.exp2(sg - jnp.max(sg, axis=0, keepdims=True)).astype(_BF))
        win = _dot(vwt_ref[0, 0, :, pl.ds(f["w0"], wlen)], jnp.concatenate(e_parts, axis=1))
        f["o_win"] = win[:HEAD_DIM] * (1.0 / win[HEAD_DIM:HEAD_DIM + 1])

    mrow = lax.broadcasted_iota(jnp.int32, (LANES, Q_BLOCK), 0)
    blk_row = lax.broadcasted_iota(jnp.int32, (LANES, 1), 0)
    blk_bit = jnp.left_shift(1, blk_row % BITS_PER_WORD).astype(_F32)
    for u, f in zip(units, front):
        gates = gt_ref[0][:, u * Q_BLOCK:(u + 1) * Q_BLOCK]

        def gate_row(branch):
            return jnp.concatenate([gates[3 * g + branch:3 * g + branch + 1] for g in range(G)], axis=1)

        tcol = lax.broadcasted_iota(jnp.int32, (LANES, Q_BLOCK), 1) + f["q0"]
        lag = tcol // SEL_BLOCK - mrow
        forced = (mrow == 0) | ((lag >= 0) & (lag < N_LOCAL_FORCED))
        score = jnp.where(forced | (lag < 0) | (mrow >= n_sel), -jnp.inf, f["imp_t"])
        notsel_t = _mark_top_blocks(score, jnp.where(forced, 0.0, 1.0), k_top - n_forced)

        sel_rows = jnp.where(row_aug == 0, slope, jnp.concatenate([notsel_t] * G, axis=1)).astype(_BF)
        lhs_ref[put, u] = jnp.concatenate([f["qt"], sel_rows], axis=0)
        part_ref[put3, u] = gate_row(0) * f["o_cmp"] + gate_row(2) * f["o_win"]
        gate_ref[put3, u] = jnp.broadcast_to(gate_row(1), gate_ref.shape[2:])

        far_end = (f["q0"] // SEL_TILE - (LOCAL_TILES - 1)) * blocks_per_tile
        blk_on = jnp.where((jnp.min(notsel_t, axis=1, keepdims=True) < 0.5)
                           & (blk_row >= blocks_per_tile) & (blk_row < far_end), 1.0, 0.0)
        for k in range(words):
            word = jnp.sum((blk_on * blk_bit)[k * BITS_PER_WORD:(k + 1) * BITS_PER_WORD])
            bits_ref[flag_base(put, u) + k] = word.astype(jnp.int32)
        bits_ref[flag_base(put, u) + words] = jnp.sum(blk_on).astype(jnp.int32)

    for u, bk in zip(units, back):
        def far_tile(j, carry, u=u, bk=bk):
            word = bits_ref[flag_base(get, u) + j // tiles_per_word]
            tile_bits = (word >> ((j % tiles_per_word) * blocks_per_tile)) & ((1 << blocks_per_tile) - 1)

            @pl.when(tile_bits != 0)
            def _():
                tile = [tile_scores(bk, j, False)]
                m_old = m_ref[u, 0:1]
                m_new = jnp.maximum(m_old, tile_max(tile))
                m_ref[u, 0:1] = m_new
                acc_ref[u] = jnp.exp2(m_old - m_new) * acc_ref[u] + tile_sums(tile, m_new)
            return carry

        @pl.when(bits_ref[flag_base(get, u) + words] != 0)
        def _(far_tile=far_tile, bk=bk):
            lax.fori_loop(1, bk["first_local"], far_tile, 0)


def _nsa_attention(qt, gt, kca, vct, ksa, vst, kwa, vwt, B, S):
    n_pair = S // (Q_BLOCK * PAIR)
    T = B * S
    nc = kca.shape[1]
    gd = Q_PER_KV * HEAD_DIM
    W = Q_PER_KV * Q_BLOCK
    n_seq = B * N_KV_HEADS
    last = n_seq * n_pair - 1
    front = lambda i: jnp.minimum(i, last)
    back = lambda i: jnp.clip(i - 1, 0, last)
    done = lambda i: jnp.maximum(i - 2, 0)
    seq_of = lambda p: p // n_pair
    blk_of = lambda p: (seq_of(p) // N_KV_HEADS) * n_pair + p % n_pair
    head_of = lambda p: seq_of(p) % N_KV_HEADS
    rows = lambda stage: pl.BlockSpec((1, 1, S, 2 * HEAD_DIM),
                                      lambda i: (seq_of(stage(i)) // N_KV_HEADS, head_of(stage(i)), 0, 0))
    cols = lambda stage: pl.BlockSpec((1, 1, VT_ROWS, S),
                                      lambda i: (seq_of(stage(i)) // N_KV_HEADS, head_of(stage(i)), 0, 0))
    return pl.pallas_call(
        functools.partial(_attn_kernel, seq=S, n_seq=n_seq),
        out_shape=jax.ShapeDtypeStruct((T, N_KV_HEADS * gd), _BF),
        grid=(last + 3,),
        in_specs=[
            pl.BlockSpec((PAIR, 1, HEAD_DIM, W), lambda i: (blk_of(front(i)), head_of(front(i)), 0, 0)),
            pl.BlockSpec((1, GATE_ROWS, PAIR * Q_BLOCK), lambda i: (head_of(front(i)), 0, blk_of(front(i)))),
            pl.BlockSpec((1, nc, 2 * HEAD_DIM), lambda i: (seq_of(front(i)), 0, 0)),
            pl.BlockSpec((1, HEAD_DIM + LANES, nc), lambda i: (seq_of(front(i)), 0, 0)),
            rows(back), cols(back), rows(front), cols(front),
        ],
        out_specs=pl.BlockSpec((PAIR * Q_BLOCK, gd), lambda i: (blk_of(done(i)), head_of(done(i)))),
        scratch_shapes=[pltpu.VMEM((PAIR, 8, W), _F32), pltpu.VMEM((PAIR, VT_ROWS, W), _F32),
                        pltpu.VMEM((2, PAIR, 2 * HEAD_DIM, W), _BF), pltpu.VMEM((3, PAIR, HEAD_DIM, W), _F32),
                        pltpu.VMEM((3, PAIR, 8, W), _F32),
                        pltpu.SMEM((2 * PAIR * (LANES // BITS_PER_WORD + 1),), jnp.int32)],
        compiler_params=pltpu.CompilerParams(
            dimension_semantics=("arbitrary",), vmem_limit_bytes=VMEM_LIMIT),
        name="nsa_attention",
    )(qt, gt, kca, vct, ksa, vst, kwa, vwt)


def _merge_ffn_kernel(x_ref, gm_ref, yn_ref, wm1_ref, wpn_ref, wo_ref, g1_ref, b1_ref,
                      w1_ref, w2_ref, g2_ref, b2_ref, o_ref):
    x = x_ref[...]
    gate = _sigmoid(_dot(x.astype(_BF), wm1_ref[...]))
    merged = gm_ref[...] + gate * _dot(yn_ref[...], wpn_ref[...])
    mix = _dot(merged.astype(_BF), wo_ref[...])
    hid = _layer_norm(DEEPNORM_ALPHA * x + mix, g1_ref[...], b1_ref[...])
    hb = hid.astype(_BF)
    f = jnp.zeros(hid.shape, _F32)
    for c in range(D_FF // D_MODEL):
        a = jnp.maximum(_dot(hb, w1_ref[:, c * D_MODEL:(c + 1) * D_MODEL]), 0.0)
        f = f + _dot((a * a).astype(_BF), w2_ref[c * D_MODEL:(c + 1) * D_MODEL, :])
    o_ref[...] = _layer_norm(DEEPNORM_ALPHA * hid + f, g2_ref[...], b2_ref[...])


def _merge_ffn(x2, gm, yn, wm1, wpn, wo, g1, b1, w1, w2, g2, b2):
    T = x2.shape[0]
    tm = ROW_TILE
    rows = lambda w: pl.BlockSpec((tm, w), lambda i: (i, 0))
    const = lambda r, c: _resident((r, c), lambda i: (0, 0))
    return pl.pallas_call(
        _merge_ffn_kernel,
        out_shape=jax.ShapeDtypeStruct((T, D_MODEL), _F32),
        grid=(T // tm,),
        in_specs=[rows(D_MODEL), rows(D_MODEL), rows(D_MODEL),
                  const(D_MODEL, D_MODEL), const(D_MODEL, D_MODEL), const(D_MODEL, D_MODEL),
                  const(1, D_MODEL), const(1, D_MODEL),
                  const(D_MODEL, D_FF), const(D_FF, D_MODEL),
                  const(1, D_MODEL), const(1, D_MODEL)],
        out_specs=rows(D_MODEL),
        compiler_params=pltpu.CompilerParams(
            dimension_semantics=("arbitrary",), vmem_limit_bytes=VMEM_LIMIT),
        name="merge_ffn",
    )(x2, gm, yn, wm1, wpn, wo, g1, b1, w1, w2, g2, b2)


def _layer(x2, B, S, w_in, layer, gm_ln_g, gm_ln_b, gm_w_s, gm_b_s, cmp_pe_k, cmp_w1_k, cmp_w2_k,
           cmp_pe_v, cmp_w1_v, cmp_w2_v, w_proj_gm, w_proj_nsa, w_out,
           ln1_g, ln1_b, w_ff1, w_ff2, ln2_g, ln2_b):
    o_q = 2 * D_MODEL
    o_m = o_q + (Q_PER_KV + 6) * N_KV_HEADS * HEAD_DIM + 3 * Q_PER_KV * N_KV_HEADS
    row = lambda v: v.reshape(1, -1)
    merge_cols = lambda j: w_in[layer * D_MODEL:(layer + 1) * D_MODEL, o_m + j * D_MODEL:o_m + (j + 1) * D_MODEL]

    gm = _gm_mixer(x2, w_in, layer, merge_cols(0), row(gm_ln_g), row(gm_ln_b), gm_w_s,
                   jnp.broadcast_to(gm_b_s[:, :, None], (GM_GROUPS, GM_CHUNK, LANES)), w_proj_gm)

    qt, kc, vc, ksa, vst, kwa, vwt, gt = _qkv_proj(x2, w_in, layer, o_q, B, S)

    heads = lambda a: a.reshape(B * N_KV_HEADS, S, HEAD_DIM)
    pairs = lambda w: w.astype(_BF).reshape(CMP_BLOCK // 2, 2 * HEAD_DIM, HEAD_DIM)
    kca, vct = _nsa_compress(heads(kc), heads(vc), cmp_pe_k, cmp_pe_v, pairs(cmp_w1_k), pairs(cmp_w1_v),
                             cmp_w2_k.astype(_BF), cmp_w2_v.astype(_BF))

    yn = _nsa_attention(qt, gt, kca, vct, ksa, vst, kwa, vwt, B, S)

    return _merge_ffn(x2, gm, yn, merge_cols(1).astype(_BF), w_proj_nsa.astype(_BF),
                      w_out.astype(_BF), row(ln1_g), row(ln1_b), w_ff1.astype(_BF), w_ff2.astype(_BF),
                      row(ln2_g), row(ln2_b))


def kernel(x, w_in, gm_ln_g, gm_ln_b, gm_w_s, gm_b_s, cmp_pe_k, cmp_w1_k, cmp_w2_k, cmp_pe_v, cmp_w1_v, cmp_w2_v, w_proj_gm, w_proj_nsa, w_out, ln1_g, ln1_b, w_ff1, w_ff2, ln2_g, ln2_b):
    B, S, D = x.shape
    assert D == D_MODEL and S % ROW_TILE == 0 and WINDOW + Q_BLOCK <= S <= SEL_BLOCK * LANES
    assert S // POS_RADIX <= POS_RADIX and S % (PAIR * Q_BLOCK) == 0
    h = x.reshape(B * S, D)
    depth, _, d_in = w_in.shape
    w_rows = w_in.reshape(depth * D, d_in)
    for l in range(depth):
        h = _layer(h, B, S, w_rows, l, gm_ln_g[l], gm_ln_b[l], gm_w_s[l], gm_b_s[l],
                   cmp_pe_k[l], cmp_w1_k[l], cmp_w2_k[l], cmp_pe_v[l], cmp_w1_v[l], cmp_w2_v[l],
                   w_proj_gm[l], w_proj_nsa[l], w_out[l], ln1_g[l], ln1_b[l],
                   w_ff1[l], w_ff2[l], ln2_g[l], ln2_b[l])
    return h.reshape(B, S, D)
```

```python
import functools
import math

import jax
import jax.numpy as jnp
from jax import lax
from jax.experimental import pallas as pl
from jax.experimental.pallas import tpu as pltpu

D_MODEL = 1024
GM_GROUPS = 8
GM_CHUNK = 128
N_KV_HEADS = 2
Q_PER_KV = 4
HEAD_DIM = 128
CMP_BLOCK = 32
CMP_STRIDE = 16
SEL_BLOCK = 64
N_SELECT = 16
N_LOCAL_FORCED = 2
WINDOW = 512
Q_BLOCK = 128
D_FF = 4 * D_MODEL
DEEPNORM_ALPHA = 2.0 ** 0.25
LN_EPS = 1e-5
NEG_INF = -1e30
LOG2_E = math.log2(math.e)

LANES = 128
SEL_TILE = 128
LOCAL_TILES = 11
PAIR = 2
PHASE_ORDER = (("finish", 0), ("finish", 1), ("front_scores", 0), ("front_scores", 1),
               ("back_scores", 0), ("front_compressed", 0), ("front_compressed", 1), ("back_scores", 1),
               ("front_window_scores", 0), ("front_window_scores", 1), ("back_sums", 0), ("back_sums", 1),
               ("front_window_sums", 0), ("front_window_sums", 1), ("front_select", 0), ("front_select", 1))
BITS_PER_WORD = 16
POS_RADIX = 256
ROW_TILE = 512
VT_ROWS = HEAD_DIM + 16
GATE_ROWS = 16
VMEM_LIMIT = 56 * 1024 * 1024

_BF = jnp.bfloat16
_F32 = jnp.float32


def _dot(a, b):
    return jnp.dot(a, b, preferred_element_type=_F32)


def _gelu(x):
    c = math.sqrt(2.0 / math.pi)
    return 0.5 * x * (1.0 + jnp.tanh(c * (x + 0.044715 * (x * x * x))))


def _sigmoid(x):
    return 1.0 / (1.0 + jnp.exp(-x))


def _layer_norm(x, g, b):
    mu = jnp.mean(x, axis=-1, keepdims=True)
    xc = x - mu
    var = jnp.mean(xc * xc, axis=-1, keepdims=True)
    return xc * lax.rsqrt(var + LN_EPS) * g + b


def _position_lanes(pos, shape):
    lane = lax.broadcasted_iota(jnp.int32, shape, 1)
    return jnp.where(lane == 0, (pos % POS_RADIX).astype(_F32),
                     jnp.where(lane == 1, (pos // POS_RADIX).astype(_F32), 0.0)).astype(_BF)


def _stage_bf16(first_step, pairs):
    @pl.when(first_step)
    def _():
        for src, dst in pairs:
            dst[...] = src[...].astype(_BF)


def _resident(shape, index_map):
    return pl.BlockSpec(shape, index_map, pipeline_mode=pl.Buffered(1))


def _gm_kernel(x_ref, wgm32_ref, wm032_ref, lng_ref, lnb_ref, ws_ref, bs_ref, wpg32_ref, o_ref,
               vg_ref, wgm_ref, wm0_ref, wpg_ref):
    _stage_bf16(pl.program_id(0) == 0, [(wgm32_ref, wgm_ref), (wm032_ref, wm0_ref), (wpg32_ref, wpg_ref)])
    tm = x_ref.shape[0]
    xb = x_ref[...].astype(_BF)
    z = _gelu(_dot(xb, wgm_ref[...]))
    gate_logits = _dot(xb, wm0_ref[...])
    u = z[:, :D_MODEL]
    v = _layer_norm(z[:, D_MODEL:], lng_ref[...], lnb_ref[...]).astype(_BF)
    row = lax.broadcasted_iota(jnp.int32, (GM_CHUNK, GM_CHUNK), 0)
    col = lax.broadcasted_iota(jnp.int32, (GM_CHUNK, GM_CHUNK), 1)
    for gi in range(GM_GROUPS):
        w = jnp.where(row >= col, ws_ref[gi], 0.0).astype(_BF)
        for c in range(tm // GM_CHUNK):
            blk = v[c * GM_CHUNK:(c + 1) * GM_CHUNK, gi * LANES:(gi + 1) * LANES]
            vg_ref[c * GM_CHUNK:(c + 1) * GM_CHUNK, gi * LANES:(gi + 1) * LANES] = _dot(w, blk) + bs_ref[gi]
    y = (u * vg_ref[...]).astype(_BF)
    o_ref[...] = _sigmoid(gate_logits) * _dot(y, wpg_ref[...])


def _gm_mixer(x2, w_in, layer, wm0, lng, lnb, ws, bs, wpg):
    T = x2.shape[0]
    tm = ROW_TILE
    const2 = lambda i: (0, 0)
    const3 = lambda i: (0, 0, 0)
    return pl.pallas_call(
        _gm_kernel,
        out_shape=jax.ShapeDtypeStruct((T, D_MODEL), _F32),
        grid=(T // tm,),
        in_specs=[
            pl.BlockSpec((tm, D_MODEL), lambda i: (i, 0)),
            _resident((D_MODEL, 2 * D_MODEL), lambda i: (layer, 0)),
            _resident((D_MODEL, D_MODEL), const2),
            pl.BlockSpec((1, D_MODEL), const2),
            pl.BlockSpec((1, D_MODEL), const2),
            pl.BlockSpec((GM_GROUPS, GM_CHUNK, GM_CHUNK), const3),
            pl.BlockSpec((GM_GROUPS, GM_CHUNK, LANES), const3),
            _resident((D_MODEL, D_MODEL), const2),
        ],
        out_specs=pl.BlockSpec((tm, D_MODEL), lambda i: (i, 0)),
        scratch_shapes=[pltpu.VMEM((tm, D_MODEL), _F32), pltpu.VMEM((D_MODEL, 2 * D_MODEL), _BF),
                        pltpu.VMEM((D_MODEL, D_MODEL), _BF), pltpu.VMEM((D_MODEL, D_MODEL), _BF)],
        compiler_params=pltpu.CompilerParams(
            dimension_semantics=("arbitrary",), vmem_limit_bytes=VMEM_LIMIT),
        name="gm_mixer",
    )(x2, w_in, wm0, lng, lnb, ws, bs, wpg)


def _qkv_kernel(x_ref, wq32_ref, wkv32_ref, wg32_ref, qt_ref, kc_ref, vc_ref, ksa_ref, vst_ref, kwa_ref, vwt_ref,
                gt_ref, wq_ref, wkv_ref, wg_ref):
    _stage_bf16((pl.program_id(0) == 0) & (pl.program_id(1) == 0),
                [(wq32_ref, wq_ref), (wkv32_ref, wkv_ref), (wg32_ref, wg_ref)])
    tm = x_ref.shape[0]
    xb = x_ref[...].astype(_BF)
    zq = _dot(xb, wq_ref[...]) * (HEAD_DIM ** -0.5 * LOG2_E)
    for tb in range(tm // Q_BLOCK):
        for h in range(N_KV_HEADS):
            for g in range(Q_PER_KV):
                c0 = (h * Q_PER_KV + g) * HEAD_DIM
                blk = zq[tb * Q_BLOCK:(tb + 1) * Q_BLOCK, c0:c0 + HEAD_DIM]
                qt_ref[tb, h, :, g * Q_BLOCK:(g + 1) * Q_BLOCK] = blk.T.astype(_BF)
    z = _dot(xb, wkv_ref[...])
    kpos = pl.program_id(1) * tm + lax.broadcasted_iota(jnp.int32, (tm, LANES), 0)
    blk_lane = lax.broadcasted_iota(jnp.int32, (tm, LANES), 1)
    sel_lanes = jnp.where(blk_lane == 0, (kpos % SEL_TILE).astype(_F32),
                          jnp.where(kpos // SEL_BLOCK == blk_lane, NEG_INF, 0.0)).astype(_BF)
    win_lanes = _position_lanes(kpos, (tm, LANES))
    ones = jnp.ones((VT_ROWS - HEAD_DIM, tm), _BF)
    for h in range(N_KV_HEADS):
        def col(j):
            return z[:, j * 2 * HEAD_DIM + h * HEAD_DIM: j * 2 * HEAD_DIM + (h + 1) * HEAD_DIM]
        kc_ref[0, h] = col(0)
        vc_ref[0, h] = col(1)
        ksa_ref[0, h, :, :HEAD_DIM] = col(2).astype(_BF)
        ksa_ref[0, h, :, HEAD_DIM:] = sel_lanes
        vst_ref[0, h, :HEAD_DIM] = col(3).T.astype(_BF)
        vst_ref[0, h, HEAD_DIM:] = ones
        kwa_ref[0, h, :, :HEAD_DIM] = col(4).astype(_BF)
        kwa_ref[0, h, :, HEAD_DIM:] = win_lanes
        vwt_ref[0, h, :HEAD_DIM] = col(5).T.astype(_BF)
        vwt_ref[0, h, HEAD_DIM:] = ones
    zg_t = _sigmoid(_dot(xb, wg_ref[...])).T
    per_head = 3 * Q_PER_KV
    for h in range(N_KV_HEADS):
        gt_ref[h] = zg_t[h * per_head:h * per_head + GATE_ROWS]


def _qkv_proj(x2, w_in, layer, q_col, B, S):
    T = x2.shape[0]
    tm = ROW_TILE
    nsb = S // tm
    gd = Q_PER_KV * Q_BLOCK
    aw = N_KV_HEADS * Q_PER_KV * HEAD_DIM
    kvw = 6 * N_KV_HEADS * HEAD_DIM
    kv_col = q_col + aw
    gate_col = kv_col + kvw
    assert q_col % aw == 0 and kv_col % kvw == 0 and gate_col % LANES == 0
    rows_spec = lambda w: pl.BlockSpec((1, N_KV_HEADS, tm, w), lambda b, s: (b, 0, s, 0))
    rows_shape = lambda w, dt: jax.ShapeDtypeStruct((B, N_KV_HEADS, S, w), dt)
    cols_spec = pl.BlockSpec((1, N_KV_HEADS, VT_ROWS, tm), lambda b, s: (b, 0, 0, s))
    cols_shape = jax.ShapeDtypeStruct((B, N_KV_HEADS, VT_ROWS, S), _BF)
    return pl.pallas_call(
        _qkv_kernel,
        out_shape=(
            jax.ShapeDtypeStruct((T // Q_BLOCK, N_KV_HEADS, HEAD_DIM, gd), _BF),
            rows_shape(HEAD_DIM, _F32), rows_shape(HEAD_DIM, _F32),
            rows_shape(2 * HEAD_DIM, _BF), cols_shape,
            rows_shape(2 * HEAD_DIM, _BF), cols_shape,
            jax.ShapeDtypeStruct((N_KV_HEADS, GATE_ROWS, T), _F32),
        ),
        grid=(B, nsb),
        in_specs=[
            pl.BlockSpec((tm, D_MODEL), lambda b, s: (b * nsb + s, 0)),
            _resident((D_MODEL, aw), lambda b, s: (layer, q_col // aw)),
            _resident((D_MODEL, kvw), lambda b, s: (layer, kv_col // kvw)),
            _resident((D_MODEL, LANES), lambda b, s: (layer, gate_col // LANES)),
        ],
        out_specs=(
            pl.BlockSpec((tm // Q_BLOCK, N_KV_HEADS, HEAD_DIM, gd), lambda b, s: (b * nsb + s, 0, 0, 0)),
            rows_spec(HEAD_DIM), rows_spec(HEAD_DIM), rows_spec(2 * HEAD_DIM), cols_spec,
            rows_spec(2 * HEAD_DIM), cols_spec,
            pl.BlockSpec((N_KV_HEADS, GATE_ROWS, tm), lambda b, s: (0, 0, b * nsb + s)),
        ),
        scratch_shapes=[pltpu.VMEM((D_MODEL, aw), _BF), pltpu.VMEM((D_MODEL, kvw), _BF),
                        pltpu.VMEM((D_MODEL, LANES), _BF)],
        compiler_params=pltpu.CompilerParams(
            dimension_semantics=("arbitrary", "arbitrary"), vmem_limit_bytes=VMEM_LIMIT),
        name="qkv_proj",
    )(x2, w_in, w_in, w_in)


def _compress_kernel(kc_ref, vc_ref, pek_ref, pev_ref, w1k_ref, w1v_ref, w2k_ref, w2v_ref, ko_ref, vo_ref):
    nc = ko_ref.shape[1]

    def tokens(src, pe, w1, w2):
        first, second = None, None
        for p in range(0, CMP_STRIDE, 2):
            rows = [src[0, pl.ds(p + d, nc, stride=CMP_STRIDE), :] for d in range(2)]
            lo = jnp.concatenate([rows[d] + pe[p + d:p + d + 1, :] for d in range(2)], axis=1)
            hi = jnp.concatenate([rows[d] + pe[CMP_STRIDE + p + d:CMP_STRIDE + p + d + 1, :] for d in range(2)], axis=1)
            a = _dot(lo.astype(_BF), w1[p // 2])
            b = _dot(hi.astype(_BF), w1[(CMP_STRIDE + p) // 2])
            first = a if first is None else first + a
            second = b if second is None else second + b
        pre = first + pltpu.roll(second, nc - 1, 0)
        return _dot(_gelu(pre).astype(_BF), w2[...])

    ko_ref[0, :, :HEAD_DIM] = tokens(kc_ref, pek_ref, w1k_ref, w2k_ref).astype(_BF)
    start = lax.broadcasted_iota(jnp.int32, (nc, LANES), 0) * CMP_STRIDE
    ko_ref[0, :, HEAD_DIM:] = _position_lanes(start, (nc, LANES))
    vo_ref[0, :HEAD_DIM] = tokens(vc_ref, pev_ref, w1v_ref, w2v_ref).T.astype(_BF)
    mi = lax.broadcasted_iota(jnp.int32, (LANES, nc), 0)
    ni = lax.broadcasted_iota(jnp.int32, (LANES, nc), 1)
    vo_ref[0, HEAD_DIM:] = jnp.where((ni * CMP_STRIDE + (CMP_BLOCK - 1) >= mi * SEL_BLOCK)
                                     & (ni * CMP_STRIDE <= mi * SEL_BLOCK + (SEL_BLOCK - 1)), 1.0, 0.0).astype(_BF)


def _nsa_compress(kc, vc, pek, pev, w1k, w1v, w2k, w2v):
    BH, S, _ = kc.shape
    nc = S // CMP_STRIDE
    pair = 2 * HEAD_DIM
    const2 = lambda i: (0, 0)
    const3 = lambda i: (0, 0, 0)
    seq_rows = pl.BlockSpec((1, S, HEAD_DIM), lambda i: (i, 0, 0))
    pe_spec = pl.BlockSpec((CMP_BLOCK, HEAD_DIM), const2)
    w1_spec = pl.BlockSpec((CMP_BLOCK // 2, pair, HEAD_DIM), const3)
    w2_spec = pl.BlockSpec((HEAD_DIM, HEAD_DIM), const2)
    return pl.pallas_call(
        _compress_kernel,
        out_shape=(jax.ShapeDtypeStruct((BH, nc, 2 * HEAD_DIM), _BF),
                   jax.ShapeDtypeStruct((BH, HEAD_DIM + LANES, nc), _BF)),
        grid=(BH,),
        in_specs=[seq_rows, seq_rows, pe_spec, pe_spec, w1_spec, w1_spec, w2_spec, w2_spec],
        out_specs=(pl.BlockSpec((1, nc, 2 * HEAD_DIM), lambda i: (i, 0, 0)),
                   pl.BlockSpec((1, HEAD_DIM + LANES, nc), lambda i: (i, 0, 0))),
        compiler_params=pltpu.CompilerParams(
            dimension_semantics=("arbitrary",), vmem_limit_bytes=VMEM_LIMIT),
        name="nsa_compress",
    )(kc, vc, pek, pev, w1k, w1v, w2k, w2v)


def _mark_top_blocks(score, notsel, rounds):
    rows = lax.broadcasted_iota(jnp.int32, score.shape, 0).astype(_F32)
    for _ in range(rounds):
        mx = jnp.max(score, axis=0, keepdims=True)
        idx = jnp.min(jnp.where(score == mx, rows, float(LANES)), axis=0, keepdims=True)
        hit = rows == idx
        notsel = jnp.where(hit, 0.0, notsel)
        score = jnp.where(hit, -jnp.inf, score)
    return notsel


def _attn_kernel(qt_ref, gt_ref, kca_ref, vct_ref, ksa_ref, vst_ref, kwa_ref, vwt_ref, o_ref,
                 m_ref, acc_ref, lhs_ref, part_ref, gate_ref, bits_ref, *, seq, n_seq):
    step = pl.program_id(0)
    n_pair = seq // (Q_BLOCK * PAIR)
    last = n_seq * n_pair - 1
    front_pair = jnp.minimum(step, last)
    back_pair = jnp.clip(step - 1, 0, last)
    nc = kca_ref.shape[1]
    n_sel = seq // SEL_BLOCK
    n_forced = 1 + N_LOCAL_FORCED
    k_top = min(N_SELECT, n_sel)
    G = Q_PER_KV
    W = G * Q_BLOCK
    words = LANES // BITS_PER_WORD
    blocks_per_tile = SEL_TILE // SEL_BLOCK
    tiles_per_word = BITS_PER_WORD // blocks_per_tile
    put = step % 2
    get = 1 - put
    put3 = step % 3
    get3 = (step + 1) % 3
    units = range(PAIR)

    def flag_base(slot, u):
        return (slot * PAIR + u) * (words + 1)

    @pl.when(step == 0)
    def _():
        lhs_ref[1] = jnp.zeros(lhs_ref.shape[1:], _BF)
        part_ref[...] = jnp.zeros(part_ref.shape, _F32)
        gate_ref[...] = jnp.zeros(gate_ref.shape, _F32)
        acc_ref[...] = jnp.ones(acc_ref.shape, _F32)
        for k in range(PAIR * (words + 1)):
            bits_ref[flag_base(1, 0) + k] = 0

    lane_w = lax.broadcasted_iota(jnp.int32, (1, W), 1)

    def head_slopes(pair_index):
        h = (pair_index // n_pair) % N_KV_HEADS
        slope = jnp.zeros((1, W), _F32)
        for g in range(G):
            sg = jnp.where(h == 0, _F32(2.0 ** -(g + 1)), _F32(2.0 ** -(G + g + 1)))
            slope = jnp.where(lane_w // Q_BLOCK == g, sg, slope)
        return (slope * LOG2_E).astype(_BF).astype(_F32)

    slope = head_slopes(front_pair)
    slope_b = head_slopes(back_pair)
    strips =[slice(g * Q_BLOCK, (g + 1) * Q_BLOCK) for g in range(G)]
    row_aug = lax.broadcasted_iota(jnp.int32, (LANES, W), 0)
    key_row = lax.broadcasted_iota(jnp.int32, (SEL_TILE, Q_BLOCK), 0)
    pos_rows = jnp.where(row_aug == 0, slope, jnp.where(row_aug == 1, slope * POS_RADIX, 0.0)).astype(_BF)

    def finish(u):
        out_t = (part_ref[get3, u] + acc_ref[u, 0:HEAD_DIM]
                 * (gate_ref[get3, u][0:1] * (1.0 / acc_ref[u, HEAD_DIM:HEAD_DIM + 1])))
        for g, cs in enumerate(strips):
            o_ref[u * Q_BLOCK:(u + 1) * Q_BLOCK, g * HEAD_DIM:(g + 1) * HEAD_DIM] = out_t[:, cs].T.astype(_BF)

    front = [{} for _ in units]

    def front_scores(u):
        f = front[u]
        f["q0"] =(front_pair % n_pair * PAIR + u) * Q_BLOCK
        f["tok"] = f["q0"] + lane_w % Q_BLOCK
        f["qt"] = qt_ref[u, 0]
        f["s"] = _dot(kca_ref[0], jnp.concatenate([f["qt"], pos_rows], axis=0))

    def tile_scores(bk, j, causal, live=None):
        k0 = j * SEL_TILE if isinstance(j, int) else pl.multiple_of(j * SEL_TILE, SEL_TILE)
        st = _dot(ksa_ref[0, 0, pl.ds(k0, SEL_TILE), :], bk["lhs"])
        if causal:
            ahead = jnp.where(k0 + key_row > bk["tok"][:, :Q_BLOCK], NEG_INF, 0.0)
            st = jnp.concatenate([st[:, cs] + ahead for cs in strips], axis=1)
        shift = slope_b * (bk["p0"] - k0).astype(_F32)
        if live is not None:
            shift = jnp.where(live, shift, -NEG_INF)
        return st, shift, k0

    def tile_max(tiles):
        m = None
        for st, shift, _ in tiles:
            cm = jnp.max(st, axis=0, keepdims=True) - shift
            m = cm if m is None else jnp.maximum(m, cm)
        return m

    def tile_sums(tiles, m):
        acc = None
        for st, shift, k0 in tiles:
            sub = m + shift
            pt = jnp.concatenate([jnp.exp2(st[:, cs] - sub[:, cs]) for cs in strips], axis=1)
            ai = _dot(vst_ref[0, 0, :, pl.ds(k0, SEL_TILE)], pt.astype(_BF))
            acc = ai if acc is None else acc + ai
        return acc

    back = [{} for _ in units]

    def back_scores(u):
        bk = back[u]
        bk["p0"] =(back_pair % n_pair * PAIR + u) * Q_BLOCK
        bk["tok"] = bk["p0"] + lane_w % Q_BLOCK
        bk["lhs"] = lhs_ref[get, u]
        bk["first_local"] = bk["p0"] // SEL_TILE - (LOCAL_TILES - 1)
        bk["tiles"] = [tile_scores(bk, 0, False, live=bk["first_local"] > 0)]
        for i in range(LOCAL_TILES):
            j = bk["first_local"] + i
            last = i == LOCAL_TILES - 1
            bk["tiles"].append(tile_scores(bk, jnp.maximum(j, 0), last, live=None if last else j >= 0))

    n_row = lax.broadcasted_iota(jnp.int32, (nc, Q_BLOCK), 0)

    def front_compressed(u):
        f = front[u]
        n_last =(f["tok"] - (CMP_BLOCK - 1)) // CMP_STRIDE
        mask_c = jnp.where(n_row <= n_last[:, :Q_BLOCK], 0.0, NEG_INF)
        e_parts, inv_parts = [], []
        for cs in strips:
            sg = f["s"][:, cs] + mask_c
            eg = jnp.exp2(sg - jnp.max(sg, axis=0, keepdims=True))
            e_parts.append(eg.astype(_BF))
            inv_parts.append(jnp.where(n_last[:, cs] >= 0, 1.0 / jnp.sum(eg, axis=0, keepdims=True), 0.0))
        both = _dot(vct_ref[0], jnp.concatenate(e_parts, axis=1)) * jnp.concatenate(inv_parts, axis=1)
        f["o_cmp"] = both[:HEAD_DIM]
        imp_t = both[HEAD_DIM:, strips[0]]
        for cs in strips[1:]:
            imp_t = imp_t + both[HEAD_DIM:, cs]
        f["imp_t"] = imp_t

    wlen = WINDOW + Q_BLOCK

    def front_window_scores(u):
        f = front[u]
        anchor =jnp.concatenate([jnp.minimum(f["imp_t"][0:1], 0.0)] * G, axis=1)
        win_rows = jnp.where(row_aug == 0, slope + anchor,
                             jnp.where(row_aug == 1, slope * POS_RADIX, 0.0)).astype(_BF)
        f["w0"] = pl.multiple_of(jnp.maximum(f["q0"] - WINDOW, 0), Q_BLOCK)
        f["sw"] = _dot(kwa_ref[0, 0, pl.ds(f["w0"], wlen), :], jnp.concatenate([f["qt"], win_rows], axis=0))

    def back_sums(u):
        bk = back[u]
        m = tile_max(bk["tiles"])
        m_ref[u, 0:1] = m
        acc_ref[u] = tile_sums(bk["tiles"], m)

    def front_window_sums(u):
        f = front[u]
        w_row =f["w0"] + lax.broadcasted_iota(jnp.int32, (wlen, Q_BLOCK), 0)
        dist = f["tok"][:, :Q_BLOCK] - w_row
        mask_w = jnp.where(dist.astype(jnp.uint32) < WINDOW, 0.0, NEG_INF)
        e_parts = []
        for cs in strips:
            sg = f["sw"][:, cs] + mask_w
            e_parts.append(jnp.exp2(sg - jnp.max(sg, axis=0, keepdims=True)).astype(_BF))
        win = _dot(vwt_ref[0, 0, :, pl.ds(f["w0"], wlen)], jnp.concatenate(e_parts, axis=1))
        f["o_win"] = win[:HEAD_DIM] * (1.0 / win[HEAD_DIM:HEAD_DIM + 1])

    mrow = lax.broadcasted_iota(jnp.int32, (LANES, Q_BLOCK), 0)
    blk_row = lax.broadcasted_iota(jnp.int32, (LANES, 1), 0)
    blk_bit = jnp.left_shift(1, blk_row % BITS_PER_WORD).astype(_F32)

    def front_select(u):
        f = front[u]
        gates =gt_ref[0][:, u * Q_BLOCK:(u + 1) * Q_BLOCK]

        def gate_row(branch):
            return jnp.concatenate([gates[3 * g + branch:3 * g + branch + 1] for g in range(G)], axis=1)

        tcol = lax.broadcasted_iota(jnp.int32, (LANES, Q_BLOCK), 1) + f["q0"]
        lag = tcol // SEL_BLOCK - mrow
        forced = (mrow == 0) | ((lag >= 0) & (lag < N_LOCAL_FORCED))
        score = jnp.where(forced | (lag < 0) | (mrow >= n_sel), -jnp.inf, f["imp_t"])
        notsel_t = _mark_top_blocks(score, jnp.where(forced, 0.0, 1.0), k_top - n_forced)

        sel_rows = jnp.where(row_aug == 0, slope, jnp.concatenate([notsel_t] * G, axis=1)).astype(_BF)
        lhs_ref[put, u] = jnp.concatenate([f["qt"], sel_rows], axis=0)
        part_ref[put3, u] = gate_row(0) * f["o_cmp"] + gate_row(2) * f["o_win"]
        gate_ref[put3, u] = jnp.broadcast_to(gate_row(1), gate_ref.shape[2:])

        far_end = (f["q0"] // SEL_TILE - (LOCAL_TILES - 1)) * blocks_per_tile
        blk_on = jnp.where((jnp.min(notsel_t, axis=1, keepdims=True) < 0.5)
                           & (blk_row >= blocks_per_tile) & (blk_row < far_end), 1.0, 0.0)
        for k in range(words):
            word = jnp.sum((blk_on * blk_bit)[k * BITS_PER_WORD:(k + 1) * BITS_PER_WORD])
            bits_ref[flag_base(put, u) + k] = word.astype(jnp.int32)
        bits_ref[flag_base(put, u) + words] = jnp.sum(blk_on).astype(jnp.int32)

    phases = {"finish": finish, "front_scores": front_scores, "back_scores": back_scores,
              "front_compressed": front_compressed, "front_window_scores": front_window_scores,
              "back_sums": back_sums, "front_window_sums": front_window_sums, "front_select": front_select}
    for phase, u in PHASE_ORDER:
        phases[phase](u)

    for u, bk in zip(units, back):
        def far_tile(j, carry, u=u, bk=bk):
            word = bits_ref[flag_base(get, u) + j // tiles_per_word]
            tile_bits = (word >> ((j % tiles_per_word) * blocks_per_tile)) & ((1 << blocks_per_tile) - 1)

            @pl.when(tile_bits != 0)
            def _():
                tile = [tile_scores(bk, j, False)]
                m_old = m_ref[u, 0:1]
                m_new = jnp.maximum(m_old, tile_max(tile))
                m_ref[u, 0:1] = m_new
                acc_ref[u] = jnp.exp2(m_old - m_new) * acc_ref[u] + tile_sums(tile, m_new)
            return carry

        @pl.when(bits_ref[flag_base(get, u) + words] != 0)
        def _(far_tile=far_tile, bk=bk):
            lax.fori_loop(1, bk["first_local"], far_tile, 0)


def _nsa_attention(qt, gt, kca, vct, ksa, vst, kwa, vwt, B, S):
    n_pair = S // (Q_BLOCK * PAIR)
    T = B * S
    nc = kca.shape[1]
    gd = Q_PER_KV * HEAD_DIM
    W = Q_PER_KV * Q_BLOCK
    n_seq = B * N_KV_HEADS
    last = n_seq * n_pair - 1
    front = lambda i: jnp.minimum(i, last)
    back = lambda i: jnp.clip(i - 1, 0, last)
    done = lambda i: jnp.maximum(i - 2, 0)
    seq_of = lambda p: p // n_pair
    blk_of = lambda p: (seq_of(p) // N_KV_HEADS) * n_pair + p % n_pair
    head_of = lambda p: seq_of(p) % N_KV_HEADS
    rows = lambda stage: pl.BlockSpec((1, 1, S, 2 * HEAD_DIM),
                                      lambda i: (seq_of(stage(i)) // N_KV_HEADS, head_of(stage(i)), 0, 0))
    cols = lambda stage: pl.BlockSpec((1, 1, VT_ROWS, S),
                                      lambda i: (seq_of(stage(i)) // N_KV_HEADS, head_of(stage(i)), 0, 0))
    return pl.pallas_call(
        functools.partial(_attn_kernel, seq=S, n_seq=n_seq),
        out_shape=jax.ShapeDtypeStruct((T, N_KV_HEADS * gd), _BF),
        grid=(last + 3,),
        in_specs=[
            pl.BlockSpec((PAIR, 1, HEAD_DIM, W), lambda i: (blk_of(front(i)), head_of(front(i)), 0, 0)),
            pl.BlockSpec((1, GATE_ROWS, PAIR * Q_BLOCK), lambda i: (head_of(front(i)), 0, blk_of(front(i)))),
            pl.BlockSpec((1, nc, 2 * HEAD_DIM), lambda i: (seq_of(front(i)), 0, 0)),
            pl.BlockSpec((1, HEAD_DIM + LANES, nc), lambda i: (seq_of(front(i)), 0, 0)),
            rows(back), cols(back), rows(front), cols(front),
        ],
        out_specs=pl.BlockSpec((PAIR * Q_BLOCK, gd), lambda i: (blk_of(done(i)), head_of(done(i)))),
        scratch_shapes=[pltpu.VMEM((PAIR, 8, W), _F32), pltpu.VMEM((PAIR, VT_ROWS, W), _F32),
                        pltpu.VMEM((2, PAIR, 2 * HEAD_DIM, W), _BF), pltpu.VMEM((3, PAIR, HEAD_DIM, W), _F32),
                        pltpu.VMEM((3, PAIR, 8, W), _F32),
                        pltpu.SMEM((2 * PAIR * (LANES // BITS_PER_WORD + 1),), jnp.int32)],
        compiler_params=pltpu.CompilerParams(
            dimension_semantics=("arbitrary",), vmem_limit_bytes=VMEM_LIMIT),
        name="nsa_attention",
    )(qt, gt, kca, vct, ksa, vst, kwa, vwt)


def _merge_ffn_kernel(x_ref, gm_ref, yn_ref, wm1_ref, wpn_ref, wo_ref, g1_ref, b1_ref,
                      w1_ref, w2_ref, g2_ref, b2_ref, o_ref):
    x = x_ref[...]
    gate = _sigmoid(_dot(x.astype(_BF), wm1_ref[...]))
    merged = gm_ref[...] + gate * _dot(yn_ref[...], wpn_ref[...])
    mix = _dot(merged.astype(_BF), wo_ref[...])
    hid = _layer_norm(DEEPNORM_ALPHA * x + mix, g1_ref[...], b1_ref[...])
    hb = hid.astype(_BF)
    f = jnp.zeros(hid.shape, _F32)
    for c in range(D_FF // D_MODEL):
        a = jnp.maximum(_dot(hb, w1_ref[:, c * D_MODEL:(c + 1) * D_MODEL]), 0.0)
        f = f + _dot((a * a).astype(_BF), w2_ref[c * D_MODEL:(c + 1) * D_MODEL, :])
    o_ref[...] = _layer_norm(DEEPNORM_ALPHA * hid + f, g2_ref[...], b2_ref[...])


def _merge_ffn(x2, gm, yn, wm1, wpn, wo, g1, b1, w1, w2, g2, b2):
    T = x2.shape[0]
    tm = ROW_TILE
    rows = lambda w: pl.BlockSpec((tm, w), lambda i: (i, 0))
    const = lambda r, c: _resident((r, c), lambda i: (0, 0))
    return pl.pallas_call(
        _merge_ffn_kernel,
        out_shape=jax.ShapeDtypeStruct((T, D_MODEL), _F32),
        grid=(T // tm,),
        in_specs=[rows(D_MODEL), rows(D_MODEL), rows(D_MODEL),
                  const(D_MODEL, D_MODEL), const(D_MODEL, D_MODEL), const(D_MODEL, D_MODEL),
                  const(1, D_MODEL), const(1, D_MODEL),
                  const(D_MODEL, D_FF), const(D_FF, D_MODEL),
                  const(1, D_MODEL), const(1, D_MODEL)],
        out_specs=rows(D_MODEL),
        compiler_params=pltpu.CompilerParams(
            dimension_semantics=("arbitrary",), vmem_limit_bytes=VMEM_LIMIT),
        name="merge_ffn",
    )(x2, gm, yn, wm1, wpn, wo, g1, b1, w1, w2, g2, b2)


def _layer(x2, B, S, w_in, layer, gm_ln_g, gm_ln_b, gm_w_s, gm_b_s, cmp_pe_k, cmp_w1_k, cmp_w2_k,
           cmp_pe_v, cmp_w1_v, cmp_w2_v, w_proj_gm, w_proj_nsa, w_out,
           ln1_g, ln1_b, w_ff1, w_ff2, ln2_g, ln2_b):
    o_q = 2 * D_MODEL
    o_m = o_q + (Q_PER_KV + 6) * N_KV_HEADS * HEAD_DIM + 3 * Q_PER_KV * N_KV_HEADS
    row = lambda v: v.reshape(1, -1)
    merge_cols = lambda j: w_in[layer * D_MODEL:(layer + 1) * D_MODEL, o_m + j * D_MODEL:o_m + (j + 1) * D_MODEL]

    gm = _gm_mixer(x2, w_in, layer, merge_cols(0), row(gm_ln_g), row(gm_ln_b), gm_w_s,
                   jnp.broadcast_to(gm_b_s[:, :, None], (GM_GROUPS, GM_CHUNK, LANES)), w_proj_gm)

    qt, kc, vc, ksa, vst, kwa, vwt, gt = _qkv_proj(x2, w_in, layer, o_q, B, S)

    heads = lambda a: a.reshape(B * N_KV_HEADS, S, HEAD_DIM)
    pairs = lambda w: w.astype(_BF).reshape(CMP_BLOCK // 2, 2 * HEAD_DIM, HEAD_DIM)
    kca, vct = _nsa_compress(heads(kc), heads(vc), cmp_pe_k, cmp_pe_v, pairs(cmp_w1_k), pairs(cmp_w1_v),
                             cmp_w2_k.astype(_BF), cmp_w2_v.astype(_BF))

    yn = _nsa_attention(qt, gt, kca, vct, ksa, vst, kwa, vwt, B, S)

    return _merge_ffn(x2, gm, yn, merge_cols(1).astype(_BF), w_proj_nsa.astype(_BF),
                      w_out.astype(_BF), row(ln1_g), row(ln1_b), w_ff1.astype(_BF), w_ff2.astype(_BF),
                      row(ln2_g), row(ln2_b))


def kernel(x, w_in, gm_ln_g, gm_ln_b, gm_w_s, gm_b_s, cmp_pe_k, cmp_w1_k, cmp_w2_k, cmp_pe_v, cmp_w1_v, cmp_w2_v, w_proj_gm, w_proj_nsa, w_out, ln1_g, ln1_b, w_ff1, w_ff2, ln2_g, ln2_b):
    B, S, D = x.shape
    assert D == D_MODEL and S % ROW_TILE == 0 and WINDOW + Q_BLOCK <= S <= SEL_BLOCK * LANES
    assert S // POS_RADIX <= POS_RADIX and S % (PAIR * Q_BLOCK) == 0
    h = x.reshape(B * S, D)
    depth, _, d_in = w_in.shape
    w_rows = w_in.reshape(depth * D, d_in)
    for l in range(depth):
        h = _layer(h, B, S, w_rows, l, gm_ln_g[l], gm_ln_b[l], gm_w_s[l], gm_b_s[l],
                   cmp_pe_k[l], cmp_w1_k[l], cmp_w2_k[l], cmp_pe_v[l], cmp_w1_v[l], cmp_w2_v[l],
                   w_proj_gm[l], w_proj_nsa[l], w_out[l], ln1_g[l], ln1_b[l],
                   w_ff1[l], w_ff2[l], ln2_g[l], ln2_b[l])
    return h.reshape(B, S, D)
```

```python
import functools
import math

import jax
import jax.numpy as jnp
from jax import lax
from jax.experimental import pallas as pl
from jax.experimental.pallas import tpu as pltpu

D_MODEL = 1024
GM_GROUPS = 8
GM_CHUNK = 128
N_KV_HEADS = 2
Q_PER_KV = 4
HEAD_DIM = 128
CMP_BLOCK = 32
CMP_STRIDE = 16
SEL_BLOCK = 64
N_SELECT = 16
N_LOCAL_FORCED = 2
WINDOW = 512
Q_BLOCK = 128
D_FF = 4 * D_MODEL
DEEPNORM_ALPHA = 2.0 ** 0.25
LN_EPS = 1e-5
NEG_INF = -1e30
LOG2_E = math.log2(math.e)

LANES = 128
SEL_TILE = 128
LOCAL_TILES = 11
PAIR = 2
PHASE_ORDER = tuple((phase, u) for phase in ("finish", "front_scores", "back_scores", "front_compressed",
                                              "front_window_scores", "back_sums", "front_window_sums",
                                              "front_select") for u in range(PAIR))
BITS_PER_WORD = 16
POS_RADIX = 256
FFN_SPLIT = 2
ROW_TILE = 512
VT_ROWS = HEAD_DIM + 16
GATE_ROWS = 16
VMEM_LIMIT = 56 * 1024 * 1024

_BF = jnp.bfloat16
_F32 = jnp.float32


def _dot(a, b):
    return jnp.dot(a, b, preferred_element_type=_F32)


def _gelu(x):
    c = math.sqrt(2.0 / math.pi)
    return 0.5 * x * (1.0 + jnp.tanh(c * (x + 0.044715 * (x * x * x))))


def _sigmoid(x):
    return 1.0 / (1.0 + jnp.exp(-x))


def _layer_norm(x, g, b):
    mu = jnp.mean(x, axis=-1, keepdims=True)
    xc = x - mu
    var = jnp.mean(xc * xc, axis=-1, keepdims=True)
    return xc * lax.rsqrt(var + LN_EPS) * g + b


def _position_lanes(pos, shape):
    lane = lax.broadcasted_iota(jnp.int32, shape, 1)
    return jnp.where(lane == 0, (pos % POS_RADIX).astype(_F32),
                     jnp.where(lane == 1, (pos // POS_RADIX).astype(_F32), 0.0)).astype(_BF)


def _stage_bf16(first_step, pairs):
    @pl.when(first_step)
    def _():
        for src, dst in pairs:
            dst[...] = src[...].astype(_BF)


def _resident(shape, index_map):
    return pl.BlockSpec(shape, index_map, pipeline_mode=pl.Buffered(1))


def _gm_kernel(x_ref, wgm32_ref, wm032_ref, lng_ref, lnb_ref, ws_ref, bs_ref, wpg32_ref, o_ref,
               vg_ref, wgm_ref, wm0_ref, wpg_ref):
    _stage_bf16(pl.program_id(0) == 0, [(wgm32_ref, wgm_ref), (wm032_ref, wm0_ref), (wpg32_ref, wpg_ref)])
    tm = x_ref.shape[0]
    xb = x_ref[...].astype(_BF)
    z = _gelu(_dot(xb, wgm_ref[...]))
    gate_logits = _dot(xb, wm0_ref[...])
    u = z[:, :D_MODEL]
    v = _layer_norm(z[:, D_MODEL:], lng_ref[...], lnb_ref[...]).astype(_BF)
    row = lax.broadcasted_iota(jnp.int32, (GM_CHUNK, GM_CHUNK), 0)
    col = lax.broadcasted_iota(jnp.int32, (GM_CHUNK, GM_CHUNK), 1)
    for gi in range(GM_GROUPS):
        w = jnp.where(row >= col, ws_ref[gi], 0.0).astype(_BF)
        for c in range(tm // GM_CHUNK):
            blk = v[c * GM_CHUNK:(c + 1) * GM_CHUNK, gi * LANES:(gi + 1) * LANES]
            vg_ref[c * GM_CHUNK:(c + 1) * GM_CHUNK, gi * LANES:(gi + 1) * LANES] = _dot(w, blk) + bs_ref[gi]
    y = (u * vg_ref[...]).astype(_BF)
    o_ref[...] = _sigmoid(gate_logits) * _dot(y, wpg_ref[...])


def _gm_mixer(x2, w_in, layer, wm0, lng, lnb, ws, bs, wpg):
    T = x2.shape[0]
    tm = ROW_TILE
    const2 = lambda i: (0, 0)
    const3 = lambda i: (0, 0, 0)
    return pl.pallas_call(
        _gm_kernel,
        out_shape=jax.ShapeDtypeStruct((T, D_MODEL), _F32),
        grid=(T // tm,),
        in_specs=[
            pl.BlockSpec((tm, D_MODEL), lambda i: (i, 0)),
            _resident((D_MODEL, 2 * D_MODEL), lambda i: (layer, 0)),
            _resident((D_MODEL, D_MODEL), const2),
            pl.BlockSpec((1, D_MODEL), const2),
            pl.BlockSpec((1, D_MODEL), const2),
            pl.BlockSpec((GM_GROUPS, GM_CHUNK, GM_CHUNK), const3),
            pl.BlockSpec((GM_GROUPS, GM_CHUNK, LANES), const3),
            _resident((D_MODEL, D_MODEL), const2),
        ],
        out_specs=pl.BlockSpec((tm, D_MODEL), lambda i: (i, 0)),
        scratch_shapes=[pltpu.VMEM((tm, D_MODEL), _F32), pltpu.VMEM((D_MODEL, 2 * D_MODEL), _BF),
                        pltpu.VMEM((D_MODEL, D_MODEL), _BF), pltpu.VMEM((D_MODEL, D_MODEL), _BF)],
        compiler_params=pltpu.CompilerParams(
            dimension_semantics=("arbitrary",), vmem_limit_bytes=VMEM_LIMIT),
        name="gm_mixer",
    )(x2, w_in, wm0, lng, lnb, ws, bs, wpg)


def _qkv_kernel(x_ref, wq32_ref, wkv32_ref, wg32_ref, qt_ref, kc_ref, vc_ref, ksa_ref, vst_ref, kwa_ref, vwt_ref,
                gt_ref, wq_ref, wkv_ref, wg_ref):
    _stage_bf16((pl.program_id(0) == 0) & (pl.program_id(1) == 0),
                [(wq32_ref, wq_ref), (wkv32_ref, wkv_ref), (wg32_ref, wg_ref)])
    tm = x_ref.shape[0]
    xb = x_ref[...].astype(_BF)
    zq = _dot(xb, wq_ref[...]) * (HEAD_DIM ** -0.5 * LOG2_E)
    for tb in range(tm // Q_BLOCK):
        for h in range(N_KV_HEADS):
            for g in range(Q_PER_KV):
                c0 = (h * Q_PER_KV + g) * HEAD_DIM
                blk = zq[tb * Q_BLOCK:(tb + 1) * Q_BLOCK, c0:c0 + HEAD_DIM]
                qt_ref[tb, h, :, g * Q_BLOCK:(g + 1) * Q_BLOCK] = blk.T.astype(_BF)
    z = _dot(xb, wkv_ref[...])
    kpos = pl.program_id(1) * tm + lax.broadcasted_iota(jnp.int32, (tm, LANES), 0)
    blk_lane = lax.broadcasted_iota(jnp.int32, (tm, LANES), 1)
    sel_lanes = jnp.where(blk_lane == 0, (kpos % SEL_TILE).astype(_F32),
                          jnp.where(kpos // SEL_BLOCK == blk_lane, NEG_INF, 0.0)).astype(_BF)
    win_lanes = _position_lanes(kpos, (tm, LANES))
    ones = jnp.ones((VT_ROWS - HEAD_DIM, tm), _BF)
    for h in range(N_KV_HEADS):
        def col(j):
            return z[:, j * 2 * HEAD_DIM + h * HEAD_DIM: j * 2 * HEAD_DIM + (h + 1) * HEAD_DIM]
        kc_ref[0, h] = col(0)
        vc_ref[0, h] = col(1)
        ksa_ref[0, h, :, :HEAD_DIM] = col(2).astype(_BF)
        ksa_ref[0, h, :, HEAD_DIM:] = sel_lanes
        vst_ref[0, h, :HEAD_DIM] = col(3).T.astype(_BF)
        vst_ref[0, h, HEAD_DIM:] = ones
        kwa_ref[0, h, :, :HEAD_DIM] = col(4).astype(_BF)
        kwa_ref[0, h, :, HEAD_DIM:] = win_lanes
        vwt_ref[0, h, :HEAD_DIM] = col(5).T.astype(_BF)
        vwt_ref[0, h, HEAD_DIM:] = ones
    zg_t = _sigmoid(_dot(xb, wg_ref[...])).T
    per_head = 3 * Q_PER_KV
    for h in range(N_KV_HEADS):
        gt_ref[h] = zg_t[h * per_head:h * per_head + GATE_ROWS]


def _qkv_proj(x2, w_in, layer, q_col, B, S):
    T = x2.shape[0]
    tm = ROW_TILE
    nsb = S // tm
    gd = Q_PER_KV * Q_BLOCK
    aw = N_KV_HEADS * Q_PER_KV * HEAD_DIM
    kvw = 6 * N_KV_HEADS * HEAD_DIM
    kv_col = q_col + aw
    gate_col = kv_col + kvw
    assert q_col % aw == 0 and kv_col % kvw == 0 and gate_col % LANES == 0
    rows_spec = lambda w: pl.BlockSpec((1, N_KV_HEADS, tm, w), lambda b, s: (b, 0, s, 0))
    rows_shape = lambda w, dt: jax.ShapeDtypeStruct((B, N_KV_HEADS, S, w), dt)
    cols_spec = pl.BlockSpec((1, N_KV_HEADS, VT_ROWS, tm), lambda b, s: (b, 0, 0, s))
    cols_shape = jax.ShapeDtypeStruct((B, N_KV_HEADS, VT_ROWS, S), _BF)
    return pl.pallas_call(
        _qkv_kernel,
        out_shape=(
            jax.ShapeDtypeStruct((T // Q_BLOCK, N_KV_HEADS, HEAD_DIM, gd), _BF),
            rows_shape(HEAD_DIM, _F32), rows_shape(HEAD_DIM, _F32),
            rows_shape(2 * HEAD_DIM, _BF), cols_shape,
            rows_shape(2 * HEAD_DIM, _BF), cols_shape,
            jax.ShapeDtypeStruct((N_KV_HEADS, GATE_ROWS, T), _F32),
        ),
        grid=(B, nsb),
        in_specs=[
            pl.BlockSpec((tm, D_MODEL), lambda b, s: (b * nsb + s, 0)),
            _resident((D_MODEL, aw), lambda b, s: (layer, q_col // aw)),
            _resident((D_MODEL, kvw), lambda b, s: (layer, kv_col // kvw)),
            _resident((D_MODEL, LANES), lambda b, s: (layer, gate_col // LANES)),
        ],
        out_specs=(
            pl.BlockSpec((tm // Q_BLOCK, N_KV_HEADS, HEAD_DIM, gd), lambda b, s: (b * nsb + s, 0, 0, 0)),
            rows_spec(HEAD_DIM), rows_spec(HEAD_DIM), rows_spec(2 * HEAD_DIM), cols_spec,
            rows_spec(2 * HEAD_DIM), cols_spec,
            pl.BlockSpec((N_KV_HEADS, GATE_ROWS, tm), lambda b, s: (0, 0, b * nsb + s)),
        ),
        scratch_shapes=[pltpu.VMEM((D_MODEL, aw), _BF), pltpu.VMEM((D_MODEL, kvw), _BF),
                        pltpu.VMEM((D_MODEL, LANES), _BF)],
        compiler_params=pltpu.CompilerParams(
            dimension_semantics=("arbitrary", "arbitrary"), vmem_limit_bytes=VMEM_LIMIT),
        name="qkv_proj",
    )(x2, w_in, w_in, w_in)


def _compress_kernel(kc_ref, vc_ref, pek_ref, pev_ref, w1k_ref, w1v_ref, w2k_ref, w2v_ref, ko_ref, vo_ref):
    nc = ko_ref.shape[1]

    def tokens(src, pe, w1, w2):
        first, second = None, None
        for p in range(0, CMP_STRIDE, 2):
            rows = [src[0, pl.ds(p + d, nc, stride=CMP_STRIDE), :] for d in range(2)]
            lo = jnp.concatenate([rows[d] + pe[p + d:p + d + 1, :] for d in range(2)], axis=1)
            hi = jnp.concatenate([rows[d] + pe[CMP_STRIDE + p + d:CMP_STRIDE + p + d + 1, :] for d in range(2)], axis=1)
            a = _dot(lo.astype(_BF), w1[p // 2])
            b = _dot(hi.astype(_BF), w1[(CMP_STRIDE + p) // 2])
            first = a if first is None else first + a
            second = b if second is None else second + b
        pre = first + pltpu.roll(second, nc - 1, 0)
        return _dot(_gelu(pre).astype(_BF), w2[...])

    ko_ref[0, :, :HEAD_DIM] = tokens(kc_ref, pek_ref, w1k_ref, w2k_ref).astype(_BF)
    start = lax.broadcasted_iota(jnp.int32, (nc, LANES), 0) * CMP_STRIDE
    ko_ref[0, :, HEAD_DIM:] = _position_lanes(start, (nc, LANES))
    vo_ref[0, :HEAD_DIM] = tokens(vc_ref, pev_ref, w1v_ref, w2v_ref).T.astype(_BF)
    mi = lax.broadcasted_iota(jnp.int32, (LANES, nc), 0)
    ni = lax.broadcasted_iota(jnp.int32, (LANES, nc), 1)
    vo_ref[0, HEAD_DIM:] = jnp.where((ni * CMP_STRIDE + (CMP_BLOCK - 1) >= mi * SEL_BLOCK)
                                     & (ni * CMP_STRIDE <= mi * SEL_BLOCK + (SEL_BLOCK - 1)), 1.0, 0.0).astype(_BF)


def _nsa_compress(kc, vc, pek, pev, w1k, w1v, w2k, w2v):
    BH, S, _ = kc.shape
    nc = S // CMP_STRIDE
    pair = 2 * HEAD_DIM
    const2 = lambda i: (0, 0)
    const3 = lambda i: (0, 0, 0)
    seq_rows = pl.BlockSpec((1, S, HEAD_DIM), lambda i: (i, 0, 0))
    pe_spec = pl.BlockSpec((CMP_BLOCK, HEAD_DIM), const2)
    w1_spec = pl.BlockSpec((CMP_BLOCK // 2, pair, HEAD_DIM), const3)
    w2_spec = pl.BlockSpec((HEAD_DIM, HEAD_DIM), const2)
    return pl.pallas_call(
        _compress_kernel,
        out_shape=(jax.ShapeDtypeStruct((BH, nc, 2 * HEAD_DIM), _BF),
                   jax.ShapeDtypeStruct((BH, HEAD_DIM + LANES, nc), _BF)),
        grid=(BH,),
        in_specs=[seq_rows, seq_rows, pe_spec, pe_spec, w1_spec, w1_spec, w2_spec, w2_spec],
        out_specs=(pl.BlockSpec((1, nc, 2 * HEAD_DIM), lambda i: (i, 0, 0)),
                   pl.BlockSpec((1, HEAD_DIM + LANES, nc), lambda i: (i, 0, 0))),
        compiler_params=pltpu.CompilerParams(
            dimension_semantics=("arbitrary",), vmem_limit_bytes=VMEM_LIMIT),
        name="nsa_compress",
    )(kc, vc, pek, pev, w1k, w1v, w2k, w2v)


def _mark_top_blocks(score, notsel, rounds):
    rows = lax.broadcasted_iota(jnp.int32, score.shape, 0).astype(_F32)
    for _ in range(rounds):
        mx = jnp.max(score, axis=0, keepdims=True)
        idx = jnp.min(jnp.where(score == mx, rows, float(LANES)), axis=0, keepdims=True)
        hit = rows == idx
        notsel = jnp.where(hit, 0.0, notsel)
        score = jnp.where(hit, -jnp.inf, score)
    return notsel


def _attn_kernel(qt_ref, gt_ref, kca_ref, vct_ref, ksa_ref, vst_ref, kwa_ref, vwt_ref, o_ref,
                 m_ref, acc_ref, lhs_ref, part_ref, gate_ref, bits_ref, *, seq, n_seq):
    step = pl.program_id(0)
    n_pair = seq // (Q_BLOCK * PAIR)
    last = n_seq * n_pair - 1
    front_pair = jnp.minimum(step, last)
    back_pair = jnp.clip(step - 1, 0, last)
    nc = kca_ref.shape[1]
    n_sel = seq // SEL_BLOCK
    n_forced = 1 + N_LOCAL_FORCED
    k_top = min(N_SELECT, n_sel)
    G = Q_PER_KV
    W = G * Q_BLOCK
    words = LANES // BITS_PER_WORD
    blocks_per_tile = SEL_TILE // SEL_BLOCK
    tiles_per_word = BITS_PER_WORD // blocks_per_tile
    put = step % 2
    get = 1 - put
    put3 = step % 3
    get3 = (step + 1) % 3
    units = range(PAIR)

    def flag_base(slot, u):
        return (slot * PAIR + u) * (words + 1)

    @pl.when(step == 0)
    def _():
        lhs_ref[1] = jnp.zeros(lhs_ref.shape[1:], _BF)
        part_ref[...] = jnp.zeros(part_ref.shape, _F32)
        gate_ref[...] = jnp.zeros(gate_ref.shape, _F32)
        acc_ref[...] = jnp.ones(acc_ref.shape, _F32)
        for k in range(PAIR * (words + 1)):
            bits_ref[flag_base(1, 0) + k] = 0

    lane_w = lax.broadcasted_iota(jnp.int32, (1, W), 1)

    def head_slopes(pair_index):
        h = (pair_index // n_pair) % N_KV_HEADS
        slope = jnp.zeros((1, W), _F32)
        for g in range(G):
            sg = jnp.where(h == 0, _F32(2.0 ** -(g + 1)), _F32(2.0 ** -(G + g + 1)))
            slope = jnp.where(lane_w // Q_BLOCK == g, sg, slope)
        return (slope * LOG2_E).astype(_BF).astype(_F32)

    slope = head_slopes(front_pair)
    slope_b = head_slopes(back_pair)
    strips =[slice(g * Q_BLOCK, (g + 1) * Q_BLOCK) for g in range(G)]
    row_aug = lax.broadcasted_iota(jnp.int32, (LANES, W), 0)
    key_row = lax.broadcasted_iota(jnp.int32, (SEL_TILE, Q_BLOCK), 0)
    pos_rows = jnp.where(row_aug == 0, slope, jnp.where(row_aug == 1, slope * POS_RADIX, 0.0)).astype(_BF)

    def finish(u):
        out_t = (part_ref[get3, u] + acc_ref[u, 0:HEAD_DIM]
                 * (gate_ref[get3, u][0:1] * (1.0 / acc_ref[u, HEAD_DIM:HEAD_DIM + 1])))
        for g, cs in enumerate(strips):
            o_ref[u * Q_BLOCK:(u + 1) * Q_BLOCK, g * HEAD_DIM:(g + 1) * HEAD_DIM] = out_t[:, cs].T.astype(_BF)

    front = [{} for _ in units]

    def front_scores(u):
        f = front[u]
        f["q0"] =(front_pair % n_pair * PAIR + u) * Q_BLOCK
        f["tok"] = f["q0"] + lane_w % Q_BLOCK
        f["qt"] = qt_ref[u, 0]
        f["s"] = _dot(kca_ref[0], jnp.concatenate([f["qt"], pos_rows], axis=0))

    def tile_scores(bk, j, causal, live=None):
        k0 = j * SEL_TILE if isinstance(j, int) else pl.multiple_of(j * SEL_TILE, SEL_TILE)
        st = _dot(ksa_ref[0, 0, pl.ds(k0, SEL_TILE), :], bk["lhs"])
        if causal:
            ahead = jnp.where(k0 + key_row > bk["tok"][:, :Q_BLOCK], NEG_INF, 0.0)
            st = jnp.concatenate([st[:, cs] + ahead for cs in strips], axis=1)
        shift = slope_b * (bk["p0"] - k0).astype(_F32)
        if live is not None:
            shift = jnp.where(live, shift, -NEG_INF)
        return st, shift, k0

    def tile_max(tiles):
        m = None
        for st, shift, _ in tiles:
            cm = jnp.max(st, axis=0, keepdims=True) - shift
            m = cm if m is None else jnp.maximum(m, cm)
        return m

    def tile_sums(tiles, m):
        acc = None
        for st, shift, k0 in tiles:
            sub = m + shift
            pt = jnp.concatenate([jnp.exp2(st[:, cs] - sub[:, cs]) for cs in strips], axis=1)
            ai = _dot(vst_ref[0, 0, :, pl.ds(k0, SEL_TILE)], pt.astype(_BF))
            acc = ai if acc is None else acc + ai
        return acc

    back = [{} for _ in units]

    def back_scores(u):
        bk = back[u]
        bk["p0"] =(back_pair % n_pair * PAIR + u) * Q_BLOCK
        bk["tok"] = bk["p0"] + lane_w % Q_BLOCK
        bk["lhs"] = lhs_ref[get, u]
        bk["first_local"] = bk["p0"] // SEL_TILE - (LOCAL_TILES - 1)
        bk["tiles"] = [tile_scores(bk, 0, False, live=bk["first_local"] > 0)]
        for i in range(LOCAL_TILES):
            j = bk["first_local"] + i
            last = i == LOCAL_TILES - 1
            bk["tiles"].append(tile_scores(bk, jnp.maximum(j, 0), last, live=None if last else j >= 0))

    n_row = lax.broadcasted_iota(jnp.int32, (nc, Q_BLOCK), 0)

    def front_compressed(u):
        f = front[u]
        n_last =(f["tok"] - (CMP_BLOCK - 1)) // CMP_STRIDE
        mask_c = jnp.where(n_row <= n_last[:, :Q_BLOCK], 0.0, NEG_INF)
        e_parts, inv_parts = [], []
        for cs in strips:
            sg = f["s"][:, cs] + mask_c
            eg = jnp.exp2(sg - jnp.max(sg, axis=0, keepdims=True))
            e_parts.append(eg.astype(_BF))
            inv_parts.append(jnp.where(n_last[:, cs] >= 0, 1.0 / jnp.sum(eg, axis=0, keepdims=True), 0.0))
        both = _dot(vct_ref[0], jnp.concatenate(e_parts, axis=1)) * jnp.concatenate(inv_parts, axis=1)
        f["o_cmp"] = both[:HEAD_DIM]
        imp_t = both[HEAD_DIM:, strips[0]]
        for cs in strips[1:]:
            imp_t = imp_t + both[HEAD_DIM:, cs]
        f["imp_t"] = imp_t

    wlen = WINDOW + Q_BLOCK

    def front_window_scores(u):
        f = front[u]
        anchor =jnp.concatenate([jnp.minimum(f["imp_t"][0:1], 0.0)] * G, axis=1)
        win_rows = jnp.where(row_aug == 0, slope + anchor,
                             jnp.where(row_aug == 1, slope * POS_RADIX, 0.0)).astype(_BF)
        f["w0"] = pl.multiple_of(jnp.maximum(f["q0"] - WINDOW, 0), Q_BLOCK)
        f["sw"] = _dot(kwa_ref[0, 0, pl.ds(f["w0"], wlen), :], jnp.concatenate([f["qt"], win_rows], axis=0))

    def back_sums(u):
        bk = back[u]
        m = tile_max(bk["tiles"])
        m_ref[u, 0:1] = m
        acc_ref[u] = tile_sums(bk["tiles"], m)

    def front_window_sums(u):
        f = front[u]
        w_row =f["w0"] + lax.broadcasted_iota(jnp.int32, (wlen, Q_BLOCK), 0)
        dist = f["tok"][:, :Q_BLOCK] - w_row
        mask_w = jnp.where(dist.astype(jnp.uint32) < WINDOW, 0.0, NEG_INF)
        e_parts = []
        for cs in strips:
            sg = f["sw"][:, cs] + mask_w
            e_parts.append(jnp.exp2(sg - jnp.max(sg, axis=0, keepdims=True)).astype(_BF))
        win = _dot(vwt_ref[0, 0, :, pl.ds(f["w0"], wlen)], jnp.concatenate(e_parts, axis=1))
        f["o_win"] = win[:HEAD_DIM] * (1.0 / win[HEAD_DIM:HEAD_DIM + 1])

    mrow = lax.broadcasted_iota(jnp.int32, (LANES, Q_BLOCK), 0)
    blk_row = lax.broadcasted_iota(jnp.int32, (LANES, 1), 0)
    blk_bit = jnp.left_shift(1, blk_row % BITS_PER_WORD).astype(_F32)

    def front_select(u):
        f = front[u]
        gates =gt_ref[0][:, u * Q_BLOCK:(u + 1) * Q_BLOCK]

        def gate_row(branch):
            return jnp.concatenate([gates[3 * g + branch:3 * g + branch + 1] for g in range(G)], axis=1)

        tcol = lax.broadcasted_iota(jnp.int32, (LANES, Q_BLOCK), 1) + f["q0"]
        lag = tcol // SEL_BLOCK - mrow
        forced = (mrow == 0) | ((lag >= 0) & (lag < N_LOCAL_FORCED))
        score = jnp.where(forced | (lag < 0) | (mrow >= n_sel), -jnp.inf, f["imp_t"])
        notsel_t = _mark_top_blocks(score, jnp.where(forced, 0.0, 1.0), k_top - n_forced)

        sel_rows = jnp.where(row_aug == 0, slope, jnp.concatenate([notsel_t] * G, axis=1)).astype(_BF)
        lhs_ref[put, u] = jnp.concatenate([f["qt"], sel_rows], axis=0)
        part_ref[put3, u] = gate_row(0) * f["o_cmp"] + gate_row(2) * f["o_win"]
        gate_ref[put3, u] = jnp.broadcast_to(gate_row(1), gate_ref.shape[2:])

        far_end = (f["q0"] // SEL_TILE - (LOCAL_TILES - 1)) * blocks_per_tile
        blk_on = jnp.where((jnp.min(notsel_t, axis=1, keepdims=True) < 0.5)
                           & (blk_row >= blocks_per_tile) & (blk_row < far_end), 1.0, 0.0)
        for k in range(words):
            word = jnp.sum((blk_on * blk_bit)[k * BITS_PER_WORD:(k + 1) * BITS_PER_WORD])
            bits_ref[flag_base(put, u) + k] = word.astype(jnp.int32)
        bits_ref[flag_base(put, u) + words] = jnp.sum(blk_on).astype(jnp.int32)

    phases = {"finish": finish, "front_scores": front_scores, "back_scores": back_scores,
              "front_compressed": front_compressed, "front_window_scores": front_window_scores,
              "back_sums": back_sums, "front_window_sums": front_window_sums, "front_select": front_select}
    for phase, u in PHASE_ORDER:
        phases[phase](u)

    for u, bk in zip(units, back):
        def far_tile(j, carry, u=u, bk=bk):
            word = bits_ref[flag_base(get, u) + j // tiles_per_word]
            tile_bits = (word >> ((j % tiles_per_word) * blocks_per_tile)) & ((1 << blocks_per_tile) - 1)

            @pl.when(tile_bits != 0)
            def _():
                tile = [tile_scores(bk, j, False)]
                m_old = m_ref[u, 0:1]
                m_new = jnp.maximum(m_old, tile_max(tile))
                m_ref[u, 0:1] = m_new
                acc_ref[u] = jnp.exp2(m_old - m_new) * acc_ref[u] + tile_sums(tile, m_new)
            return carry

        @pl.when(bits_ref[flag_base(get, u) + words] != 0)
        def _(far_tile=far_tile, bk=bk):
            lax.fori_loop(1, bk["first_local"], far_tile, 0)


def _nsa_attention(qt, gt, kca, vct, ksa, vst, kwa, vwt, B, S):
    n_pair = S // (Q_BLOCK * PAIR)
    T = B * S
    nc = kca.shape[1]
    gd = Q_PER_KV * HEAD_DIM
    W = Q_PER_KV * Q_BLOCK
    n_seq = B * N_KV_HEADS
    last = n_seq * n_pair - 1
    front = lambda i: jnp.minimum(i, last)
    back = lambda i: jnp.clip(i - 1, 0, last)
    done = lambda i: jnp.maximum(i - 2, 0)
    seq_of = lambda p: p // n_pair
    blk_of = lambda p: (seq_of(p) // N_KV_HEADS) * n_pair + p % n_pair
    head_of = lambda p: seq_of(p) % N_KV_HEADS
    rows = lambda stage: pl.BlockSpec((1, 1, S, 2 * HEAD_DIM),
                                      lambda i: (seq_of(stage(i)) // N_KV_HEADS, head_of(stage(i)), 0, 0))
    cols = lambda stage: pl.BlockSpec((1, 1, VT_ROWS, S),
                                      lambda i: (seq_of(stage(i)) // N_KV_HEADS, head_of(stage(i)), 0, 0))
    return pl.pallas_call(
        functools.partial(_attn_kernel, seq=S, n_seq=n_seq),
        out_shape=jax.ShapeDtypeStruct((T, N_KV_HEADS * gd), _BF),
        grid=(last + 3,),
        in_specs=[
            pl.BlockSpec((PAIR, 1, HEAD_DIM, W), lambda i: (blk_of(front(i)), head_of(front(i)), 0, 0)),
            pl.BlockSpec((1, GATE_ROWS, PAIR * Q_BLOCK), lambda i: (head_of(front(i)), 0, blk_of(front(i)))),
            pl.BlockSpec((1, nc, 2 * HEAD_DIM), lambda i: (seq_of(front(i)), 0, 0)),
            pl.BlockSpec((1, HEAD_DIM + LANES, nc), lambda i: (seq_of(front(i)), 0, 0)),
            rows(back), cols(back), rows(front), cols(front),
        ],
        out_specs=pl.BlockSpec((PAIR * Q_BLOCK, gd), lambda i: (blk_of(done(i)), head_of(done(i)))),
        scratch_shapes=[pltpu.VMEM((PAIR, 8, W), _F32), pltpu.VMEM((PAIR, VT_ROWS, W), _F32),
                        pltpu.VMEM((2, PAIR, 2 * HEAD_DIM, W), _BF), pltpu.VMEM((3, PAIR, HEAD_DIM, W), _F32),
                        pltpu.VMEM((3, PAIR, 8, W), _F32),
                        pltpu.SMEM((2 * PAIR * (LANES // BITS_PER_WORD + 1),), jnp.int32)],
        compiler_params=pltpu.CompilerParams(
            dimension_semantics=("arbitrary",), vmem_limit_bytes=VMEM_LIMIT),
        name="nsa_attention",
    )(qt, gt, kca, vct, ksa, vst, kwa, vwt)


def _merge_ffn_kernel(x_ref, gm_ref, yn_ref, wm1_ref, wpn_ref, wo_ref, g1_ref, b1_ref,
                      w1_ref, w2_ref, g2_ref, b2_ref, o_ref):
    half = x_ref.shape[0] // FFN_SPLIT
    parts = [slice(p * half, (p + 1) * half) for p in range(FFN_SPLIT)]
    x = [x_ref[rs, :] for rs in parts]
    gate = [_sigmoid(_dot(xp.astype(_BF), wm1_ref[...])) for xp in x]
    merged = [gm_ref[rs, :] + g * _dot(yn_ref[rs, :], wpn_ref[...]) for rs, g in zip(parts, gate)]
    mix = [_dot(mp.astype(_BF), wo_ref[...]) for mp in merged]
    hid = [_layer_norm(DEEPNORM_ALPHA * xp + mp, g1_ref[...], b1_ref[...]) for xp, mp in zip(x, mix)]
    hb = [hp.astype(_BF) for hp in hid]
    f = [None] * FFN_SPLIT
    for c in range(D_FF // D_MODEL):
        a = [jnp.maximum(_dot(hp, w1_ref[:, c * D_MODEL:(c + 1) * D_MODEL]), 0.0) for hp in hb]
        for p, ap in enumerate(a):
            fc = _dot((ap * ap).astype(_BF), w2_ref[c * D_MODEL:(c + 1) * D_MODEL, :])
            f[p] = fc if f[p] is None else f[p] + fc
    for rs, hp, fp in zip(parts, hid, f):
        o_ref[rs, :] = _layer_norm(DEEPNORM_ALPHA * hp + fp, g2_ref[...], b2_ref[...])


def _merge_ffn(x2, gm, yn, wm1, wpn, wo, g1, b1, w1, w2, g2, b2):
    T = x2.shape[0]
    tm = ROW_TILE
    rows = lambda w: pl.BlockSpec((tm, w), lambda i: (i, 0))
    const = lambda r, c: _resident((r, c), lambda i: (0, 0))
    return pl.pallas_call(
        _merge_ffn_kernel,
        out_shape=jax.ShapeDtypeStruct((T, D_MODEL), _F32),
        grid=(T // tm,),
        in_specs=[rows(D_MODEL), rows(D_MODEL), rows(D_MODEL),
                  const(D_MODEL, D_MODEL), const(D_MODEL, D_MODEL), const(D_MODEL, D_MODEL),
                  const(1, D_MODEL), const(1, D_MODEL),
                  const(D_MODEL, D_FF), const(D_FF, D_MODEL),
                  const(1, D_MODEL), const(1, D_MODEL)],
        out_specs=rows(D_MODEL),
        compiler_params=pltpu.CompilerParams(
            dimension_semantics=("arbitrary",), vmem_limit_bytes=VMEM_LIMIT),
        name="merge_ffn",
    )(x2, gm, yn, wm1, wpn, wo, g1, b1, w1, w2, g2, b2)


def _layer(x2, B, S, w_in, layer, gm_ln_g, gm_ln_b, gm_w_s, gm_b_s, cmp_pe_k, cmp_w1_k, cmp_w2_k,
           cmp_pe_v, cmp_w1_v, cmp_w2_v, w_proj_gm, w_proj_nsa, w_out,
           ln1_g, ln1_b, w_ff1, w_ff2, ln2_g, ln2_b):
    o_q = 2 * D_MODEL
    o_m = o_q + (Q_PER_KV + 6) * N_KV_HEADS * HEAD_DIM + 3 * Q_PER_KV * N_KV_HEADS
    row = lambda v: v.reshape(1, -1)
    merge_cols = lambda j: w_in[layer * D_MODEL:(layer + 1) * D_MODEL, o_m + j * D_MODEL:o_m + (j + 1) * D_MODEL]

    gm = _gm_mixer(x2, w_in, layer, merge_cols(0), row(gm_ln_g), row(gm_ln_b), gm_w_s,
                   jnp.broadcast_to(gm_b_s[:, :, None], (GM_GROUPS, GM_CHUNK, LANES)), w_proj_gm)

    qt, kc, vc, ksa, vst, kwa, vwt, gt = _qkv_proj(x2, w_in, layer, o_q, B, S)

    heads = lambda a: a.reshape(B * N_KV_HEADS, S, HEAD_DIM)
    pairs = lambda w: w.astype(_BF).reshape(CMP_BLOCK // 2, 2 * HEAD_DIM, HEAD_DIM)
    kca, vct = _nsa_compress(heads(kc), heads(vc), cmp_pe_k, cmp_pe_v, pairs(cmp_w1_k), pairs(cmp_w1_v),
                             cmp_w2_k.astype(_BF), cmp_w2_v.astype(_BF))

    yn = _nsa_attention(qt, gt, kca, vct, ksa, vst, kwa, vwt, B, S)

    return _merge_ffn(x2, gm, yn, merge_cols(1).astype(_BF), w_proj_nsa.astype(_BF),
                      w_out.astype(_BF), row(ln1_g), row(ln1_b), w_ff1.astype(_BF), w_ff2.astype(_BF),
                      row(ln2_g), row(ln2_b))


def kernel(x, w_in, gm_ln_g, gm_ln_b, gm_w_s, gm_b_s, cmp_pe_k, cmp_w1_k, cmp_w2_k, cmp_pe_v, cmp_w1_v, cmp_w2_v, w_proj_gm, w_proj_nsa, w_out, ln1_g, ln1_b, w_ff1, w_ff2, ln2_g, ln2_b):
    B, S, D = x.shape
    assert D == D_MODEL and S % ROW_TILE == 0 and WINDOW + Q_BLOCK <= S <= SEL_BLOCK * LANES
    assert S // POS_RADIX <= POS_RADIX and S % (PAIR * Q_BLOCK) == 0
    h = x.reshape(B * S, D)
    depth, _, d_in = w_in.shape
    w_rows = w_in.reshape(depth * D, d_in)
    for l in range(depth):
        h = _layer(h, B, S, w_rows, l, gm_ln_g[l], gm_ln_b[l], gm_w_s[l], gm_b_s[l],
                   cmp_pe_k[l], cmp_w1_k[l], cmp_w2_k[l], cmp_pe_v[l], cmp_w1_v[l], cmp_w2_v[l],
                   w_proj_gm[l], w_proj_nsa[l], w_out[l], ln1_g[l], ln1_b[l],
                   w_ff1[l], w_ff2[l], ln2_g[l], ln2_b[l])
    return h.reshape(B, S, D)
```

```python
import functools
import math

import jax
import jax.numpy as jnp
from jax import lax
from jax.experimental import pallas as pl
from jax.experimental.pallas import tpu as pltpu

D_MODEL = 1024
GM_GROUPS = 8
GM_CHUNK = 128
N_KV_HEADS = 2
Q_PER_KV = 4
HEAD_DIM = 128
CMP_BLOCK = 32
CMP_STRIDE = 16
SEL_BLOCK = 64
N_SELECT = 16
N_LOCAL_FORCED = 2
WINDOW = 512
Q_BLOCK = 128
D_FF = 4 * D_MODEL
DEEPNORM_ALPHA = 2.0 ** 0.25
LN_EPS = 1e-5
NEG_INF = -1e30
LOG2_E = math.log2(math.e)

LANES = 128
SEL_TILE = 128
LOCAL_TILES = 10
PAIR = 2
PHASE_ORDER = tuple((phase, u) for phase in ("finish", "front_scores", "back_scores", "front_compressed",
                                              "front_window_scores", "back_sums", "front_window_sums",
                                              "front_select") for u in range(PAIR))
BITS_PER_WORD = 16
POS_RADIX = 256
GM_SPLIT = 2
FFN_SPLIT = 2
ROW_TILE = 512
VT_ROWS = HEAD_DIM + 16
GATE_ROWS = 16
VMEM_LIMIT = 56 * 1024 * 1024

_BF = jnp.bfloat16
_F32 = jnp.float32


def _dot(a, b):
    return jnp.dot(a, b, preferred_element_type=_F32)


def _gelu(x):
    c = math.sqrt(2.0 / math.pi)
    return 0.5 * x * (1.0 + jnp.tanh(c * (x + 0.044715 * (x * x * x))))


def _sigmoid(x):
    return 1.0 / (1.0 + jnp.exp(-x))


def _layer_norm(x, g, b):
    mu = jnp.mean(x, axis=-1, keepdims=True)
    xc = x - mu
    var = jnp.mean(xc * xc, axis=-1, keepdims=True)
    return xc * lax.rsqrt(var + LN_EPS) * g + b


def _position_lanes(pos, shape):
    lane = lax.broadcasted_iota(jnp.int32, shape, 1)
    return jnp.where(lane == 0, (pos % POS_RADIX).astype(_F32),
                     jnp.where(lane == 1, (pos // POS_RADIX).astype(_F32), 0.0)).astype(_BF)


def _stage_bf16(first_step, pairs):
    @pl.when(first_step)
    def _():
        for src, dst in pairs:
            dst[...] = src[...].astype(_BF)


def _resident(shape, index_map):
    return pl.BlockSpec(shape, index_map, pipeline_mode=pl.Buffered(1))


def _gm_kernel(x_ref, wgm32_ref, wm032_ref, lng_ref, lnb_ref, ws_ref, bs_ref, wpg32_ref, o_ref,
               vg_ref, wgm_ref, wm0_ref, wpg_ref):
    _stage_bf16(pl.program_id(0) == 0, [(wgm32_ref, wgm_ref), (wm032_ref, wm0_ref), (wpg32_ref, wpg_ref)])
    tm = x_ref.shape[0]
    rows = tm // GM_SPLIT
    parts = [slice(p * rows, (p + 1) * rows) for p in range(GM_SPLIT)]
    xb = [x_ref[rs, :].astype(_BF) for rs in parts]
    z = [_gelu(_dot(xp, wgm_ref[...])) for xp in xb]
    gate_logits = [_dot(xp, wm0_ref[...]) for xp in xb]
    v = [_layer_norm(zp[:, D_MODEL:], lng_ref[...], lnb_ref[...]).astype(_BF) for zp in z]
    row = lax.broadcasted_iota(jnp.int32, (GM_CHUNK, GM_CHUNK), 0)
    col = lax.broadcasted_iota(jnp.int32, (GM_CHUNK, GM_CHUNK), 1)
    w = [jnp.where(row >= col, ws_ref[gi], 0.0).astype(_BF) for gi in range(GM_GROUPS)]
    for p, rs in enumerate(parts):
        for gi in range(GM_GROUPS):
            for c in range(rows // GM_CHUNK):
                blk = v[p][c * GM_CHUNK:(c + 1) * GM_CHUNK, gi * LANES:(gi + 1) * LANES]
                r0 = p * rows + c * GM_CHUNK
                vg_ref[r0:r0 + GM_CHUNK, gi * LANES:(gi + 1) * LANES] = _dot(w[gi], blk) + bs_ref[gi]
    for p, rs in enumerate(parts):
        y = (z[p][:, :D_MODEL] * vg_ref[rs, :]).astype(_BF)
        o_ref[rs, :] = _sigmoid(gate_logits[p]) * _dot(y, wpg_ref[...])


def _gm_mixer(x2, w_in, layer, wm0, lng, lnb, ws, bs, wpg):
    T = x2.shape[0]
    tm = ROW_TILE
    const2 = lambda i: (0, 0)
    const3 = lambda i: (0, 0, 0)
    return pl.pallas_call(
        _gm_kernel,
        out_shape=jax.ShapeDtypeStruct((T, D_MODEL), _F32),
        grid=(T // tm,),
        in_specs=[
            pl.BlockSpec((tm, D_MODEL), lambda i: (i, 0)),
            _resident((D_MODEL, 2 * D_MODEL), lambda i: (layer, 0)),
            _resident((D_MODEL, D_MODEL), const2),
            pl.BlockSpec((1, D_MODEL), const2),
            pl.BlockSpec((1, D_MODEL), const2),
            pl.BlockSpec((GM_GROUPS, GM_CHUNK, GM_CHUNK), const3),
            pl.BlockSpec((GM_GROUPS, GM_CHUNK, LANES), const3),
            _resident((D_MODEL, D_MODEL), const2),
        ],
        out_specs=pl.BlockSpec((tm, D_MODEL), lambda i: (i, 0)),
        scratch_shapes=[pltpu.VMEM((tm, D_MODEL), _F32), pltpu.VMEM((D_MODEL, 2 * D_MODEL), _BF),
                        pltpu.VMEM((D_MODEL, D_MODEL), _BF), pltpu.VMEM((D_MODEL, D_MODEL), _BF)],
        compiler_params=pltpu.CompilerParams(
            dimension_semantics=("arbitrary",), vmem_limit_bytes=VMEM_LIMIT),
        name="gm_mixer",
    )(x2, w_in, wm0, lng, lnb, ws, bs, wpg)


def _qkv_kernel(x_ref, wq32_ref, wkv32_ref, wg32_ref, qt_ref, kc_ref, vc_ref, ksa_ref, vst_ref, kwa_ref, vwt_ref,
                gt_ref, wq_ref, wkv_ref, wg_ref):
    _stage_bf16((pl.program_id(0) == 0) & (pl.program_id(1) == 0),
                [(wq32_ref, wq_ref), (wkv32_ref, wkv_ref), (wg32_ref, wg_ref)])
    tm = x_ref.shape[0]
    xb = x_ref[...].astype(_BF)
    zq = _dot(xb, wq_ref[...]) * (HEAD_DIM ** -0.5 * LOG2_E)
    for tb in range(tm // Q_BLOCK):
        for h in range(N_KV_HEADS):
            for g in range(Q_PER_KV):
                c0 = (h * Q_PER_KV + g) * HEAD_DIM
                blk = zq[tb * Q_BLOCK:(tb + 1) * Q_BLOCK, c0:c0 + HEAD_DIM]
                qt_ref[tb, h, :, g * Q_BLOCK:(g + 1) * Q_BLOCK] = blk.T.astype(_BF)
    z = _dot(xb, wkv_ref[...])
    kpos = pl.program_id(1) * tm + lax.broadcasted_iota(jnp.int32, (tm, LANES), 0)
    blk_lane = lax.broadcasted_iota(jnp.int32, (tm, LANES), 1)
    sel_lanes = jnp.where(blk_lane == 0, (kpos % SEL_TILE).astype(_F32),
                          jnp.where(kpos // SEL_BLOCK == blk_lane, NEG_INF, 0.0)).astype(_BF)
    win_lanes = _position_lanes(kpos, (tm, LANES))
    ones = jnp.ones((VT_ROWS - HEAD_DIM, tm), _BF)
    for h in range(N_KV_HEADS):
        def col(j):
            return z[:, j * 2 * HEAD_DIM + h * HEAD_DIM: j * 2 * HEAD_DIM + (h + 1) * HEAD_DIM]
        kc_ref[0, h] = col(0)
        vc_ref[0, h] = col(1)
        ksa_ref[0, h, :, :HEAD_DIM] = col(2).astype(_BF)
        ksa_ref[0, h, :, HEAD_DIM:] = sel_lanes
        vst_ref[0, h, :HEAD_DIM] = col(3).T.astype(_BF)
        vst_ref[0, h, HEAD_DIM:] = ones
        kwa_ref[0, h, :, :HEAD_DIM] = col(4).astype(_BF)
        kwa_ref[0, h, :, HEAD_DIM:] = win_lanes
        vwt_ref[0, h, :HEAD_DIM] = col(5).T.astype(_BF)
        vwt_ref[0, h, HEAD_DIM:] = ones
    zg_t = _sigmoid(_dot(xb, wg_ref[...])).T
    per_head = 3 * Q_PER_KV
    for h in range(N_KV_HEADS):
        gt_ref[h] = zg_t[h * per_head:h * per_head + GATE_ROWS]


def _qkv_proj(x2, w_in, layer, q_col, B, S):
    T = x2.shape[0]
    tm = ROW_TILE
    nsb = S // tm
    gd = Q_PER_KV * Q_BLOCK
    aw = N_KV_HEADS * Q_PER_KV * HEAD_DIM
    kvw = 6 * N_KV_HEADS * HEAD_DIM
    kv_col = q_col + aw
    gate_col = kv_col + kvw
    assert q_col % aw == 0 and kv_col % kvw == 0 and gate_col % LANES == 0
    rows_spec = lambda w: pl.BlockSpec((1, N_KV_HEADS, tm, w), lambda b, s: (b, 0, s, 0))
    rows_shape = lambda w, dt: jax.ShapeDtypeStruct((B, N_KV_HEADS, S, w), dt)
    cols_spec = pl.BlockSpec((1, N_KV_HEADS, VT_ROWS, tm), lambda b, s: (b, 0, 0, s))
    cols_shape = jax.ShapeDtypeStruct((B, N_KV_HEADS, VT_ROWS, S), _BF)
    return pl.pallas_call(
        _qkv_kernel,
        out_shape=(
            jax.ShapeDtypeStruct((T // Q_BLOCK, N_KV_HEADS, HEAD_DIM, gd), _BF),
            rows_shape(HEAD_DIM, _F32), rows_shape(HEAD_DIM, _F32),
            rows_shape(2 * HEAD_DIM, _BF), cols_shape,
            rows_shape(2 * HEAD_DIM, _BF), cols_shape,
            jax.ShapeDtypeStruct((N_KV_HEADS, GATE_ROWS, T), _F32),
        ),
        grid=(B, nsb),
        in_specs=[
            pl.BlockSpec((tm, D_MODEL), lambda b, s: (b * nsb + s, 0)),
            _resident((D_MODEL, aw), lambda b, s: (layer, q_col // aw)),
            _resident((D_MODEL, kvw), lambda b, s: (layer, kv_col // kvw)),
            _resident((D_MODEL, LANES), lambda b, s: (layer, gate_col // LANES)),
        ],
        out_specs=(
            pl.BlockSpec((tm // Q_BLOCK, N_KV_HEADS, HEAD_DIM, gd), lambda b, s: (b * nsb + s, 0, 0, 0)),
            rows_spec(HEAD_DIM), rows_spec(HEAD_DIM), rows_spec(2 * HEAD_DIM), cols_spec,
            rows_spec(2 * HEAD_DIM), cols_spec,
            pl.BlockSpec((N_KV_HEADS, GATE_ROWS, tm), lambda b, s: (0, 0, b * nsb + s)),
        ),
        scratch_shapes=[pltpu.VMEM((D_MODEL, aw), _BF), pltpu.VMEM((D_MODEL, kvw), _BF),
                        pltpu.VMEM((D_MODEL, LANES), _BF)],
        compiler_params=pltpu.CompilerParams(
            dimension_semantics=("arbitrary", "arbitrary"), vmem_limit_bytes=VMEM_LIMIT),
        name="qkv_proj",
    )(x2, w_in, w_in, w_in)


def _compress_kernel(kc_ref, vc_ref, pek_ref, pev_ref, w1k_ref, w1v_ref, w2k_ref, w2v_ref, ko_ref, vo_ref):
    nc = ko_ref.shape[1]

    def tokens(src, pe, w1, w2):
        first, second = None, None
        for p in range(0, CMP_STRIDE, 2):
            rows = [src[0, pl.ds(p + d, nc, stride=CMP_STRIDE), :] for d in range(2)]
            lo = jnp.concatenate([rows[d] + pe[p + d:p + d + 1, :] for d in range(2)], axis=1)
            hi = jnp.concatenate([rows[d] + pe[CMP_STRIDE + p + d:CMP_STRIDE + p + d + 1, :] for d in range(2)], axis=1)
            a = _dot(lo.astype(_BF), w1[p // 2])
            b = _dot(hi.astype(_BF), w1[(CMP_STRIDE + p) // 2])
            first = a if first is None else first + a
            second = b if second is None else second + b
        pre = first + pltpu.roll(second, nc - 1, 0)
        return _dot(_gelu(pre).astype(_BF), w2[...])

    ko_ref[0, :, :HEAD_DIM] = tokens(kc_ref, pek_ref, w1k_ref, w2k_ref).astype(_BF)
    start = lax.broadcasted_iota(jnp.int32, (nc, LANES), 0) * CMP_STRIDE
    ko_ref[0, :, HEAD_DIM:] = _position_lanes(start, (nc, LANES))
    vo_ref[0, :HEAD_DIM] = tokens(vc_ref, pev_ref, w1v_ref, w2v_ref).T.astype(_BF)
    mi = lax.broadcasted_iota(jnp.int32, (LANES, nc), 0)
    ni = lax.broadcasted_iota(jnp.int32, (LANES, nc), 1)
    vo_ref[0, HEAD_DIM:] = jnp.where((ni * CMP_STRIDE + (CMP_BLOCK - 1) >= mi * SEL_BLOCK)
                                     & (ni * CMP_STRIDE <= mi * SEL_BLOCK + (SEL_BLOCK - 1)), 1.0, 0.0).astype(_BF)


def _nsa_compress(kc, vc, pek, pev, w1k, w1v, w2k, w2v):
    BH, S, _ = kc.shape
    nc = S // CMP_STRIDE
    pair = 2 * HEAD_DIM
    const2 = lambda i: (0, 0)
    const3 = lambda i: (0, 0, 0)
    seq_rows = pl.BlockSpec((1, S, HEAD_DIM), lambda i: (i, 0, 0))
    pe_spec = pl.BlockSpec((CMP_BLOCK, HEAD_DIM), const2)
    w1_spec = pl.BlockSpec((CMP_BLOCK // 2, pair, HEAD_DIM), const3)
    w2_spec = pl.BlockSpec((HEAD_DIM, HEAD_DIM), const2)
    return pl.pallas_call(
        _compress_kernel,
        out_shape=(jax.ShapeDtypeStruct((BH, nc, 2 * HEAD_DIM), _BF),
                   jax.ShapeDtypeStruct((BH, HEAD_DIM + LANES, nc), _BF)),
        grid=(BH,),
        in_specs=[seq_rows, seq_rows, pe_spec, pe_spec, w1_spec, w1_spec, w2_spec, w2_spec],
        out_specs=(pl.BlockSpec((1, nc, 2 * HEAD_DIM), lambda i: (i, 0, 0)),
                   pl.BlockSpec((1, HEAD_DIM + LANES, nc), lambda i: (i, 0, 0))),
        compiler_params=pltpu.CompilerParams(
            dimension_semantics=("arbitrary",), vmem_limit_bytes=VMEM_LIMIT),
        name="nsa_compress",
    )(kc, vc, pek, pev, w1k, w1v, w2k, w2v)


def _mark_top_blocks(score, notsel, rounds):
    rows = lax.broadcasted_iota(jnp.int32, score.shape, 0).astype(_F32)
    for _ in range(rounds):
        mx = jnp.max(score, axis=0, keepdims=True)
        idx = jnp.min(jnp.where(score == mx, rows, float(LANES)), axis=0, keepdims=True)
        hit = rows == idx
        notsel = jnp.where(hit, 0.0, notsel)
        score = jnp.where(hit, -jnp.inf, score)
    return notsel


def _attn_kernel(qt_ref, gt_ref, kca_ref, vct_ref, ksa_ref, vst_ref, kwa_ref, vwt_ref, o_ref,
                 m_ref, acc_ref, lhs_ref, part_ref, gate_ref, bits_ref, *, seq, n_seq):
    step = pl.program_id(0)
    n_pair = seq // (Q_BLOCK * PAIR)
    last = n_seq * n_pair - 1
    front_pair = jnp.minimum(step, last)
    back_pair = jnp.clip(step - 1, 0, last)
    nc = kca_ref.shape[1]
    n_sel = seq // SEL_BLOCK
    n_forced = 1 + N_LOCAL_FORCED
    k_top = min(N_SELECT, n_sel)
    G = Q_PER_KV
    W = G * Q_BLOCK
    words = LANES // BITS_PER_WORD
    blocks_per_tile = SEL_TILE // SEL_BLOCK
    tiles_per_word = BITS_PER_WORD // blocks_per_tile
    put = step % 2
    get = 1 - put
    put3 = step % 3
    get3 = (step + 1) % 3
    units = range(PAIR)

    def flag_base(slot, u):
        return (slot * PAIR + u) * (words + 1)

    @pl.when(step == 0)
    def _():
        lhs_ref[1] = jnp.zeros(lhs_ref.shape[1:], _BF)
        part_ref[...] = jnp.zeros(part_ref.shape, _F32)
        gate_ref[...] = jnp.zeros(gate_ref.shape, _F32)
        acc_ref[...] = jnp.ones(acc_ref.shape, _F32)
        for k in range(PAIR * (words + 1)):
            bits_ref[flag_base(1, 0) + k] = 0

    lane_w = lax.broadcasted_iota(jnp.int32, (1, W), 1)

    def head_slopes(pair_index):
        h = (pair_index // n_pair) % N_KV_HEADS
        slope = jnp.zeros((1, W), _F32)
        for g in range(G):
            sg = jnp.where(h == 0, _F32(2.0 ** -(g + 1)), _F32(2.0 ** -(G + g + 1)))
            slope = jnp.where(lane_w // Q_BLOCK == g, sg, slope)
        return (slope * LOG2_E).astype(_BF).astype(_F32)

    slope = head_slopes(front_pair)
    slope_b = head_slopes(back_pair)
    strips =[slice(g * Q_BLOCK, (g + 1) * Q_BLOCK) for g in range(G)]
    row_aug = lax.broadcasted_iota(jnp.int32, (LANES, W), 0)
    key_row = lax.broadcasted_iota(jnp.int32, (SEL_TILE, Q_BLOCK), 0)
    pos_rows = jnp.where(row_aug == 0, slope, jnp.where(row_aug == 1, slope * POS_RADIX, 0.0)).astype(_BF)

    def finish(u):
        out_t = (part_ref[get3, u] + acc_ref[u, 0:HEAD_DIM]
                 * (gate_ref[get3, u][0:1] * (1.0 / acc_ref[u, HEAD_DIM:HEAD_DIM + 1])))
        for g, cs in enumerate(strips):
            o_ref[u * Q_BLOCK:(u + 1) * Q_BLOCK, g * HEAD_DIM:(g + 1) * HEAD_DIM] = out_t[:, cs].T.astype(_BF)

    front = [{} for _ in units]

    def front_scores(u):
        f = front[u]
        f["q0"] =(front_pair % n_pair * PAIR + u) * Q_BLOCK
        f["tok"] = f["q0"] + lane_w % Q_BLOCK
        f["qt"] = qt_ref[u, 0]
        f["s"] = _dot(kca_ref[0], jnp.concatenate([f["qt"], pos_rows], axis=0))

    def tile_scores(bk, j, causal, live=None):
        k0 = j * SEL_TILE if isinstance(j, int) else pl.multiple_of(j * SEL_TILE, SEL_TILE)
        st = _dot(ksa_ref[0, 0, pl.ds(k0, SEL_TILE), :], bk["lhs"])
        if causal:
            ahead = jnp.where(k0 + key_row > bk["tok"][:, :Q_BLOCK], NEG_INF, 0.0)
            st = jnp.concatenate([st[:, cs] + ahead for cs in strips], axis=1)
        shift = slope_b * (bk["p0"] - k0).astype(_F32)
        if live is not None:
            shift = jnp.where(live, shift, -NEG_INF)
        return st, shift, k0

    def tile_max(tiles):
        m = None
        for st, shift, _ in tiles:
            cm = jnp.max(st, axis=0, keepdims=True) - shift
            m = cm if m is None else jnp.maximum(m, cm)
        return m

    def tile_sums(tiles, m):
        acc = None
        for st, shift, k0 in tiles:
            sub = m + shift
            pt = jnp.concatenate([jnp.exp2(st[:, cs] - sub[:, cs]) for cs in strips], axis=1)
            ai = _dot(vst_ref[0, 0, :, pl.ds(k0, SEL_TILE)], pt.astype(_BF))
            acc = ai if acc is None else acc + ai
        return acc

    back = [{} for _ in units]

    def back_scores(u):
        bk = back[u]
        bk["p0"] =(back_pair % n_pair * PAIR + u) * Q_BLOCK
        bk["tok"] = bk["p0"] + lane_w % Q_BLOCK
        bk["lhs"] = lhs_ref[get, u]
        bk["first_local"] = bk["p0"] // SEL_TILE - (LOCAL_TILES - 1)
        bk["tiles"] = [tile_scores(bk, 0, False, live=bk["first_local"] > 0)]
        for i in range(LOCAL_TILES):
            j = bk["first_local"] + i
            last = i == LOCAL_TILES - 1
            bk["tiles"].append(tile_scores(bk, jnp.maximum(j, 0), last, live=None if last else j >= 0))

    n_row = lax.broadcasted_iota(jnp.int32, (nc, Q_BLOCK), 0)

    def front_compressed(u):
        f = front[u]
        n_last =(f["tok"] - (CMP_BLOCK - 1)) // CMP_STRIDE
        mask_c = jnp.where(n_row <= n_last[:, :Q_BLOCK], 0.0, NEG_INF)
        e_parts, inv_parts = [], []
        for cs in strips:
            sg = f["s"][:, cs] + mask_c
            eg = jnp.exp2(sg - jnp.max(sg, axis=0, keepdims=True))
            e_parts.append(eg.astype(_BF))
            inv_parts.append(jnp.where(n_last[:, cs] >= 0, 1.0 / jnp.sum(eg, axis=0, keepdims=True), 0.0))
        both = _dot(vct_ref[0], jnp.concatenate(e_parts, axis=1)) * jnp.concatenate(inv_parts, axis=1)
        f["o_cmp"] = both[:HEAD_DIM]
        imp_t = both[HEAD_DIM:, strips[0]]
        for cs in strips[1:]:
            imp_t = imp_t + both[HEAD_DIM:, cs]
        f["imp_t"] = imp_t

    wlen = WINDOW + Q_BLOCK

    def front_window_scores(u):
        f = front[u]
        anchor =jnp.concatenate([jnp.minimum(f["imp_t"][0:1], 0.0)] * G, axis=1)
        win_rows = jnp.where(row_aug == 0, slope + anchor,
                             jnp.where(row_aug == 1, slope * POS_RADIX, 0.0)).astype(_BF)
        f["w0"] = pl.multiple_of(jnp.maximum(f["q0"] - WINDOW, 0), Q_BLOCK)
        f["sw"] = _dot(kwa_ref[0, 0, pl.ds(f["w0"], wlen), :], jnp.concatenate([f["qt"], win_rows], axis=0))

    def back_sums(u):
        bk = back[u]
        m = tile_max(bk["tiles"])
        m_ref[u, 0:1] = m
        acc_ref[u] = tile_sums(bk["tiles"], m)

    def front_window_sums(u):
        f = front[u]
        w_row =f["w0"] + lax.broadcasted_iota(jnp.int32, (wlen, Q_BLOCK), 0)
        dist = f["tok"][:, :Q_BLOCK] - w_row
        mask_w = jnp.where(dist.astype(jnp.uint32) < WINDOW, 0.0, NEG_INF)
        e_parts = []
        for cs in strips:
            sg = f["sw"][:, cs] + mask_w
            e_parts.append(jnp.exp2(sg - jnp.max(sg, axis=0, keepdims=True)).astype(_BF))
        win = _dot(vwt_ref[0, 0, :, pl.ds(f["w0"], wlen)], jnp.concatenate(e_parts, axis=1))
        f["o_win"] = win[:HEAD_DIM] * (1.0 / win[HEAD_DIM:HEAD_DIM + 1])

    mrow = lax.broadcasted_iota(jnp.int32, (LANES, Q_BLOCK), 0)
    blk_row = lax.broadcasted_iota(jnp.int32, (LANES, 1), 0)
    blk_bit = jnp.left_shift(1, blk_row % BITS_PER_WORD).astype(_F32)

    def front_select(u):
        f = front[u]
        gates =gt_ref[0][:, u * Q_BLOCK:(u + 1) * Q_BLOCK]

        def gate_row(branch):
            return jnp.concatenate([gates[3 * g + branch:3 * g + branch + 1] for g in range(G)], axis=1)

        tcol = lax.broadcasted_iota(jnp.int32, (LANES, Q_BLOCK), 1) + f["q0"]
        lag = tcol // SEL_BLOCK - mrow
        forced = (mrow == 0) | ((lag >= 0) & (lag < N_LOCAL_FORCED))
        score = jnp.where(forced | (lag < 0) | (mrow >= n_sel), -jnp.inf, f["imp_t"])
        notsel_t = _mark_top_blocks(score, jnp.where(forced, 0.0, 1.0), k_top - n_forced)

        sel_rows = jnp.where(row_aug == 0, slope, jnp.concatenate([notsel_t] * G, axis=1)).astype(_BF)
        lhs_ref[put, u] = jnp.concatenate([f["qt"], sel_rows], axis=0)
        part_ref[put3, u] = gate_row(0) * f["o_cmp"] + gate_row(2) * f["o_win"]
        gate_ref[put3, u] = jnp.broadcast_to(gate_row(1), gate_ref.shape[2:])

        far_end = (f["q0"] // SEL_TILE - (LOCAL_TILES - 1)) * blocks_per_tile
        blk_on = jnp.where((jnp.min(notsel_t, axis=1, keepdims=True) < 0.5)
                           & (blk_row >= blocks_per_tile) & (blk_row < far_end), 1.0, 0.0)
        for k in range(words):
            word = jnp.sum((blk_on * blk_bit)[k * BITS_PER_WORD:(k + 1) * BITS_PER_WORD])
            bits_ref[flag_base(put, u) + k] = word.astype(jnp.int32)
        bits_ref[flag_base(put, u) + words] = jnp.sum(blk_on).astype(jnp.int32)

    phases = {"finish": finish, "front_scores": front_scores, "back_scores": back_scores,
              "front_compressed": front_compressed, "front_window_scores": front_window_scores,
              "back_sums": back_sums, "front_window_sums": front_window_sums, "front_select": front_select}
    for phase, u in PHASE_ORDER:
        phases[phase](u)

    for u, bk in zip(units, back):
        def far_tile(j, carry, u=u, bk=bk):
            word = bits_ref[flag_base(get, u) + j // tiles_per_word]
            tile_bits = (word >> ((j % tiles_per_word) * blocks_per_tile)) & ((1 << blocks_per_tile) - 1)

            @pl.when(tile_bits != 0)
            def _():
                tile = [tile_scores(bk, j, False)]
                m_old = m_ref[u, 0:1]
                m_new = jnp.maximum(m_old, tile_max(tile))
                m_ref[u, 0:1] = m_new
                acc_ref[u] = jnp.exp2(m_old - m_new) * acc_ref[u] + tile_sums(tile, m_new)
            return carry

        @pl.when(bits_ref[flag_base(get, u) + words] != 0)
        def _(far_tile=far_tile, bk=bk):
            lax.fori_loop(1, bk["first_local"], far_tile, 0)


def _nsa_attention(qt, gt, kca, vct, ksa, vst, kwa, vwt, B, S):
    n_pair = S // (Q_BLOCK * PAIR)
    T = B * S
    nc = kca.shape[1]
    gd = Q_PER_KV * HEAD_DIM
    W = Q_PER_KV * Q_BLOCK
    n_seq = B * N_KV_HEADS
    last = n_seq * n_pair - 1
    front = lambda i: jnp.minimum(i, last)
    back = lambda i: jnp.clip(i - 1, 0, last)
    done = lambda i: jnp.maximum(i - 2, 0)
    seq_of = lambda p: p // n_pair
    blk_of = lambda p: (seq_of(p) // N_KV_HEADS) * n_pair + p % n_pair
    head_of = lambda p: seq_of(p) % N_KV_HEADS
    rows = lambda stage: pl.BlockSpec((1, 1, S, 2 * HEAD_DIM),
                                      lambda i: (seq_of(stage(i)) // N_KV_HEADS, head_of(stage(i)), 0, 0))
    cols = lambda stage: pl.BlockSpec((1, 1, VT_ROWS, S),
                                      lambda i: (seq_of(stage(i)) // N_KV_HEADS, head_of(stage(i)), 0, 0))
    return pl.pallas_call(
        functools.partial(_attn_kernel, seq=S, n_seq=n_seq),
        out_shape=jax.ShapeDtypeStruct((T, N_KV_HEADS * gd), _BF),
        grid=(last + 3,),
        in_specs=[
            pl.BlockSpec((PAIR, 1, HEAD_DIM, W), lambda i: (blk_of(front(i)), head_of(front(i)), 0, 0)),
            pl.BlockSpec((1, GATE_ROWS, PAIR * Q_BLOCK), lambda i: (head_of(front(i)), 0, blk_of(front(i)))),
            pl.BlockSpec((1, nc, 2 * HEAD_DIM), lambda i: (seq_of(front(i)), 0, 0)),
            pl.BlockSpec((1, HEAD_DIM + LANES, nc), lambda i: (seq_of(front(i)), 0, 0)),
            rows(back), cols(back), rows(front), cols(front),
        ],
        out_specs=pl.BlockSpec((PAIR * Q_BLOCK, gd), lambda i: (blk_of(done(i)), head_of(done(i)))),
        scratch_shapes=[pltpu.VMEM((PAIR, 8, W), _F32), pltpu.VMEM((PAIR, VT_ROWS, W), _F32),
                        pltpu.VMEM((2, PAIR, 2 * HEAD_DIM, W), _BF), pltpu.VMEM((3, PAIR, HEAD_DIM, W), _F32),
                        pltpu.VMEM((3, PAIR, 8, W), _F32),
                        pltpu.SMEM((2 * PAIR * (LANES // BITS_PER_WORD + 1),), jnp.int32)],
        compiler_params=pltpu.CompilerParams(
            dimension_semantics=("arbitrary",), vmem_limit_bytes=VMEM_LIMIT),
        name="nsa_attention",
    )(qt, gt, kca, vct, ksa, vst, kwa, vwt)


def _merge_ffn_kernel(x_ref, gm_ref, yn_ref, wm1_ref, wpn_ref, wo_ref, g1_ref, b1_ref,
                      w1_ref, w2_ref, g2_ref, b2_ref, o_ref):
    half = x_ref.shape[0] // FFN_SPLIT
    parts = [slice(p * half, (p + 1) * half) for p in range(FFN_SPLIT)]
    x = [x_ref[rs, :] for rs in parts]
    gate = [_sigmoid(_dot(xp.astype(_BF), wm1_ref[...])) for xp in x]
    merged = [gm_ref[rs, :] + g * _dot(yn_ref[rs, :], wpn_ref[...]) for rs, g in zip(parts, gate)]
    mix = [_dot(mp.astype(_BF), wo_ref[...]) for mp in merged]
    hid = [_layer_norm(DEEPNORM_ALPHA * xp + mp, g1_ref[...], b1_ref[...]) for xp, mp in zip(x, mix)]
    hb = [hp.astype(_BF) for hp in hid]
    f = [None] * FFN_SPLIT
    for c in range(D_FF // D_MODEL):
        a = [jnp.maximum(_dot(hp, w1_ref[:, c * D_MODEL:(c + 1) * D_MODEL]), 0.0) for hp in hb]
        for p, ap in enumerate(a):
            fc = _dot((ap * ap).astype(_BF), w2_ref[c * D_MODEL:(c + 1) * D_MODEL, :])
            f[p] = fc if f[p] is None else f[p] + fc
    for rs, hp, fp in zip(parts, hid, f):
        o_ref[rs, :] = _layer_norm(DEEPNORM_ALPHA * hp + fp, g2_ref[...], b2_ref[...])


def _merge_ffn(x2, gm, yn, wm1, wpn, wo, g1, b1, w1, w2, g2, b2):
    T = x2.shape[0]
    tm = ROW_TILE
    rows = lambda w: pl.BlockSpec((tm, w), lambda i: (i, 0))
    const = lambda r, c: _resident((r, c), lambda i: (0, 0))
    return pl.pallas_call(
        _merge_ffn_kernel,
        out_shape=jax.ShapeDtypeStruct((T, D_MODEL), _F32),
        grid=(T // tm,),
        in_specs=[rows(D_MODEL), rows(D_MODEL), rows(D_MODEL),
                  const(D_MODEL, D_MODEL), const(D_MODEL, D_MODEL), const(D_MODEL, D_MODEL),
                  const(1, D_MODEL), const(1, D_MODEL),
                  const(D_MODEL, D_FF), const(D_FF, D_MODEL),
                  const(1, D_MODEL), const(1, D_MODEL)],
        out_specs=rows(D_MODEL),
        compiler_params=pltpu.CompilerParams(
            dimension_semantics=("arbitrary",), vmem_limit_bytes=VMEM_LIMIT),
        name="merge_ffn",
    )(x2, gm, yn, wm1, wpn, wo, g1, b1, w1, w2, g2, b2)


def _layer(x2, B, S, w_in, layer, gm_ln_g, gm_ln_b, gm_w_s, gm_b_s, cmp_pe_k, cmp_w1_k, cmp_w2_k,
           cmp_pe_v, cmp_w1_v, cmp_w2_v, w_proj_gm, w_proj_nsa, w_out,
           ln1_g, ln1_b, w_ff1, w_ff2, ln2_g, ln2_b):
    o_q = 2 * D_MODEL
    o_m = o_q + (Q_PER_KV + 6) * N_KV_HEADS * HEAD_DIM + 3 * Q_PER_KV * N_KV_HEADS
    row = lambda v: v.reshape(1, -1)
    merge_cols = lambda j: w_in[layer * D_MODEL:(layer + 1) * D_MODEL, o_m + j * D_MODEL:o_m + (j + 1) * D_MODEL]

    gm = _gm_mixer(x2, w_in, layer, merge_cols(0), row(gm_ln_g), row(gm_ln_b), gm_w_s,
                   jnp.broadcast_to(gm_b_s[:, :, None], (GM_GROUPS, GM_CHUNK, LANES)), w_proj_gm)

    qt, kc, vc, ksa, vst, kwa, vwt, gt = _qkv_proj(x2, w_in, layer, o_q, B, S)

    heads = lambda a: a.reshape(B * N_KV_HEADS, S, HEAD_DIM)
    pairs = lambda w: w.astype(_BF).reshape(CMP_BLOCK // 2, 2 * HEAD_DIM, HEAD_DIM)
    kca, vct = _nsa_compress(heads(kc), heads(vc), cmp_pe_k, cmp_pe_v, pairs(cmp_w1_k), pairs(cmp_w1_v),
                             cmp_w2_k.astype(_BF), cmp_w2_v.astype(_BF))

    yn = _nsa_attention(qt, gt, kca, vct, ksa, vst, kwa, vwt, B, S)

    return _merge_ffn(x2, gm, yn, merge_cols(1).astype(_BF), w_proj_nsa.astype(_BF),
                      w_out.astype(_BF), row(ln1_g), row(ln1_b), w_ff1.astype(_BF), w_ff2.astype(_BF),
                      row(ln2_g), row(ln2_b))


def kernel(x, w_in, gm_ln_g, gm_ln_b, gm_w_s, gm_b_s, cmp_pe_k, cmp_w1_k, cmp_w2_k, cmp_pe_v, cmp_w1_v, cmp_w2_v, w_proj_gm, w_proj_nsa, w_out, ln1_g, ln1_b, w_ff1, w_ff2, ln2_g, ln2_b):
    B, S, D = x.shape
    assert D == D_MODEL and S % ROW_TILE == 0 and WINDOW + Q_BLOCK <= S <= SEL_BLOCK * LANES
    assert S // POS_RADIX <= POS_RADIX and S % (PAIR * Q_BLOCK) == 0
    h = x.reshape(B * S, D)
    depth, _, d_in = w_in.shape
    w_rows = w_in.reshape(depth * D, d_in)
    for l in range(depth):
        h = _layer(h, B, S, w_rows, l, gm_ln_g[l], gm_ln_b[l], gm_w_s[l], gm_b_s[l],
                   cmp_pe_k[l], cmp_w1_k[l], cmp_w2_k[l], cmp_pe_v[l], cmp_w1_v[l], cmp_w2_v[l],
                   w_proj_gm[l], w_proj_nsa[l], w_out[l], ln1_g[l], ln1_b[l],
                   w_ff1[l], w_ff2[l], ln2_g[l], ln2_b[l])
    return h.reshape(B, S, D)
```

```python
import functools
import math

import jax
import jax.numpy as jnp
from jax import lax
from jax.experimental import pallas as pl
from jax.experimental.pallas import tpu as pltpu

D_MODEL = 1024
GM_GROUPS = 8
GM_CHUNK = 128
N_KV_HEADS = 2
Q_PER_KV = 4
HEAD_DIM = 128
CMP_BLOCK = 32
CMP_STRIDE = 16
SEL_BLOCK = 64
N_SELECT = 16
N_LOCAL_FORCED = 2
WINDOW = 512
Q_BLOCK = 128
D_FF = 4 * D_MODEL
DEEPNORM_ALPHA = 2.0 ** 0.25
LN_EPS = 1e-5
NEG_INF = -1e30
LOG2_E = math.log2(math.e)

LANES = 128
SEL_TILE = 128
LOCAL_TILES = 10
PAIR = 4
PHASE_ORDER = tuple((phase, u) for phase in ("finish", "front_scores", "back_scores", "front_compressed",
                                              "front_window_scores", "back_sums", "front_window_sums",
                                              "front_select") for u in range(PAIR))
BITS_PER_WORD = 16
POS_RADIX = 256
GM_SPLIT = 2
FFN_SPLIT = 2
ROW_TILE = 512
VT_ROWS = HEAD_DIM + 16
CMP_ROWS = HEAD_DIM + LANES + 16
TOP_ROWS = 16
GATE_ROWS = 16
VMEM_LIMIT = 56 * 1024 * 1024

_BF = jnp.bfloat16
_F32 = jnp.float32


def _dot(a, b):
    return jnp.dot(a, b, preferred_element_type=_F32)


def _gelu(x):
    c = math.sqrt(2.0 / math.pi)
    return 0.5 * x * (1.0 + jnp.tanh(c * (x + 0.044715 * (x * x * x))))


def _sigmoid(x):
    return 1.0 / (1.0 + jnp.exp(-x))


def _layer_norm(x, g, b):
    mu = jnp.mean(x, axis=-1, keepdims=True)
    xc = x - mu
    var = jnp.mean(xc * xc, axis=-1, keepdims=True)
    return xc * lax.rsqrt(var + LN_EPS) * g + b


def _position_lanes(pos, shape):
    lane = lax.broadcasted_iota(jnp.int32, shape, 1)
    return jnp.where(lane == 0, (pos % POS_RADIX).astype(_F32),
                     jnp.where(lane == 1, (pos // POS_RADIX).astype(_F32), 0.0)).astype(_BF)


def _stage_bf16(first_step, pairs):
    @pl.when(first_step)
    def _():
        for src, dst in pairs:
            dst[...] = src[...].astype(_BF)


def _resident(shape, index_map):
    return pl.BlockSpec(shape, index_map, pipeline_mode=pl.Buffered(1))


def _gm_kernel(x_ref, wgm32_ref, wm032_ref, lng_ref, lnb_ref, ws_ref, bs_ref, wpg32_ref, o_ref,
               vg_ref, wgm_ref, wm0_ref, wpg_ref):
    _stage_bf16(pl.program_id(0) == 0, [(wgm32_ref, wgm_ref), (wm032_ref, wm0_ref), (wpg32_ref, wpg_ref)])
    tm = x_ref.shape[0]
    rows = tm // GM_SPLIT
    parts = [slice(p * rows, (p + 1) * rows) for p in range(GM_SPLIT)]
    xb = [x_ref[rs, :].astype(_BF) for rs in parts]
    z = [_gelu(_dot(xp, wgm_ref[...])) for xp in xb]
    gate_logits = [_dot(xp, wm0_ref[...]) for xp in xb]
    v = [_layer_norm(zp[:, D_MODEL:], lng_ref[...], lnb_ref[...]).astype(_BF) for zp in z]
    row = lax.broadcasted_iota(jnp.int32, (GM_CHUNK, GM_CHUNK), 0)
    col = lax.broadcasted_iota(jnp.int32, (GM_CHUNK, GM_CHUNK), 1)
    w = [jnp.where(row >= col, ws_ref[gi], 0.0).astype(_BF) for gi in range(GM_GROUPS)]
    for p, rs in enumerate(parts):
        for gi in range(GM_GROUPS):
            for c in range(rows // GM_CHUNK):
                blk = v[p][c * GM_CHUNK:(c + 1) * GM_CHUNK, gi * LANES:(gi + 1) * LANES]
                r0 = p * rows + c * GM_CHUNK
                vg_ref[r0:r0 + GM_CHUNK, gi * LANES:(gi + 1) * LANES] = _dot(w[gi], blk) + bs_ref[gi]
    for p, rs in enumerate(parts):
        y = (z[p][:, :D_MODEL] * vg_ref[rs, :]).astype(_BF)
        o_ref[rs, :] = _sigmoid(gate_logits[p]) * _dot(y, wpg_ref[...])


def _gm_mixer(x2, w_in, layer, wm0, lng, lnb, ws, bs, wpg):
    T = x2.shape[0]
    tm = ROW_TILE
    const2 = lambda i: (0, 0)
    const3 = lambda i: (0, 0, 0)
    return pl.pallas_call(
        _gm_kernel,
        out_shape=jax.ShapeDtypeStruct((T, D_MODEL), _F32),
        grid=(T // tm,),
        in_specs=[
            pl.BlockSpec((tm, D_MODEL), lambda i: (i, 0)),
            _resident((D_MODEL, 2 * D_MODEL), lambda i: (layer, 0)),
            _resident((D_MODEL, D_MODEL), const2),
            pl.BlockSpec((1, D_MODEL), const2),
            pl.BlockSpec((1, D_MODEL), const2),
            pl.BlockSpec((GM_GROUPS, GM_CHUNK, GM_CHUNK), const3),
            pl.BlockSpec((GM_GROUPS, GM_CHUNK, LANES), const3),
            _resident((D_MODEL, D_MODEL), const2),
        ],
        out_specs=pl.BlockSpec((tm, D_MODEL), lambda i: (i, 0)),
        scratch_shapes=[pltpu.VMEM((tm, D_MODEL), _F32), pltpu.VMEM((D_MODEL, 2 * D_MODEL), _BF),
                        pltpu.VMEM((D_MODEL, D_MODEL), _BF), pltpu.VMEM((D_MODEL, D_MODEL), _BF)],
        compiler_params=pltpu.CompilerParams(
            dimension_semantics=("arbitrary",), vmem_limit_bytes=VMEM_LIMIT),
        name="gm_mixer",
    )(x2, w_in, wm0, lng, lnb, ws, bs, wpg)


def _qkv_kernel(x_ref, wq32_ref, wkv32_ref, wg32_ref, qt_ref, kc_ref, vc_ref, ksa_ref, vst_ref, kwa_ref, vwt_ref,
                gt_ref, wq_ref, wkv_ref, wg_ref):
    _stage_bf16((pl.program_id(0) == 0) & (pl.program_id(1) == 0),
                [(wq32_ref, wq_ref), (wkv32_ref, wkv_ref), (wg32_ref, wg_ref)])
    tm = x_ref.shape[0]
    xb = x_ref[...].astype(_BF)
    zq = _dot(xb, wq_ref[...]) * (HEAD_DIM ** -0.5 * LOG2_E)
    for tb in range(tm // Q_BLOCK):
        for h in range(N_KV_HEADS):
            for g in range(Q_PER_KV):
                c0 = (h * Q_PER_KV + g) * HEAD_DIM
                blk = zq[tb * Q_BLOCK:(tb + 1) * Q_BLOCK, c0:c0 + HEAD_DIM]
                qt_ref[tb, h, :, g * Q_BLOCK:(g + 1) * Q_BLOCK] = blk.T.astype(_BF)
    z = _dot(xb, wkv_ref[...])
    kpos = pl.program_id(1) * tm + lax.broadcasted_iota(jnp.int32, (tm, LANES), 0)
    blk_lane = lax.broadcasted_iota(jnp.int32, (tm, LANES), 1)
    sel_lanes = jnp.where(blk_lane == 0, (kpos % SEL_TILE).astype(_F32),
                          jnp.where(kpos // SEL_BLOCK == blk_lane, NEG_INF, 0.0)).astype(_BF)
    win_lanes = _position_lanes(kpos, (tm, LANES))
    ones = jnp.ones((VT_ROWS - HEAD_DIM, tm), _BF)
    for h in range(N_KV_HEADS):
        def col(j):
            return z[:, j * 2 * HEAD_DIM + h * HEAD_DIM: j * 2 * HEAD_DIM + (h + 1) * HEAD_DIM]
        kc_ref[0, h] = col(0)
        vc_ref[0, h] = col(1)
        ksa_ref[0, h, :, :HEAD_DIM] = col(2).astype(_BF)
        ksa_ref[0, h, :, HEAD_DIM:] = sel_lanes
        vst_ref[0, h, :HEAD_DIM] = col(3).T.astype(_BF)
        vst_ref[0, h, HEAD_DIM:] = ones
        kwa_ref[0, h, :, :HEAD_DIM] = col(4).astype(_BF)
        kwa_ref[0, h, :, HEAD_DIM:] = win_lanes
        vwt_ref[0, h, :HEAD_DIM] = col(5).T.astype(_BF)
        vwt_ref[0, h, HEAD_DIM:] = ones
    zg_t = _sigmoid(_dot(xb, wg_ref[...])).T
    per_head = 3 * Q_PER_KV
    for h in range(N_KV_HEADS):
        gt_ref[h] = zg_t[h * per_head:h * per_head + GATE_ROWS]


def _qkv_proj(x2, w_in, layer, q_col, B, S):
    T = x2.shape[0]
    tm = ROW_TILE
    nsb = S // tm
    gd = Q_PER_KV * Q_BLOCK
    aw = N_KV_HEADS * Q_PER_KV * HEAD_DIM
    kvw = 6 * N_KV_HEADS * HEAD_DIM
    kv_col = q_col + aw
    gate_col = kv_col + kvw
    assert q_col % aw == 0 and kv_col % kvw == 0 and gate_col % LANES == 0
    rows_spec = lambda w: pl.BlockSpec((1, N_KV_HEADS, tm, w), lambda b, s: (b, 0, s, 0))
    rows_shape = lambda w, dt: jax.ShapeDtypeStruct((B, N_KV_HEADS, S, w), dt)
    cols_spec = pl.BlockSpec((1, N_KV_HEADS, VT_ROWS, tm), lambda b, s: (b, 0, 0, s))
    cols_shape = jax.ShapeDtypeStruct((B, N_KV_HEADS, VT_ROWS, S), _BF)
    return pl.pallas_call(
        _qkv_kernel,
        out_shape=(
            jax.ShapeDtypeStruct((T // Q_BLOCK, N_KV_HEADS, HEAD_DIM, gd), _BF),
            rows_shape(HEAD_DIM, _F32), rows_shape(HEAD_DIM, _F32),
            rows_shape(2 * HEAD_DIM, _BF), cols_shape,
            rows_shape(2 * HEAD_DIM, _BF), cols_shape,
            jax.ShapeDtypeStruct((N_KV_HEADS, GATE_ROWS, T), _F32),
        ),
        grid=(B, nsb),
        in_specs=[
            pl.BlockSpec((tm, D_MODEL), lambda b, s: (b * nsb + s, 0)),
            _resident((D_MODEL, aw), lambda b, s: (layer, q_col // aw)),
            _resident((D_MODEL, kvw), lambda b, s: (layer, kv_col // kvw)),
            _resident((D_MODEL, LANES), lambda b, s: (layer, gate_col // LANES)),
        ],
        out_specs=(
            pl.BlockSpec((tm // Q_BLOCK, N_KV_HEADS, HEAD_DIM, gd), lambda b, s: (b * nsb + s, 0, 0, 0)),
            rows_spec(HEAD_DIM), rows_spec(HEAD_DIM), rows_spec(2 * HEAD_DIM), cols_spec,
            rows_spec(2 * HEAD_DIM), cols_spec,
            pl.BlockSpec((N_KV_HEADS, GATE_ROWS, tm), lambda b, s: (0, 0, b * nsb + s)),
        ),
        scratch_shapes=[pltpu.VMEM((D_MODEL, aw), _BF), pltpu.VMEM((D_MODEL, kvw), _BF),
                        pltpu.VMEM((D_MODEL, LANES), _BF)],
        compiler_params=pltpu.CompilerParams(
            dimension_semantics=("arbitrary", "arbitrary"), vmem_limit_bytes=VMEM_LIMIT),
        name="qkv_proj",
    )(x2, w_in, w_in, w_in)


def _compress_kernel(kc_ref, vc_ref, pek_ref, pev_ref, w1k_ref, w1v_ref, w2k_ref, w2v_ref, ko_ref, vo_ref):
    nc = ko_ref.shape[1]

    def tokens(src, pe, w1, w2):
        first, second = None, None
        for p in range(0, CMP_STRIDE, 2):
            rows = [src[0, pl.ds(p + d, nc, stride=CMP_STRIDE), :] for d in range(2)]
            lo = jnp.concatenate([rows[d] + pe[p + d:p + d + 1, :] for d in range(2)], axis=1)
            hi = jnp.concatenate([rows[d] + pe[CMP_STRIDE + p + d:CMP_STRIDE + p + d + 1, :] for d in range(2)], axis=1)
            a = _dot(lo.astype(_BF), w1[p // 2])
            b = _dot(hi.astype(_BF), w1[(CMP_STRIDE + p) // 2])
            first = a if first is None else first + a
            second = b if second is None else second + b
        pre = first + pltpu.roll(second, nc - 1, 0)
        return _dot(_gelu(pre).astype(_BF), w2[...])

    ko_ref[0, :, :HEAD_DIM] = tokens(kc_ref, pek_ref, w1k_ref, w2k_ref).astype(_BF)
    start = lax.broadcasted_iota(jnp.int32, (nc, LANES), 0) * CMP_STRIDE
    ko_ref[0, :, HEAD_DIM:] = _position_lanes(start, (nc, LANES))
    vo_ref[0, :HEAD_DIM] = tokens(vc_ref, pev_ref, w1v_ref, w2v_ref).T.astype(_BF)
    mi = lax.broadcasted_iota(jnp.int32, (LANES, nc), 0)
    ni = lax.broadcasted_iota(jnp.int32, (LANES, nc), 1)
    overlap_t = jnp.where((ni * CMP_STRIDE + (CMP_BLOCK - 1) >= mi * SEL_BLOCK)
                          & (ni * CMP_STRIDE <= mi * SEL_BLOCK + (SEL_BLOCK - 1)), 1.0, 0.0)
    vo_ref[0, HEAD_DIM:HEAD_DIM + LANES] = overlap_t.astype(_BF)
    vo_ref[0, HEAD_DIM + LANES:] = jnp.ones((CMP_ROWS - HEAD_DIM - LANES, nc), _BF)


def _nsa_compress(kc, vc, pek, pev, w1k, w1v, w2k, w2v):
    BH, S, _ = kc.shape
    nc = S // CMP_STRIDE
    pair = 2 * HEAD_DIM
    const2 = lambda i: (0, 0)
    const3 = lambda i: (0, 0, 0)
    seq_rows = pl.BlockSpec((1, S, HEAD_DIM), lambda i: (i, 0, 0))
    pe_spec = pl.BlockSpec((CMP_BLOCK, HEAD_DIM), const2)
    w1_spec = pl.BlockSpec((CMP_BLOCK // 2, pair, HEAD_DIM), const3)
    w2_spec = pl.BlockSpec((HEAD_DIM, HEAD_DIM), const2)
    return pl.pallas_call(
        _compress_kernel,
        out_shape=(jax.ShapeDtypeStruct((BH, nc, 2 * HEAD_DIM), _BF),
                   jax.ShapeDtypeStruct((BH, CMP_ROWS, nc), _BF)),
        grid=(BH,),
        in_specs=[seq_rows, seq_rows, pe_spec, pe_spec, w1_spec, w1_spec, w2_spec, w2_spec],
        out_specs=(pl.BlockSpec((1, nc, 2 * HEAD_DIM), lambda i: (i, 0, 0)),
                   pl.BlockSpec((1, CMP_ROWS, nc), lambda i: (i, 0, 0))),
        compiler_params=pltpu.CompilerParams(
            dimension_semantics=("arbitrary",), vmem_limit_bytes=VMEM_LIMIT),
        name="nsa_compress",
    )(kc, vc, pek, pev, w1k, w1v, w2k, w2v)


def _mark_top_blocks(score, notsel, rounds):
    rows = lax.broadcasted_iota(jnp.int32, score.shape, 0).astype(_F32)
    left = score
    for _ in range(rounds):
        mx = jnp.max(left, axis=0, keepdims=True)
        idx = jnp.min(jnp.where(left == mx, rows, float(LANES)), axis=0, keepdims=True)
        left = jnp.where(rows == idx, -jnp.inf, left)
    return jnp.where((score > -jnp.inf) & (left == -jnp.inf), 0.0, notsel)


def _attn_kernel(qt_ref, gt_ref, kca_ref, vct_ref, ksa_ref, vst_ref, kwa_ref, vwt_ref, o_ref,
                 m_ref, acc_ref, lhs_ref, part_ref, gate_ref, bits_ref, *, seq, n_seq):
    step = pl.program_id(0)
    n_pair = seq // (Q_BLOCK * PAIR)
    last = n_seq * n_pair - 1
    front_pair = jnp.minimum(step, last)
    back_pair = jnp.clip(step - 1, 0, last)
    nc = kca_ref.shape[1]
    n_sel = seq // SEL_BLOCK
    n_forced = 1 + N_LOCAL_FORCED
    k_top = min(N_SELECT, n_sel)
    G = Q_PER_KV
    W = G * Q_BLOCK
    words = LANES // BITS_PER_WORD
    blocks_per_tile = SEL_TILE // SEL_BLOCK
    tiles_per_word = BITS_PER_WORD // blocks_per_tile
    put = step % 2
    get = 1 - put
    put3 = step % 3
    get3 = (step + 1) % 3
    units = range(PAIR)

    def flag_base(slot, u):
        return (slot * PAIR + u) * (words + 1)

    @pl.when(step == 0)
    def _():
        lhs_ref[1] = jnp.zeros(lhs_ref.shape[1:], _BF)
        part_ref[...] = jnp.zeros(part_ref.shape, _F32)
        gate_ref[...] = jnp.zeros(gate_ref.shape, _F32)
        acc_ref[...] = jnp.ones(acc_ref.shape, _F32)
        for k in range(PAIR * (words + 1)):
            bits_ref[flag_base(1, 0) + k] = 0

    lane_w = lax.broadcasted_iota(jnp.int32, (1, W), 1)

    def head_slopes(pair_index):
        h = (pair_index // n_pair) % N_KV_HEADS
        slope = jnp.zeros((1, W), _F32)
        for g in range(G):
            sg = jnp.where(h == 0, _F32(2.0 ** -(g + 1)), _F32(2.0 ** -(G + g + 1)))
            slope = jnp.where(lane_w // Q_BLOCK == g, sg, slope)
        return (slope * LOG2_E).astype(_BF).astype(_F32)

    slope = head_slopes(front_pair)
    slope_b = head_slopes(back_pair)
    strips =[slice(g * Q_BLOCK, (g + 1) * Q_BLOCK) for g in range(G)]
    top_row = lax.broadcasted_iota(jnp.int32, (TOP_ROWS, W), 0)

    def position_rows(first):
        top = jnp.where(top_row == 0, first, jnp.where(top_row == 1, slope * POS_RADIX, 0.0)).astype(_BF)
        return jnp.concatenate([top, jnp.zeros((LANES - TOP_ROWS, W), _BF)], axis=0)

    key_row = lax.broadcasted_iota(jnp.int32, (SEL_TILE, Q_BLOCK), 0)
    pos_rows = position_rows(slope)

    def finish(u):
        out_t = (part_ref[get3, u] + acc_ref[u, 0:HEAD_DIM]
                 * (gate_ref[get3, u][0:1] * (1.0 / acc_ref[u, HEAD_DIM:HEAD_DIM + 1])))
        for g, cs in enumerate(strips):
            o_ref[u * Q_BLOCK:(u + 1) * Q_BLOCK, g * HEAD_DIM:(g + 1) * HEAD_DIM] = out_t[:, cs].T.astype(_BF)

    front = [{} for _ in units]

    def front_scores(u):
        f = front[u]
        f["q0"] =(front_pair % n_pair * PAIR + u) * Q_BLOCK
        f["tok"] = f["q0"] + lane_w % Q_BLOCK
        f["qt"] = qt_ref[u, 0]
        f["s"] = _dot(kca_ref[0], jnp.concatenate([f["qt"], pos_rows], axis=0))

    def tile_scores(bk, j, causal, live=None):
        k0 = j * SEL_TILE if isinstance(j, int) else pl.multiple_of(j * SEL_TILE, SEL_TILE)
        st = _dot(ksa_ref[0, 0, pl.ds(k0, SEL_TILE), :], bk["lhs"])
        if causal:
            ahead = jnp.where(k0 + key_row > bk["tok"][:, :Q_BLOCK], NEG_INF, 0.0)
            st = jnp.concatenate([st[:, cs] + ahead for cs in strips], axis=1)
        shift = slope_b * (bk["p0"] - k0).astype(_F32)
        if live is not None:
            shift = jnp.where(live, shift, -NEG_INF)
        return st, shift, k0

    def tile_max(tiles):
        m = None
        for st, shift, _ in tiles:
            cm = jnp.max(st, axis=0, keepdims=True) - shift
            m = cm if m is None else jnp.maximum(m, cm)
        return m

    def tile_sums(tiles, m):
        acc = None
        for st, shift, k0 in tiles:
            sub = m + shift
            pt = jnp.concatenate([jnp.exp2(st[:, cs] - sub[:, cs]) for cs in strips], axis=1)
            ai = _dot(vst_ref[0, 0, :, pl.ds(k0, SEL_TILE)], pt.astype(_BF))
            acc = ai if acc is None else acc + ai
        return acc

    back = [{} for _ in units]

    def back_scores(u):
        bk = back[u]
        bk["p0"] =(back_pair % n_pair * PAIR + u) * Q_BLOCK
        bk["tok"] = bk["p0"] + lane_w % Q_BLOCK
        bk["lhs"] = lhs_ref[get, u]
        bk["first_local"] = bk["p0"] // SEL_TILE - (LOCAL_TILES - 1)
        bk["tiles"] = [tile_scores(bk, 0, False, live=bk["first_local"] > 0)]
        for i in range(LOCAL_TILES):
            j = bk["first_local"] + i
            last = i == LOCAL_TILES - 1
            bk["tiles"].append(tile_scores(bk, jnp.maximum(j, 0), last, live=None if last else j >= 0))

    n_row = lax.broadcasted_iota(jnp.int32, (nc, Q_BLOCK), 0)

    def front_compressed(u):
        f = front[u]
        n_last = (f["tok"] - (CMP_BLOCK - 1)) // CMP_STRIDE
        mask_c = jnp.where(n_row <= n_last[:, :Q_BLOCK], 0.0, NEG_INF)
        e_parts = []
        for cs in strips:
            sg = f["s"][:, cs] + mask_c
            e_parts.append(jnp.exp2(sg - jnp.max(sg, axis=0, keepdims=True)).astype(_BF))
        both = _dot(vct_ref[0], jnp.concatenate(e_parts, axis=1))
        both = both * jnp.where(n_last >= 0, 1.0 / both[HEAD_DIM + LANES:HEAD_DIM + LANES + 1], 0.0)
        f["o_cmp"] = both[:HEAD_DIM]
        imp_t = both[HEAD_DIM:HEAD_DIM + LANES, strips[0]]
        for cs in strips[1:]:
            imp_t = imp_t + both[HEAD_DIM:HEAD_DIM + LANES, cs]
        f["imp_t"] = imp_t

    wlen = WINDOW + Q_BLOCK

    def front_window_scores(u):
        f = front[u]
        anchor =jnp.concatenate([jnp.minimum(f["imp_t"][0:1], 0.0)] * G, axis=1)
        win_rows = position_rows(slope + anchor)
        f["w0"] = pl.multiple_of(jnp.maximum(f["q0"] - WINDOW, 0), Q_BLOCK)
        f["sw"] = _dot(kwa_ref[0, 0, pl.ds(f["w0"], wlen), :], jnp.concatenate([f["qt"], win_rows], axis=0))

    def back_sums(u):
        bk = back[u]
        m = tile_max(bk["tiles"])
        m_ref[u, 0:1] = m
        acc_ref[u] = tile_sums(bk["tiles"], m)

    def front_window_sums(u):
        f = front[u]
        w_row =f["w0"] + lax.broadcasted_iota(jnp.int32, (wlen, Q_BLOCK), 0)
        dist = f["tok"][:, :Q_BLOCK] - w_row
        mask_w = jnp.where(dist.astype(jnp.uint32) < WINDOW, 0.0, NEG_INF)
        e_parts = []
        for cs in strips:
            sg = f["sw"][:, cs] + mask_w
            e_parts.append(jnp.exp2(sg - jnp.max(sg, axis=0, keepdims=True)).astype(_BF))
        win = _dot(vwt_ref[0, 0, :, pl.ds(f["w0"], wlen)], jnp.concatenate(e_parts, axis=1))
        f["o_win"] = win[:HEAD_DIM] * (1.0 / win[HEAD_DIM:HEAD_DIM + 1])

    mrow = lax.broadcasted_iota(jnp.int32, (LANES, Q_BLOCK), 0)
    blk_row = lax.broadcasted_iota(jnp.int32, (LANES, 1), 0)
    blk_bit = jnp.left_shift(1, blk_row % BITS_PER_WORD).astype(_F32)

    def front_select(u):
        f = front[u]
        gates =gt_ref[0][:, u * Q_BLOCK:(u + 1) * Q_BLOCK]

        def gate_row(branch):
            return jnp.concatenate([gates[3 * g + branch:3 * g + branch + 1] for g in range(G)], axis=1)

        tcol = lax.broadcasted_iota(jnp.int32, (LANES, Q_BLOCK), 1) + f["q0"]
        lag = tcol // SEL_BLOCK - mrow
        forced = (mrow == 0) | ((lag >= 0) & (lag < N_LOCAL_FORCED))
        score = jnp.where(forced | (lag < 0) | (mrow >= n_sel), -jnp.inf, f["imp_t"])
        notsel_t = _mark_top_blocks(score, jnp.where(forced, 0.0, 1.0), k_top - n_forced)

        flags = jnp.concatenate([notsel_t] * G, axis=1)
        sel_rows = jnp.concatenate([jnp.where(top_row == 0, slope, flags[:TOP_ROWS]), flags[TOP_ROWS:]],
                                   axis=0).astype(_BF)
        lhs_ref[put, u] = jnp.concatenate([f["qt"], sel_rows], axis=0)
        part_ref[put3, u] = gate_row(0) * f["o_cmp"] + gate_row(2) * f["o_win"]
        gate_ref[put3, u] = jnp.broadcast_to(gate_row(1), gate_ref.shape[2:])

        far_end = (f["q0"] // SEL_TILE - (LOCAL_TILES - 1)) * blocks_per_tile
        blk_on = jnp.where((jnp.min(notsel_t, axis=1, keepdims=True) < 0.5)
                           & (blk_row >= blocks_per_tile) & (blk_row < far_end), 1.0, 0.0)
        for k in range(words):
            word = jnp.sum((blk_on * blk_bit)[k * BITS_PER_WORD:(k + 1) * BITS_PER_WORD])
            bits_ref[flag_base(put, u) + k] = word.astype(jnp.int32)
        bits_ref[flag_base(put, u) + words] = jnp.sum(blk_on).astype(jnp.int32)

    phases = {"finish": finish, "front_scores": front_scores, "back_scores": back_scores,
              "front_compressed": front_compressed, "front_window_scores": front_window_scores,
              "back_sums": back_sums, "front_window_sums": front_window_sums, "front_select": front_select}
    for phase, u in PHASE_ORDER:
        phases[phase](u)

    for u, bk in zip(units, back):
        def far_tile(j, carry, u=u, bk=bk):
            word = bits_ref[flag_base(get, u) + j // tiles_per_word]
            tile_bits = (word >> ((j % tiles_per_word) * blocks_per_tile)) & ((1 << blocks_per_tile) - 1)

            @pl.when(tile_bits != 0)
            def _():
                tile = [tile_scores(bk, j, False)]
                m_old = m_ref[u, 0:1]
                m_new = jnp.maximum(m_old, tile_max(tile))
                m_ref[u, 0:1] = m_new
                acc_ref[u] = jnp.exp2(m_old - m_new) * acc_ref[u] + tile_sums(tile, m_new)
            return carry

        @pl.when(bits_ref[flag_base(get, u) + words] != 0)
        def _(far_tile=far_tile, bk=bk):
            lax.fori_loop(1, bk["first_local"], far_tile, 0)


def _nsa_attention(qt, gt, kca, vct, ksa, vst, kwa, vwt, B, S):
    n_pair = S // (Q_BLOCK * PAIR)
    T = B * S
    nc = kca.shape[1]
    gd = Q_PER_KV * HEAD_DIM
    W = Q_PER_KV * Q_BLOCK
    n_seq = B * N_KV_HEADS
    last = n_seq * n_pair - 1
    front = lambda i: jnp.minimum(i, last)
    back = lambda i: jnp.clip(i - 1, 0, last)
    done = lambda i: jnp.maximum(i - 2, 0)
    seq_of = lambda p: p // n_pair
    blk_of = lambda p: (seq_of(p) // N_KV_HEADS) * n_pair + p % n_pair
    head_of = lambda p: seq_of(p) % N_KV_HEADS
    rows = lambda stage: pl.BlockSpec((1, 1, S, 2 * HEAD_DIM),
                                      lambda i: (seq_of(stage(i)) // N_KV_HEADS, head_of(stage(i)), 0, 0))
    cols = lambda stage: pl.BlockSpec((1, 1, VT_ROWS, S),
                                      lambda i: (seq_of(stage(i)) // N_KV_HEADS, head_of(stage(i)), 0, 0))
    return pl.pallas_call(
        functools.partial(_attn_kernel, seq=S, n_seq=n_seq),
        out_shape=jax.ShapeDtypeStruct((T, N_KV_HEADS * gd), _BF),
        grid=(last + 3,),
        in_specs=[
            pl.BlockSpec((PAIR, 1, HEAD_DIM, W), lambda i: (blk_of(front(i)), head_of(front(i)), 0, 0)),
            pl.BlockSpec((1, GATE_ROWS, PAIR * Q_BLOCK), lambda i: (head_of(front(i)), 0, blk_of(front(i)))),
            pl.BlockSpec((1, nc, 2 * HEAD_DIM), lambda i: (seq_of(front(i)), 0, 0)),
            pl.BlockSpec((1, CMP_ROWS, nc), lambda i: (seq_of(front(i)), 0, 0)),
            rows(back), cols(back), rows(front), cols(front),
        ],
        out_specs=pl.BlockSpec((PAIR * Q_BLOCK, gd), lambda i: (blk_of(done(i)), head_of(done(i)))),
        scratch_shapes=[pltpu.VMEM((PAIR, 8, W), _F32), pltpu.VMEM((PAIR, VT_ROWS, W), _F32),
                        pltpu.VMEM((2, PAIR, 2 * HEAD_DIM, W), _BF), pltpu.VMEM((3, PAIR, HEAD_DIM, W), _F32),
                        pltpu.VMEM((3, PAIR, 8, W), _F32),
                        pltpu.SMEM((2 * PAIR * (LANES // BITS_PER_WORD + 1),), jnp.int32)],
        compiler_params=pltpu.CompilerParams(
            dimension_semantics=("arbitrary",), vmem_limit_bytes=VMEM_LIMIT),
        name="nsa_attention",
    )(qt, gt, kca, vct, ksa, vst, kwa, vwt)


def _merge_ffn_kernel(x_ref, gm_ref, yn_ref, wm1_ref, wpn_ref, wo_ref, g1_ref, b1_ref,
                      w1_ref, w2_ref, g2_ref, b2_ref, o_ref):
    half = x_ref.shape[0] // FFN_SPLIT
    parts = [slice(p * half, (p + 1) * half) for p in range(FFN_SPLIT)]
    x = [x_ref[rs, :] for rs in parts]
    gate = [_sigmoid(_dot(xp.astype(_BF), wm1_ref[...])) for xp in x]
    merged = [gm_ref[rs, :] + g * _dot(yn_ref[rs, :], wpn_ref[...]) for rs, g in zip(parts, gate)]
    mix = [_dot(mp.astype(_BF), wo_ref[...]) for mp in merged]
    hid = [_layer_norm(DEEPNORM_ALPHA * xp + mp, g1_ref[...], b1_ref[...]) for xp, mp in zip(x, mix)]
    hb = [hp.astype(_BF) for hp in hid]
    f = [None] * FFN_SPLIT
    for c in range(D_FF // D_MODEL):
        a = [jnp.maximum(_dot(hp, w1_ref[:, c * D_MODEL:(c + 1) * D_MODEL]), 0.0) for hp in hb]
        for p, ap in enumerate(a):
            fc = _dot((ap * ap).astype(_BF), w2_ref[c * D_MODEL:(c + 1) * D_MODEL, :])
            f[p] = fc if f[p] is None else f[p] + fc
    for rs, hp, fp in zip(parts, hid, f):
        o_ref[rs, :] = _layer_norm(DEEPNORM_ALPHA * hp + fp, g2_ref[...], b2_ref[...])


def _merge_ffn(x2, gm, yn, wm1, wpn, wo, g1, b1, w1, w2, g2, b2):
    T = x2.shape[0]
    tm = ROW_TILE
    rows = lambda w: pl.BlockSpec((tm, w), lambda i: (i, 0))
    const = lambda r, c: _resident((r, c), lambda i: (0, 0))
    return pl.pallas_call(
        _merge_ffn_kernel,
        out_shape=jax.ShapeDtypeStruct((T, D_MODEL), _F32),
        grid=(T // tm,),
        in_specs=[rows(D_MODEL), rows(D_MODEL), rows(D_MODEL),
                  const(D_MODEL, D_MODEL), const(D_MODEL, D_MODEL), const(D_MODEL, D_MODEL),
                  const(1, D_MODEL), const(1, D_MODEL),
                  const(D_MODEL, D_FF), const(D_FF, D_MODEL),
                  const(1, D_MODEL), const(1, D_MODEL)],
        out_specs=rows(D_MODEL),
        compiler_params=pltpu.CompilerParams(
            dimension_semantics=("arbitrary",), vmem_limit_bytes=VMEM_LIMIT),
        name="merge_ffn",
    )(x2, gm, yn, wm1, wpn, wo, g1, b1, w1, w2, g2, b2)


def _layer(x2, B, S, w_in, layer, gm_ln_g, gm_ln_b, gm_w_s, gm_b_s, cmp_pe_k, cmp_w1_k, cmp_w2_k,
           cmp_pe_v, cmp_w1_v, cmp_w2_v, w_proj_gm, w_proj_nsa, w_out,
           ln1_g, ln1_b, w_ff1, w_ff2, ln2_g, ln2_b):
    o_q = 2 * D_MODEL
    o_m = o_q + (Q_PER_KV + 6) * N_KV_HEADS * HEAD_DIM + 3 * Q_PER_KV * N_KV_HEADS
    row = lambda v: v.reshape(1, -1)
    merge_cols = lambda j: w_in[layer * D_MODEL:(layer + 1) * D_MODEL, o_m + j * D_MODEL:o_m + (j + 1) * D_MODEL]

    gm = _gm_mixer(x2, w_in, layer, merge_cols(0), row(gm_ln_g), row(gm_ln_b), gm_w_s,
                   jnp.broadcast_to(gm_b_s[:, :, None], (GM_GROUPS, GM_CHUNK, LANES)), w_proj_gm)

    qt, kc, vc, ksa, vst, kwa, vwt, gt = _qkv_proj(x2, w_in, layer, o_q, B, S)

    heads = lambda a: a.reshape(B * N_KV_HEADS, S, HEAD_DIM)
    pairs = lambda w: w.astype(_BF).reshape(CMP_BLOCK // 2, 2 * HEAD_DIM, HEAD_DIM)
    kca, vct = _nsa_compress(heads(kc), heads(vc), cmp_pe_k, cmp_pe_v, pairs(cmp_w1_k), pairs(cmp_w1_v),
                             cmp_w2_k.astype(_BF), cmp_w2_v.astype(_BF))

    yn = _nsa_attention(qt, gt, kca, vct, ksa, vst, kwa, vwt, B, S)

    return _merge_ffn(x2, gm, yn, merge_cols(1).astype(_BF), w_proj_nsa.astype(_BF),
                      w_out.astype(_BF), row(ln1_g), row(ln1_b), w_ff1.astype(_BF), w_ff2.astype(_BF),
                      row(ln2_g), row(ln2_b))


def kernel(x, w_in, gm_ln_g, gm_ln_b, gm_w_s, gm_b_s, cmp_pe_k, cmp_w1_k, cmp_w2_k, cmp_pe_v, cmp_w1_v, cmp_w2_v, w_proj_gm, w_proj_nsa, w_out, ln1_g, ln1_b, w_ff1, w_ff2, ln2_g, ln2_b):
    B, S, D = x.shape
    assert D == D_MODEL and S % ROW_TILE == 0 and WINDOW + Q_BLOCK <= S <= SEL_BLOCK * LANES
    assert S // POS_RADIX <= POS_RADIX and S % (PAIR * Q_BLOCK) == 0
    h = x.reshape(B * S, D)
    depth, _, d_in = w_in.shape
    w_rows = w_in.reshape(depth * D, d_in)
    for l in range(depth):
        h = _layer(h, B, S, w_rows, l, gm_ln_g[l], gm_ln_b[l], gm_w_s[l], gm_b_s[l],
                   cmp_pe_k[l], cmp_w1_k[l], cmp_w2_k[l], cmp_pe_v[l], cmp_w1_v[l], cmp_w2_v[l],
                   w_proj_gm[l], w_proj_nsa[l], w_out[l], ln1_g[l], ln1_b[l],
                   w_ff1[l], w_ff2[l], ln2_g[l], ln2_b[l])
    return h.reshape(B, S, D)
```

```python
import functools
import math

import jax
import jax.numpy as jnp
from jax import lax
from jax.experimental import pallas as pl
from jax.experimental.pallas import tpu as pltpu

D_MODEL = 1024
GM_GROUPS = 8
GM_CHUNK = 128
N_KV_HEADS = 2
Q_PER_KV = 4
HEAD_DIM = 128
CMP_BLOCK = 32
CMP_STRIDE = 16
SEL_BLOCK = 64
N_SELECT = 16
N_LOCAL_FORCED = 2
WINDOW = 512
Q_BLOCK = 128
D_FF = 4 * D_MODEL
DEEPNORM_ALPHA = 2.0 ** 0.25
LN_EPS = 1e-5
NEG_INF = -1e30
LOG2_E = math.log2(math.e)

LANES = 128
SEL_TILE = 128
LOCAL_TILES = 10
PAIR = 2
PHASE_ORDER = tuple((phase, u) for phase in ("finish", "front_scores", "back_scores", "front_compressed",
                                              "front_window_scores", "back_sums", "front_window_sums",
                                              "front_select") for u in range(PAIR))
BITS_PER_WORD = 16
POS_RADIX = 256
GM_ROW_TILE = 1024
GM_SPLIT = 4
FFN_SPLIT = 2
ROW_TILE = 512
VT_ROWS = HEAD_DIM + 16
CMP_ROWS = HEAD_DIM + LANES + 16
TOP_ROWS = 16
GATE_ROWS = 16
VMEM_LIMIT = 56 * 1024 * 1024

_BF = jnp.bfloat16
_F32 = jnp.float32


def _dot(a, b):
    return jnp.dot(a, b, preferred_element_type=_F32)


def _gelu(x):
    c = math.sqrt(2.0 / math.pi)
    return 0.5 * x * (1.0 + jnp.tanh(c * (x + 0.044715 * (x * x * x))))


def _sigmoid(x):
    return 1.0 / (1.0 + jnp.exp(-x))


def _layer_norm(x, g, b):
    mu = jnp.mean(x, axis=-1, keepdims=True)
    xc = x - mu
    var = jnp.mean(xc * xc, axis=-1, keepdims=True)
    return xc * lax.rsqrt(var + LN_EPS) * g + b


def _position_lanes(pos, shape):
    lane = lax.broadcasted_iota(jnp.int32, shape, 1)
    return jnp.where(lane == 0, (pos % POS_RADIX).astype(_F32),
                     jnp.where(lane == 1, (pos // POS_RADIX).astype(_F32), 0.0)).astype(_BF)


def _stage_bf16(first_step, pairs):
    @pl.when(first_step)
    def _():
        for src, dst in pairs:
            dst[...] = src[...].astype(_BF)


def _resident(shape, index_map):
    return pl.BlockSpec(shape, index_map, pipeline_mode=pl.Buffered(1))


def _gm_kernel(x_ref, wgm32_ref, wm032_ref, lng_ref, lnb_ref, ws_ref, bs_ref, wpg32_ref, o_ref,
               vg_ref, wgm_ref, wm0_ref, wpg_ref):
    _stage_bf16(pl.program_id(0) == 0, [(wgm32_ref, wgm_ref), (wm032_ref, wm0_ref), (wpg32_ref, wpg_ref)])
    tm = x_ref.shape[0]
    rows = tm // GM_SPLIT
    parts = [slice(p * rows, (p + 1) * rows) for p in range(GM_SPLIT)]
    xb = [x_ref[rs, :].astype(_BF) for rs in parts]
    z = [_gelu(_dot(xp, wgm_ref[...])) for xp in xb]
    gate_logits = [_dot(xp, wm0_ref[...]) for xp in xb]
    v = [_layer_norm(zp[:, D_MODEL:], lng_ref[...], lnb_ref[...]).astype(_BF) for zp in z]
    row = lax.broadcasted_iota(jnp.int32, (GM_CHUNK, GM_CHUNK), 0)
    col = lax.broadcasted_iota(jnp.int32, (GM_CHUNK, GM_CHUNK), 1)
    w = [jnp.where(row >= col, ws_ref[gi], 0.0).astype(_BF) for gi in range(GM_GROUPS)]
    for p, rs in enumerate(parts):
        for gi in range(GM_GROUPS):
            for c in range(rows // GM_CHUNK):
                blk = v[p][c * GM_CHUNK:(c + 1) * GM_CHUNK, gi * LANES:(gi + 1) * LANES]
                r0 = p * rows + c * GM_CHUNK
                vg_ref[r0:r0 + GM_CHUNK, gi * LANES:(gi + 1) * LANES] = _dot(w[gi], blk) + bs_ref[gi]
    for p, rs in enumerate(parts):
        y = (z[p][:, :D_MODEL] * vg_ref[rs, :]).astype(_BF)
        o_ref[rs, :] = _sigmoid(gate_logits[p]) * _dot(y, wpg_ref[...])


def _gm_mixer(x2, w_in, layer, wm0, lng, lnb, ws, bs, wpg):
    T = x2.shape[0]
    tm = GM_ROW_TILE
    const2 = lambda i: (0, 0)
    const3 = lambda i: (0, 0, 0)
    return pl.pallas_call(
        _gm_kernel,
        out_shape=jax.ShapeDtypeStruct((T, D_MODEL), _F32),
        grid=(T // tm,),
        in_specs=[
            pl.BlockSpec((tm, D_MODEL), lambda i: (i, 0)),
            _resident((D_MODEL, 2 * D_MODEL), lambda i: (layer, 0)),
            _resident((D_MODEL, D_MODEL), const2),
            pl.BlockSpec((1, D_MODEL), const2),
            pl.BlockSpec((1, D_MODEL), const2),
            pl.BlockSpec((GM_GROUPS, GM_CHUNK, GM_CHUNK), const3),
            pl.BlockSpec((GM_GROUPS, GM_CHUNK, LANES), const3),
            _resident((D_MODEL, D_MODEL), const2),
        ],
        out_specs=pl.BlockSpec((tm, D_MODEL), lambda i: (i, 0)),
        scratch_shapes=[pltpu.VMEM((tm, D_MODEL), _F32), pltpu.VMEM((D_MODEL, 2 * D_MODEL), _BF),
                        pltpu.VMEM((D_MODEL, D_MODEL), _BF), pltpu.VMEM((D_MODEL, D_MODEL), _BF)],
        compiler_params=pltpu.CompilerParams(
            dimension_semantics=("arbitrary",), vmem_limit_bytes=VMEM_LIMIT),
        name="gm_mixer",
    )(x2, w_in, wm0, lng, lnb, ws, bs, wpg)


def _qkv_kernel(x_ref, wq32_ref, wkv32_ref, wg32_ref, qt_ref, kc_ref, vc_ref, ksa_ref, vst_ref, kwa_ref, vwt_ref,
                gt_ref, wq_ref, wkv_ref, wg_ref):
    _stage_bf16((pl.program_id(0) == 0) & (pl.program_id(1) == 0),
                [(wq32_ref, wq_ref), (wkv32_ref, wkv_ref), (wg32_ref, wg_ref)])
    tm = x_ref.shape[0]
    xb = x_ref[...].astype(_BF)
    zq = _dot(xb, wq_ref[...]) * (HEAD_DIM ** -0.5 * LOG2_E)
    for tb in range(tm // Q_BLOCK):
        for h in range(N_KV_HEADS):
            for g in range(Q_PER_KV):
                c0 = (h * Q_PER_KV + g) * HEAD_DIM
                blk = zq[tb * Q_BLOCK:(tb + 1) * Q_BLOCK, c0:c0 + HEAD_DIM]
                qt_ref[tb, h, :, g * Q_BLOCK:(g + 1) * Q_BLOCK] = blk.T.astype(_BF)
    z = _dot(xb, wkv_ref[...])
    kpos = pl.program_id(1) * tm + lax.broadcasted_iota(jnp.int32, (tm, LANES), 0)
    blk_lane = lax.broadcasted_iota(jnp.int32, (tm, LANES), 1)
    sel_lanes = jnp.where(blk_lane == 0, (kpos % SEL_TILE).astype(_F32),
                          jnp.where(kpos // SEL_BLOCK == blk_lane, NEG_INF, 0.0)).astype(_BF)
    win_lanes = _position_lanes(kpos, (tm, LANES))
    ones = jnp.ones((VT_ROWS - HEAD_DIM, tm), _BF)
    for h in range(N_KV_HEADS):
        def col(j):
            return z[:, j * 2 * HEAD_DIM + h * HEAD_DIM: j * 2 * HEAD_DIM + (h + 1) * HEAD_DIM]
        kc_ref[0, h] = col(0)
        vc_ref[0, h] = col(1)
        ksa_ref[0, h, :, :HEAD_DIM] = col(2).astype(_BF)
        ksa_ref[0, h, :, HEAD_DIM:] = sel_lanes
        vst_ref[0, h, :HEAD_DIM] = col(3).T.astype(_BF)
        vst_ref[0, h, HEAD_DIM:] = ones
        kwa_ref[0, h, :, :HEAD_DIM] = col(4).astype(_BF)
        kwa_ref[0, h, :, HEAD_DIM:] = win_lanes
        vwt_ref[0, h, :HEAD_DIM] = col(5).T.astype(_BF)
        vwt_ref[0, h, HEAD_DIM:] = ones
    zg_t = _sigmoid(_dot(xb, wg_ref[...])).T
    per_head = 3 * Q_PER_KV
    for h in range(N_KV_HEADS):
        gt_ref[h] = zg_t[h * per_head:h * per_head + GATE_ROWS]


def _qkv_proj(x2, w_in, layer, q_col, B, S):
    T = x2.shape[0]
    tm = ROW_TILE
    nsb = S // tm
    gd = Q_PER_KV * Q_BLOCK
    aw = N_KV_HEADS * Q_PER_KV * HEAD_DIM
    kvw = 6 * N_KV_HEADS * HEAD_DIM
    kv_col = q_col + aw
    gate_col = kv_col + kvw
    assert q_col % aw == 0 and kv_col % kvw == 0 and gate_col % LANES == 0
    rows_spec = lambda w: pl.BlockSpec((1, N_KV_HEADS, tm, w), lambda b, s: (b, 0, s, 0))
    rows_shape = lambda w, dt: jax.ShapeDtypeStruct((B, N_KV_HEADS, S, w), dt)
    cols_spec = pl.BlockSpec((1, N_KV_HEADS, VT_ROWS, tm), lambda b, s: (b, 0, 0, s))
    cols_shape = jax.ShapeDtypeStruct((B, N_KV_HEADS, VT_ROWS, S), _BF)
    return pl.pallas_call(
        _qkv_kernel,
        out_shape=(
            jax.ShapeDtypeStruct((T // Q_BLOCK, N_KV_HEADS, HEAD_DIM, gd), _BF),
            rows_shape(HEAD_DIM, _F32), rows_shape(HEAD_DIM, _F32),
            rows_shape(2 * HEAD_DIM, _BF), cols_shape,
            rows_shape(2 * HEAD_DIM, _BF), cols_shape,
            jax.ShapeDtypeStruct((N_KV_HEADS, GATE_ROWS, T), _F32),
        ),
        grid=(B, nsb),
        in_specs=[
            pl.BlockSpec((tm, D_MODEL), lambda b, s: (b * nsb + s, 0)),
            _resident((D_MODEL, aw), lambda b, s: (layer, q_col // aw)),
            _resident((D_MODEL, kvw), lambda b, s: (layer, kv_col // kvw)),
            _resident((D_MODEL, LANES), lambda b, s: (layer, gate_col // LANES)),
        ],
        out_specs=(
            pl.BlockSpec((tm // Q_BLOCK, N_KV_HEADS, HEAD_DIM, gd), lambda b, s: (b * nsb + s, 0, 0, 0)),
            rows_spec(HEAD_DIM), rows_spec(HEAD_DIM), rows_spec(2 * HEAD_DIM), cols_spec,
            rows_spec(2 * HEAD_DIM), cols_spec,
            pl.BlockSpec((N_KV_HEADS, GATE_ROWS, tm), lambda b, s: (0, 0, b * nsb + s)),
        ),
        scratch_shapes=[pltpu.VMEM((D_MODEL, aw), _BF), pltpu.VMEM((D_MODEL, kvw), _BF),
                        pltpu.VMEM((D_MODEL, LANES), _BF)],
        compiler_params=pltpu.CompilerParams(
            dimension_semantics=("arbitrary", "arbitrary"), vmem_limit_bytes=VMEM_LIMIT),
        name="qkv_proj",
    )(x2, w_in, w_in, w_in)


def _compress_kernel(kc_ref, vc_ref, pek_ref, pev_ref, w1k_ref, w1v_ref, w2k_ref, w2v_ref, ko_ref, vo_ref):
    nc = ko_ref.shape[1]

    def tokens(src, pe, w1, w2):
        first, second = None, None
        for p in range(0, CMP_STRIDE, 2):
            rows = [src[0, pl.ds(p + d, nc, stride=CMP_STRIDE), :] for d in range(2)]
            lo = jnp.concatenate([rows[d] + pe[p + d:p + d + 1, :] for d in range(2)], axis=1)
            hi = jnp.concatenate([rows[d] + pe[CMP_STRIDE + p + d:CMP_STRIDE + p + d + 1, :] for d in range(2)], axis=1)
            a = _dot(lo.astype(_BF), w1[p // 2])
            b = _dot(hi.astype(_BF), w1[(CMP_STRIDE + p) // 2])
            first = a if first is None else first + a
            second = b if second is None else second + b
        pre = first + pltpu.roll(second, nc - 1, 0)
        return _dot(_gelu(pre).astype(_BF), w2[...])

    ko_ref[0, :, :HEAD_DIM] = tokens(kc_ref, pek_ref, w1k_ref, w2k_ref).astype(_BF)
    start = lax.broadcasted_iota(jnp.int32, (nc, LANES), 0) * CMP_STRIDE
    ko_ref[0, :, HEAD_DIM:] = _position_lanes(start, (nc, LANES))
    vo_ref[0, :HEAD_DIM] = tokens(vc_ref, pev_ref, w1v_ref, w2v_ref).T.astype(_BF)
    mi = lax.broadcasted_iota(jnp.int32, (LANES, nc), 0)
    ni = lax.broadcasted_iota(jnp.int32, (LANES, nc), 1)
    overlap_t = jnp.where((ni * CMP_STRIDE + (CMP_BLOCK - 1) >= mi * SEL_BLOCK)
                          & (ni * CMP_STRIDE <= mi * SEL_BLOCK + (SEL_BLOCK - 1)), 1.0, 0.0)
    vo_ref[0, HEAD_DIM:HEAD_DIM + LANES] = overlap_t.astype(_BF)
    vo_ref[0, HEAD_DIM + LANES:] = jnp.ones((CMP_ROWS - HEAD_DIM - LANES, nc), _BF)


def _nsa_compress(kc, vc, pek, pev, w1k, w1v, w2k, w2v):
    BH, S, _ = kc.shape
    nc = S // CMP_STRIDE
    pair = 2 * HEAD_DIM
    const2 = lambda i: (0, 0)
    const3 = lambda i: (0, 0, 0)
    seq_rows = pl.BlockSpec((1, S, HEAD_DIM), lambda i: (i, 0, 0))
    pe_spec = pl.BlockSpec((CMP_BLOCK, HEAD_DIM), const2)
    w1_spec = pl.BlockSpec((CMP_BLOCK // 2, pair, HEAD_DIM), const3)
    w2_spec = pl.BlockSpec((HEAD_DIM, HEAD_DIM), const2)
    return pl.pallas_call(
        _compress_kernel,
        out_shape=(jax.ShapeDtypeStruct((BH, nc, 2 * HEAD_DIM), _BF),
                   jax.ShapeDtypeStruct((BH, CMP_ROWS, nc), _BF)),
        grid=(BH,),
        in_specs=[seq_rows, seq_rows, pe_spec, pe_spec, w1_spec, w1_spec, w2_spec, w2_spec],
        out_specs=(pl.BlockSpec((1, nc, 2 * HEAD_DIM), lambda i: (i, 0, 0)),
                   pl.BlockSpec((1, CMP_ROWS, nc), lambda i: (i, 0, 0))),
        compiler_params=pltpu.CompilerParams(
            dimension_semantics=("arbitrary",), vmem_limit_bytes=VMEM_LIMIT),
        name="nsa_compress",
    )(kc, vc, pek, pev, w1k, w1v, w2k, w2v)


def _mark_top_blocks(score, notsel, rounds):
    rows = lax.broadcasted_iota(jnp.int32, score.shape, 0).astype(_F32)
    left = score
    for _ in range(rounds):
        mx = jnp.max(left, axis=0, keepdims=True)
        idx = jnp.min(jnp.where(left == mx, rows, float(LANES)), axis=0, keepdims=True)
        left = jnp.where(rows == idx, -jnp.inf, left)
    return jnp.where((score > -jnp.inf) & (left == -jnp.inf), 0.0, notsel)


def _attn_kernel(qt_ref, gt_ref, kca_ref, vct_ref, ksa_ref, vst_ref, kwa_ref, vwt_ref, o_ref,
                 m_ref, acc_ref, lhs_ref, part_ref, gate_ref, bits_ref, *, seq, n_seq):
    step = pl.program_id(0)
    n_pair = seq // (Q_BLOCK * PAIR)
    last = n_seq * n_pair - 1
    front_pair = jnp.minimum(step, last)
    back_pair = jnp.clip(step - 1, 0, last)
    nc = kca_ref.shape[1]
    n_sel = seq // SEL_BLOCK
    n_forced = 1 + N_LOCAL_FORCED
    k_top = min(N_SELECT, n_sel)
    G = Q_PER_KV
    W = G * Q_BLOCK
    words = LANES // BITS_PER_WORD
    blocks_per_tile = SEL_TILE // SEL_BLOCK
    tiles_per_word = BITS_PER_WORD // blocks_per_tile
    put = step % 2
    get = 1 - put
    put3 = step % 3
    get3 = (step + 1) % 3
    units = range(PAIR)

    def flag_base(slot, u):
        return (slot * PAIR + u) * (words + 1)

    @pl.when(step == 0)
    def _():
        lhs_ref[1] = jnp.zeros(lhs_ref.shape[1:], _BF)
        part_ref[...] = jnp.zeros(part_ref.shape, _F32)
        gate_ref[...] = jnp.zeros(gate_ref.shape, _F32)
        acc_ref[...] = jnp.ones(acc_ref.shape, _F32)
        for k in range(PAIR * (words + 1)):
            bits_ref[flag_base(1, 0) + k] = 0

    lane_w = lax.broadcasted_iota(jnp.int32, (1, W), 1)

    def head_slopes(pair_index):
        h = (pair_index // n_pair) % N_KV_HEADS
        slope = jnp.zeros((1, W), _F32)
        for g in range(G):
            sg = jnp.where(h == 0, _F32(2.0 ** -(g + 1)), _F32(2.0 ** -(G + g + 1)))
            slope = jnp.where(lane_w // Q_BLOCK == g, sg, slope)
        return (slope * LOG2_E).astype(_BF).astype(_F32)

    slope = head_slopes(front_pair)
    slope_b = head_slopes(back_pair)
    strips =[slice(g * Q_BLOCK, (g + 1) * Q_BLOCK) for g in range(G)]
    top_row = lax.broadcasted_iota(jnp.int32, (TOP_ROWS, W), 0)

    def position_rows(first):
        top = jnp.where(top_row == 0, first, jnp.where(top_row == 1, slope * POS_RADIX, 0.0)).astype(_BF)
        return jnp.concatenate([top, jnp.zeros((LANES - TOP_ROWS, W), _BF)], axis=0)

    key_row = lax.broadcasted_iota(jnp.int32, (SEL_TILE, Q_BLOCK), 0)
    pos_rows = position_rows(slope)

    def gate_row(u, branch):
        gates = gt_ref[0][:, u * Q_BLOCK:(u + 1) * Q_BLOCK]
        return jnp.concatenate([gates[3 * g + branch:3 * g + branch + 1] for g in range(G)], axis=1)

    def finish(u):
        out_t = (part_ref[get3, u] + acc_ref[u, 0:HEAD_DIM]
                 * (gate_ref[get3, u][0:1] * (1.0 / acc_ref[u, HEAD_DIM:HEAD_DIM + 1])))
        for g, cs in enumerate(strips):
            o_ref[u * Q_BLOCK:(u + 1) * Q_BLOCK, g * HEAD_DIM:(g + 1) * HEAD_DIM] = out_t[:, cs].T.astype(_BF)

    front = [{} for _ in units]

    def front_scores(u):
        f = front[u]
        f["q0"] =(front_pair % n_pair * PAIR + u) * Q_BLOCK
        f["tok"] = f["q0"] + lane_w % Q_BLOCK
        f["qt"] = qt_ref[u, 0]
        f["s"] = _dot(kca_ref[0], jnp.concatenate([f["qt"], pos_rows], axis=0))

    def tile_scores(bk, j, causal, live=None):
        k0 = j * SEL_TILE if isinstance(j, int) else pl.multiple_of(j * SEL_TILE, SEL_TILE)
        st = _dot(ksa_ref[0, 0, pl.ds(k0, SEL_TILE), :], bk["lhs"])
        if causal:
            ahead = jnp.where(k0 + key_row > bk["tok"][:, :Q_BLOCK], NEG_INF, 0.0)
            st = jnp.concatenate([st[:, cs] + ahead for cs in strips], axis=1)
        shift = slope_b * (bk["p0"] - k0).astype(_F32)
        if live is not None:
            shift = jnp.where(live, shift, -NEG_INF)
        return st, shift, k0

    def tile_max(tiles):
        m = None
        for st, shift, _ in tiles:
            cm = jnp.max(st, axis=0, keepdims=True) - shift
            m = cm if m is None else jnp.maximum(m, cm)
        return m

    def tile_sums(tiles, m):
        acc = None
        for st, shift, k0 in tiles:
            sub = m + shift
            pt = jnp.concatenate([jnp.exp2(st[:, cs] - sub[:, cs]) for cs in strips], axis=1)
            ai = _dot(vst_ref[0, 0, :, pl.ds(k0, SEL_TILE)], pt.astype(_BF))
            acc = ai if acc is None else acc + ai
        return acc

    back = [{} for _ in units]

    def back_scores(u):
        bk = back[u]
        bk["p0"] =(back_pair % n_pair * PAIR + u) * Q_BLOCK
        bk["tok"] = bk["p0"] + lane_w % Q_BLOCK
        bk["lhs"] = lhs_ref[get, u]
        bk["first_local"] = bk["p0"] // SEL_TILE - (LOCAL_TILES - 1)
        bk["tiles"] = [tile_scores(bk, 0, False, live=bk["first_local"] > 0)]
        for i in range(LOCAL_TILES):
            j = bk["first_local"] + i
            last = i == LOCAL_TILES - 1
            bk["tiles"].append(tile_scores(bk, jnp.maximum(j, 0), last, live=None if last else j >= 0))

    n_row = lax.broadcasted_iota(jnp.int32, (nc, Q_BLOCK), 0)

    def front_compressed(u):
        f = front[u]
        n_last = (f["tok"] - (CMP_BLOCK - 1)) // CMP_STRIDE
        mask_c = jnp.where(n_row <= n_last[:, :Q_BLOCK], 0.0, NEG_INF)
        e_parts = []
        for cs in strips:
            sg = f["s"][:, cs] + mask_c
            e_parts.append(jnp.exp2(sg - jnp.max(sg, axis=0, keepdims=True)).astype(_BF))
        both = _dot(vct_ref[0], jnp.concatenate(e_parts, axis=1))
        inv = jnp.where(n_last >= 0, 1.0 / both[HEAD_DIM + LANES:HEAD_DIM + LANES + 1], 0.0)
        f["o_cmp"] = both[:HEAD_DIM] * (inv * gate_row(u, 0))
        sums = both[HEAD_DIM:HEAD_DIM + LANES] * inv
        imp_t = sums[:, strips[0]]
        for cs in strips[1:]:
            imp_t = imp_t + sums[:, cs]
        f["imp_t"] = imp_t

    wlen = WINDOW + Q_BLOCK

    def front_window_scores(u):
        f = front[u]
        anchor =jnp.concatenate([jnp.minimum(f["imp_t"][0:1], 0.0)] * G, axis=1)
        win_rows = position_rows(slope + anchor)
        f["w0"] = pl.multiple_of(jnp.maximum(f["q0"] - WINDOW, 0), Q_BLOCK)
        f["sw"] = _dot(kwa_ref[0, 0, pl.ds(f["w0"], wlen), :], jnp.concatenate([f["qt"], win_rows], axis=0))

    def back_sums(u):
        bk = back[u]
        m = tile_max(bk["tiles"])
        m_ref[u, 0:1] = m
        acc_ref[u] = tile_sums(bk["tiles"], m)

    def front_window_sums(u):
        f = front[u]
        w_row =f["w0"] + lax.broadcasted_iota(jnp.int32, (wlen, Q_BLOCK), 0)
        dist = f["tok"][:, :Q_BLOCK] - w_row
        mask_w = jnp.where(dist.astype(jnp.uint32) < WINDOW, 0.0, NEG_INF)
        e_parts = []
        for cs in strips:
            sg = f["sw"][:, cs] + mask_w
            e_parts.append(jnp.exp2(sg - jnp.max(sg, axis=0, keepdims=True)).astype(_BF))
        win = _dot(vwt_ref[0, 0, :, pl.ds(f["w0"], wlen)], jnp.concatenate(e_parts, axis=1))
        f["o_win"] = win[:HEAD_DIM] * (gate_row(u, 2) / win[HEAD_DIM:HEAD_DIM + 1])

    mrow = lax.broadcasted_iota(jnp.int32, (LANES, Q_BLOCK), 0)
    blk_row = lax.broadcasted_iota(jnp.int32, (LANES, 1), 0)
    blk_bit = jnp.left_shift(1, blk_row % BITS_PER_WORD).astype(_F32)

    def front_select(u):
        f = front[u]
        tcol = lax.broadcasted_iota(jnp.int32, (LANES, Q_BLOCK), 1) + f["q0"]
        lag = tcol // SEL_BLOCK - mrow
        forced = (mrow == 0) | ((lag >= 0) & (lag < N_LOCAL_FORCED))
        score = jnp.where(forced | (lag < 0) | (mrow >= n_sel), -jnp.inf, f["imp_t"])
        notsel_t = _mark_top_blocks(score, jnp.where(forced, 0.0, 1.0), k_top - n_forced)

        flags = jnp.concatenate([notsel_t] * G, axis=1)
        sel_rows = jnp.concatenate([jnp.where(top_row == 0, slope, flags[:TOP_ROWS]), flags[TOP_ROWS:]],
                                   axis=0).astype(_BF)
        lhs_ref[put, u] = jnp.concatenate([f["qt"], sel_rows], axis=0)
        part_ref[put3, u] = f["o_cmp"] + f["o_win"]
        gate_ref[put3, u] = jnp.broadcast_to(gate_row(u, 1), gate_ref.shape[2:])

        far_end = (f["q0"] // SEL_TILE - (LOCAL_TILES - 1)) * blocks_per_tile
        blk_on = jnp.where((jnp.min(notsel_t, axis=1, keepdims=True) < 0.5)
                           & (blk_row >= blocks_per_tile) & (blk_row < far_end), 1.0, 0.0)
        for k in range(words):
            word = jnp.sum((blk_on * blk_bit)[k * BITS_PER_WORD:(k + 1) * BITS_PER_WORD])
            bits_ref[flag_base(put, u) + k] = word.astype(jnp.int32)
        bits_ref[flag_base(put, u) + words] = jnp.sum(blk_on).astype(jnp.int32)

    phases = {"finish": finish, "front_scores": front_scores, "back_scores": back_scores,
              "front_compressed": front_compressed, "front_window_scores": front_window_scores,
              "back_sums": back_sums, "front_window_sums": front_window_sums, "front_select": front_select}
    for phase, u in PHASE_ORDER:
        phases[phase](u)

    for u, bk in zip(units, back):
        def far_tile(j, carry, u=u, bk=bk):
            word = bits_ref[flag_base(get, u) + j // tiles_per_word]
            tile_bits = (word >> ((j % tiles_per_word) * blocks_per_tile)) & ((1 << blocks_per_tile) - 1)

            @pl.when(tile_bits != 0)
            def _():
                tile = [tile_scores(bk, j, False)]
                m_old = m_ref[u, 0:1]
                m_new = jnp.maximum(m_old, tile_max(tile))
                m_ref[u, 0:1] = m_new
                acc_ref[u] = jnp.exp2(m_old - m_new) * acc_ref[u] + tile_sums(tile, m_new)
            return carry

        @pl.when(bits_ref[flag_base(get, u) + words] != 0)
        def _(far_tile=far_tile, bk=bk):
            lax.fori_loop(1, bk["first_local"], far_tile, 0)


def _nsa_attention(qt, gt, kca, vct, ksa, vst, kwa, vwt, B, S):
    n_pair = S // (Q_BLOCK * PAIR)
    T = B * S
    nc = kca.shape[1]
    gd = Q_PER_KV * HEAD_DIM
    W = Q_PER_KV * Q_BLOCK
    n_seq = B * N_KV_HEADS
    last = n_seq * n_pair - 1
    front = lambda i: jnp.minimum(i, last)
    back = lambda i: jnp.clip(i - 1, 0, last)
    done = lambda i: jnp.maximum(i - 2, 0)
    seq_of = lambda p: p // n_pair
    blk_of = lambda p: (seq_of(p) // N_KV_HEADS) * n_pair + p % n_pair
    head_of = lambda p: seq_of(p) % N_KV_HEADS
    rows = lambda stage: pl.BlockSpec((1, 1, S, 2 * HEAD_DIM),
                                      lambda i: (seq_of(stage(i)) // N_KV_HEADS, head_of(stage(i)), 0, 0))
    cols = lambda stage: pl.BlockSpec((1, 1, VT_ROWS, S),
                                      lambda i: (seq_of(stage(i)) // N_KV_HEADS, head_of(stage(i)), 0, 0))
    return pl.pallas_call(
        functools.partial(_attn_kernel, seq=S, n_seq=n_seq),
        out_shape=jax.ShapeDtypeStruct((T, N_KV_HEADS * gd), _BF),
        grid=(last + 3,),
        in_specs=[
            pl.BlockSpec((PAIR, 1, HEAD_DIM, W), lambda i: (blk_of(front(i)), head_of(front(i)), 0, 0)),
            pl.BlockSpec((1, GATE_ROWS, PAIR * Q_BLOCK), lambda i: (head_of(front(i)), 0, blk_of(front(i)))),
            pl.BlockSpec((1, nc, 2 * HEAD_DIM), lambda i: (seq_of(front(i)), 0, 0)),
            pl.BlockSpec((1, CMP_ROWS, nc), lambda i: (seq_of(front(i)), 0, 0)),
            rows(back), cols(back), rows(front), cols(front),
        ],
        out_specs=pl.BlockSpec((PAIR * Q_BLOCK, gd), lambda i: (blk_of(done(i)), head_of(done(i)))),
        scratch_shapes=[pltpu.VMEM((PAIR, 8, W), _F32), pltpu.VMEM((PAIR, VT_ROWS, W), _F32),
                        pltpu.VMEM((2, PAIR, 2 * HEAD_DIM, W), _BF), pltpu.VMEM((3, PAIR, HEAD_DIM, W), _F32),
                        pltpu.VMEM((3, PAIR, 8, W), _F32),
                        pltpu.SMEM((2 * PAIR * (LANES // BITS_PER_WORD + 1),), jnp.int32)],
        compiler_params=pltpu.CompilerParams(
            dimension_semantics=("arbitrary",), vmem_limit_bytes=VMEM_LIMIT),
        name="nsa_attention",
    )(qt, gt, kca, vct, ksa, vst, kwa, vwt)


def _merge_ffn_kernel(x_ref, gm_ref, yn_ref, wm1_ref, wpn_ref, wo_ref, g1_ref, b1_ref,
                      w1_ref, w2_ref, g2_ref, b2_ref, o_ref):
    half = x_ref.shape[0] // FFN_SPLIT
    parts = [slice(p * half, (p + 1) * half) for p in range(FFN_SPLIT)]
    x = [x_ref[rs, :] for rs in parts]
    gate = [_sigmoid(_dot(xp.astype(_BF), wm1_ref[...])) for xp in x]
    merged = [gm_ref[rs, :] + g * _dot(yn_ref[rs, :], wpn_ref[...]) for rs, g in zip(parts, gate)]
    mix = [_dot(mp.astype(_BF), wo_ref[...]) for mp in merged]
    hid = [_layer_norm(DEEPNORM_ALPHA * xp + mp, g1_ref[...], b1_ref[...]) for xp, mp in zip(x, mix)]
    hb = [hp.astype(_BF) for hp in hid]
    f = [None] * FFN_SPLIT
    for c in range(D_FF // D_MODEL):
        a = [jnp.maximum(_dot(hp, w1_ref[:, c * D_MODEL:(c + 1) * D_MODEL]), 0.0) for hp in hb]
        for p, ap in enumerate(a):
            fc = _dot((ap * ap).astype(_BF), w2_ref[c * D_MODEL:(c + 1) * D_MODEL, :])
            f[p] = fc if f[p] is None else f[p] + fc
    for rs, hp, fp in zip(parts, hid, f):
        o_ref[rs, :] = _layer_norm(DEEPNORM_ALPHA * hp + fp, g2_ref[...], b2_ref[...])


def _merge_ffn(x2, gm, yn, wm1, wpn, wo, g1, b1, w1, w2, g2, b2):
    T = x2.shape[0]
    tm = ROW_TILE
    rows = lambda w: pl.BlockSpec((tm, w), lambda i: (i, 0))
    const = lambda r, c: _resident((r, c), lambda i: (0, 0))
    return pl.pallas_call(
        _merge_ffn_kernel,
        out_shape=jax.ShapeDtypeStruct((T, D_MODEL), _F32),
        grid=(T // tm,),
        in_specs=[rows(D_MODEL), rows(D_MODEL), rows(D_MODEL),
                  const(D_MODEL, D_MODEL), const(D_MODEL, D_MODEL), const(D_MODEL, D_MODEL),
                  const(1, D_MODEL), const(1, D_MODEL),
                  const(D_MODEL, D_FF), const(D_FF, D_MODEL),
                  const(1, D_MODEL), const(1, D_MODEL)],
        out_specs=rows(D_MODEL),
        compiler_params=pltpu.CompilerParams(
            dimension_semantics=("arbitrary",), vmem_limit_bytes=VMEM_LIMIT),
        name="merge_ffn",
    )(x2, gm, yn, wm1, wpn, wo, g1, b1, w1, w2, g2, b2)


def _layer(x2, B, S, w_in, layer, gm_ln_g, gm_ln_b, gm_w_s, gm_b_s, cmp_pe_k, cmp_w1_k, cmp_w2_k,
           cmp_pe_v, cmp_w1_v, cmp_w2_v, w_proj_gm, w_proj_nsa, w_out,
           ln1_g, ln1_b, w_ff1, w_ff2, ln2_g, ln2_b):
    o_q = 2 * D_MODEL
    o_m = o_q + (Q_PER_KV + 6) * N_KV_HEADS * HEAD_DIM + 3 * Q_PER_KV * N_KV_HEADS
    row = lambda v: v.reshape(1, -1)
    merge_cols = lambda j: w_in[layer * D_MODEL:(layer + 1) * D_MODEL, o_m + j * D_MODEL:o_m + (j + 1) * D_MODEL]

    gm = _gm_mixer(x2, w_in, layer, merge_cols(0), row(gm_ln_g), row(gm_ln_b), gm_w_s,
                   jnp.broadcast_to(gm_b_s[:, :, None], (GM_GROUPS, GM_CHUNK, LANES)), w_proj_gm)

    qt, kc, vc, ksa, vst, kwa, vwt, gt = _qkv_proj(x2, w_in, layer, o_q, B, S)

    heads = lambda a: a.reshape(B * N_KV_HEADS, S, HEAD_DIM)
    pairs = lambda w: w.astype(_BF).reshape(CMP_BLOCK // 2, 2 * HEAD_DIM, HEAD_DIM)
    kca, vct = _nsa_compress(heads(kc), heads(vc), cmp_pe_k, cmp_pe_v, pairs(cmp_w1_k), pairs(cmp_w1_v),
                             cmp_w2_k.astype(_BF), cmp_w2_v.astype(_BF))

    yn = _nsa_attention(qt, gt, kca, vct, ksa, vst, kwa, vwt, B, S)

    return _merge_ffn(x2, gm, yn, merge_cols(1).astype(_BF), w_proj_nsa.astype(_BF),
                      w_out.astype(_BF), row(ln1_g), row(ln1_b), w_ff1.astype(_BF), w_ff2.astype(_BF),
                      row(ln2_g), row(ln2_b))


def kernel(x, w_in, gm_ln_g, gm_ln_b, gm_w_s, gm_b_s, cmp_pe_k, cmp_w1_k, cmp_w2_k, cmp_pe_v, cmp_w1_v, cmp_w2_v, w_proj_gm, w_proj_nsa, w_out, ln1_g, ln1_b, w_ff1, w_ff2, ln2_g, ln2_b):
    B, S, D = x.shape
    assert D == D_MODEL and S % ROW_TILE == 0 and S % GM_ROW_TILE == 0 and WINDOW + Q_BLOCK <= S <= SEL_BLOCK * LANES
    assert S // POS_RADIX <= POS_RADIX and S % (PAIR * Q_BLOCK) == 0
    h = x.reshape(B * S, D)
    depth, _, d_in = w_in.shape
    w_rows = w_in.reshape(depth * D, d_in)
    for l in range(depth):
        h = _layer(h, B, S, w_rows, l, gm_ln_g[l], gm_ln_b[l], gm_w_s[l], gm_b_s[l],
                   cmp_pe_k[l], cmp_w1_k[l], cmp_w2_k[l], cmp_pe_v[l], cmp_w1_v[l], cmp_w2_v[l],
                   w_proj_gm[l], w_proj_nsa[l], w_out[l], ln1_g[l], ln1_b[l],
                   w_ff1[l], w_ff2[l], ln2_g[l], ln2_b[l])
    return h.reshape(B, S, D)
```

```python
import functools
import math

import jax
import jax.numpy as jnp
from jax import lax
from jax.experimental import pallas as pl
from jax.experimental.pallas import tpu as pltpu

D_MODEL = 1024
GM_GROUPS = 8
GM_CHUNK = 128
N_KV_HEADS = 2
Q_PER_KV = 4
HEAD_DIM = 128
CMP_BLOCK = 32
CMP_STRIDE = 16
SEL_BLOCK = 64
N_SELECT = 16
N_LOCAL_FORCED = 2
WINDOW = 512
Q_BLOCK = 128
D_FF = 4 * D_MODEL
DEEPNORM_ALPHA = 2.0 ** 0.25
LN_EPS = 1e-5
NEG_INF = -1e30
LOG2_E = math.log2(math.e)

LANES = 128
SEL_TILE = 128
LOCAL_TILES = 10
PAIR = 2
PHASE_ORDER = tuple((phase, u) for phase in ("finish", "front_scores", "back_scores", "front_compressed",
                                              "front_window_scores", "back_sums", "front_window_sums",
                                              "front_select") for u in range(PAIR))
BITS_PER_WORD = 16
POS_RADIX = 256
WIDE_ROW_TILE = 1024
GM_SPLIT = 4
FFN_SPLIT = 2
ROW_TILE = 512
VT_ROWS = HEAD_DIM + 16
CMP_ROWS = HEAD_DIM + LANES + 16
TOP_ROWS = 16
GATE_ROWS = 16
VMEM_LIMIT = 56 * 1024 * 1024

_BF = jnp.bfloat16
_F32 = jnp.float32


def _dot(a, b):
    return jnp.dot(a, b, preferred_element_type=_F32)


def _gelu(x):
    c = math.sqrt(2.0 / math.pi)
    return 0.5 * x * (1.0 + jnp.tanh(c * (x + 0.044715 * (x * x * x))))


def _sigmoid(x):
    return 1.0 / (1.0 + jnp.exp(-x))


def _layer_norm(x, g, b):
    mu = jnp.mean(x, axis=-1, keepdims=True)
    xc = x - mu
    var = jnp.mean(xc * xc, axis=-1, keepdims=True)
    return xc * lax.rsqrt(var + LN_EPS) * g + b


def _position_lanes(pos, shape):
    lane = lax.broadcasted_iota(jnp.int32, shape, 1)
    return jnp.where(lane == 0, (pos % POS_RADIX).astype(_F32),
                     jnp.where(lane == 1, (pos // POS_RADIX).astype(_F32), 0.0)).astype(_BF)


def _stage_bf16(first_step, pairs):
    @pl.when(first_step)
    def _():
        for src, dst in pairs:
            dst[...] = src[...].astype(_BF)


def _resident(shape, index_map):
    return pl.BlockSpec(shape, index_map, pipeline_mode=pl.Buffered(1))


def _gm_kernel(x_ref, wgm32_ref, wm032_ref, lng_ref, lnb_ref, ws_ref, bs_ref, wpg32_ref, o_ref,
               vg_ref, wgm_ref, wm0_ref, wpg_ref):
    _stage_bf16(pl.program_id(0) == 0, [(wgm32_ref, wgm_ref), (wm032_ref, wm0_ref), (wpg32_ref, wpg_ref)])
    tm = x_ref.shape[0]
    rows = tm // GM_SPLIT
    parts = [slice(p * rows, (p + 1) * rows) for p in range(GM_SPLIT)]
    xb = [x_ref[rs, :].astype(_BF) for rs in parts]
    z = [_gelu(_dot(xp, wgm_ref[...])) for xp in xb]
    gate_logits = [_dot(xp, wm0_ref[...]) for xp in xb]
    v = [_layer_norm(zp[:, D_MODEL:], lng_ref[...], lnb_ref[...]).astype(_BF) for zp in z]
    row = lax.broadcasted_iota(jnp.int32, (GM_CHUNK, GM_CHUNK), 0)
    col = lax.broadcasted_iota(jnp.int32, (GM_CHUNK, GM_CHUNK), 1)
    w = [jnp.where(row >= col, ws_ref[gi], 0.0).astype(_BF) for gi in range(GM_GROUPS)]
    for p, rs in enumerate(parts):
        for gi in range(GM_GROUPS):
            for c in range(rows // GM_CHUNK):
                blk = v[p][c * GM_CHUNK:(c + 1) * GM_CHUNK, gi * LANES:(gi + 1) * LANES]
                r0 = p * rows + c * GM_CHUNK
                vg_ref[r0:r0 + GM_CHUNK, gi * LANES:(gi + 1) * LANES] = _dot(w[gi], blk) + bs_ref[gi]
    for p, rs in enumerate(parts):
        y = (z[p][:, :D_MODEL] * vg_ref[rs, :]).astype(_BF)
        o_ref[rs, :] = _sigmoid(gate_logits[p]) * _dot(y, wpg_ref[...])


def _gm_mixer(x2, w_in, layer, wm0, lng, lnb, ws, bs, wpg):
    T = x2.shape[0]
    tm = WIDE_ROW_TILE
    const2 = lambda i: (0, 0)
    const3 = lambda i: (0, 0, 0)
    return pl.pallas_call(
        _gm_kernel,
        out_shape=jax.ShapeDtypeStruct((T, D_MODEL), _F32),
        grid=(T // tm,),
        in_specs=[
            pl.BlockSpec((tm, D_MODEL), lambda i: (i, 0)),
            _resident((D_MODEL, 2 * D_MODEL), lambda i: (layer, 0)),
            _resident((D_MODEL, D_MODEL), const2),
            pl.BlockSpec((1, D_MODEL), const2),
            pl.BlockSpec((1, D_MODEL), const2),
            pl.BlockSpec((GM_GROUPS, GM_CHUNK, GM_CHUNK), const3),
            pl.BlockSpec((GM_GROUPS, GM_CHUNK, LANES), const3),
            _resident((D_MODEL, D_MODEL), const2),
        ],
        out_specs=pl.BlockSpec((tm, D_MODEL), lambda i: (i, 0)),
        scratch_shapes=[pltpu.VMEM((tm, D_MODEL), _F32), pltpu.VMEM((D_MODEL, 2 * D_MODEL), _BF),
                        pltpu.VMEM((D_MODEL, D_MODEL), _BF), pltpu.VMEM((D_MODEL, D_MODEL), _BF)],
        compiler_params=pltpu.CompilerParams(
            dimension_semantics=("arbitrary",), vmem_limit_bytes=VMEM_LIMIT),
        name="gm_mixer",
    )(x2, w_in, wm0, lng, lnb, ws, bs, wpg)


def _qkv_kernel(x_ref, wq32_ref, wkv32_ref, wg32_ref, qt_ref, kc_ref, vc_ref, ksa_ref, vst_ref, kwa_ref, vwt_ref,
                gt_ref, wq_ref, wkv_ref, wg_ref):
    _stage_bf16((pl.program_id(0) == 0) & (pl.program_id(1) == 0),
                [(wq32_ref, wq_ref), (wkv32_ref, wkv_ref), (wg32_ref, wg_ref)])
    tm = x_ref.shape[0]
    xb = x_ref[...].astype(_BF)
    zq = _dot(xb, wq_ref[...]) * (HEAD_DIM ** -0.5 * LOG2_E)
    for tb in range(tm // Q_BLOCK):
        for h in range(N_KV_HEADS):
            for g in range(Q_PER_KV):
                c0 = (h * Q_PER_KV + g) * HEAD_DIM
                blk = zq[tb * Q_BLOCK:(tb + 1) * Q_BLOCK, c0:c0 + HEAD_DIM]
                qt_ref[tb, h, :, g * Q_BLOCK:(g + 1) * Q_BLOCK] = blk.T.astype(_BF)
    z = _dot(xb, wkv_ref[...])
    kpos = pl.program_id(1) * tm + lax.broadcasted_iota(jnp.int32, (tm, LANES), 0)
    blk_lane = lax.broadcasted_iota(jnp.int32, (tm, LANES), 1)
    sel_lanes = jnp.where(blk_lane == 0, (kpos % SEL_TILE).astype(_F32),
                          jnp.where(kpos // SEL_BLOCK == blk_lane, NEG_INF, 0.0)).astype(_BF)
    win_lanes = _position_lanes(kpos, (tm, LANES))
    ones = jnp.ones((VT_ROWS - HEAD_DIM, tm), _BF)
    for h in range(N_KV_HEADS):
        def col(j):
            return z[:, j * 2 * HEAD_DIM + h * HEAD_DIM: j * 2 * HEAD_DIM + (h + 1) * HEAD_DIM]
        kc_ref[0, h] = col(0)
        vc_ref[0, h] = col(1)
        ksa_ref[0, h, :, :HEAD_DIM] = col(2).astype(_BF)
        ksa_ref[0, h, :, HEAD_DIM:] = sel_lanes
        vst_ref[0, h, :HEAD_DIM] = col(3).T.astype(_BF)
        vst_ref[0, h, HEAD_DIM:] = ones
        kwa_ref[0, h, :, :HEAD_DIM] = col(4).astype(_BF)
        kwa_ref[0, h, :, HEAD_DIM:] = win_lanes
        vwt_ref[0, h, :HEAD_DIM] = col(5).T.astype(_BF)
        vwt_ref[0, h, HEAD_DIM:] = ones
    zg_t = _sigmoid(_dot(xb, wg_ref[...])).T
    per_head = 3 * Q_PER_KV
    for h in range(N_KV_HEADS):
        gt_ref[h] = zg_t[h * per_head:h * per_head + GATE_ROWS]


def _qkv_proj(x2, w_in, layer, q_col, B, S):
    T = x2.shape[0]
    tm = WIDE_ROW_TILE
    nsb = S // tm
    gd = Q_PER_KV * Q_BLOCK
    aw = N_KV_HEADS * Q_PER_KV * HEAD_DIM
    kvw = 6 * N_KV_HEADS * HEAD_DIM
    kv_col = q_col + aw
    gate_col = kv_col + kvw
    assert q_col % aw == 0 and kv_col % kvw == 0 and gate_col % LANES == 0
    rows_spec = lambda w: pl.BlockSpec((1, N_KV_HEADS, tm, w), lambda b, s: (b, 0, s, 0))
    rows_shape = lambda w, dt: jax.ShapeDtypeStruct((B, N_KV_HEADS, S, w), dt)
    cols_spec = pl.BlockSpec((1, N_KV_HEADS, VT_ROWS, tm), lambda b, s: (b, 0, 0, s))
    cols_shape = jax.ShapeDtypeStruct((B, N_KV_HEADS, VT_ROWS, S), _BF)
    return pl.pallas_call(
        _qkv_kernel,
        out_shape=(
            jax.ShapeDtypeStruct((T // Q_BLOCK, N_KV_HEADS, HEAD_DIM, gd), _BF),
            rows_shape(HEAD_DIM, _F32), rows_shape(HEAD_DIM, _F32),
            rows_shape(2 * HEAD_DIM, _BF), cols_shape,
            rows_shape(2 * HEAD_DIM, _BF), cols_shape,
            jax.ShapeDtypeStruct((N_KV_HEADS, GATE_ROWS, T), _F32),
        ),
        grid=(B, nsb),
        in_specs=[
            pl.BlockSpec((tm, D_MODEL), lambda b, s: (b * nsb + s, 0)),
            _resident((D_MODEL, aw), lambda b, s: (layer, q_col // aw)),
            _resident((D_MODEL, kvw), lambda b, s: (layer, kv_col // kvw)),
            _resident((D_MODEL, LANES), lambda b, s: (layer, gate_col // LANES)),
        ],
        out_specs=(
            pl.BlockSpec((tm // Q_BLOCK, N_KV_HEADS, HEAD_DIM, gd), lambda b, s: (b * nsb + s, 0, 0, 0)),
            rows_spec(HEAD_DIM), rows_spec(HEAD_DIM), rows_spec(2 * HEAD_DIM), cols_spec,
            rows_spec(2 * HEAD_DIM), cols_spec,
            pl.BlockSpec((N_KV_HEADS, GATE_ROWS, tm), lambda b, s: (0, 0, b * nsb + s)),
        ),
        scratch_shapes=[pltpu.VMEM((D_MODEL, aw), _BF), pltpu.VMEM((D_MODEL, kvw), _BF),
                        pltpu.VMEM((D_MODEL, LANES), _BF)],
        compiler_params=pltpu.CompilerParams(
            dimension_semantics=("arbitrary", "arbitrary"), vmem_limit_bytes=VMEM_LIMIT),
        name="qkv_proj",
    )(x2, w_in, w_in, w_in)


def _compress_kernel(kc_ref, vc_ref, pek_ref, pev_ref, w1k_ref, w1v_ref, w2k_ref, w2v_ref, ko_ref, vo_ref):
    nc = ko_ref.shape[1]

    def tokens(src, pe, w1, w2):
        first, second = None, None
        for p in range(0, CMP_STRIDE, 2):
            rows = [src[0, pl.ds(p + d, nc, stride=CMP_STRIDE), :] for d in range(2)]
            lo = jnp.concatenate([rows[d] + pe[p + d:p + d + 1, :] for d in range(2)], axis=1)
            hi = jnp.concatenate([rows[d] + pe[CMP_STRIDE + p + d:CMP_STRIDE + p + d + 1, :] for d in range(2)], axis=1)
            a = _dot(lo.astype(_BF), w1[p // 2])
            b = _dot(hi.astype(_BF), w1[(CMP_STRIDE + p) // 2])
            first = a if first is None else first + a
            second = b if second is None else second + b
        pre = first + pltpu.roll(second, nc - 1, 0)
        return _dot(_gelu(pre).astype(_BF), w2[...])

    ko_ref[0, :, :HEAD_DIM] = tokens(kc_ref, pek_ref, w1k_ref, w2k_ref).astype(_BF)
    start = lax.broadcasted_iota(jnp.int32, (nc, LANES), 0) * CMP_STRIDE
    ko_ref[0, :, HEAD_DIM:] = _position_lanes(start, (nc, LANES))
    vo_ref[0, :HEAD_DIM] = tokens(vc_ref, pev_ref, w1v_ref, w2v_ref).T.astype(_BF)
    mi = lax.broadcasted_iota(jnp.int32, (LANES, nc), 0)
    ni = lax.broadcasted_iota(jnp.int32, (LANES, nc), 1)
    overlap_t = jnp.where((ni * CMP_STRIDE + (CMP_BLOCK - 1) >= mi * SEL_BLOCK)
                          & (ni * CMP_STRIDE <= mi * SEL_BLOCK + (SEL_BLOCK - 1)), 1.0, 0.0)
    vo_ref[0, HEAD_DIM:HEAD_DIM + LANES] = overlap_t.astype(_BF)
    vo_ref[0, HEAD_DIM + LANES:] = jnp.ones((CMP_ROWS - HEAD_DIM - LANES, nc), _BF)


def _nsa_compress(kc, vc, pek, pev, w1k, w1v, w2k, w2v):
    BH, S, _ = kc.shape
    nc = S // CMP_STRIDE
    pair = 2 * HEAD_DIM
    const2 = lambda i: (0, 0)
    const3 = lambda i: (0, 0, 0)
    seq_rows = pl.BlockSpec((1, S, HEAD_DIM), lambda i: (i, 0, 0))
    pe_spec = pl.BlockSpec((CMP_BLOCK, HEAD_DIM), const2)
    w1_spec = pl.BlockSpec((CMP_BLOCK // 2, pair, HEAD_DIM), const3)
    w2_spec = pl.BlockSpec((HEAD_DIM, HEAD_DIM), const2)
    return pl.pallas_call(
        _compress_kernel,
        out_shape=(jax.ShapeDtypeStruct((BH, nc, 2 * HEAD_DIM), _BF),
                   jax.ShapeDtypeStruct((BH, CMP_ROWS, nc), _BF)),
        grid=(BH,),
        in_specs=[seq_rows, seq_rows, pe_spec, pe_spec, w1_spec, w1_spec, w2_spec, w2_spec],
        out_specs=(pl.BlockSpec((1, nc, 2 * HEAD_DIM), lambda i: (i, 0, 0)),
                   pl.BlockSpec((1, CMP_ROWS, nc), lambda i: (i, 0, 0))),
        compiler_params=pltpu.CompilerParams(
            dimension_semantics=("arbitrary",), vmem_limit_bytes=VMEM_LIMIT),
        name="nsa_compress",
    )(kc, vc, pek, pev, w1k, w1v, w2k, w2v)


def _mark_top_blocks(score, notsel, rounds):
    rows = lax.broadcasted_iota(jnp.int32, score.shape, 0).astype(_F32)
    left = score
    for _ in range(rounds):
        mx = jnp.max(left, axis=0, keepdims=True)
        idx = jnp.min(jnp.where(left == mx, rows, float(LANES)), axis=0, keepdims=True)
        left = jnp.where(rows == idx, -jnp.inf, left)
    return jnp.where((score > -jnp.inf) & (left == -jnp.inf), 0.0, notsel)


def _attn_kernel(qt_ref, gt_ref, kca_ref, vct_ref, ksa_ref, vst_ref, kwa_ref, vwt_ref, o_ref,
                 m_ref, acc_ref, lhs_ref, part_ref, gate_ref, bits_ref, *, seq, n_seq):
    step = pl.program_id(0)
    n_pair = seq // (Q_BLOCK * PAIR)
    last = n_seq * n_pair - 1
    front_pair = jnp.minimum(step, last)
    back_pair = jnp.clip(step - 1, 0, last)
    nc = kca_ref.shape[1]
    n_sel = seq // SEL_BLOCK
    n_forced = 1 + N_LOCAL_FORCED
    k_top = min(N_SELECT, n_sel)
    G = Q_PER_KV
    W = G * Q_BLOCK
    words = LANES // BITS_PER_WORD
    blocks_per_tile = SEL_TILE // SEL_BLOCK
    tiles_per_word = BITS_PER_WORD // blocks_per_tile
    put = step % 2
    get = 1 - put
    put3 = step % 3
    get3 = (step + 1) % 3
    units = range(PAIR)

    def flag_base(slot, u):
        return (slot * PAIR + u) * (words + 1)

    @pl.when(step == 0)
    def _():
        lhs_ref[1] = jnp.zeros(lhs_ref.shape[1:], _BF)
        part_ref[...] = jnp.zeros(part_ref.shape, _F32)
        gate_ref[...] = jnp.zeros(gate_ref.shape, _F32)
        acc_ref[...] = jnp.ones(acc_ref.shape, _F32)
        for k in range(PAIR * (words + 1)):
            bits_ref[flag_base(1, 0) + k] = 0

    lane_w = lax.broadcasted_iota(jnp.int32, (1, W), 1)

    def head_slopes(pair_index):
        h = (pair_index // n_pair) % N_KV_HEADS
        slope = jnp.zeros((1, W), _F32)
        for g in range(G):
            sg = jnp.where(h == 0, _F32(2.0 ** -(g + 1)), _F32(2.0 ** -(G + g + 1)))
            slope = jnp.where(lane_w // Q_BLOCK == g, sg, slope)
        return (slope * LOG2_E).astype(_BF).astype(_F32)

    slope = head_slopes(front_pair)
    slope_b = head_slopes(back_pair)
    strips =[slice(g * Q_BLOCK, (g + 1) * Q_BLOCK) for g in range(G)]
    top_row = lax.broadcasted_iota(jnp.int32, (TOP_ROWS, W), 0)

    def position_rows(first):
        top = jnp.where(top_row == 0, first, jnp.where(top_row == 1, slope * POS_RADIX, 0.0)).astype(_BF)
        return jnp.concatenate([top, jnp.zeros((LANES - TOP_ROWS, W), _BF)], axis=0)

    key_row = lax.broadcasted_iota(jnp.int32, (SEL_TILE, Q_BLOCK), 0)
    pos_rows = position_rows(slope)

    def gate_row(u, branch):
        gates = gt_ref[0][:, u * Q_BLOCK:(u + 1) * Q_BLOCK]
        return jnp.concatenate([gates[3 * g + branch:3 * g + branch + 1] for g in range(G)], axis=1)

    def finish(u):
        out_t = (part_ref[get3, u] + acc_ref[u, 0:HEAD_DIM]
                 * (gate_ref[get3, u][0:1] * (1.0 / acc_ref[u, HEAD_DIM:HEAD_DIM + 1])))
        for g, cs in enumerate(strips):
            o_ref[u * Q_BLOCK:(u + 1) * Q_BLOCK, g * HEAD_DIM:(g + 1) * HEAD_DIM] = out_t[:, cs].T.astype(_BF)

    front = [{} for _ in units]

    def front_scores(u):
        f = front[u]
        f["q0"] =(front_pair % n_pair * PAIR + u) * Q_BLOCK
        f["tok"] = f["q0"] + lane_w % Q_BLOCK
        f["qt"] = qt_ref[u, 0]
        f["s"] = _dot(kca_ref[0], jnp.concatenate([f["qt"], pos_rows], axis=0))

    def tile_scores(bk, j, causal, live=None):
        k0 = j * SEL_TILE if isinstance(j, int) else pl.multiple_of(j * SEL_TILE, SEL_TILE)
        st = _dot(ksa_ref[0, 0, pl.ds(k0, SEL_TILE), :], bk["lhs"])
        if causal:
            ahead = jnp.where(k0 + key_row > bk["tok"][:, :Q_BLOCK], NEG_INF, 0.0)
            st = jnp.concatenate([st[:, cs] + ahead for cs in strips], axis=1)
        shift = slope_b * (bk["p0"] - k0).astype(_F32)
        if live is not None:
            shift = jnp.where(live, shift, -NEG_INF)
        return st, shift, k0

    def tile_max(tiles):
        m = None
        for st, shift, _ in tiles:
            cm = jnp.max(st, axis=0, keepdims=True) - shift
            m = cm if m is None else jnp.maximum(m, cm)
        return m

    def tile_sums(tiles, m):
        acc = None
        for st, shift, k0 in tiles:
            sub = m + shift
            pt = jnp.concatenate([jnp.exp2(st[:, cs] - sub[:, cs]) for cs in strips], axis=1)
            ai = _dot(vst_ref[0, 0, :, pl.ds(k0, SEL_TILE)], pt.astype(_BF))
            acc = ai if acc is None else acc + ai
        return acc

    back = [{} for _ in units]

    def back_scores(u):
        bk = back[u]
        bk["p0"] =(back_pair % n_pair * PAIR + u) * Q_BLOCK
        bk["tok"] = bk["p0"] + lane_w % Q_BLOCK
        bk["lhs"] = lhs_ref[get, u]
        bk["first_local"] = bk["p0"] // SEL_TILE - (LOCAL_TILES - 1)
        bk["tiles"] = [tile_scores(bk, 0, False, live=bk["first_local"] > 0)]
        for i in range(LOCAL_TILES):
            j = bk["first_local"] + i
            last = i == LOCAL_TILES - 1
            bk["tiles"].append(tile_scores(bk, jnp.maximum(j, 0), last, live=None if last else j >= 0))

    n_row = lax.broadcasted_iota(jnp.int32, (nc, Q_BLOCK), 0)

    def front_compressed(u):
        f = front[u]
        n_last = (f["tok"] - (CMP_BLOCK - 1)) // CMP_STRIDE
        mask_c = jnp.where(n_row <= n_last[:, :Q_BLOCK], 0.0, NEG_INF)
        e_parts = []
        for cs in strips:
            sg = f["s"][:, cs] + mask_c
            e_parts.append(jnp.exp2(sg - jnp.max(sg, axis=0, keepdims=True)).astype(_BF))
        both = _dot(vct_ref[0], jnp.concatenate(e_parts, axis=1))
        inv = jnp.where(n_last >= 0, 1.0 / both[HEAD_DIM + LANES:HEAD_DIM + LANES + 1], 0.0)
        f["o_cmp"] = both[:HEAD_DIM] * (inv * gate_row(u, 0))
        sums = both[HEAD_DIM:HEAD_DIM + LANES] * inv
        imp_t = sums[:, strips[0]]
        for cs in strips[1:]:
            imp_t = imp_t + sums[:, cs]
        f["imp_t"] = imp_t

    wlen = WINDOW + Q_BLOCK

    def front_window_scores(u):
        f = front[u]
        anchor =jnp.concatenate([jnp.minimum(f["imp_t"][0:1], 0.0)] * G, axis=1)
        win_rows = position_rows(slope + anchor)
        f["w0"] = pl.multiple_of(jnp.maximum(f["q0"] - WINDOW, 0), Q_BLOCK)
        f["sw"] = _dot(kwa_ref[0, 0, pl.ds(f["w0"], wlen), :], jnp.concatenate([f["qt"], win_rows], axis=0))

    def back_sums(u):
        bk = back[u]
        m = tile_max(bk["tiles"])
        m_ref[u, 0:1] = m
        acc_ref[u] = tile_sums(bk["tiles"], m)

    def front_window_sums(u):
        f = front[u]
        w_row =f["w0"] + lax.broadcasted_iota(jnp.int32, (wlen, Q_BLOCK), 0)
        dist = f["tok"][:, :Q_BLOCK] - w_row
        mask_w = jnp.where(dist.astype(jnp.uint32) < WINDOW, 0.0, NEG_INF)
        e_parts = []
        for cs in strips:
            sg = f["sw"][:, cs] + mask_w
            e_parts.append(jnp.exp2(sg - jnp.max(sg, axis=0, keepdims=True)).astype(_BF))
        win = _dot(vwt_ref[0, 0, :, pl.ds(f["w0"], wlen)], jnp.concatenate(e_parts, axis=1))
        f["o_win"] = win[:HEAD_DIM] * (gate_row(u, 2) / win[HEAD_DIM:HEAD_DIM + 1])

    mrow = lax.broadcasted_iota(jnp.int32, (LANES, Q_BLOCK), 0)
    blk_row = lax.broadcasted_iota(jnp.int32, (LANES, 1), 0)
    blk_bit = jnp.left_shift(1, blk_row % BITS_PER_WORD).astype(_F32)

    def front_select(u):
        f = front[u]
        tcol = lax.broadcasted_iota(jnp.int32, (LANES, Q_BLOCK), 1) + f["q0"]
        lag = tcol // SEL_BLOCK - mrow
        forced = (mrow == 0) | ((lag >= 0) & (lag < N_LOCAL_FORCED))
        score = jnp.where(forced | (lag < 0) | (mrow >= n_sel), -jnp.inf, f["imp_t"])
        notsel_t = _mark_top_blocks(score, jnp.where(forced, 0.0, 1.0), k_top - n_forced)

        flags = jnp.concatenate([notsel_t] * G, axis=1)
        sel_rows = jnp.concatenate([jnp.where(top_row == 0, slope, flags[:TOP_ROWS]), flags[TOP_ROWS:]],
                                   axis=0).astype(_BF)
        lhs_ref[put, u] = jnp.concatenate([f["qt"], sel_rows], axis=0)
        part_ref[put3, u] = f["o_cmp"] + f["o_win"]
        gate_ref[put3, u] = jnp.broadcast_to(gate_row(u, 1), gate_ref.shape[2:])

        far_end = (f["q0"] // SEL_TILE - (LOCAL_TILES - 1)) * blocks_per_tile
        blk_on = jnp.where((jnp.min(notsel_t, axis=1, keepdims=True) < 0.5)
                           & (blk_row >= blocks_per_tile) & (blk_row < far_end), 1.0, 0.0)
        for k in range(words):
            word = jnp.sum((blk_on * blk_bit)[k * BITS_PER_WORD:(k + 1) * BITS_PER_WORD])
            bits_ref[flag_base(put, u) + k] = word.astype(jnp.int32)
        bits_ref[flag_base(put, u) + words] = jnp.sum(blk_on).astype(jnp.int32)

    phases = {"finish": finish, "front_scores": front_scores, "back_scores": back_scores,
              "front_compressed": front_compressed, "front_window_scores": front_window_scores,
              "back_sums": back_sums, "front_window_sums": front_window_sums, "front_select": front_select}
    for phase, u in PHASE_ORDER:
        phases[phase](u)

    for u, bk in zip(units, back):
        def far_tile(j, carry, u=u, bk=bk):
            word = bits_ref[flag_base(get, u) + j // tiles_per_word]
            tile_bits = (word >> ((j % tiles_per_word) * blocks_per_tile)) & ((1 << blocks_per_tile) - 1)

            @pl.when(tile_bits != 0)
            def _():
                tile = [tile_scores(bk, j, False)]
                m_old = m_ref[u, 0:1]
                m_new = jnp.maximum(m_old, tile_max(tile))
                m_ref[u, 0:1] = m_new
                acc_ref[u] = jnp.exp2(m_old - m_new) * acc_ref[u] + tile_sums(tile, m_new)
            return carry

        @pl.when(bits_ref[flag_base(get, u) + words] != 0)
        def _(far_tile=far_tile, bk=bk):
            lax.fori_loop(1, bk["first_local"], far_tile, 0)


def _nsa_attention(qt, gt, kca, vct, ksa, vst, kwa, vwt, B, S):
    n_pair = S // (Q_BLOCK * PAIR)
    T = B * S
    nc = kca.shape[1]
    gd = Q_PER_KV * HEAD_DIM
    W = Q_PER_KV * Q_BLOCK
    n_seq = B * N_KV_HEADS
    last = n_seq * n_pair - 1
    front = lambda i: jnp.minimum(i, last)
    back = lambda i: jnp.clip(i - 1, 0, last)
    done = lambda i: jnp.maximum(i - 2, 0)
    seq_of = lambda p: p // n_pair
    blk_of = lambda p: (seq_of(p) // N_KV_HEADS) * n_pair + p % n_pair
    head_of = lambda p: seq_of(p) % N_KV_HEADS
    rows = lambda stage: pl.BlockSpec((1, 1, S, 2 * HEAD_DIM),
                                      lambda i: (seq_of(stage(i)) // N_KV_HEADS, head_of(stage(i)), 0, 0))
    cols = lambda stage: pl.BlockSpec((1, 1, VT_ROWS, S),
                                      lambda i: (seq_of(stage(i)) // N_KV_HEADS, head_of(stage(i)), 0, 0))
    return pl.pallas_call(
        functools.partial(_attn_kernel, seq=S, n_seq=n_seq),
        out_shape=jax.ShapeDtypeStruct((T, N_KV_HEADS * gd), _BF),
        grid=(last + 3,),
        in_specs=[
            pl.BlockSpec((PAIR, 1, HEAD_DIM, W), lambda i: (blk_of(front(i)), head_of(front(i)), 0, 0)),
            pl.BlockSpec((1, GATE_ROWS, PAIR * Q_BLOCK), lambda i: (head_of(front(i)), 0, blk_of(front(i)))),
            pl.BlockSpec((1, nc, 2 * HEAD_DIM), lambda i: (seq_of(front(i)), 0, 0)),
            pl.BlockSpec((1, CMP_ROWS, nc), lambda i: (seq_of(front(i)), 0, 0)),
            rows(back), cols(back), rows(front), cols(front),
        ],
        out_specs=pl.BlockSpec((PAIR * Q_BLOCK, gd), lambda i: (blk_of(done(i)), head_of(done(i)))),
        scratch_shapes=[pltpu.VMEM((PAIR, 8, W), _F32), pltpu.VMEM((PAIR, VT_ROWS, W), _F32),
                        pltpu.VMEM((2, PAIR, 2 * HEAD_DIM, W), _BF), pltpu.VMEM((3, PAIR, HEAD_DIM, W), _F32),
                        pltpu.VMEM((3, PAIR, 8, W), _F32),
                        pltpu.SMEM((2 * PAIR * (LANES // BITS_PER_WORD + 1),), jnp.int32)],
        compiler_params=pltpu.CompilerParams(
            dimension_semantics=("arbitrary",), vmem_limit_bytes=VMEM_LIMIT),
        name="nsa_attention",
    )(qt, gt, kca, vct, ksa, vst, kwa, vwt)


def _merge_ffn_kernel(x_ref, gm_ref, yn_ref, wm1_ref, wpn_ref, wo_ref, g1_ref, b1_ref,
                      w1_ref, w2_ref, g2_ref, b2_ref, o_ref):
    half = x_ref.shape[0] // FFN_SPLIT
    parts = [slice(p * half, (p + 1) * half) for p in range(FFN_SPLIT)]
    x = [x_ref[rs, :] for rs in parts]
    gate = [_sigmoid(_dot(xp.astype(_BF), wm1_ref[...])) for xp in x]
    merged = [gm_ref[rs, :] + g * _dot(yn_ref[rs, :], wpn_ref[...]) for rs, g in zip(parts, gate)]
    mix = [_dot(mp.astype(_BF), wo_ref[...]) for mp in merged]
    hid = [_layer_norm(DEEPNORM_ALPHA * xp + mp, g1_ref[...], b1_ref[...]) for xp, mp in zip(x, mix)]
    hb = [hp.astype(_BF) for hp in hid]
    f = [None] * FFN_SPLIT
    for c in range(D_FF // D_MODEL):
        a = [jnp.maximum(_dot(hp, w1_ref[:, c * D_MODEL:(c + 1) * D_MODEL]), 0.0) for hp in hb]
        for p, ap in enumerate(a):
            fc = _dot((ap * ap).astype(_BF), w2_ref[c * D_MODEL:(c + 1) * D_MODEL, :])
            f[p] = fc if f[p] is None else f[p] + fc
    for rs, hp, fp in zip(parts, hid, f):
        o_ref[rs, :] = _layer_norm(DEEPNORM_ALPHA * hp + fp, g2_ref[...], b2_ref[...])


def _merge_ffn(x2, gm, yn, wm1, wpn, wo, g1, b1, w1, w2, g2, b2):
    T = x2.shape[0]
    tm = ROW_TILE
    rows = lambda w: pl.BlockSpec((tm, w), lambda i: (i, 0))
    const = lambda r, c: _resident((r, c), lambda i: (0, 0))
    return pl.pallas_call(
        _merge_ffn_kernel,
        out_shape=jax.ShapeDtypeStruct((T, D_MODEL), _F32),
        grid=(T // tm,),
        in_specs=[rows(D_MODEL), rows(D_MODEL), rows(D_MODEL),
                  const(D_MODEL, D_MODEL), const(D_MODEL, D_MODEL), const(D_MODEL, D_MODEL),
                  const(1, D_MODEL), const(1, D_MODEL),
                  const(D_MODEL, D_FF), const(D_FF, D_MODEL),
                  const(1, D_MODEL), const(1, D_MODEL)],
        out_specs=rows(D_MODEL),
        compiler_params=pltpu.CompilerParams(
            dimension_semantics=("arbitrary",), vmem_limit_bytes=VMEM_LIMIT),
        name="merge_ffn",
    )(x2, gm, yn, wm1, wpn, wo, g1, b1, w1, w2, g2, b2)


def _layer(x2, B, S, w_in, layer, gm_ln_g, gm_ln_b, gm_w_s, gm_b_s, cmp_pe_k, cmp_w1_k, cmp_w2_k,
           cmp_pe_v, cmp_w1_v, cmp_w2_v, w_proj_gm, w_proj_nsa, w_out,
           ln1_g, ln1_b, w_ff1, w_ff2, ln2_g, ln2_b):
    o_q = 2 * D_MODEL
    o_m = o_q + (Q_PER_KV + 6) * N_KV_HEADS * HEAD_DIM + 3 * Q_PER_KV * N_KV_HEADS
    row = lambda v: v.reshape(1, -1)
    merge_cols = lambda j: w_in[layer * D_MODEL:(layer + 1) * D_MODEL, o_m + j * D_MODEL:o_m + (j + 1) * D_MODEL]

    gm = _gm_mixer(x2, w_in, layer, merge_cols(0), row(gm_ln_g), row(gm_ln_b), gm_w_s,
                   jnp.broadcast_to(gm_b_s[:, :, None], (GM_GROUPS, GM_CHUNK, LANES)), w_proj_gm)

    qt, kc, vc, ksa, vst, kwa, vwt, gt = _qkv_proj(x2, w_in, layer, o_q, B, S)

    heads = lambda a: a.reshape(B * N_KV_HEADS, S, HEAD_DIM)
    pairs = lambda w: w.astype(_BF).reshape(CMP_BLOCK // 2, 2 * HEAD_DIM, HEAD_DIM)
    kca, vct = _nsa_compress(heads(kc), heads(vc), cmp_pe_k, cmp_pe_v, pairs(cmp_w1_k), pairs(cmp_w1_v),
                             cmp_w2_k.astype(_BF), cmp_w2_v.astype(_BF))

    yn = _nsa_attention(qt, gt, kca, vct, ksa, vst, kwa, vwt, B, S)

    return _merge_ffn(x2, gm, yn, merge_cols(1).astype(_BF), w_proj_nsa.astype(_BF),
                      w_out.astype(_BF), row(ln1_g), row(ln1_b), w_ff1.astype(_BF), w_ff2.astype(_BF),
                      row(ln2_g), row(ln2_b))


def kernel(x, w_in, gm_ln_g, gm_ln_b, gm_w_s, gm_b_s, cmp_pe_k, cmp_w1_k, cmp_w2_k, cmp_pe_v, cmp_w1_v, cmp_w2_v, w_proj_gm, w_proj_nsa, w_out, ln1_g, ln1_b, w_ff1, w_ff2, ln2_g, ln2_b):
    B, S, D = x.shape
    assert D == D_MODEL and S % ROW_TILE == 0 and S % WIDE_ROW_TILE == 0 and WINDOW + Q_BLOCK <= S <= SEL_BLOCK * LANES
    assert S // POS_RADIX <= POS_RADIX and S % (PAIR * Q_BLOCK) == 0
    h = x.reshape(B * S, D)
    depth, _, d_in = w_in.shape
    w_rows = w_in.reshape(depth * D, d_in)
    for l in range(depth):
        h = _layer(h, B, S, w_rows, l, gm_ln_g[l], gm_ln_b[l], gm_w_s[l], gm_b_s[l],
                   cmp_pe_k[l], cmp_w1_k[l], cmp_w2_k[l], cmp_pe_v[l], cmp_w1_v[l], cmp_w2_v[l],
                   w_proj_gm[l], w_proj_nsa[l], w_out[l], ln1_g[l], ln1_b[l],
                   w_ff1[l], w_ff2[l], ln2_g[l], ln2_b[l])
    return h.reshape(B, S, D)
```
